```python
import math
import jax, jax.numpy as jnp
from jax import lax
import numpy as np

D_MODEL = 1024
BATCH = 2
SEQ = 8192
DEPTH = 1

CHUNK = 64
QBLOCK = 128
EPS = 1e-6
NEG_INF = -1e30

DA_HEADS = 4
DA_QK_DIM = 64
DA_V_DIM = 128
DA_WIDTH = DA_HEADS * DA_V_DIM

GDN_HEADS = 4
GDN_K_DIM = 128
GDN_V_DIM = 128
GDN_WIDTH = GDN_HEADS * GDN_V_DIM
CONV_K = 4
GDN_CONV_DIM = 2 * GDN_HEADS * GDN_K_DIM + GDN_HEADS * GDN_V_DIM

N_BRANCH = 2

N_GROUPS = 4
EXPERTS_PER_GROUP = 8
N_EXPERTS = N_GROUPS * EXPERTS_PER_GROUP
TOP_K = 2
D_EXPERT = 512
MOE_BLOCK = 128

IN_SPLITS = [DA_HEADS * 2 * DA_QK_DIM, DA_HEADS * 2 * DA_QK_DIM, DA_HEADS * DA_V_DIM,
             GDN_HEADS * GDN_K_DIM, GDN_HEADS * GDN_K_DIM, GDN_HEADS * GDN_V_DIM, GDN_HEADS * GDN_V_DIM,
             GDN_HEADS, GDN_HEADS, N_BRANCH * D_MODEL]
D_IN = sum(IN_SPLITS)

kernel_name = 'hybrid_diffattn_gdn_hmoe_adaln'


def rms_norm(x, gain):
    xf = x.astype(jnp.float32)
    y = xf * lax.rsqrt(jnp.mean(xf * xf, axis=-1, keepdims=True) + EPS)
    return (y * gain.astype(jnp.float32)).astype(x.dtype)


def l2_norm(x):
    xf = x.astype(jnp.float32)
    return xf * lax.rsqrt(jnp.sum(xf * xf, axis=-1, keepdims=True) + EPS)


def split_points(sizes):
    pts, acc = [], 0
    for s in sizes[:-1]:
        acc += s
        pts.append(acc)
    return pts


def causal_conv(u, w):
    return lax.conv_general_dilated(u, w[:, None, :].astype(u.dtype), (1,), [(CONV_K - 1, 0)],
                                    dimension_numbers=('NWC', 'WIO', 'NWC'),
                                    feature_group_count=u.shape[-1])


def diff_attention(q, k, v, lam):
    Bn, H, S = v.shape[:3]
    nb = S // QBLOCK
    k1, k2 = k[..., 0, :], k[..., 1, :]
    qb = q.reshape(Bn, H, nb, QBLOCK, 2, DA_QK_DIM).transpose(2, 0, 1, 3, 4, 5)
    slopes = 2.0 ** (-8.0 * jnp.arange(1, H + 1, dtype=jnp.float32) / H)
    kpos = jnp.arange(S)
    scale = DA_QK_DIM ** -0.5

    def block(args):
        i, qblk = args
        qpos = i * QBLOCK + jnp.arange(QBLOCK)
        allowed = (kpos[None, :] // CHUNK) <= (qpos[:, None] // CHUNK)
        dist = jnp.abs(qpos[:, None] - kpos[None, :]).astype(jnp.float32)
        bias = -slopes[:, None, None] * dist

        def probs(qm, km):
            s = jnp.einsum('bhqd,bhkd->bhqk', qm, km).astype(jnp.float32) * scale + bias
            return jax.nn.softmax(jnp.where(allowed, s, NEG_INF), axis=-1)

        a = probs(qblk[..., 0, :], k1) - lam * probs(qblk[..., 1, :], k2)
        return jnp.einsum('bhqk,bhkd->bhqd', a.astype(v.dtype), v)

    o = lax.map(block, (jnp.arange(nb), qb))
    return o.transpose(1, 2, 0, 3, 4).reshape(Bn, H, S, v.shape[-1])


def gated_delta_rule(q, k, v, g, beta):
    Bn, H, S, dk = q.shape
    dv = v.shape[-1]
    nc = S // CHUNK
    q = (q * dk ** -0.5).reshape(Bn, H, nc, CHUNK, dk)
    k = k.reshape(Bn, H, nc, CHUNK, dk)
    v = v.reshape(Bn, H, nc, CHUNK, dv)
    g = g.reshape(Bn, H, nc, CHUNK)
    beta = beta.reshape(Bn, H, nc, CHUNK)
    gcum = jnp.cumsum(g, axis=-1)
    tri = jnp.tril(jnp.ones((CHUNK, CHUNK), dtype=bool))
    strict = jnp.tril(jnp.ones((CHUNK, CHUNK), dtype=bool), -1)
    diff = gcum[..., :, None] - gcum[..., None, :]
    decay = jnp.where(tri, jnp.exp(jnp.where(tri, diff, 0.0)), 0.0)
    kk = jnp.einsum('bhnid,bhnjd->bhnij', k, k)
    lower = jnp.where(strict, beta[..., :, None] * kk * decay, 0.0)
    a_mat = jnp.eye(CHUNK, dtype=q.dtype) + lower
    rhs = jnp.concatenate([v * beta[..., None], k * (beta * jnp.exp(gcum))[..., None]], axis=-1)
    sol = lax.linalg.triangular_solve(a_mat, rhs, left_side=True, lower=True, unit_diagonal=True)
    u, w = sol[..., :dv], sol[..., dv:]
    qk = jnp.einsum('bhnid,bhnjd->bhnij', q, k) * decay
    q_dec = q * jnp.exp(gcum)[..., None]
    k_dec = k * jnp.exp(gcum[..., -1:] - gcum)[..., None]
    g_last = jnp.exp(gcum[..., -1])
    xs = tuple(jnp.moveaxis(t, 2, 0) for t in (u, w, qk, q_dec, k_dec, g_last))

    def step(state, inp):
        u_c, w_c, qk_c, qd_c, kd_c, gl_c = inp
        v_new = u_c - jnp.einsum('bhck,bhkv->bhcv', w_c, state)
        o = jnp.einsum('bhck,bhkv->bhcv', qd_c, state) + jnp.einsum('bhcs,bhsv->bhcv', qk_c, v_new)
        state = state * gl_c[..., None, None] + jnp.einsum('bhck,bhcv->bhkv', kd_c, v_new)
        return state, o

    s0 = jnp.zeros((Bn, H, dk, dv), q.dtype)
    _, o = lax.scan(step, s0, xs)
    return jnp.moveaxis(o, 0, 2).reshape(Bn, H, S, dv)


def hybrid_mixer(h, w_in, q_norm, k_norm, lam, lambda_init, out_norm_a, conv_w, a_log, dt_bias,
                 out_norm_b, w_branch_a, w_branch_b, w_out):
    Bn, S, _ = h.shape
    proj = h @ w_in
    da_q, da_k, da_v, g_q, g_k, g_v, g_z, g_b, g_a, gates = jnp.split(proj, split_points(IN_SPLITS), axis=-1)

    q = rms_norm(da_q.reshape(Bn, S, DA_HEADS, 2, DA_QK_DIM), q_norm).transpose(0, 2, 1, 3, 4)
    k = rms_norm(da_k.reshape(Bn, S, DA_HEADS, 2, DA_QK_DIM), k_norm).transpose(0, 2, 1, 3, 4)
    v = da_v.reshape(Bn, S, DA_HEADS, DA_V_DIM).transpose(0, 2, 1, 3)
    o_a = diff_attention(q, k, v, lam)
    o_a = rms_norm(o_a.transpose(0, 2, 1, 3), out_norm_a) * (1.0 - lambda_init)
    y_a = o_a.reshape(Bn, S, DA_WIDTH) @ w_branch_a

    qkv = jax.nn.silu(causal_conv(jnp.concatenate([g_q, g_k, g_v], axis=-1), conv_w))
    cq, ck, cv = jnp.split(qkv, [GDN_HEADS * GDN_K_DIM, 2 * GDN_HEADS * GDN_K_DIM], axis=-1)
    cq = l2_norm(cq.reshape(Bn, S, GDN_HEADS, GDN_K_DIM)).transpose(0, 2, 1, 3)
    ck = l2_norm(ck.reshape(Bn, S, GDN_HEADS, GDN_K_DIM)).transpose(0, 2, 1, 3)
    cv = cv.reshape(Bn, S, GDN_HEADS, GDN_V_DIM).astype(jnp.float32).transpose(0, 2, 1, 3)
    beta = jax.nn.sigmoid(g_b.astype(jnp.float32)).transpose(0, 2, 1)
    g = (-jnp.exp(a_log.astype(jnp.float32))
         * jax.nn.softplus(g_a.astype(jnp.float32) + dt_bias.astype(jnp.float32))).transpose(0, 2, 1)
    o_b = gated_delta_rule(cq, ck, cv, g, beta).astype(h.dtype)
    o_b = rms_norm(o_b.transpose(0, 2, 1, 3), out_norm_b) * jax.nn.silu(g_z.reshape(Bn, S, GDN_HEADS, GDN_V_DIM))
    y_b = o_b.reshape(Bn, S, GDN_WIDTH) @ w_branch_b

    gate = jax.nn.sigmoid(gates).reshape(Bn, S, N_BRANCH, D_MODEL)
    merged = gate[:, :, 0] * y_a + gate[:, :, 1] * y_b
    return merged @ w_out


def hier_moe(h, w_group, b_group, w_router, b_router, w1, w3, w2):
    Bn, S, D = h.shape
    N = Bn * S
    t = h.reshape(N, D)
    g_prob = jax.nn.softmax((t @ w_group).astype(jnp.float32) + b_group.astype(jnp.float32), axis=-1)
    g_top_p, g_top = lax.top_k(g_prob, 1)
    e_logits = ((t @ w_router).astype(jnp.float32) + b_router.astype(jnp.float32)).reshape(N, N_GROUPS, EXPERTS_PER_GROUP)
    e_in = jnp.take_along_axis(e_logits, g_top[:, :, None], axis=1)[:, 0]
    e_top_v, e_top_i = lax.top_k(e_in, TOP_K)
    gate = g_top_p * jax.nn.softmax(e_top_v, axis=-1)
    expert = g_top * EXPERTS_PER_GROUP + e_top_i

    A = N * TOP_K
    e_flat = expert.reshape(A)
    tok_flat = jnp.repeat(jnp.arange(N, dtype=jnp.int32), TOP_K)
    w_flat = gate.reshape(A)
    order = jnp.argsort(e_flat)
    e_sorted = e_flat[order]
    counts = jnp.bincount(e_flat, length=N_EXPERTS)
    padded = (counts + MOE_BLOCK - 1) // MOE_BLOCK * MOE_BLOCK
    start = jnp.cumsum(counts) - counts
    pstart = jnp.cumsum(padded) - padded
    pend = pstart + padded
    dest = pstart[e_sorted] + (jnp.arange(A) - start[e_sorted])
    cap = A + N_EXPERTS * MOE_BLOCK
    n_blocks = cap // MOE_BLOCK
    buf_tok = jnp.zeros((cap,), jnp.int32).at[dest].set(tok_flat[order])
    buf_w = jnp.zeros((cap,), jnp.float32).at[dest].set(w_flat[order])
    blk_start = jnp.arange(n_blocks) * MOE_BLOCK
    blk_expert = jnp.minimum(jnp.sum(pend[None, :] <= blk_start[:, None], axis=1), N_EXPERTS - 1)
    xb = t[buf_tok].reshape(n_blocks, MOE_BLOCK, D)

    def run(args):
        xblk, e = args
        hid = jax.nn.silu(xblk @ w1[e]) * (xblk @ w3[e])
        return hid @ w2[e]

    yb = lax.map(run, (xb, blk_expert)).reshape(cap, D) * buf_w[:, None].astype(t.dtype)
    y = jnp.zeros((N, D), t.dtype).at[buf_tok].add(yb)
    return y.reshape(Bn, S, D)


def setup_inputs(seed: int = 0) -> dict:
    key = jax.random.key(seed)
    ks = jax.random.split(key, 28)
    f32 = jnp.float32
    L, D = DEPTH, D_MODEL

    def nrm(k, shape, s):
        return jax.random.normal(k, shape, f32) * s

    dt = jnp.exp(jax.random.uniform(ks[15], (L, GDN_HEADS), f32, math.log(1e-3), math.log(1e-1)))
    return {
        'x': nrm(ks[0], (BATCH, SEQ, D), 1.0),
        'c': nrm(ks[1], (BATCH, D), 1.0),
        'w_ada': nrm(ks[2], (L, D, 6 * D), 0.5 * D ** -0.5),
        'b_ada': nrm(ks[3], (L, 6 * D), 0.02),
        'norm1_gain': 1.0 + nrm(ks[4], (L, D), 0.02),
        'w_in': nrm(ks[5], (L, D, D_IN), D ** -0.5),
        'da_q_norm': 1.0 + nrm(ks[6], (L, DA_QK_DIM), 0.02),
        'da_k_norm': 1.0 + nrm(ks[7], (L, DA_QK_DIM), 0.02),
        'da_lambda_q1': nrm(ks[8], (L, DA_QK_DIM), 0.1),
        'da_lambda_k1': nrm(ks[9], (L, DA_QK_DIM), 0.1),
        'da_lambda_q2': nrm(ks[10], (L, DA_QK_DIM), 0.1),
        'da_lambda_k2': nrm(ks[11], (L, DA_QK_DIM), 0.1),
        'da_out_norm': 1.0 + nrm(ks[12], (L, DA_V_DIM), 0.02),
        'gdn_conv': nrm(ks[13], (L, CONV_K, GDN_CONV_DIM), CONV_K ** -0.5),
        'gdn_a_log': jnp.log(jax.random.uniform(ks[14], (L, GDN_HEADS), f32, 1.0, 16.0)),
        'gdn_dt_bias': jnp.log(jnp.expm1(dt)),
        'gdn_out_norm': 1.0 + nrm(ks[16], (L, GDN_V_DIM), 0.02),
        'w_branch_a': nrm(ks[17], (L, DA_WIDTH, D), DA_WIDTH ** -0.5),
        'w_branch_b': nrm(ks[18], (L, GDN_WIDTH, D), GDN_WIDTH ** -0.5),
        'w_out': nrm(ks[19], (L, D, D), D ** -0.5),
        'norm2_gain': 1.0 + nrm(ks[20], (L, D), 0.02),
        'w_group': nrm(ks[21], (L, D, N_GROUPS), D ** -0.5),
        'b_group': nrm(ks[22], (L, N_GROUPS), 0.01),
        'w_router': nrm(ks[23], (L, D, N_EXPERTS), D ** -0.5),
        'b_router': nrm(ks[24], (L, N_EXPERTS), 0.01),
        'w1': nrm(ks[25], (L, N_EXPERTS, D, D_EXPERT), D ** -0.5),
        'w3': nrm(ks[26], (L, N_EXPERTS, D, D_EXPERT), D ** -0.5),
        'w2': nrm(ks[27], (L, N_EXPERTS, D_EXPERT, D), D_EXPERT ** -0.5),
    }


def reference(x, c, w_ada, b_ada, norm1_gain, w_in, da_q_norm, da_k_norm, da_lambda_q1, da_lambda_k1,
              da_lambda_q2, da_lambda_k2, da_out_norm, gdn_conv, gdn_a_log, gdn_dt_bias, gdn_out_norm,
              w_branch_a, w_branch_b, w_out, norm2_gain, w_group, b_group, w_router, b_router, w1, w3, w2):
    for layer in range(DEPTH):
        mod = (jax.nn.silu(c) @ w_ada[layer] + b_ada[layer])[:, None, :]
        shift1, scale1, gate1, shift2, scale2, gate2 = jnp.split(mod, 6, axis=-1)
        lambda_init = 0.8 - 0.6 * math.exp(-0.3 * layer)
        lam = (jnp.exp(jnp.sum(da_lambda_q1[layer].astype(jnp.float32) * da_lambda_k1[layer].astype(jnp.float32)))
               - jnp.exp(jnp.sum(da_lambda_q2[layer].astype(jnp.float32) * da_lambda_k2[layer].astype(jnp.float32)))
               + lambda_init)
        h = rms_norm(x, norm1_gain[layer]) * (1.0 + scale1) + shift1
        x = x + gate1 * hybrid_mixer(h, w_in[layer], da_q_norm[layer], da_k_norm[layer], lam, lambda_init,
                                     da_out_norm[layer], gdn_conv[layer], gdn_a_log[layer], gdn_dt_bias[layer],
                                     gdn_out_norm[layer], w_branch_a[layer], w_branch_b[layer], w_out[layer])
        h = rms_norm(x, norm2_gain[layer]) * (1.0 + scale2) + shift2
        x = x + gate2 * hier_moe(h, w_group[layer], b_group[layer], w_router[layer], b_router[layer],
                                 w1[layer], w3[layer], w2[layer])
    return x
```

```python
import functools
import math

import jax
import jax.numpy as jnp
from jax import lax
from jax.experimental import pallas as pl
from jax.experimental.pallas import tpu as pltpu

F32 = jnp.float32
BF16 = jnp.bfloat16
I32 = jnp.int32

EPS = 1e-6
NEG_INF = -1e30
MASK_CHUNK = 64

DA_HEADS = 4
DA_QK_DIM = 64
DA_V_DIM = 128
GDN_HEADS = 4
GDN_DIM = 128
CONV_K = 4
N_GROUPS = 4
EXPERTS_PER_GROUP = 8
N_EXPERTS = N_GROUPS * EXPERTS_PER_GROUP
D_EXPERT = 512

LANES = 128
VMEM_LIMIT = 56 * 1024 * 1024

GDN_CHUNK = 128
MOE_ROWS = 256


def _cparams(n_axes):
    return pltpu.CompilerParams(dimension_semantics=("arbitrary",) * n_axes,
                                vmem_limit_bytes=VMEM_LIMIT)


def _mm(a, b):
    return jnp.dot(a.astype(BF16), b.astype(BF16), preferred_element_type=F32)


def _mm_nt(a, b):
    return lax.dot_general(a.astype(BF16), b.astype(BF16), (((1,), (1,)), ((), ())),
                           preferred_element_type=F32)


def _split2(a):
    hi = a.astype(BF16)
    lo = (a - hi.astype(F32)).astype(BF16)
    return hi, lo


def _mm3(a, b):
    ah, al = _split2(a)
    bh, bl = _split2(b)
    out = jnp.dot(ah, bh, preferred_element_type=F32)
    out = out + jnp.dot(ah, bl, preferred_element_type=F32)
    out = out + jnp.dot(al, bh, preferred_element_type=F32)
    return out


def _mm_exact_lhs(a_bf16, b):
    b1 = b.astype(BF16)
    r1 = b - b1.astype(F32)
    b2 = r1.astype(BF16)
    b3 = (r1 - b2.astype(F32)).astype(BF16)
    out = jnp.dot(a_bf16, b1, preferred_element_type=F32)
    out = out + jnp.dot(a_bf16, b2, preferred_element_type=F32)
    out = out + jnp.dot(a_bf16, b3, preferred_element_type=F32)
    return out


def _sigmoid(x):
    return 1.0 / (1.0 + jnp.exp(-x))


def _silu(x):
    return x * _sigmoid(x)


def _softplus(x):
    return jnp.maximum(x, 0.0) + jnp.log1p(jnp.exp(-jnp.abs(x)))


def _rms(x, gain):
    return x * lax.rsqrt(jnp.mean(x * x, axis=-1, keepdims=True) + EPS) * gain


def _ada_kernel(c_ref, w_ref, b_ref, o_ref):
    sc = _silu(c_ref[...])
    o_ref[...] = _mm3(sc, w_ref[...]) + b_ref[...]


def _ada(c_pad, w_ada, b_ada):
    rows, d = c_pad.shape
    n = w_ada.shape[1]
    tn = d
    return pl.pallas_call(
        _ada_kernel,
        grid=(n // tn,),
        in_specs=[pl.BlockSpec((rows, d), lambda j: (0, 0)),
                  pl.BlockSpec((d, tn), lambda j: (0, j)),
                  pl.BlockSpec((1, tn), lambda j: (0, j))],
        out_specs=pl.BlockSpec((rows, tn), lambda j: (0, j)),
        out_shape=jax.ShapeDtypeStruct((rows, n), F32),
        compiler_params=_cparams(1),
    )(c_pad, w_ada, b_ada.reshape(1, n))


def _group_rms64(x, gain):
    tm, width = x.shape
    lane = lax.broadcasted_iota(I32, (tm, LANES), 1)
    low = lane < DA_QK_DIM
    parts = []
    for j in range(width // LANES):
        blk = x[:, j * LANES:(j + 1) * LANES]
        sq = blk * blk
        s_lo = jnp.sum(jnp.where(low, sq, 0.0), axis=-1, keepdims=True)
        s_hi = jnp.sum(jnp.where(low, 0.0, sq), axis=-1, keepdims=True)
        ms = jnp.where(low, s_lo, s_hi) * (1.0 / DA_QK_DIM)
        parts.append(blk * lax.rsqrt(ms + EPS))
    return jnp.concatenate(parts, axis=-1) * gain


def _proj_kernel(x_ref, mod_ref, g1_ref, wda_ref, wg_ref, wba_ref, qn_ref, kn_ref,
                 q_ref, k_ref, v_ref, gqkv_ref, z_ref, ba_ref):
    x = x_ref[...]
    mod = mod_ref[0]
    shift, scale = mod[0:1, :], mod[1:2, :]
    hb = (_rms(x, g1_ref[...]) * (1.0 + scale) + shift).astype(BF16)
    da_w = DA_HEADS * 2 * DA_QK_DIM
    da = jnp.dot(hb, wda_ref[...], preferred_element_type=F32)
    q = _group_rms64(da[:, :da_w], qn_ref[...]) * (DA_QK_DIM ** -0.5)
    k = _group_rms64(da[:, da_w:2 * da_w], kn_ref[...])
    q_ref[...] = q.astype(BF16)
    k_ref[...] = k.astype(BF16)
    v_ref[...] = da[:, 2 * da_w:].astype(BF16)
    g = jnp.dot(hb, wg_ref[...], preferred_element_type=F32)
    conv_w = 3 * GDN_HEADS * GDN_DIM
    gqkv_ref[...] = g[:, :conv_w]
    z_ref[...] = g[:, conv_w:]
    ba_ref[...] = jnp.dot(hb, wba_ref[...], preferred_element_type=F32)


def _proj(x2, mod3, g1, w_da, w_g, w_ba, qn, kn, tm, tiles_per_batch):
    n, d = x2.shape
    da_w = DA_HEADS * 2 * DA_QK_DIM
    dv_w = DA_HEADS * DA_V_DIM
    conv_w = 3 * GDN_HEADS * GDN_DIM
    z_w = GDN_HEADS * GDN_DIM
    const = lambda i: (0, 0)
    row = lambda i: (i, 0)
    return pl.pallas_call(
        _proj_kernel,
        grid=(n // tm,),
        in_specs=[pl.BlockSpec((tm, d), row),
                  pl.BlockSpec((1, 6, d), lambda i: (i // tiles_per_batch, 0, 0)),
                  pl.BlockSpec((1, d), const),
                  pl.BlockSpec(w_da.shape, const),
                  pl.BlockSpec(w_g.shape, const),
                  pl.BlockSpec(w_ba.shape, const),
                  pl.BlockSpec((1, da_w), const),
                  pl.BlockSpec((1, da_w), const)],
        out_specs=[pl.BlockSpec((tm, da_w), row), pl.BlockSpec((tm, da_w), row),
                   pl.BlockSpec((tm, dv_w), row), pl.BlockSpec((tm, conv_w), row),
                   pl.BlockSpec((tm, z_w), row), pl.BlockSpec((tm, LANES), row)],
        out_shape=[jax.ShapeDtypeStruct((n, da_w), BF16), jax.ShapeDtypeStruct((n, da_w), BF16),
                   jax.ShapeDtypeStruct((n, dv_w), BF16), jax.ShapeDtypeStruct((n, conv_w), F32),
                   jax.ShapeDtypeStruct((n, z_w), F32), jax.ShapeDtypeStruct((n, LANES), F32)],
        compiler_params=_cparams(1),
    )(x2, mod3, g1, w_da, w_g, w_ba, qn, kn)


def _attn_kernel(it_ref, jt_ref, slopes_ref, q_ref, kt_ref, v_ref, lq1_ref, lk1_ref, lq2_ref, lk2_ref,
                 on_ref, o_ref, qs_ref, m_ref, acc_ref, doff_ref, ddiag_ref, *, t, lambda_init):
    h = pl.program_id(1)
    step = pl.program_id(2)
    i = it_ref[step]
    j = jt_ref[step]
    slope = slopes_ref[h]

    @pl.when(step == 0)
    def _tables():
        r = lax.broadcasted_iota(I32, (t, t), 0)
        c = lax.broadcasted_iota(I32, (t, t), 1)
        d = (r - c).astype(F32)
        doff_ref[...] = -slope * d
        allowed = (c // MASK_CHUNK) <= (r // MASK_CHUNK)
        ddiag_ref[...] = jnp.where(allowed, -slope * jnp.abs(d), NEG_INF)

    @pl.when(j == 0)
    def _init():
        q = q_ref[...]
        lane = lax.broadcasted_iota(I32, q.shape, 1)
        zero = jnp.zeros_like(q)
        qs_ref[0:t, :] = jnp.where(lane < DA_QK_DIM, q, zero)
        qs_ref[t:2 * t, :] = jnp.where(lane < DA_QK_DIM, zero, q)
        m_ref[...] = jnp.full(m_ref.shape, NEG_INF, F32)
        acc_ref[...] = jnp.zeros(acc_ref.shape, F32)

    def update(bias_ref, shift):
        s = jnp.dot(qs_ref[...], kt_ref[0, 0], preferred_element_type=F32)
        bias = bias_ref[...]
        sb = jnp.concatenate([s[0:t] + bias, s[t:2 * t] + bias], axis=0)
        m_prev = m_ref[...]
        m_next = jnp.maximum(m_prev, jnp.max(sb, axis=-1, keepdims=True) + shift)
        alpha = jnp.exp(m_prev - m_next)
        p = jnp.exp(sb - (m_next - shift))
        v = v_ref[...]
        vext = jnp.concatenate([v, jnp.ones_like(v)], axis=1)
        acc_ref[...] = alpha * acc_ref[...] + jnp.dot(p.astype(BF16), vext, preferred_element_type=F32)
        m_ref[...] = m_next

    @pl.when(j < i)
    def _off_diagonal():
        update(doff_ref, -slope * ((i - j) * t).astype(F32))

    @pl.when(j == i)
    def _diagonal():
        update(ddiag_ref, jnp.float32(0.0))
        acc = acc_ref[...]
        o_all = acc[:, 0:DA_V_DIM] / acc[:, DA_V_DIM:2 * DA_V_DIM]
        lam = (jnp.exp(jnp.sum(lq1_ref[...] * lk1_ref[...], axis=-1, keepdims=True))
               - jnp.exp(jnp.sum(lq2_ref[...] * lk2_ref[...], axis=-1, keepdims=True)) + lambda_init)
        o = o_all[0:t] - lam * o_all[t:2 * t]
        o = _rms(o, on_ref[...]) * (1.0 - lambda_init)
        o_ref[...] = o.astype(o_ref.dtype)


def _attention(q, kt, v, lq1, lk1, lq2, lk2, out_norm, batch, seq, t, lambda_init):
    n = batch * seq
    nq = seq // t
    pairs = [(i, j) for i in range(nq) for j in range(i + 1)]
    i_tbl = jnp.asarray([p[0] for p in pairs], I32)
    j_tbl = jnp.asarray([p[1] for p in pairs], I32)
    slopes = jnp.asarray([2.0 ** (-8.0 * (hh + 1) / DA_HEADS) for hh in range(DA_HEADS)], F32)
    vec = lambda b, h, s, it, jt, sl: (0, 0)
    grid_spec = pltpu.PrefetchScalarGridSpec(
        num_scalar_prefetch=3,
        grid=(batch, DA_HEADS, len(pairs)),
        in_specs=[pl.BlockSpec((t, LANES), lambda b, h, s, it, jt, sl: (b * nq + it[s], h)),
                  pl.BlockSpec((1, 1, LANES, t), lambda b, h, s, it, jt, sl: (b, h, 0, jt[s])),
                  pl.BlockSpec((t, LANES), lambda b, h, s, it, jt, sl: (b * nq + jt[s], h)),
                  pl.BlockSpec((1, DA_QK_DIM), vec), pl.BlockSpec((1, DA_QK_DIM), vec),
                  pl.BlockSpec((1, DA_QK_DIM), vec), pl.BlockSpec((1, DA_QK_DIM), vec),
                  pl.BlockSpec((1, DA_V_DIM), vec)],
        out_specs=pl.BlockSpec((t, LANES), lambda b, h, s, it, jt, sl: (b * nq + it[s], h)),
        scratch_shapes=[pltpu.VMEM((2 * t, LANES), BF16),
                        pltpu.VMEM((2 * t, 1), F32),
                        pltpu.VMEM((2 * t, 2 * DA_V_DIM), F32),
                        pltpu.VMEM((t, t), F32),
                        pltpu.VMEM((t, t), F32)])
    return pl.pallas_call(
        functools.partial(_attn_kernel, t=t, lambda_init=lambda_init),
        grid_spec=grid_spec,
        out_shape=jax.ShapeDtypeStruct((n, DA_HEADS * DA_V_DIM), BF16),
        compiler_params=_cparams(3),
    )(i_tbl, j_tbl, slopes, q, kt, v, lq1, lk1, lq2, lk2, out_norm)


def _gdn_kernel(u_ref, z_ref, ba_ref, cw_ref, alog_ref, dt_ref, on_ref, o_ref, stage_ref, state_ref, *, batch):
    c = GDN_CHUNK
    hd = GDN_HEADS * GDN_DIM
    step = pl.program_id(0)

    @pl.when(step == 0)
    def _init():
        state_ref[...] = jnp.zeros(state_ref.shape, F32)
        stage_ref[:, 0:8, :] = jnp.zeros((batch, 8, stage_ref.shape[2]), F32)

    row = lax.broadcasted_iota(I32, (c, c), 0)
    col = lax.broadcasted_iota(I32, (c, c), 1)
    tri = row >= col
    strict = row > col
    tril_ones = jnp.where(tri, 1.0, 0.0).astype(BF16)
    eye = jnp.where(row == col, 1.0, 0.0).astype(F32)

    for b in range(batch):
        stage_ref[b, 8:8 + c, :] = u_ref[b]
        y = cw_ref[0:1, :] * stage_ref[b, 5:5 + c, :]
        for tap in range(1, CONV_K):
            y = y + cw_ref[tap:tap + 1, :] * stage_ref[b, 5 + tap:5 + tap + c, :]
        stage_ref[b, 0:8, :] = stage_ref[b, c:c + 8, :]
        y = _silu(y)

        ba = ba_ref[b]
        beta_all = _sigmoid(ba)
        g_all = -jnp.exp(alog_ref[...]) * _softplus(ba + dt_ref[...])
        gcum_all = _mm_exact_lhs(tril_ones, g_all)
        gcum_t = gcum_all.T

        for h in range(GDN_HEADS):
            q = y[:, h * GDN_DIM:(h + 1) * GDN_DIM]
            k = y[:, hd + h * GDN_DIM:hd + (h + 1) * GDN_DIM]
            v = y[:, 2 * hd + h * GDN_DIM:2 * hd + (h + 1) * GDN_DIM]
            q = q * lax.rsqrt(jnp.sum(q * q, axis=-1, keepdims=True) + EPS) * (GDN_DIM ** -0.5)
            k = k * lax.rsqrt(jnp.sum(k * k, axis=-1, keepdims=True) + EPS)
            beta = beta_all[:, h:h + 1]
            gc = gcum_all[:, GDN_HEADS + h:GDN_HEADS + h + 1]
            gr = gcum_t[GDN_HEADS + h:GDN_HEADS + h + 1, :]
            g_last = gc[c - 1:c, :]
            decay = jnp.where(tri, jnp.exp(jnp.where(tri, gc - gr, 0.0)), 0.0)
            kk = _mm_nt(k, k)
            p = jnp.where(strict, -(beta * kk * decay), 0.0)
            tinv = eye + p
            pk = p
            for _ in range(int(math.log2(c)) - 1):
                pk = _mm3(pk, pk)
                tinv = tinv + _mm3(tinv, pk)
            e_gc = jnp.exp(gc)
            rhs = jnp.concatenate([v * beta, k * (beta * e_gc)], axis=1)
            sol = _mm3(tinv, rhs)
            u_c, w_c = sol[:, :GDN_DIM], sol[:, GDN_DIM:]
            qk = _mm_nt(q, k) * decay
            q_dec = q * e_gc
            k_dec = k * jnp.exp(g_last - gc)
            s_prev = state_ref[b * GDN_HEADS + h]
            v_new = u_c - _mm(w_c, s_prev)
            o = _mm(q_dec, s_prev) + _mm(qk, v_new)
            state_ref[b * GDN_HEADS + h] = s_prev * jnp.exp(g_last) + _mm(k_dec.T, v_new)
            zg = z_ref[b, :, h * GDN_DIM:(h + 1) * GDN_DIM]
            o = _rms(o, on_ref[...]) * _silu(zg)
            o_ref[b, :, h * GDN_DIM:(h + 1) * GDN_DIM] = o.astype(o_ref.dtype)


def _gdn(gqkv3, z3, ba3, conv_w, alog_row, dt_row, out_norm):
    batch, seq, cw = gqkv3.shape
    c = GDN_CHUNK
    hd = GDN_HEADS * GDN_DIM
    blk = lambda s: (0, s, 0)
    const = lambda s: (0, 0)
    return pl.pallas_call(
        functools.partial(_gdn_kernel, batch=batch),
        grid=(seq // c,),
        in_specs=[pl.BlockSpec((batch, c, cw), blk),
                  pl.BlockSpec((batch, c, hd), blk),
                  pl.BlockSpec((batch, c, LANES), blk),
                  pl.BlockSpec((CONV_K, cw), const),
                  pl.BlockSpec((1, LANES), const),
                  pl.BlockSpec((1, LANES), const),
                  pl.BlockSpec((1, GDN_DIM), const)],
        out_specs=pl.BlockSpec((batch, c, hd), blk),
        out_shape=jax.ShapeDtypeStruct((batch, seq, hd), BF16),
        scratch_shapes=[pltpu.VMEM((batch, c + 8, cw), F32),
                        pltpu.VMEM((batch * GDN_HEADS, GDN_DIM, GDN_DIM), F32)],
        compiler_params=_cparams(1),
    )(gqkv3, z3, ba3, conv_w, alog_row, dt_row, out_norm)


def _merge_kernel(x_ref, mod_ref, g1_ref, g2_ref, oa_ref, ob_ref, wgate_ref, wa_ref, wb_ref, wout_ref,
                  wrt_ref, brt_ref, x1_ref, h2_ref, lg_ref):
    x = x_ref[...]
    d = x.shape[1]
    mod = mod_ref[0]
    shift1, scale1, gate1 = mod[0:1, :], mod[1:2, :], mod[2:3, :]
    shift2, scale2 = mod[3:4, :], mod[4:5, :]
    hb = (_rms(x, g1_ref[...]) * (1.0 + scale1) + shift1).astype(BF16)
    gates = _sigmoid(jnp.dot(hb, wgate_ref[...], preferred_element_type=F32))
    ya = jnp.dot(oa_ref[...], wa_ref[...], preferred_element_type=F32)
    yb = jnp.dot(ob_ref[...], wb_ref[...], preferred_element_type=F32)
    merged = gates[:, :d] * ya + gates[:, d:] * yb
    x1 = x + gate1 * _mm(merged, wout_ref[...])
    x1_ref[...] = x1
    h2 = _rms(x1, g2_ref[...]) * (1.0 + scale2) + shift2
    h2_ref[...] = h2.astype(BF16)
    lg_ref[...] = _mm3(h2, wrt_ref[...]) + brt_ref[...]


def _merge(x2, mod3, g1, g2, oa, ob, w_gate, w_a, w_b, w_out, w_rt, b_rt, tm, tiles_per_batch):
    n, d = x2.shape
    const = lambda i: (0, 0)
    row = lambda i: (i, 0)
    return pl.pallas_call(
        _merge_kernel,
        grid=(n // tm,),
        in_specs=[pl.BlockSpec((tm, d), row),
                  pl.BlockSpec((1, 6, d), lambda i: (i // tiles_per_batch, 0, 0)),
                  pl.BlockSpec((1, d), const), pl.BlockSpec((1, d), const),
                  pl.BlockSpec((tm, oa.shape[1]), row), pl.BlockSpec((tm, ob.shape[1]), row),
                  pl.BlockSpec(w_gate.shape, const), pl.BlockSpec(w_a.shape, const),
                  pl.BlockSpec(w_b.shape, const), pl.BlockSpec(w_out.shape, const),
                  pl.BlockSpec(w_rt.shape, const), pl.BlockSpec((1, LANES), const)],
        out_specs=[pl.BlockSpec((tm, d), row), pl.BlockSpec((tm, d), row), pl.BlockSpec((tm, LANES), row)],
        out_shape=[jax.ShapeDtypeStruct((n, d), F32), jax.ShapeDtypeStruct((n, d), BF16),
                   jax.ShapeDtypeStruct((n, LANES), F32)],
        compiler_params=_cparams(1),
    )(x2, mod3, g1, g2, oa, ob, w_gate, w_a, w_b, w_out, w_rt, b_rt)


def _route_kernel(lg_ref, ri_ref, rf_ref, cnt_ref, carry_ref):
    step = pl.program_id(0)
    tm = lg_ref.shape[0]

    @pl.when(step == 0)
    def _init():
        carry_ref[...] = jnp.zeros(carry_ref.shape, F32)

    lg = lg_ref[...]
    lane = lax.broadcasted_iota(I32, lg.shape, 1)
    big = jnp.int32(LANES)
    is_grp = (lane >= N_EXPERTS) & (lane < N_EXPERTS + N_GROUPS)
    gl = jnp.where(is_grp, lg, NEG_INF)
    gmax = jnp.max(gl, axis=-1, keepdims=True)
    g_top = jnp.min(jnp.where(is_grp & (gl == gmax), lane, big), axis=-1, keepdims=True) - N_EXPERTS
    g_top_p = 1.0 / jnp.sum(jnp.where(is_grp, jnp.exp(gl - gmax), 0.0), axis=-1, keepdims=True)
    in_grp = (lane < N_EXPERTS) & ((lane // EXPERTS_PER_GROUP) == g_top)
    el = jnp.where(in_grp, lg, NEG_INF)
    v1 = jnp.max(el, axis=-1, keepdims=True)
    e1 = jnp.min(jnp.where(in_grp & (el == v1), lane, big), axis=-1, keepdims=True)
    rest = in_grp & (lane != e1)
    el2 = jnp.where(rest, lg, NEG_INF)
    v2 = jnp.max(el2, axis=-1, keepdims=True)
    e2 = jnp.min(jnp.where(rest & (el2 == v2), lane, big), axis=-1, keepdims=True)
    ex = jnp.exp(v2 - v1)
    w1 = g_top_p / (1.0 + ex)
    w2 = g_top_p * ex / (1.0 + ex)

    oh1 = lane == e1
    oh2 = lane == e2
    ohs = (jnp.where(oh1, 1.0, 0.0) + jnp.where(oh2, 1.0, 0.0)).astype(BF16)
    r = lax.broadcasted_iota(I32, (tm, tm), 0)
    c = lax.broadcasted_iota(I32, (tm, tm), 1)
    before = jnp.where(r > c, 1.0, 0.0).astype(BF16)
    prior = jnp.dot(before, ohs, preferred_element_type=F32) + carry_ref[0:1, :]
    rank1 = jnp.sum(jnp.where(oh1, prior, 0.0), axis=-1, keepdims=True)
    rank2 = jnp.sum(jnp.where(oh2, prior, 0.0), axis=-1, keepdims=True)
    carry_ref[...] = carry_ref[...] + jnp.sum(ohs.astype(F32), axis=0, keepdims=True)
    cnt_ref[...] = carry_ref[...]

    r1 = rank1.astype(I32)
    r2 = rank2.astype(I32)
    ri = jnp.where(lane == 0, e1, jnp.where(lane == 1, e2, jnp.where(lane == 2, r1, jnp.where(lane == 3, r2, 0))))
    ri_ref[...] = ri
    rf_ref[...] = jnp.where(lane == 0, w1, jnp.where(lane == 1, w2, 0.0))


def _route(logits, tm):
    n = logits.shape[0]
    row = lambda i: (i, 0)
    return pl.pallas_call(
        _route_kernel,
        grid=(n // tm,),
        in_specs=[pl.BlockSpec((tm, LANES), row)],
        out_specs=[pl.BlockSpec((tm, LANES), row), pl.BlockSpec((tm, LANES), row),
                   pl.BlockSpec((8, LANES), lambda i: (0, 0))],
        out_shape=[jax.ShapeDtypeStruct((n, LANES), I32), jax.ShapeDtypeStruct((n, LANES), F32),
                   jax.ShapeDtypeStruct((8, LANES), F32)],
        scratch_shapes=[pltpu.VMEM((8, LANES), F32)],
        compiler_params=_cparams(1),
    )(logits)


def _moe_kernel(be_ref, act_ref, xb_ref, w1_ref, w3_ref, w2_ref, o_ref, w1b_ref, w3b_ref, w2b_ref):
    i = pl.program_id(0)
    prev = be_ref[jnp.maximum(i - 1, 0)]

    @pl.when((i == 0) | (be_ref[i] != prev))
    def _cast_weights():
        w1b_ref[...] = w1_ref[...].astype(BF16)
        w3b_ref[...] = w3_ref[...].astype(BF16)
        w2b_ref[...] = w2_ref[...].astype(BF16)

    @pl.when(act_ref[i] == 1)
    def _compute():
        xb = xb_ref[...]
        a = jnp.dot(xb, w1b_ref[...], preferred_element_type=F32)
        g = jnp.dot(xb, w3b_ref[...], preferred_element_type=F32)
        hid = (_silu(a) * g).astype(BF16)
        o_ref[...] = jnp.dot(hid, w2b_ref[...], preferred_element_type=F32)

    @pl.when(act_ref[i] == 0)
    def _idle():
        o_ref[...] = jnp.zeros(o_ref.shape, o_ref.dtype)


def _moe(blk_expert, blk_active, xb, w1, w3, w2):
    cap, d = xb.shape
    nb = cap // MOE_ROWS
    de = w1.shape[2]
    grid_spec = pltpu.PrefetchScalarGridSpec(
        num_scalar_prefetch=2,
        grid=(nb,),
        in_specs=[pl.BlockSpec((MOE_ROWS, d), lambda i, be, act: (i, 0)),
                  pl.BlockSpec((None, d, de), lambda i, be, act: (be[i], 0, 0)),
                  pl.BlockSpec((None, d, de), lambda i, be, act: (be[i], 0, 0)),
                  pl.BlockSpec((None, de, d), lambda i, be, act: (be[i], 0, 0))],
        out_specs=pl.BlockSpec((MOE_ROWS, d), lambda i, be, act: (i, 0)),
        scratch_shapes=[pltpu.VMEM((d, de), BF16), pltpu.VMEM((d, de), BF16), pltpu.VMEM((de, d), BF16)])
    return pl.pallas_call(
        _moe_kernel,
        grid_spec=grid_spec,
        out_shape=jax.ShapeDtypeStruct((cap, d), F32),
        compiler_params=_cparams(1),
    )(blk_expert, blk_active, xb, w1, w3, w2)


def _combine_kernel(x1_ref, mod_ref, rf_ref, ya_ref, yb_ref, o_ref):
    gate2 = mod_ref[0][5:6, :]
    rf = rf_ref[...]
    y = rf[:, 0:1] * ya_ref[...] + rf[:, 1:2] * yb_ref[...]
    o_ref[...] = x1_ref[...] + gate2 * y


def _combine(x1, mod3, rf, ya, yb, tm, tiles_per_batch):
    n, d = x1.shape
    row = lambda i: (i, 0)
    return pl.pallas_call(
        _combine_kernel,
        grid=(n // tm,),
        in_specs=[pl.BlockSpec((tm, d), row),
                  pl.BlockSpec((1, 6, d), lambda i: (i // tiles_per_batch, 0, 0)),
                  pl.BlockSpec((tm, LANES), row),
                  pl.BlockSpec((tm, d), row), pl.BlockSpec((tm, d), row)],
        out_specs=pl.BlockSpec((tm, d), row),
        out_shape=jax.ShapeDtypeStruct((n, d), F32),
        compiler_params=_cparams(1),
    )(x1, mod3, rf, ya, yb)


def _pad_lanes(a, offset=0):
    return jnp.pad(a, ((0, 0), (offset, LANES - offset - a.shape[1])))


def _layer(x, c, layer, w_ada, b_ada, norm1_gain, w_in, da_q_norm, da_k_norm, lq1, lk1, lq2, lk2, da_out_norm,
           gdn_conv, gdn_a_log, gdn_dt_bias, gdn_out_norm, w_branch_a, w_branch_b, w_out, norm2_gain,
           w_group, b_group, w_router, b_router, w1, w3, w2):
    batch, seq, d = x.shape
    n = batch * seq
    tm = min(512, seq)
    tiles_per_batch = seq // tm
    t_attn = min(512, seq)
    lambda_init = 0.8 - 0.6 * math.exp(-0.3 * layer)

    mod = _ada(jnp.pad(c, ((0, 8 - batch), (0, 0))), w_ada, b_ada)[:batch]
    mod3 = mod.reshape(batch, 6, d)

    da_w = DA_HEADS * 2 * DA_QK_DIM
    dv_w = DA_HEADS * DA_V_DIM
    gd_w = GDN_HEADS * GDN_DIM
    o0 = 2 * da_w + dv_w
    o1 = o0 + 4 * gd_w
    o2 = o1 + 2 * GDN_HEADS
    w_da = w_in[:, :o0].astype(BF16)
    w_g = w_in[:, o0:o1].astype(BF16)
    w_ba = _pad_lanes(w_in[:, o1:o2]).astype(BF16)
    w_gate = w_in[:, o2:].astype(BF16)
    qn = jnp.tile(da_q_norm, 2 * DA_HEADS).reshape(1, da_w)
    kn = jnp.tile(da_k_norm, 2 * DA_HEADS).reshape(1, da_w)

    x2 = x.reshape(n, d)
    g1 = norm1_gain.reshape(1, d)
    q, k, v, gqkv, z, ba = _proj(x2, mod3, g1, w_da, w_g, w_ba, qn, kn, tm, tiles_per_batch)

    kt = k.reshape(batch, seq, DA_HEADS, 2 * DA_QK_DIM).transpose(0, 2, 3, 1)
    o_a = _attention(q, kt, v, lq1.reshape(1, -1), lk1.reshape(1, -1), lq2.reshape(1, -1), lk2.reshape(1, -1),
                     da_out_norm.reshape(1, -1), batch, seq, t_attn, lambda_init)

    alog_row = _pad_lanes(gdn_a_log.reshape(1, -1), GDN_HEADS)
    dt_row = _pad_lanes(gdn_dt_bias.reshape(1, -1), GDN_HEADS)
    o_b = _gdn(gqkv.reshape(batch, seq, -1), z.reshape(batch, seq, -1), ba.reshape(batch, seq, -1),
               gdn_conv, alog_row, dt_row, gdn_out_norm.reshape(1, -1)).reshape(n, gd_w)

    w_rt = _pad_lanes(jnp.concatenate([w_router, w_group], axis=1))
    b_rt = _pad_lanes(jnp.concatenate([b_router, b_group]).reshape(1, -1))
    x1, h2, logits = _merge(x2, mod3, g1, norm2_gain.reshape(1, d), o_a, o_b, w_gate,
                            w_branch_a.astype(BF16), w_branch_b.astype(BF16), w_out.astype(BF16),
                            w_rt, b_rt, tm, tiles_per_batch)

    ri, rf, cnt = _route(logits, tm)

    counts = cnt[0, :N_EXPERTS].astype(I32)
    padded = (counts + MOE_ROWS - 1) // MOE_ROWS * MOE_ROWS
    pend = jnp.cumsum(padded)
    pstart = pend - padded
    cap = 2 * n + N_EXPERTS * MOE_ROWS
    nb = cap // MOE_ROWS
    blk_start = jnp.arange(nb, dtype=I32) * MOE_ROWS
    blk_expert = jnp.minimum(jnp.sum(pend[None, :] <= blk_start[:, None], axis=1), N_EXPERTS - 1).astype(I32)
    blk_active = (blk_start < pend[-1]).astype(I32)
    dest = pstart[ri[:, 0:2]] + ri[:, 2:4]
    tok = jnp.broadcast_to(jnp.arange(n, dtype=I32)[:, None], (n, 2))
    buf_tok = jnp.zeros((cap,), I32).at[dest.reshape(-1)].set(tok.reshape(-1))

    xb = jnp.take(h2, buf_tok, axis=0)
    yb = _moe(blk_expert, blk_active, xb, w1, w3, w2)
    y1 = jnp.take(yb, dest[:, 0], axis=0)
    y2 = jnp.take(yb, dest[:, 1], axis=0)
    out = _combine(x1, mod3, rf, y1, y2, tm, tiles_per_batch)
    return out.reshape(batch, seq, d)


def kernel(x, c, w_ada, b_ada, norm1_gain, w_in, da_q_norm, da_k_norm, da_lambda_q1, da_lambda_k1, da_lambda_q2,
           da_lambda_k2, da_out_norm, gdn_conv, gdn_a_log, gdn_dt_bias, gdn_out_norm, w_branch_a, w_branch_b,
           w_out, norm2_gain, w_group, b_group, w_router, b_router, w1, w3, w2):
    for layer in range(w_ada.shape[0]):
        x = _layer(x, c, layer, w_ada[layer], b_ada[layer], norm1_gain[layer], w_in[layer], da_q_norm[layer],
                   da_k_norm[layer], da_lambda_q1[layer], da_lambda_k1[layer], da_lambda_q2[layer],
                   da_lambda_k2[layer], da_out_norm[layer], gdn_conv[layer], gdn_a_log[layer], gdn_dt_bias[layer],
                   gdn_out_norm[layer], w_branch_a[layer], w_branch_b[layer], w_out[layer], norm2_gain[layer],
                   w_group[layer], b_group[layer], w_router[layer], b_router[layer], w1[layer], w3[layer], w2[layer])
    return x
```

```python
import functools
import math

import jax
import jax.numpy as jnp
from jax import lax
from jax.experimental import pallas as pl
from jax.experimental.pallas import tpu as pltpu
from jax.experimental.pallas import tpu_sc as plsc

F32 = jnp.float32
BF16 = jnp.bfloat16
I32 = jnp.int32

EPS = 1e-6
NEG_INF = -1e30
MASK_CHUNK = 64

DA_HEADS = 4
DA_QK_DIM = 64
DA_V_DIM = 128
GDN_HEADS = 4
GDN_DIM = 128
CONV_K = 4
N_GROUPS = 4
EXPERTS_PER_GROUP = 8
N_EXPERTS = N_GROUPS * EXPERTS_PER_GROUP
D_EXPERT = 512

LANES = 128
VMEM_LIMIT = 56 * 1024 * 1024

GDN_CHUNK = 128
MOE_ROWS = 256
SC_WINDOW = 128


def _cparams(n_axes):
    return pltpu.CompilerParams(dimension_semantics=("arbitrary",) * n_axes,
                                vmem_limit_bytes=VMEM_LIMIT)


def _mm(a, b):
    return jnp.dot(a.astype(BF16), b.astype(BF16), preferred_element_type=F32)


def _mm_nt(a, b):
    return lax.dot_general(a.astype(BF16), b.astype(BF16), (((1,), (1,)), ((), ())),
                           preferred_element_type=F32)


def _split2(a):
    hi = a.astype(BF16)
    lo = (a - hi.astype(F32)).astype(BF16)
    return hi, lo


def _mm3(a, b):
    ah, al = _split2(a)
    bh, bl = _split2(b)
    out = jnp.dot(ah, bh, preferred_element_type=F32)
    out = out + jnp.dot(ah, bl, preferred_element_type=F32)
    out = out + jnp.dot(al, bh, preferred_element_type=F32)
    return out


def _mm_exact_lhs(a_bf16, b):
    b1 = b.astype(BF16)
    r1 = b - b1.astype(F32)
    b2 = r1.astype(BF16)
    b3 = (r1 - b2.astype(F32)).astype(BF16)
    out = jnp.dot(a_bf16, b1, preferred_element_type=F32)
    out = out + jnp.dot(a_bf16, b2, preferred_element_type=F32)
    out = out + jnp.dot(a_bf16, b3, preferred_element_type=F32)
    return out


def _sigmoid(x):
    return 1.0 / (1.0 + jnp.exp(-x))


def _silu(x):
    return x * _sigmoid(x)


def _softplus(x):
    return jnp.maximum(x, 0.0) + jnp.log1p(jnp.exp(-jnp.abs(x)))


def _pack_bf16_pairs(x):
    w = x.shape[1] // 2
    bits = lax.bitcast_convert_type(x.astype(BF16).astype(F32), jnp.uint32)
    lo = lax.shift_right_logical(bits[:, :w], jnp.uint32(16))
    hi = bits[:, w:] & jnp.uint32(0xFFFF0000)
    return lax.bitcast_convert_type(hi | lo, I32)


def _unpack_bf16_pairs(p):
    bits = lax.bitcast_convert_type(p, jnp.uint32)
    lo = lax.bitcast_convert_type(lax.shift_left(bits, jnp.uint32(16)), F32)
    hi = lax.bitcast_convert_type(bits & jnp.uint32(0xFFFF0000), F32)
    return jnp.concatenate([lo, hi], axis=1).astype(BF16)


def _rms(x, gain):
    return x * lax.rsqrt(jnp.mean(x * x, axis=-1, keepdims=True) + EPS) * gain


def _ada_kernel(c_ref, w_ref, b_ref, o_ref):
    sc = _silu(c_ref[...])
    o_ref[...] = _mm3(sc, w_ref[...]) + b_ref[...]


def _ada(c_pad, w_ada, b_ada):
    rows, d = c_pad.shape
    n = w_ada.shape[1]
    tn = d
    return pl.pallas_call(
        _ada_kernel,
        grid=(n // tn,),
        in_specs=[pl.BlockSpec((rows, d), lambda j: (0, 0)),
                  pl.BlockSpec((d, tn), lambda j: (0, j)),
                  pl.BlockSpec((1, tn), lambda j: (0, j))],
        out_specs=pl.BlockSpec((rows, tn), lambda j: (0, j)),
        out_shape=jax.ShapeDtypeStruct((rows, n), F32),
        compiler_params=_cparams(1),
    )(c_pad, w_ada, b_ada.reshape(1, n))


def _group_rms64(x, gain):
    tm, width = x.shape
    lane = lax.broadcasted_iota(I32, (tm, LANES), 1)
    low = lane < DA_QK_DIM
    parts = []
    for j in range(width // LANES):
        blk = x[:, j * LANES:(j + 1) * LANES]
        sq = blk * blk
        s_lo = jnp.sum(jnp.where(low, sq, 0.0), axis=-1, keepdims=True)
        s_hi = jnp.sum(jnp.where(low, 0.0, sq), axis=-1, keepdims=True)
        ms = jnp.where(low, s_lo, s_hi) * (1.0 / DA_QK_DIM)
        parts.append(blk * lax.rsqrt(ms + EPS))
    return jnp.concatenate(parts, axis=-1) * gain


def _proj_kernel(x_ref, mod_ref, g1_ref, wda_ref, wg_ref, wba_ref, qn_ref, kn_ref,
                 q_ref, k_ref, v_ref, gqkv_ref, z_ref, ba_ref):
    x = x_ref[...]
    mod = mod_ref[0]
    shift, scale = mod[0:1, :], mod[1:2, :]
    hb = (_rms(x, g1_ref[...]) * (1.0 + scale) + shift).astype(BF16)
    da_w = DA_HEADS * 2 * DA_QK_DIM
    da = jnp.dot(hb, wda_ref[...], preferred_element_type=F32)
    q = _group_rms64(da[:, :da_w], qn_ref[...]) * (DA_QK_DIM ** -0.5)
    k = _group_rms64(da[:, da_w:2 * da_w], kn_ref[...])
    q_ref[...] = q.astype(BF16)
    k_ref[...] = k.astype(BF16)
    v_ref[...] = da[:, 2 * da_w:].astype(BF16)
    g = jnp.dot(hb, wg_ref[...], preferred_element_type=F32)
    conv_w = 3 * GDN_HEADS * GDN_DIM
    gqkv_ref[...] = g[:, :conv_w]
    z_ref[...] = g[:, conv_w:]
    ba_ref[...] = jnp.dot(hb, wba_ref[...], preferred_element_type=F32)


def _proj(x2, mod3, g1, w_da, w_g, w_ba, qn, kn, tm, tiles_per_batch):
    n, d = x2.shape
    da_w = DA_HEADS * 2 * DA_QK_DIM
    dv_w = DA_HEADS * DA_V_DIM
    conv_w = 3 * GDN_HEADS * GDN_DIM
    z_w = GDN_HEADS * GDN_DIM
    const = lambda i: (0, 0)
    row = lambda i: (i, 0)
    return pl.pallas_call(
        _proj_kernel,
        grid=(n // tm,),
        in_specs=[pl.BlockSpec((tm, d), row),
                  pl.BlockSpec((1, 6, d), lambda i: (i // tiles_per_batch, 0, 0)),
                  pl.BlockSpec((1, d), const),
                  pl.BlockSpec(w_da.shape, const),
                  pl.BlockSpec(w_g.shape, const),
                  pl.BlockSpec(w_ba.shape, const),
                  pl.BlockSpec((1, da_w), const),
                  pl.BlockSpec((1, da_w), const)],
        out_specs=[pl.BlockSpec((tm, da_w), row), pl.BlockSpec((tm, da_w), row),
                   pl.BlockSpec((tm, dv_w), row), pl.BlockSpec((tm, conv_w), row),
                   pl.BlockSpec((tm, z_w), row), pl.BlockSpec((tm, LANES), row)],
        out_shape=[jax.ShapeDtypeStruct((n, da_w), BF16), jax.ShapeDtypeStruct((n, da_w), BF16),
                   jax.ShapeDtypeStruct((n, dv_w), BF16), jax.ShapeDtypeStruct((n, conv_w), F32),
                   jax.ShapeDtypeStruct((n, z_w), F32), jax.ShapeDtypeStruct((n, LANES), F32)],
        compiler_params=_cparams(1),
    )(x2, mod3, g1, w_da, w_g, w_ba, qn, kn)


def _attn_kernel(it_ref, jt_ref, slopes_ref, q_ref, kt_ref, v_ref, lq1_ref, lk1_ref, lq2_ref, lk2_ref,
                 on_ref, o_ref, qs_ref, m_ref, acc_ref, doff_ref, ddiag_ref, *, t, lambda_init):
    h = pl.program_id(1)
    step = pl.program_id(2)
    i = it_ref[step]
    j = jt_ref[step]
    slope = slopes_ref[h]

    @pl.when(step == 0)
    def _tables():
        r = lax.broadcasted_iota(I32, (t, t), 0)
        c = lax.broadcasted_iota(I32, (t, t), 1)
        d = (r - c).astype(F32)
        doff_ref[...] = -slope * d
        allowed = (c // MASK_CHUNK) <= (r // MASK_CHUNK)
        ddiag_ref[...] = jnp.where(allowed, -slope * jnp.abs(d), NEG_INF)

    @pl.when(j == 0)
    def _init():
        q = q_ref[...]
        lane = lax.broadcasted_iota(I32, q.shape, 1)
        zero = jnp.zeros_like(q)
        qs_ref[0:t, :] = jnp.where(lane < DA_QK_DIM, q, zero)
        qs_ref[t:2 * t, :] = jnp.where(lane < DA_QK_DIM, zero, q)
        m_ref[...] = jnp.full(m_ref.shape, NEG_INF, F32)
        acc_ref[...] = jnp.zeros(acc_ref.shape, F32)

    def update(bias_ref, shift):
        s = jnp.dot(qs_ref[...], kt_ref[0, 0], preferred_element_type=F32)
        bias = bias_ref[...]
        sb = jnp.concatenate([s[0:t] + bias, s[t:2 * t] + bias], axis=0)
        m_prev = m_ref[...]
        m_next = jnp.maximum(m_prev, jnp.max(sb, axis=-1, keepdims=True) + shift)
        alpha = jnp.exp(m_prev - m_next)
        p = jnp.exp(sb - (m_next - shift))
        v = v_ref[...]
        vext = jnp.concatenate([v, jnp.ones_like(v)], axis=1)
        acc_ref[...] = alpha * acc_ref[...] + jnp.dot(p.astype(BF16), vext, preferred_element_type=F32)
        m_ref[...] = m_next

    @pl.when(j < i)
    def _off_diagonal():
        update(doff_ref, -slope * ((i - j) * t).astype(F32))

    @pl.when(j == i)
    def _diagonal():
        update(ddiag_ref, jnp.float32(0.0))
        acc = acc_ref[...]
        o_all = acc[:, 0:DA_V_DIM] / acc[:, DA_V_DIM:2 * DA_V_DIM]
        lam = (jnp.exp(jnp.sum(lq1_ref[...] * lk1_ref[...], axis=-1, keepdims=True))
               - jnp.exp(jnp.sum(lq2_ref[...] * lk2_ref[...], axis=-1, keepdims=True)) + lambda_init)
        o = o_all[0:t] - lam * o_all[t:2 * t]
        o = _rms(o, on_ref[...]) * (1.0 - lambda_init)
        o_ref[...] = o.astype(o_ref.dtype)


def _attention(q, kt, v, lq1, lk1, lq2, lk2, out_norm, batch, seq, t, lambda_init):
    n = batch * seq
    nq = seq // t
    pairs = [(i, j) for i in range(nq) for j in range(i + 1)]
    i_tbl = jnp.asarray([p[0] for p in pairs], I32)
    j_tbl = jnp.asarray([p[1] for p in pairs], I32)
    slopes = jnp.asarray([2.0 ** (-8.0 * (hh + 1) / DA_HEADS) for hh in range(DA_HEADS)], F32)
    vec = lambda b, h, s, it, jt, sl: (0, 0)
    grid_spec = pltpu.PrefetchScalarGridSpec(
        num_scalar_prefetch=3,
        grid=(batch, DA_HEADS, len(pairs)),
        in_specs=[pl.BlockSpec((t, LANES), lambda b, h, s, it, jt, sl: (b * nq + it[s], h)),
                  pl.BlockSpec((1, 1, LANES, t), lambda b, h, s, it, jt, sl: (b, h, 0, jt[s])),
                  pl.BlockSpec((t, LANES), lambda b, h, s, it, jt, sl: (b * nq + jt[s], h)),
                  pl.BlockSpec((1, DA_QK_DIM), vec), pl.BlockSpec((1, DA_QK_DIM), vec),
                  pl.BlockSpec((1, DA_QK_DIM), vec), pl.BlockSpec((1, DA_QK_DIM), vec),
                  pl.BlockSpec((1, DA_V_DIM), vec)],
        out_specs=pl.BlockSpec((t, LANES), lambda b, h, s, it, jt, sl: (b * nq + it[s], h)),
        scratch_shapes=[pltpu.VMEM((2 * t, LANES), BF16),
                        pltpu.VMEM((2 * t, 1), F32),
                        pltpu.VMEM((2 * t, 2 * DA_V_DIM), F32),
                        pltpu.VMEM((t, t), F32),
                        pltpu.VMEM((t, t), F32)])
    return pl.pallas_call(
        functools.partial(_attn_kernel, t=t, lambda_init=lambda_init),
        grid_spec=grid_spec,
        out_shape=jax.ShapeDtypeStruct((n, DA_HEADS * DA_V_DIM), BF16),
        compiler_params=_cparams(3),
    )(i_tbl, j_tbl, slopes, q, kt, v, lq1, lk1, lq2, lk2, out_norm)


def _gdn_kernel(u_ref, z_ref, ba_ref, cw_ref, alog_ref, dt_ref, on_ref, o_ref, stage_ref, state_ref, *, batch):
    c = GDN_CHUNK
    hd = GDN_HEADS * GDN_DIM
    step = pl.program_id(0)

    @pl.when(step == 0)
    def _init():
        state_ref[...] = jnp.zeros(state_ref.shape, F32)
        stage_ref[:, 0:8, :] = jnp.zeros((batch, 8, stage_ref.shape[2]), F32)

    row = lax.broadcasted_iota(I32, (c, c), 0)
    col = lax.broadcasted_iota(I32, (c, c), 1)
    tri = row >= col
    strict = row > col
    tril_ones = jnp.where(tri, 1.0, 0.0).astype(BF16)
    eye = jnp.where(row == col, 1.0, 0.0).astype(F32)

    for b in range(batch):
        stage_ref[b, 8:8 + c, :] = u_ref[b]
        y = cw_ref[0:1, :] * stage_ref[b, 5:5 + c, :]
        for tap in range(1, CONV_K):
            y = y + cw_ref[tap:tap + 1, :] * stage_ref[b, 5 + tap:5 + tap + c, :]
        stage_ref[b, 0:8, :] = stage_ref[b, c:c + 8, :]
        y = _silu(y)

        ba = ba_ref[b]
        beta_all = _sigmoid(ba)
        g_all = -jnp.exp(alog_ref[...]) * _softplus(ba + dt_ref[...])
        gcum_all = _mm_exact_lhs(tril_ones, g_all)
        gcum_t = gcum_all.T

        for h in range(GDN_HEADS):
            q = y[:, h * GDN_DIM:(h + 1) * GDN_DIM]
            k = y[:, hd + h * GDN_DIM:hd + (h + 1) * GDN_DIM]
            v = y[:, 2 * hd + h * GDN_DIM:2 * hd + (h + 1) * GDN_DIM]
            q = q * lax.rsqrt(jnp.sum(q * q, axis=-1, keepdims=True) + EPS) * (GDN_DIM ** -0.5)
            k = k * lax.rsqrt(jnp.sum(k * k, axis=-1, keepdims=True) + EPS)
            beta = beta_all[:, h:h + 1]
            gc = gcum_all[:, GDN_HEADS + h:GDN_HEADS + h + 1]
            gr = gcum_t[GDN_HEADS + h:GDN_HEADS + h + 1, :]
            g_last = gc[c - 1:c, :]
            decay = jnp.where(tri, jnp.exp(jnp.where(tri, gc - gr, 0.0)), 0.0)
            kk = _mm_nt(k, k)
            p = jnp.where(strict, -(beta * kk * decay), 0.0)
            tinv = eye + p
            pk = p
            for _ in range(int(math.log2(c)) - 1):
                pk = _mm3(pk, pk)
                tinv = tinv + _mm3(tinv, pk)
            e_gc = jnp.exp(gc)
            rhs = jnp.concatenate([v * beta, k * (beta * e_gc)], axis=1)
            sol = _mm3(tinv, rhs)
            u_c, w_c = sol[:, :GDN_DIM], sol[:, GDN_DIM:]
            qk = _mm_nt(q, k) * decay
            q_dec = q * e_gc
            k_dec = k * jnp.exp(g_last - gc)
            s_prev = state_ref[b * GDN_HEADS + h]
            v_new = u_c - _mm(w_c, s_prev)
            o = _mm(q_dec, s_prev) + _mm(qk, v_new)
            state_ref[b * GDN_HEADS + h] = s_prev * jnp.exp(g_last) + _mm(k_dec.T, v_new)
            zg = z_ref[b, :, h * GDN_DIM:(h + 1) * GDN_DIM]
            o = _rms(o, on_ref[...]) * _silu(zg)
            o_ref[b, :, h * GDN_DIM:(h + 1) * GDN_DIM] = o.astype(o_ref.dtype)


def _gdn(gqkv3, z3, ba3, conv_w, alog_row, dt_row, out_norm):
    batch, seq, cw = gqkv3.shape
    c = GDN_CHUNK
    hd = GDN_HEADS * GDN_DIM
    blk = lambda s: (0, s, 0)
    const = lambda s: (0, 0)
    return pl.pallas_call(
        functools.partial(_gdn_kernel, batch=batch),
        grid=(seq // c,),
        in_specs=[pl.BlockSpec((batch, c, cw), blk),
                  pl.BlockSpec((batch, c, hd), blk),
                  pl.BlockSpec((batch, c, LANES), blk),
                  pl.BlockSpec((CONV_K, cw), const),
                  pl.BlockSpec((1, LANES), const),
                  pl.BlockSpec((1, LANES), const),
                  pl.BlockSpec((1, GDN_DIM), const)],
        out_specs=pl.BlockSpec((batch, c, hd), blk),
        out_shape=jax.ShapeDtypeStruct((batch, seq, hd), BF16),
        scratch_shapes=[pltpu.VMEM((batch, c + 8, cw), F32),
                        pltpu.VMEM((batch * GDN_HEADS, GDN_DIM, GDN_DIM), F32)],
        compiler_params=_cparams(1),
    )(gqkv3, z3, ba3, conv_w, alog_row, dt_row, out_norm)


def _merge_kernel(x_ref, mod_ref, g1_ref, g2_ref, oa_ref, ob_ref, wgate_ref, wa_ref, wb_ref, wout_ref,
                  wrt_ref, brt_ref, x1_ref, h2_ref, lg_ref):
    x = x_ref[...]
    d = x.shape[1]
    mod = mod_ref[0]
    shift1, scale1, gate1 = mod[0:1, :], mod[1:2, :], mod[2:3, :]
    shift2, scale2 = mod[3:4, :], mod[4:5, :]
    hb = (_rms(x, g1_ref[...]) * (1.0 + scale1) + shift1).astype(BF16)
    gates = _sigmoid(jnp.dot(hb, wgate_ref[...], preferred_element_type=F32))
    ya = jnp.dot(oa_ref[...], wa_ref[...], preferred_element_type=F32)
    yb = jnp.dot(ob_ref[...], wb_ref[...], preferred_element_type=F32)
    merged = gates[:, :d] * ya + gates[:, d:] * yb
    x1 = x + gate1 * _mm(merged, wout_ref[...])
    x1_ref[...] = x1
    h2 = _rms(x1, g2_ref[...]) * (1.0 + scale2) + shift2
    h2_ref[...] = _pack_bf16_pairs(h2)
    lg_ref[...] = _mm3(h2, wrt_ref[...]) + brt_ref[...]


def _merge(x2, mod3, g1, g2, oa, ob, w_gate, w_a, w_b, w_out, w_rt, b_rt, tm, tiles_per_batch):
    n, d = x2.shape
    const = lambda i: (0, 0)
    row = lambda i: (i, 0)
    return pl.pallas_call(
        _merge_kernel,
        grid=(n // tm,),
        in_specs=[pl.BlockSpec((tm, d), row),
                  pl.BlockSpec((1, 6, d), lambda i: (i // tiles_per_batch, 0, 0)),
                  pl.BlockSpec((1, d), const), pl.BlockSpec((1, d), const),
                  pl.BlockSpec((tm, oa.shape[1]), row), pl.BlockSpec((tm, ob.shape[1]), row),
                  pl.BlockSpec(w_gate.shape, const), pl.BlockSpec(w_a.shape, const),
                  pl.BlockSpec(w_b.shape, const), pl.BlockSpec(w_out.shape, const),
                  pl.BlockSpec(w_rt.shape, const), pl.BlockSpec((1, LANES), const)],
        out_specs=[pl.BlockSpec((tm, d), row), pl.BlockSpec((tm, d // 2), row), pl.BlockSpec((tm, LANES), row)],
        out_shape=[jax.ShapeDtypeStruct((n, d), F32), jax.ShapeDtypeStruct((n, d // 2), I32),
                   jax.ShapeDtypeStruct((n, LANES), F32)],
        compiler_params=_cparams(1),
    )(x2, mod3, g1, g2, oa, ob, w_gate, w_a, w_b, w_out, w_rt, b_rt)


def _route_kernel(lg_ref, ri_ref, rf_ref, cnt_ref, carry_ref):
    step = pl.program_id(0)
    tm = lg_ref.shape[0]

    @pl.when(step == 0)
    def _init():
        carry_ref[...] = jnp.zeros(carry_ref.shape, F32)

    lg = lg_ref[...]
    lane = lax.broadcasted_iota(I32, lg.shape, 1)
    big = jnp.int32(LANES)
    is_grp = (lane >= N_EXPERTS) & (lane < N_EXPERTS + N_GROUPS)
    gl = jnp.where(is_grp, lg, NEG_INF)
    gmax = jnp.max(gl, axis=-1, keepdims=True)
    g_top = jnp.min(jnp.where(is_grp & (gl == gmax), lane, big), axis=-1, keepdims=True) - N_EXPERTS
    g_top_p = 1.0 / jnp.sum(jnp.where(is_grp, jnp.exp(gl - gmax), 0.0), axis=-1, keepdims=True)
    in_grp = (lane < N_EXPERTS) & ((lane // EXPERTS_PER_GROUP) == g_top)
    el = jnp.where(in_grp, lg, NEG_INF)
    v1 = jnp.max(el, axis=-1, keepdims=True)
    e1 = jnp.min(jnp.where(in_grp & (el == v1), lane, big), axis=-1, keepdims=True)
    rest = in_grp & (lane != e1)
    el2 = jnp.where(rest, lg, NEG_INF)
    v2 = jnp.max(el2, axis=-1, keepdims=True)
    e2 = jnp.min(jnp.where(rest & (el2 == v2), lane, big), axis=-1, keepdims=True)
    ex = jnp.exp(v2 - v1)
    w1 = g_top_p / (1.0 + ex)
    w2 = g_top_p * ex / (1.0 + ex)

    oh1 = lane == e1
    oh2 = lane == e2
    ohs = (jnp.where(oh1, 1.0, 0.0) + jnp.where(oh2, 1.0, 0.0)).astype(BF16)
    r = lax.broadcasted_iota(I32, (tm, tm), 0)
    c = lax.broadcasted_iota(I32, (tm, tm), 1)
    before = jnp.where(r > c, 1.0, 0.0).astype(BF16)
    prior = jnp.dot(before, ohs, preferred_element_type=F32) + carry_ref[0:1, :]
    rank1 = jnp.sum(jnp.where(oh1, prior, 0.0), axis=-1, keepdims=True)
    rank2 = jnp.sum(jnp.where(oh2, prior, 0.0), axis=-1, keepdims=True)
    carry_ref[...] = carry_ref[...] + jnp.sum(ohs.astype(F32), axis=0, keepdims=True)
    cnt_ref[...] = carry_ref[...]

    r1 = rank1.astype(I32)
    r2 = rank2.astype(I32)
    ri = jnp.where(lane == 0, e1, jnp.where(lane == 1, e2, jnp.where(lane == 2, r1, jnp.where(lane == 3, r2, 0))))
    ri_ref[...] = ri
    rf_ref[...] = jnp.where(lane == 0, w1, jnp.where(lane == 1, w2, 0.0))


def _route(logits, tm):
    n = logits.shape[0]
    row = lambda i: (i, 0)
    return pl.pallas_call(
        _route_kernel,
        grid=(n // tm,),
        in_specs=[pl.BlockSpec((tm, LANES), row)],
        out_specs=[pl.BlockSpec((tm, LANES), row), pl.BlockSpec((tm, LANES), row),
                   pl.BlockSpec((8, LANES), lambda i: (0, 0))],
        out_shape=[jax.ShapeDtypeStruct((n, LANES), I32), jax.ShapeDtypeStruct((n, LANES), F32),
                   jax.ShapeDtypeStruct((8, LANES), F32)],
        scratch_shapes=[pltpu.VMEM((8, LANES), F32)],
        compiler_params=_cparams(1),
    )(logits)


def _moe_kernel(be_ref, act_ref, xb_ref, w1_ref, w3_ref, w2_ref, o_ref, w1b_ref, w3b_ref, w2b_ref):
    i = pl.program_id(0)
    prev = be_ref[jnp.maximum(i - 1, 0)]

    @pl.when((i == 0) | (be_ref[i] != prev))
    def _cast_weights():
        w1b_ref[...] = w1_ref[...].astype(BF16)
        w3b_ref[...] = w3_ref[...].astype(BF16)
        w2b_ref[...] = w2_ref[...].astype(BF16)

    @pl.when(act_ref[i] > 0)
    def _compute():
        xp = xb_ref[...]
        rows = lax.broadcasted_iota(I32, xp.shape, 0)
        xb = _unpack_bf16_pairs(jnp.where(rows < act_ref[i], xp, 0))
        a = jnp.dot(xb, w1b_ref[...], preferred_element_type=F32)
        g = jnp.dot(xb, w3b_ref[...], preferred_element_type=F32)
        hid = (_silu(a) * g).astype(BF16)
        o_ref[...] = _pack_bf16_pairs(jnp.dot(hid, w2b_ref[...], preferred_element_type=F32))

    @pl.when(act_ref[i] == 0)
    def _idle():
        o_ref[...] = jnp.zeros(o_ref.shape, o_ref.dtype)


def _moe(blk_expert, blk_rows, xb, w1, w3, w2):
    cap = xb.shape[0]
    d = w1.shape[1]
    nb = cap // MOE_ROWS
    de = w1.shape[2]
    grid_spec = pltpu.PrefetchScalarGridSpec(
        num_scalar_prefetch=2,
        grid=(nb,),
        in_specs=[pl.BlockSpec((MOE_ROWS, d // 2), lambda i, be, act: (i, 0)),
                  pl.BlockSpec((None, d, de), lambda i, be, act: (be[i], 0, 0)),
                  pl.BlockSpec((None, d, de), lambda i, be, act: (be[i], 0, 0)),
                  pl.BlockSpec((None, de, d), lambda i, be, act: (be[i], 0, 0))],
        out_specs=pl.BlockSpec((MOE_ROWS, d // 2), lambda i, be, act: (i, 0)),
        scratch_shapes=[pltpu.VMEM((d, de), BF16), pltpu.VMEM((d, de), BF16), pltpu.VMEM((de, d), BF16)])
    return pl.pallas_call(
        _moe_kernel,
        grid_spec=grid_spec,
        out_shape=jax.ShapeDtypeStruct((cap, d // 2), I32),
        compiler_params=_cparams(1),
    )(blk_expert, blk_rows, xb, w1, w3, w2)


def _dest_kernel(ri_ref, pstart_ref, o_ref):
    ri = ri_ref[...]
    lane = lax.broadcasted_iota(I32, ri.shape, 1)
    pstart = pstart_ref[...]
    base1 = jnp.sum(jnp.where(lane == ri[:, 0:1], pstart, 0), axis=-1, keepdims=True)
    base2 = jnp.sum(jnp.where(lane == ri[:, 1:2], pstart, 0), axis=-1, keepdims=True)
    dest = jnp.where(lane == 0, base1 + ri[:, 2:3], jnp.where(lane == 1, base2 + ri[:, 3:4], 0))
    o_ref[...] = dest.astype(F32).T[0:8, :].astype(I32)


def _dest(ri, pstart_row, tm):
    n = ri.shape[0]
    return pl.pallas_call(
        _dest_kernel,
        grid=(n // tm,),
        in_specs=[pl.BlockSpec((tm, LANES), lambda i: (i, 0)), pl.BlockSpec((1, LANES), lambda i: (0, 0))],
        out_specs=pl.BlockSpec((8, tm), lambda i: (0, i)),
        out_shape=jax.ShapeDtypeStruct((8, n), I32),
        compiler_params=_cparams(1),
    )(ri, pstart_row)


def _sc_mesh():
    return plsc.VectorSubcoreMesh(core_axis_name="core", subcore_axis_name="subcore")


def _sc_scatter_rows(x, idx, cap):
    n, d = x.shape
    windows = idx.shape[0]
    tiles = n // SC_WINDOW
    mesh = _sc_mesh()
    workers = mesh.num_cores * mesh.num_subcores
    per_worker = windows // workers

    @functools.partial(pl.kernel, out_type=jax.ShapeDtypeStruct((cap, d), x.dtype), mesh=mesh,
                       scratch_types=[pltpu.VMEM((1, SC_WINDOW), I32), pltpu.VMEM((SC_WINDOW, d), x.dtype)])
    def scatter(x_hbm, i_hbm, o_hbm, idx_v, rows_v):
        wid = lax.axis_index("subcore") * mesh.num_cores + lax.axis_index("core")

        @pl.loop(0, per_worker)
        def _(j):
            w = wid * per_worker + j
            pltpu.sync_copy(i_hbm.at[pl.ds(w, 1)], idx_v)
            pltpu.sync_copy(x_hbm.at[pl.ds(lax.rem(w, tiles) * SC_WINDOW, SC_WINDOW)], rows_v)
            pltpu.sync_copy(rows_v, o_hbm.at[idx_v.at[0]])

    return scatter(x, idx)


def _sc_gather_rows(table, idx):
    d = table.shape[1]
    windows = idx.shape[0]
    mesh = _sc_mesh()
    workers = mesh.num_cores * mesh.num_subcores
    per_worker = windows // workers

    @functools.partial(pl.kernel, out_type=jax.ShapeDtypeStruct((windows * SC_WINDOW, d), table.dtype), mesh=mesh,
                       scratch_types=[pltpu.VMEM((1, SC_WINDOW), I32), pltpu.VMEM((SC_WINDOW, d), table.dtype)])
    def gather(t_hbm, i_hbm, o_hbm, idx_v, rows_v):
        wid = lax.axis_index("subcore") * mesh.num_cores + lax.axis_index("core")

        @pl.loop(0, per_worker)
        def _(j):
            w = wid * per_worker + j
            pltpu.sync_copy(i_hbm.at[pl.ds(w, 1)], idx_v)
            pltpu.sync_copy(t_hbm.at[idx_v.at[0]], rows_v)
            pltpu.sync_copy(rows_v, o_hbm.at[pl.ds(w * SC_WINDOW, SC_WINDOW)])

    return gather(table, idx)


def _combine_kernel(x1_ref, mod_ref, rf_ref, ya_ref, yb_ref, o_ref):
    gate2 = mod_ref[0][5:6, :]
    rf = rf_ref[...]
    ya = _unpack_bf16_pairs(ya_ref[...]).astype(F32)
    yb = _unpack_bf16_pairs(yb_ref[...]).astype(F32)
    y = rf[:, 0:1] * ya + rf[:, 1:2] * yb
    o_ref[...] = x1_ref[...] + gate2 * y


def _combine(x1, mod3, rf, y12, tm, tiles_per_batch):
    n, d = x1.shape
    tiles = n // tm
    row = lambda i: (i, 0)
    return pl.pallas_call(
        _combine_kernel,
        grid=(tiles,),
        in_specs=[pl.BlockSpec((tm, d), row),
                  pl.BlockSpec((1, 6, d), lambda i: (i // tiles_per_batch, 0, 0)),
                  pl.BlockSpec((tm, LANES), row),
                  pl.BlockSpec((tm, d // 2), row), pl.BlockSpec((tm, d // 2), lambda i: (i + tiles, 0))],
        out_specs=pl.BlockSpec((tm, d), row),
        out_shape=jax.ShapeDtypeStruct((n, d), F32),
        compiler_params=_cparams(1),
    )(x1, mod3, rf, y12, y12)


def _pad_lanes(a, offset=0):
    return jnp.pad(a, ((0, 0), (offset, LANES - offset - a.shape[1])))


def _layer(x, c, layer, w_ada, b_ada, norm1_gain, w_in, da_q_norm, da_k_norm, lq1, lk1, lq2, lk2, da_out_norm,
           gdn_conv, gdn_a_log, gdn_dt_bias, gdn_out_norm, w_branch_a, w_branch_b, w_out, norm2_gain,
           w_group, b_group, w_router, b_router, w1, w3, w2):
    batch, seq, d = x.shape
    n = batch * seq
    tm = min(512, seq)
    tiles_per_batch = seq // tm
    t_attn = min(512, seq)
    lambda_init = 0.8 - 0.6 * math.exp(-0.3 * layer)

    mod = _ada(jnp.pad(c, ((0, 8 - batch), (0, 0))), w_ada, b_ada)[:batch]
    mod3 = mod.reshape(batch, 6, d)

    da_w = DA_HEADS * 2 * DA_QK_DIM
    dv_w = DA_HEADS * DA_V_DIM
    gd_w = GDN_HEADS * GDN_DIM
    o0 = 2 * da_w + dv_w
    o1 = o0 + 4 * gd_w
    o2 = o1 + 2 * GDN_HEADS
    w_da = w_in[:, :o0].astype(BF16)
    w_g = w_in[:, o0:o1].astype(BF16)
    w_ba = _pad_lanes(w_in[:, o1:o2]).astype(BF16)
    w_gate = w_in[:, o2:].astype(BF16)
    qn = jnp.tile(da_q_norm, 2 * DA_HEADS).reshape(1, da_w)
    kn = jnp.tile(da_k_norm, 2 * DA_HEADS).reshape(1, da_w)

    x2 = x.reshape(n, d)
    g1 = norm1_gain.reshape(1, d)
    q, k, v, gqkv, z, ba = _proj(x2, mod3, g1, w_da, w_g, w_ba, qn, kn, tm, tiles_per_batch)

    kt = k.reshape(batch, seq, DA_HEADS, 2 * DA_QK_DIM).transpose(0, 2, 3, 1)
    o_a = _attention(q, kt, v, lq1.reshape(1, -1), lk1.reshape(1, -1), lq2.reshape(1, -1), lk2.reshape(1, -1),
                     da_out_norm.reshape(1, -1), batch, seq, t_attn, lambda_init)

    alog_row = _pad_lanes(gdn_a_log.reshape(1, -1), GDN_HEADS)
    dt_row = _pad_lanes(gdn_dt_bias.reshape(1, -1), GDN_HEADS)
    o_b = _gdn(gqkv.reshape(batch, seq, -1), z.reshape(batch, seq, -1), ba.reshape(batch, seq, -1),
               gdn_conv, alog_row, dt_row, gdn_out_norm.reshape(1, -1)).reshape(n, gd_w)

    w_rt = _pad_lanes(jnp.concatenate([w_router, w_group], axis=1))
    b_rt = _pad_lanes(jnp.concatenate([b_router, b_group]).reshape(1, -1))
    x1, h2, logits = _merge(x2, mod3, g1, norm2_gain.reshape(1, d), o_a, o_b, w_gate,
                            w_branch_a.astype(BF16), w_branch_b.astype(BF16), w_out.astype(BF16),
                            w_rt, b_rt, tm, tiles_per_batch)

    ri, rf, cnt = _route(logits, tm)

    counts = cnt[0, :N_EXPERTS].astype(I32)
    padded = (counts + MOE_ROWS - 1) // MOE_ROWS * MOE_ROWS
    pend = jnp.cumsum(padded)
    pstart = pend - padded
    cap = 2 * n + N_EXPERTS * MOE_ROWS
    nb = cap // MOE_ROWS
    blk_start = jnp.arange(nb, dtype=I32) * MOE_ROWS
    blk_expert = jnp.minimum(jnp.sum(pend[None, :] <= blk_start[:, None], axis=1), N_EXPERTS - 1).astype(I32)
    blk_rows = jnp.clip((pstart + counts)[blk_expert] - blk_start, 0, MOE_ROWS).astype(I32)
    blk_rows = jnp.where(blk_start < pend[-1], blk_rows, 0)
    dest = _dest(ri, _pad_lanes(pstart.reshape(1, -1)), tm)
    idx = dest[0:2].reshape(2 * n // SC_WINDOW, SC_WINDOW)

    xb = _sc_scatter_rows(h2, idx, cap)
    yb = _moe(blk_expert, blk_rows, xb, w1, w3, w2)
    y12 = _sc_gather_rows(yb, idx)
    out = _combine(x1, mod3, rf, y12, tm, tiles_per_batch)
    return out.reshape(batch, seq, d)


def kernel(x, c, w_ada, b_ada, norm1_gain, w_in, da_q_norm, da_k_norm, da_lambda_q1, da_lambda_k1, da_lambda_q2,
           da_lambda_k2, da_out_norm, gdn_conv, gdn_a_log, gdn_dt_bias, gdn_out_norm, w_branch_a, w_branch_b,
           w_out, norm2_gain, w_group, b_group, w_router, b_router, w1, w3, w2):
    for layer in range(w_ada.shape[0]):
        x = _layer(x, c, layer, w_ada[layer], b_ada[layer], norm1_gain[layer], w_in[layer], da_q_norm[layer],
                   da_k_norm[layer], da_lambda_q1[layer], da_lambda_k1[layer], da_lambda_q2[layer],
                   da_lambda_k2[layer], da_out_norm[layer], gdn_conv[layer], gdn_a_log[layer], gdn_dt_bias[layer],
                   gdn_out_norm[layer], w_branch_a[layer], w_branch_b[layer], w_out[layer], norm2_gain[layer],
                   w_group[layer], b_group[layer], w_router[layer], b_router[layer], w1[layer], w3[layer], w2[layer])
    return x
```

```python
import functools
import math

import jax
import jax.numpy as jnp
import numpy as np
from jax import lax
from jax.experimental import pallas as pl
from jax.experimental.pallas import tpu as pltpu
from jax.experimental.pallas import tpu_sc as plsc

F32 = jnp.float32
BF16 = jnp.bfloat16
I32 = jnp.int32

EPS = 1e-6
NEG_INF = -1e30
MASK_CHUNK = 64
LOG2E = 1.4426950408889634
ALIBI_ROWS = 3

DA_HEADS = 4
DA_QK_DIM = 64
DA_V_DIM = 128
GDN_HEADS = 4
GDN_DIM = 128
CONV_K = 4
N_GROUPS = 4
EXPERTS_PER_GROUP = 8
N_EXPERTS = N_GROUPS * EXPERTS_PER_GROUP
D_EXPERT = 512

LANES = 128
VMEM_LIMIT = 56 * 1024 * 1024

GDN_CHUNK = 128
MOE_ROWS = 256
SC_WINDOW = 128


def _cparams(n_axes):
    return pltpu.CompilerParams(dimension_semantics=("arbitrary",) * n_axes,
                                vmem_limit_bytes=VMEM_LIMIT)


def _mm(a, b):
    return jnp.dot(a.astype(BF16), b.astype(BF16), preferred_element_type=F32)


def _mm_nt(a, b):
    return lax.dot_general(a.astype(BF16), b.astype(BF16), (((1,), (1,)), ((), ())),
                           preferred_element_type=F32)


def _split2(a):
    hi = a.astype(BF16)
    lo = (a - hi.astype(F32)).astype(BF16)
    return hi, lo


def _mm3(a, b):
    ah, al = _split2(a)
    bh, bl = _split2(b)
    out = jnp.dot(ah, bh, preferred_element_type=F32)
    out = out + jnp.dot(ah, bl, preferred_element_type=F32)
    out = out + jnp.dot(al, bh, preferred_element_type=F32)
    return out


def _mm_exact_lhs(a_bf16, b):
    b1 = b.astype(BF16)
    r1 = b - b1.astype(F32)
    b2 = r1.astype(BF16)
    b3 = (r1 - b2.astype(F32)).astype(BF16)
    out = jnp.dot(a_bf16, b1, preferred_element_type=F32)
    out = out + jnp.dot(a_bf16, b2, preferred_element_type=F32)
    out = out + jnp.dot(a_bf16, b3, preferred_element_type=F32)
    return out


def _sigmoid(x):
    return 1.0 / (1.0 + jnp.exp(-x))


def _silu(x):
    return x * _sigmoid(x)


def _softplus(x):
    return jnp.maximum(x, 0.0) + jnp.log1p(jnp.exp(-jnp.abs(x)))


def _pack_bf16_pairs(x):
    w = x.shape[1] // 2
    bits = lax.bitcast_convert_type(x.astype(BF16).astype(F32), jnp.uint32)
    lo = lax.shift_right_logical(bits[:, :w], jnp.uint32(16))
    hi = bits[:, w:] & jnp.uint32(0xFFFF0000)
    return lax.bitcast_convert_type(hi | lo, I32)


def _unpack_bf16_pairs(p):
    bits = lax.bitcast_convert_type(p, jnp.uint32)
    lo = lax.bitcast_convert_type(lax.shift_left(bits, jnp.uint32(16)), F32)
    hi = lax.bitcast_convert_type(bits & jnp.uint32(0xFFFF0000), F32)
    return jnp.concatenate([lo, hi], axis=1).astype(BF16)


def _rms(x, gain):
    return x * lax.rsqrt(jnp.mean(x * x, axis=-1, keepdims=True) + EPS) * gain


def _ada_kernel(c_ref, w_ref, b_ref, o_ref):
    sc = _silu(c_ref[...])
    o_ref[...] = _mm3(sc, w_ref[...]) + b_ref[...]


def _ada(c_pad, w_ada, b_ada):
    rows, d = c_pad.shape
    n = w_ada.shape[1]
    tn = d
    return pl.pallas_call(
        _ada_kernel,
        grid=(n // tn,),
        in_specs=[pl.BlockSpec((rows, d), lambda j: (0, 0)),
                  pl.BlockSpec((d, tn), lambda j: (0, j)),
                  pl.BlockSpec((1, tn), lambda j: (0, j))],
        out_specs=pl.BlockSpec((rows, tn), lambda j: (0, j)),
        out_shape=jax.ShapeDtypeStruct((rows, n), F32),
        compiler_params=_cparams(1),
    )(c_pad, w_ada, b_ada.reshape(1, n))


def _group_rms64(x, gain):
    tm, width = x.shape
    lane = lax.broadcasted_iota(I32, (tm, LANES), 1)
    low = lane < DA_QK_DIM
    parts = []
    for j in range(width // LANES):
        blk = x[:, j * LANES:(j + 1) * LANES]
        sq = blk * blk
        s_lo = jnp.sum(jnp.where(low, sq, 0.0), axis=-1, keepdims=True)
        s_hi = jnp.sum(jnp.where(low, 0.0, sq), axis=-1, keepdims=True)
        ms = jnp.where(low, s_lo, s_hi) * (1.0 / DA_QK_DIM)
        parts.append(blk * lax.rsqrt(ms + EPS))
    return jnp.concatenate(parts, axis=-1) * gain


def _proj_kernel(x_ref, mod_ref, g1_ref, wda_ref, wg_ref, wba_ref, qn_ref, kn_ref,
                 q_ref, k_ref, v_ref, gqkv_ref, z_ref, ba_ref):
    x = x_ref[...]
    mod = mod_ref[0]
    shift, scale = mod[0:1, :], mod[1:2, :]
    hb = (_rms(x, g1_ref[...]) * (1.0 + scale) + shift).astype(BF16)
    da_w = DA_HEADS * 2 * DA_QK_DIM
    da = jnp.dot(hb, wda_ref[...], preferred_element_type=F32)
    q = _group_rms64(da[:, :da_w], qn_ref[...]) * (DA_QK_DIM ** -0.5 * LOG2E)
    k = _group_rms64(da[:, da_w:2 * da_w], kn_ref[...])
    q_ref[...] = q.astype(BF16)
    k_ref[...] = k.astype(BF16)
    v_ref[...] = da[:, 2 * da_w:].astype(BF16)
    g = jnp.dot(hb, wg_ref[...], preferred_element_type=F32)
    conv_w = 3 * GDN_HEADS * GDN_DIM
    gqkv_ref[...] = g[:, :conv_w]
    z_ref[...] = g[:, conv_w:]
    ba_ref[...] = jnp.dot(hb, wba_ref[...], preferred_element_type=F32)


def _proj(x2, mod3, g1, w_da, w_g, w_ba, qn, kn, tm, tiles_per_batch):
    n, d = x2.shape
    da_w = DA_HEADS * 2 * DA_QK_DIM
    dv_w = DA_HEADS * DA_V_DIM
    conv_w = 3 * GDN_HEADS * GDN_DIM
    z_w = GDN_HEADS * GDN_DIM
    const = lambda i: (0, 0)
    row = lambda i: (i, 0)
    return pl.pallas_call(
        _proj_kernel,
        grid=(n // tm,),
        in_specs=[pl.BlockSpec((tm, d), row),
                  pl.BlockSpec((1, 6, d), lambda i: (i // tiles_per_batch, 0, 0)),
                  pl.BlockSpec((1, d), const),
                  pl.BlockSpec(w_da.shape, const),
                  pl.BlockSpec(w_g.shape, const),
                  pl.BlockSpec(w_ba.shape, const),
                  pl.BlockSpec((1, da_w), const),
                  pl.BlockSpec((1, da_w), const)],
        out_specs=[pl.BlockSpec((tm, da_w), row), pl.BlockSpec((tm, da_w), row),
                   pl.BlockSpec((tm, dv_w), row), pl.BlockSpec((tm, conv_w), row),
                   pl.BlockSpec((tm, z_w), row), pl.BlockSpec((tm, LANES), row)],
        out_shape=[jax.ShapeDtypeStruct((n, da_w), BF16), jax.ShapeDtypeStruct((n, da_w), BF16),
                   jax.ShapeDtypeStruct((n, dv_w), BF16), jax.ShapeDtypeStruct((n, conv_w), F32),
                   jax.ShapeDtypeStruct((n, z_w), F32), jax.ShapeDtypeStruct((n, LANES), F32)],
        compiler_params=_cparams(1),
    )(x2, mod3, g1, w_da, w_g, w_ba, qn, kn)


def _attn_kernel(slopes_ref, q_ref, kt_ref, v_ref, lq1_ref, lk1_ref, lq2_ref, lk2_ref, on_ref, o_ref,
                 qs_ref, m_ref, acc_ref, p_ref, ddiag_ref, *, t, lambda_init):
    h = pl.program_id(1)
    i = pl.program_id(2)
    slope2 = slopes_ref[h]
    reps = t // LANES

    @pl.when(i == 0)
    def _tables():
        r = lax.broadcasted_iota(I32, (t, t), 0)
        c = lax.broadcasted_iota(I32, (t, t), 1)
        ahead = jnp.maximum(c - r, 0).astype(F32)
        allowed = (c // MASK_CHUNK) <= (r // MASK_CHUNK)
        ddiag_ref[...] = jnp.where(allowed, (-2.0 * slope2) * ahead, NEG_INF)

    q = q_ref[...]
    lane = lax.broadcasted_iota(I32, q.shape, 1)
    zero = jnp.zeros_like(q)
    ones = jnp.where(lane < ALIBI_ROWS, 1.0, 0.0).astype(BF16)
    qs_ref[0:t, :] = jnp.concatenate([jnp.where(lane < DA_QK_DIM, q, zero), ones], axis=1)
    qs_ref[t:2 * t, :] = jnp.concatenate([jnp.where(lane < DA_QK_DIM, zero, q), ones], axis=1)
    m_ref[...] = jnp.full(m_ref.shape, NEG_INF, F32)
    acc_ref[...] = jnp.zeros(acc_ref.shape, F32)

    def scores(j):
        start = pl.multiple_of(j * t, t)
        return jnp.dot(qs_ref[...], kt_ref[0, 0, :, pl.ds(start, t)], preferred_element_type=F32)

    def consume(s, j, slot, diagonal):
        start = pl.multiple_of(j * t, t)
        alphas = []
        for half in range(2):
            rows = slice(half * t, (half + 1) * t)
            sb = s[rows] + ddiag_ref[...] if diagonal else s[rows]
            m_prev = m_ref[rows]
            m_next = jnp.maximum(m_prev, jnp.max(sb, axis=-1, keepdims=True))
            alphas.append(jnp.exp2(m_prev - m_next))
            p_ref[slot, rows] = jnp.exp2(sb - jnp.tile(m_next, (1, reps))).astype(BF16)
            m_ref[rows] = m_next
        v = v_ref[pl.ds(start, t), :]
        vext = jnp.concatenate([v, jnp.ones_like(v)], axis=1)
        pv = jnp.dot(p_ref[slot], vext, preferred_element_type=F32)
        for half in range(2):
            rows = slice(half * t, (half + 1) * t)
            acc_ref[rows] = jnp.tile(alphas[half], (1, 2)) * acc_ref[rows] + pv[rows]

    def run_blocks(first, count):
        s_next = scores(first)
        for k in range(count):
            s_cur = s_next
            if k + 1 < count:
                s_next = scores(first + k + 1)
            consume(s_cur, first + k, k % 2, False)

    def quad_body(jj, carry):
        run_blocks(4 * jj, 4)
        return carry

    lax.fori_loop(0, i // 4, quad_body, 0)
    done = (i // 4) * 4

    @pl.when((i // 2) % 2 == 1)
    def _pair():
        run_blocks(done, 2)

    @pl.when(i % 2 == 1)
    def _single():
        run_blocks(i - 1, 1)

    consume(scores(i), i, 1, True)

    acc = acc_ref[...]
    o_all = acc[:, 0:DA_V_DIM] / acc[:, DA_V_DIM:2 * DA_V_DIM]
    lam = (jnp.exp(jnp.sum(lq1_ref[...] * lk1_ref[...], axis=-1, keepdims=True))
           - jnp.exp(jnp.sum(lq2_ref[...] * lk2_ref[...], axis=-1, keepdims=True)) + lambda_init)
    o = o_all[0:t] - lam * o_all[t:2 * t]
    o = _rms(o, on_ref[...]) * (1.0 - lambda_init)
    o_ref[...] = o.astype(o_ref.dtype)


def _alibi_slopes_log2():
    return np.asarray([2.0 ** (-8.0 * (hh + 1) / DA_HEADS) * LOG2E for hh in range(DA_HEADS)], np.float32)


def _alibi_key_rows(seq):
    col = _alibi_slopes_log2()[:, None] * np.arange(seq, dtype=np.float32)[None, :]
    b1 = col.astype(BF16)
    r1 = col - b1.astype(np.float32)
    b2 = r1.astype(BF16)
    b3 = (r1 - b2.astype(np.float32)).astype(BF16)
    rows = np.zeros((DA_HEADS, LANES, seq), BF16)
    rows[:, 0], rows[:, 1], rows[:, 2] = b1, b2, b3
    return jnp.asarray(rows)


def _attention(q, kt, v, lq1, lk1, lq2, lk2, out_norm, batch, seq, t, lambda_init):
    n = batch * seq
    nq = seq // t
    slopes = jnp.asarray(_alibi_slopes_log2())
    vec = lambda b, h, i, sl: (0, 0)
    grid_spec = pltpu.PrefetchScalarGridSpec(
        num_scalar_prefetch=1,
        grid=(batch, DA_HEADS, nq),
        in_specs=[pl.BlockSpec((t, LANES), lambda b, h, i, sl: (b * nq + i, h)),
                  pl.BlockSpec((1, 1, 2 * LANES, seq), lambda b, h, i, sl: (b, h, 0, 0)),
                  pl.BlockSpec((seq, LANES), lambda b, h, i, sl: (b, h)),
                  pl.BlockSpec((1, DA_QK_DIM), vec), pl.BlockSpec((1, DA_QK_DIM), vec),
                  pl.BlockSpec((1, DA_QK_DIM), vec), pl.BlockSpec((1, DA_QK_DIM), vec),
                  pl.BlockSpec((1, DA_V_DIM), vec)],
        out_specs=pl.BlockSpec((t, LANES), lambda b, h, i, sl: (b * nq + i, h)),
        scratch_shapes=[pltpu.VMEM((2 * t, 2 * LANES), BF16),
                        pltpu.VMEM((2 * t, LANES), F32),
                        pltpu.VMEM((2 * t, 2 * DA_V_DIM), F32),
                        pltpu.VMEM((2, 2 * t, t), BF16),
                        pltpu.VMEM((t, t), F32)])
    return pl.pallas_call(
        functools.partial(_attn_kernel, t=t, lambda_init=lambda_init),
        grid_spec=grid_spec,
        out_shape=jax.ShapeDtypeStruct((n, DA_HEADS * DA_V_DIM), BF16),
        compiler_params=_cparams(3),
    )(slopes, q, kt, v, lq1, lk1, lq2, lk2, out_norm)


def _gdn_kernel(u_ref, z_ref, ba_ref, cw_ref, alog_ref, dt_ref, on_ref, o_ref, stage_ref, state_ref, *, batch):
    c = GDN_CHUNK
    hd = GDN_HEADS * GDN_DIM
    step = pl.program_id(0)

    @pl.when(step == 0)
    def _init():
        state_ref[...] = jnp.zeros(state_ref.shape, F32)
        stage_ref[:, 0:8, :] = jnp.zeros((batch, 8, stage_ref.shape[2]), F32)

    row = lax.broadcasted_iota(I32, (c, c), 0)
    col = lax.broadcasted_iota(I32, (c, c), 1)
    tri = row >= col
    strict = row > col
    tril_ones = jnp.where(tri, 1.0, 0.0).astype(BF16)
    eye = jnp.where(row == col, 1.0, 0.0).astype(F32)

    for b in range(batch):
        stage_ref[b, 8:8 + c, :] = u_ref[b]
        y = cw_ref[0:1, :] * stage_ref[b, 5:5 + c, :]
        for tap in range(1, CONV_K):
            y = y + cw_ref[tap:tap + 1, :] * stage_ref[b, 5 + tap:5 + tap + c, :]
        stage_ref[b, 0:8, :] = stage_ref[b, c:c + 8, :]
        y = _silu(y)

        ba = ba_ref[b]
        beta_all = _sigmoid(ba)
        g_all = -jnp.exp(alog_ref[...]) * _softplus(ba + dt_ref[...])
        gcum_all = _mm_exact_lhs(tril_ones, g_all)
        gcum_t = gcum_all.T

        for h in range(GDN_HEADS):
            q = y[:, h * GDN_DIM:(h + 1) * GDN_DIM]
            k = y[:, hd + h * GDN_DIM:hd + (h + 1) * GDN_DIM]
            v = y[:, 2 * hd + h * GDN_DIM:2 * hd + (h + 1) * GDN_DIM]
            q = q * lax.rsqrt(jnp.sum(q * q, axis=-1, keepdims=True) + EPS) * (GDN_DIM ** -0.5)
            k = k * lax.rsqrt(jnp.sum(k * k, axis=-1, keepdims=True) + EPS)
            beta = beta_all[:, h:h + 1]
            gc = gcum_all[:, GDN_HEADS + h:GDN_HEADS + h + 1]
            gr = gcum_t[GDN_HEADS + h:GDN_HEADS + h + 1, :]
            g_last = gc[c - 1:c, :]
            decay = jnp.where(tri, jnp.exp(jnp.where(tri, gc - gr, 0.0)), 0.0)
            kk = _mm_nt(k, k)
            p = jnp.where(strict, -(beta * kk * decay), 0.0)
            tinv = eye + p
            pk = p
            for _ in range(int(math.log2(c)) - 1):
                pk = _mm3(pk, pk)
                tinv = tinv + _mm3(tinv, pk)
            e_gc = jnp.exp(gc)
            rhs = jnp.concatenate([v * beta, k * (beta * e_gc)], axis=1)
            sol = _mm3(tinv, rhs)
            u_c, w_c = sol[:, :GDN_DIM], sol[:, GDN_DIM:]
            qk = _mm_nt(q, k) * decay
            q_dec = q * e_gc
            k_dec = k * jnp.exp(g_last - gc)
            s_prev = state_ref[b * GDN_HEADS + h]
            v_new = u_c - _mm(w_c, s_prev)
            o = _mm(q_dec, s_prev) + _mm(qk, v_new)
            state_ref[b * GDN_HEADS + h] = s_prev * jnp.exp(g_last) + _mm(k_dec.T, v_new)
            zg = z_ref[b, :, h * GDN_DIM:(h + 1) * GDN_DIM]
            o = _rms(o, on_ref[...]) * _silu(zg)
            o_ref[b, :, h * GDN_DIM:(h + 1) * GDN_DIM] = o.astype(o_ref.dtype)


def _gdn(gqkv3, z3, ba3, conv_w, alog_row, dt_row, out_norm):
    batch, seq, cw = gqkv3.shape
    c = GDN_CHUNK
    hd = GDN_HEADS * GDN_DIM
    blk = lambda s: (0, s, 0)
    const = lambda s: (0, 0)
    return pl.pallas_call(
        functools.partial(_gdn_kernel, batch=batch),
        grid=(seq // c,),
        in_specs=[pl.BlockSpec((batch, c, cw), blk),
                  pl.BlockSpec((batch, c, hd), blk),
                  pl.BlockSpec((batch, c, LANES), blk),
                  pl.BlockSpec((CONV_K, cw), const),
                  pl.BlockSpec((1, LANES), const),
                  pl.BlockSpec((1, LANES), const),
                  pl.BlockSpec((1, GDN_DIM), const)],
        out_specs=pl.BlockSpec((batch, c, hd), blk),
        out_shape=jax.ShapeDtypeStruct((batch, seq, hd), BF16),
        scratch_shapes=[pltpu.VMEM((batch, c + 8, cw), F32),
                        pltpu.VMEM((batch * GDN_HEADS, GDN_DIM, GDN_DIM), F32)],
        compiler_params=_cparams(1),
    )(gqkv3, z3, ba3, conv_w, alog_row, dt_row, out_norm)


def _merge_kernel(x_ref, mod_ref, g1_ref, g2_ref, oa_ref, ob_ref, wgate_ref, wa_ref, wb_ref, wout_ref,
                  wrt_ref, brt_ref, x1_ref, h2_ref, lg_ref):
    x = x_ref[...]
    d = x.shape[1]
    mod = mod_ref[0]
    shift1, scale1, gate1 = mod[0:1, :], mod[1:2, :], mod[2:3, :]
    shift2, scale2 = mod[3:4, :], mod[4:5, :]
    hb = (_rms(x, g1_ref[...]) * (1.0 + scale1) + shift1).astype(BF16)
    gates = _sigmoid(jnp.dot(hb, wgate_ref[...], preferred_element_type=F32))
    ya = jnp.dot(oa_ref[...], wa_ref[...], preferred_element_type=F32)
    yb = jnp.dot(ob_ref[...], wb_ref[...], preferred_element_type=F32)
    merged = gates[:, :d] * ya + gates[:, d:] * yb
    x1 = x + gate1 * _mm(merged, wout_ref[...])
    x1_ref[...] = x1
    h2 = _rms(x1, g2_ref[...]) * (1.0 + scale2) + shift2
    h2_ref[...] = _pack_bf16_pairs(h2)
    lg_ref[...] = _mm3(h2, wrt_ref[...]) + brt_ref[...]


def _merge(x2, mod3, g1, g2, oa, ob, w_gate, w_a, w_b, w_out, w_rt, b_rt, tm, tiles_per_batch):
    n, d = x2.shape
    const = lambda i: (0, 0)
    row = lambda i: (i, 0)
    return pl.pallas_call(
        _merge_kernel,
        grid=(n // tm,),
        in_specs=[pl.BlockSpec((tm, d), row),
                  pl.BlockSpec((1, 6, d), lambda i: (i // tiles_per_batch, 0, 0)),
                  pl.BlockSpec((1, d), const), pl.BlockSpec((1, d), const),
                  pl.BlockSpec((tm, oa.shape[1]), row), pl.BlockSpec((tm, ob.shape[1]), row),
                  pl.BlockSpec(w_gate.shape, const), pl.BlockSpec(w_a.shape, const),
                  pl.BlockSpec(w_b.shape, const), pl.BlockSpec(w_out.shape, const),
                  pl.BlockSpec(w_rt.shape, const), pl.BlockSpec((1, LANES), const)],
        out_specs=[pl.BlockSpec((tm, d), row), pl.BlockSpec((tm, d // 2), row), pl.BlockSpec((tm, LANES), row)],
        out_shape=[jax.ShapeDtypeStruct((n, d), F32), jax.ShapeDtypeStruct((n, d // 2), I32),
                   jax.ShapeDtypeStruct((n, LANES), F32)],
        compiler_params=_cparams(1),
    )(x2, mod3, g1, g2, oa, ob, w_gate, w_a, w_b, w_out, w_rt, b_rt)


def _route_kernel(lg_ref, ri_ref, rf_ref, cnt_ref, carry_ref):
    step = pl.program_id(0)
    tm = lg_ref.shape[0]

    @pl.when(step == 0)
    def _init():
        carry_ref[...] = jnp.zeros(carry_ref.shape, F32)

    lg = lg_ref[...]
    lane = lax.broadcasted_iota(I32, lg.shape, 1)
    big = jnp.int32(LANES)
    is_grp = (lane >= N_EXPERTS) & (lane < N_EXPERTS + N_GROUPS)
    gl = jnp.where(is_grp, lg, NEG_INF)
    gmax = jnp.max(gl, axis=-1, keepdims=True)
    g_top = jnp.min(jnp.where(is_grp & (gl == gmax), lane, big), axis=-1, keepdims=True) - N_EXPERTS
    g_top_p = 1.0 / jnp.sum(jnp.where(is_grp, jnp.exp(gl - gmax), 0.0), axis=-1, keepdims=True)
    in_grp = (lane < N_EXPERTS) & ((lane // EXPERTS_PER_GROUP) == g_top)
    el = jnp.where(in_grp, lg, NEG_INF)
    v1 = jnp.max(el, axis=-1, keepdims=True)
    e1 = jnp.min(jnp.where(in_grp & (el == v1), lane, big), axis=-1, keepdims=True)
    rest = in_grp & (lane != e1)
    el2 = jnp.where(rest, lg, NEG_INF)
    v2 = jnp.max(el2, axis=-1, keepdims=True)
    e2 = jnp.min(jnp.where(rest & (el2 == v2), lane, big), axis=-1, keepdims=True)
    ex = jnp.exp(v2 - v1)
    w1 = g_top_p / (1.0 + ex)
    w2 = g_top_p * ex / (1.0 + ex)

    oh1 = lane == e1
    oh2 = lane == e2
    ohs = (jnp.where(oh1, 1.0, 0.0) + jnp.where(oh2, 1.0, 0.0)).astype(BF16)
    r = lax.broadcasted_iota(I32, (tm, tm), 0)
    c = lax.broadcasted_iota(I32, (tm, tm), 1)
    before = jnp.where(r > c, 1.0, 0.0).astype(BF16)
    prior = jnp.dot(before, ohs, preferred_element_type=F32) + carry_ref[0:1, :]
    rank1 = jnp.sum(jnp.where(oh1, prior, 0.0), axis=-1, keepdims=True)
    rank2 = jnp.sum(jnp.where(oh2, prior, 0.0), axis=-1, keepdims=True)
    carry_ref[...] = carry_ref[...] + jnp.sum(ohs.astype(F32), axis=0, keepdims=True)
    cnt_ref[...] = carry_ref[...]

    r1 = rank1.astype(I32)
    r2 = rank2.astype(I32)
    ri = jnp.where(lane == 0, e1, jnp.where(lane == 1, e2, jnp.where(lane == 2, r1, jnp.where(lane == 3, r2, 0))))
    ri_ref[...] = ri
    rf_ref[...] = jnp.where(lane == 0, w1, jnp.where(lane == 1, w2, 0.0))


def _route(logits, tm):
    n = logits.shape[0]
    row = lambda i: (i, 0)
    return pl.pallas_call(
        _route_kernel,
        grid=(n // tm,),
        in_specs=[pl.BlockSpec((tm, LANES), row)],
        out_specs=[pl.BlockSpec((tm, LANES), row), pl.BlockSpec((tm, LANES), row),
                   pl.BlockSpec((8, LANES), lambda i: (0, 0))],
        out_shape=[jax.ShapeDtypeStruct((n, LANES), I32), jax.ShapeDtypeStruct((n, LANES), F32),
                   jax.ShapeDtypeStruct((8, LANES), F32)],
        scratch_shapes=[pltpu.VMEM((8, LANES), F32)],
        compiler_params=_cparams(1),
    )(logits)


def _moe_kernel(be_ref, act_ref, xb_ref, w1_ref, w3_ref, w2_ref, o_ref, w1b_ref, w3b_ref, w2b_ref):
    i = pl.program_id(0)
    prev = be_ref[jnp.maximum(i - 1, 0)]

    @pl.when((i == 0) | (be_ref[i] != prev))
    def _cast_weights():
        w1b_ref[...] = w1_ref[...].astype(BF16)
        w3b_ref[...] = w3_ref[...].astype(BF16)
        w2b_ref[...] = w2_ref[...].astype(BF16)

    @pl.when(act_ref[i] > 0)
    def _compute():
        xp = xb_ref[...]
        rows = lax.broadcasted_iota(I32, xp.shape, 0)
        xb = _unpack_bf16_pairs(jnp.where(rows < act_ref[i], xp, 0))
        a = jnp.dot(xb, w1b_ref[...], preferred_element_type=F32)
        g = jnp.dot(xb, w3b_ref[...], preferred_element_type=F32)
        hid = (_silu(a) * g).astype(BF16)
        o_ref[...] = _pack_bf16_pairs(jnp.dot(hid, w2b_ref[...], preferred_element_type=F32))

    @pl.when(act_ref[i] == 0)
    def _idle():
        o_ref[...] = jnp.zeros(o_ref.shape, o_ref.dtype)


def _moe(blk_expert, blk_rows, xb, w1, w3, w2):
    cap = xb.shape[0]
    d = w1.shape[1]
    nb = cap // MOE_ROWS
    de = w1.shape[2]
    grid_spec = pltpu.PrefetchScalarGridSpec(
        num_scalar_prefetch=2,
        grid=(nb,),
        in_specs=[pl.BlockSpec((MOE_ROWS, d // 2), lambda i, be, act: (i, 0)),
                  pl.BlockSpec((None, d, de), lambda i, be, act: (be[i], 0, 0)),
                  pl.BlockSpec((None, d, de), lambda i, be, act: (be[i], 0, 0)),
                  pl.BlockSpec((None, de, d), lambda i, be, act: (be[i], 0, 0))],
        out_specs=pl.BlockSpec((MOE_ROWS, d // 2), lambda i, be, act: (i, 0)),
        scratch_shapes=[pltpu.VMEM((d, de), BF16), pltpu.VMEM((d, de), BF16), pltpu.VMEM((de, d), BF16)])
    return pl.pallas_call(
        _moe_kernel,
        grid_spec=grid_spec,
        out_shape=jax.ShapeDtypeStruct((cap, d // 2), I32),
        compiler_params=_cparams(1),
    )(blk_expert, blk_rows, xb, w1, w3, w2)


def _dest_kernel(ri_ref, pstart_ref, o_ref):
    ri = ri_ref[...]
    lane = lax.broadcasted_iota(I32, ri.shape, 1)
    pstart = pstart_ref[...]
    base1 = jnp.sum(jnp.where(lane == ri[:, 0:1], pstart, 0), axis=-1, keepdims=True)
    base2 = jnp.sum(jnp.where(lane == ri[:, 1:2], pstart, 0), axis=-1, keepdims=True)
    dest = jnp.where(lane == 0, base1 + ri[:, 2:3], jnp.where(lane == 1, base2 + ri[:, 3:4], 0))
    o_ref[...] = dest.astype(F32).T[0:8, :].astype(I32)


def _dest(ri, pstart_row, tm):
    n = ri.shape[0]
    return pl.pallas_call(
        _dest_kernel,
        grid=(n // tm,),
        in_specs=[pl.BlockSpec((tm, LANES), lambda i: (i, 0)), pl.BlockSpec((1, LANES), lambda i: (0, 0))],
        out_specs=pl.BlockSpec((8, tm), lambda i: (0, i)),
        out_shape=jax.ShapeDtypeStruct((8, n), I32),
        compiler_params=_cparams(1),
    )(ri, pstart_row)


def _sc_mesh():
    return plsc.VectorSubcoreMesh(core_axis_name="core", subcore_axis_name="subcore")


def _sc_scatter_rows(x, idx, cap):
    n, d = x.shape
    windows = idx.shape[0]
    tiles = n // SC_WINDOW
    mesh = _sc_mesh()
    workers = mesh.num_cores * mesh.num_subcores
    per_worker = windows // workers

    @functools.partial(pl.kernel, out_type=jax.ShapeDtypeStruct((cap, d), x.dtype), mesh=mesh,
                       scratch_types=[pltpu.VMEM((1, SC_WINDOW), I32), pltpu.VMEM((SC_WINDOW, d), x.dtype)])
    def scatter(x_hbm, i_hbm, o_hbm, idx_v, rows_v):
        wid = lax.axis_index("subcore") * mesh.num_cores + lax.axis_index("core")

        @pl.loop(0, per_worker)
        def _(j):
            w = wid * per_worker + j
            pltpu.sync_copy(i_hbm.at[pl.ds(w, 1)], idx_v)
            pltpu.sync_copy(x_hbm.at[pl.ds(lax.rem(w, tiles) * SC_WINDOW, SC_WINDOW)], rows_v)
            pltpu.sync_copy(rows_v, o_hbm.at[idx_v.at[0]])

    return scatter(x, idx)


def _sc_gather_rows(table, idx):
    d = table.shape[1]
    windows = idx.shape[0]
    mesh = _sc_mesh()
    workers = mesh.num_cores * mesh.num_subcores
    per_worker = windows // workers

    @functools.partial(pl.kernel, out_type=jax.ShapeDtypeStruct((windows * SC_WINDOW, d), table.dtype), mesh=mesh,
                       scratch_types=[pltpu.VMEM((1, SC_WINDOW), I32), pltpu.VMEM((SC_WINDOW, d), table.dtype)])
    def gather(t_hbm, i_hbm, o_hbm, idx_v, rows_v):
        wid = lax.axis_index("subcore") * mesh.num_cores + lax.axis_index("core")

        @pl.loop(0, per_worker)
        def _(j):
            w = wid * per_worker + j
            pltpu.sync_copy(i_hbm.at[pl.ds(w, 1)], idx_v)
            pltpu.sync_copy(t_hbm.at[idx_v.at[0]], rows_v)
            pltpu.sync_copy(rows_v, o_hbm.at[pl.ds(w * SC_WINDOW, SC_WINDOW)])

    return gather(table, idx)


def _combine_kernel(x1_ref, mod_ref, rf_ref, ya_ref, yb_ref, o_ref):
    gate2 = mod_ref[0][5:6, :]
    rf = rf_ref[...]
    ya = _unpack_bf16_pairs(ya_ref[...]).astype(F32)
    yb = _unpack_bf16_pairs(yb_ref[...]).astype(F32)
    y = rf[:, 0:1] * ya + rf[:, 1:2] * yb
    o_ref[...] = x1_ref[...] + gate2 * y


def _combine(x1, mod3, rf, y12, tm, tiles_per_batch):
    n, d = x1.shape
    tiles = n // tm
    row = lambda i: (i, 0)
    return pl.pallas_call(
        _combine_kernel,
        grid=(tiles,),
        in_specs=[pl.BlockSpec((tm, d), row),
                  pl.BlockSpec((1, 6, d), lambda i: (i // tiles_per_batch, 0, 0)),
                  pl.BlockSpec((tm, LANES), row),
                  pl.BlockSpec((tm, d // 2), row), pl.BlockSpec((tm, d // 2), lambda i: (i + tiles, 0))],
        out_specs=pl.BlockSpec((tm, d), row),
        out_shape=jax.ShapeDtypeStruct((n, d), F32),
        compiler_params=_cparams(1),
    )(x1, mod3, rf, y12, y12)


def _pad_lanes(a, offset=0):
    return jnp.pad(a, ((0, 0), (offset, LANES - offset - a.shape[1])))


def _layer(x, c, layer, w_ada, b_ada, norm1_gain, w_in, da_q_norm, da_k_norm, lq1, lk1, lq2, lk2, da_out_norm,
           gdn_conv, gdn_a_log, gdn_dt_bias, gdn_out_norm, w_branch_a, w_branch_b, w_out, norm2_gain,
           w_group, b_group, w_router, b_router, w1, w3, w2):
    batch, seq, d = x.shape
    n = batch * seq
    tm = min(512, seq)
    tiles_per_batch = seq // tm
    t_attn = min(512, seq)
    lambda_init = 0.8 - 0.6 * math.exp(-0.3 * layer)

    mod = _ada(jnp.pad(c, ((0, 8 - batch), (0, 0))), w_ada, b_ada)[:batch]
    mod3 = mod.reshape(batch, 6, d)

    da_w = DA_HEADS * 2 * DA_QK_DIM
    dv_w = DA_HEADS * DA_V_DIM
    gd_w = GDN_HEADS * GDN_DIM
    o0 = 2 * da_w + dv_w
    o1 = o0 + 4 * gd_w
    o2 = o1 + 2 * GDN_HEADS
    w_da = w_in[:, :o0].astype(BF16)
    w_g = w_in[:, o0:o1].astype(BF16)
    w_ba = _pad_lanes(w_in[:, o1:o2]).astype(BF16)
    w_gate = w_in[:, o2:].astype(BF16)
    qn = jnp.tile(da_q_norm, 2 * DA_HEADS).reshape(1, da_w)
    kn = jnp.tile(da_k_norm, 2 * DA_HEADS).reshape(1, da_w)

    x2 = x.reshape(n, d)
    g1 = norm1_gain.reshape(1, d)
    q, k, v, gqkv, z, ba = _proj(x2, mod3, g1, w_da, w_g, w_ba, qn, kn, tm, tiles_per_batch)

    kt = k.reshape(batch, seq, DA_HEADS, 2 * DA_QK_DIM).transpose(0, 2, 3, 1)
    kt = jnp.concatenate([kt, jnp.broadcast_to(_alibi_key_rows(seq), kt.shape)], axis=2)
    o_a = _attention(q, kt, v, lq1.reshape(1, -1), lk1.reshape(1, -1), lq2.reshape(1, -1), lk2.reshape(1, -1),
                     da_out_norm.reshape(1, -1), batch, seq, t_attn, lambda_init)

    alog_row = _pad_lanes(gdn_a_log.reshape(1, -1), GDN_HEADS)
    dt_row = _pad_lanes(gdn_dt_bias.reshape(1, -1), GDN_HEADS)
    o_b = _gdn(gqkv.reshape(batch, seq, -1), z.reshape(batch, seq, -1), ba.reshape(batch, seq, -1),
               gdn_conv, alog_row, dt_row, gdn_out_norm.reshape(1, -1)).reshape(n, gd_w)

    w_rt = _pad_lanes(jnp.concatenate([w_router, w_group], axis=1))
    b_rt = _pad_lanes(jnp.concatenate([b_router, b_group]).reshape(1, -1))
    x1, h2, logits = _merge(x2, mod3, g1, norm2_gain.reshape(1, d), o_a, o_b, w_gate,
                            w_branch_a.astype(BF16), w_branch_b.astype(BF16), w_out.astype(BF16),
                            w_rt, b_rt, tm, tiles_per_batch)

    ri, rf, cnt = _route(logits, tm)

    counts = cnt[0, :N_EXPERTS].astype(I32)
    padded = (counts + MOE_ROWS - 1) // MOE_ROWS * MOE_ROWS
    pend = jnp.cumsum(padded)
    pstart = pend - padded
    cap = 2 * n + N_EXPERTS * MOE_ROWS
    nb = cap // MOE_ROWS
    blk_start = jnp.arange(nb, dtype=I32) * MOE_ROWS
    blk_expert = jnp.minimum(jnp.sum(pend[None, :] <= blk_start[:, None], axis=1), N_EXPERTS - 1).astype(I32)
    blk_rows = jnp.clip((pstart + counts)[blk_expert] - blk_start, 0, MOE_ROWS).astype(I32)
    blk_rows = jnp.where(blk_start < pend[-1], blk_rows, 0)
    dest = _dest(ri, _pad_lanes(pstart.reshape(1, -1)), tm)
    idx = dest[0:2].reshape(2 * n // SC_WINDOW, SC_WINDOW)

    xb = _sc_scatter_rows(h2, idx, cap)
    yb = _moe(blk_expert, blk_rows, xb, w1, w3, w2)
    y12 = _sc_gather_rows(yb, idx)
    out = _combine(x1, mod3, rf, y12, tm, tiles_per_batch)
    return out.reshape(batch, seq, d)


def kernel(x, c, w_ada, b_ada, norm1_gain, w_in, da_q_norm, da_k_norm, da_lambda_q1, da_lambda_k1, da_lambda_q2,
           da_lambda_k2, da_out_norm, gdn_conv, gdn_a_log, gdn_dt_bias, gdn_out_norm, w_branch_a, w_branch_b,
           w_out, norm2_gain, w_group, b_group, w_router, b_router, w1, w3, w2):
    for layer in range(w_ada.shape[0]):
        x = _layer(x, c, layer, w_ada[layer], b_ada[layer], norm1_gain[layer], w_in[layer], da_q_norm[layer],
                   da_k_norm[layer], da_lambda_q1[layer], da_lambda_k1[layer], da_lambda_q2[layer],
                   da_lambda_k2[layer], da_out_norm[layer], gdn_conv[layer], gdn_a_log[layer], gdn_dt_bias[layer],
                   gdn_out_norm[layer], w_branch_a[layer], w_branch_b[layer], w_out[layer], norm2_gain[layer],
                   w_group[layer], b_group[layer], w_router[layer], b_router[layer], w1[layer], w3[layer], w2[layer])
    return x
```

```python
import functools
import math

import jax
import jax.numpy as jnp
import numpy as np
from jax import lax
from jax.experimental import pallas as pl
from jax.experimental.pallas import tpu as pltpu
from jax.experimental.pallas import tpu_sc as plsc

F32 = jnp.float32
BF16 = jnp.bfloat16
I32 = jnp.int32

EPS = 1e-6
NEG_INF = -1e30
MASK_CHUNK = 64
LOG2E = 1.4426950408889634
ALIBI_ROWS = 3

DA_HEADS = 4
DA_QK_DIM = 64
DA_V_DIM = 128
GDN_HEADS = 4
GDN_DIM = 128
CONV_K = 4
N_GROUPS = 4
EXPERTS_PER_GROUP = 8
N_EXPERTS = N_GROUPS * EXPERTS_PER_GROUP
D_EXPERT = 512

LANES = 128
VMEM_LIMIT = 56 * 1024 * 1024

GDN_CHUNK = 128
GDN_BASE = 16
MOE_ROWS = 256
SC_WINDOW = 128


def _cparams(n_axes):
    return pltpu.CompilerParams(dimension_semantics=("arbitrary",) * n_axes,
                                vmem_limit_bytes=VMEM_LIMIT)


def _mm(a, b):
    return jnp.dot(a.astype(BF16), b.astype(BF16), preferred_element_type=F32)


def _mm_nt(a, b):
    return lax.dot_general(a.astype(BF16), b.astype(BF16), (((1,), (1,)), ((), ())),
                           preferred_element_type=F32)


def _split2(a):
    hi = a.astype(BF16)
    lo = (a - hi.astype(F32)).astype(BF16)
    return hi, lo


def _mm3(a, b):
    ah, al = _split2(a)
    bh, bl = _split2(b)
    out = jnp.dot(ah, bh, preferred_element_type=F32)
    out = out + jnp.dot(ah, bl, preferred_element_type=F32)
    out = out + jnp.dot(al, bh, preferred_element_type=F32)
    return out


def _mm_exact_lhs(a_bf16, b):
    b1 = b.astype(BF16)
    r1 = b - b1.astype(F32)
    b2 = r1.astype(BF16)
    b3 = (r1 - b2.astype(F32)).astype(BF16)
    out = jnp.dot(a_bf16, b1, preferred_element_type=F32)
    out = out + jnp.dot(a_bf16, b2, preferred_element_type=F32)
    out = out + jnp.dot(a_bf16, b3, preferred_element_type=F32)
    return out


def _sigmoid(x):
    return 1.0 / (1.0 + jnp.exp(-x))


def _silu(x):
    return x * _sigmoid(x)


def _softplus(x):
    return jnp.maximum(x, 0.0) + jnp.log1p(jnp.exp(-jnp.abs(x)))


def _pack_bf16_pairs(x):
    w = x.shape[1] // 2
    bits = lax.bitcast_convert_type(x.astype(BF16).astype(F32), jnp.uint32)
    lo = lax.shift_right_logical(bits[:, :w], jnp.uint32(16))
    hi = bits[:, w:] & jnp.uint32(0xFFFF0000)
    return lax.bitcast_convert_type(hi | lo, I32)


def _unpack_bf16_pairs(p):
    bits = lax.bitcast_convert_type(p, jnp.uint32)
    lo = lax.bitcast_convert_type(lax.shift_left(bits, jnp.uint32(16)), F32)
    hi = lax.bitcast_convert_type(bits & jnp.uint32(0xFFFF0000), F32)
    return jnp.concatenate([lo, hi], axis=1).astype(BF16)


def _rms(x, gain):
    return x * lax.rsqrt(jnp.mean(x * x, axis=-1, keepdims=True) + EPS) * gain


def _ada_kernel(c_ref, w_ref, b_ref, o_ref):
    sc = _silu(c_ref[...])
    o_ref[...] = _mm3(sc, w_ref[...]) + b_ref[...]


def _ada(c_pad, w_ada, b_ada):
    rows, d = c_pad.shape
    n = w_ada.shape[1]
    tn = d
    return pl.pallas_call(
        _ada_kernel,
        grid=(n // tn,),
        in_specs=[pl.BlockSpec((rows, d), lambda j: (0, 0)),
                  pl.BlockSpec((d, tn), lambda j: (0, j)),
                  pl.BlockSpec((1, tn), lambda j: (0, j))],
        out_specs=pl.BlockSpec((rows, tn), lambda j: (0, j)),
        out_shape=jax.ShapeDtypeStruct((rows, n), F32),
        compiler_params=_cparams(1),
    )(c_pad, w_ada, b_ada.reshape(1, n))


def _group_rms64(x, gain):
    tm, width = x.shape
    lane = lax.broadcasted_iota(I32, (tm, LANES), 1)
    low = lane < DA_QK_DIM
    parts = []
    for j in range(width // LANES):
        blk = x[:, j * LANES:(j + 1) * LANES]
        sq = blk * blk
        s_lo = jnp.sum(jnp.where(low, sq, 0.0), axis=-1, keepdims=True)
        s_hi = jnp.sum(jnp.where(low, 0.0, sq), axis=-1, keepdims=True)
        ms = jnp.where(low, s_lo, s_hi) * (1.0 / DA_QK_DIM)
        parts.append(blk * lax.rsqrt(ms + EPS))
    return jnp.concatenate(parts, axis=-1) * gain


def _proj_kernel(x_ref, mod_ref, g1_ref, wda_ref, wg_ref, wba_ref, qn_ref, kn_ref,
                 q_ref, k_ref, v_ref, gqkv_ref, z_ref, ba_ref):
    x = x_ref[...]
    mod = mod_ref[0]
    shift, scale = mod[0:1, :], mod[1:2, :]
    hb = (_rms(x, g1_ref[...]) * (1.0 + scale) + shift).astype(BF16)
    da_w = DA_HEADS * 2 * DA_QK_DIM
    da = jnp.dot(hb, wda_ref[...], preferred_element_type=F32)
    q = _group_rms64(da[:, :da_w], qn_ref[...]) * (DA_QK_DIM ** -0.5 * LOG2E)
    k = _group_rms64(da[:, da_w:2 * da_w], kn_ref[...])
    q_ref[...] = q.astype(BF16)
    k_ref[...] = k.astype(BF16)
    v_ref[...] = da[:, 2 * da_w:].astype(BF16)
    g = jnp.dot(hb, wg_ref[...], preferred_element_type=F32)
    conv_w = 3 * GDN_HEADS * GDN_DIM
    gqkv_ref[...] = g[:, :conv_w]
    z_ref[...] = g[:, conv_w:]
    ba_ref[...] = jnp.dot(hb, wba_ref[...], preferred_element_type=F32)


def _proj(x2, mod3, g1, w_da, w_g, w_ba, qn, kn, tm, tiles_per_batch):
    n, d = x2.shape
    da_w = DA_HEADS * 2 * DA_QK_DIM
    dv_w = DA_HEADS * DA_V_DIM
    conv_w = 3 * GDN_HEADS * GDN_DIM
    z_w = GDN_HEADS * GDN_DIM
    const = lambda i: (0, 0)
    row = lambda i: (i, 0)
    return pl.pallas_call(
        _proj_kernel,
        grid=(n // tm,),
        in_specs=[pl.BlockSpec((tm, d), row),
                  pl.BlockSpec((1, 6, d), lambda i: (i // tiles_per_batch, 0, 0)),
                  pl.BlockSpec((1, d), const),
                  pl.BlockSpec(w_da.shape, const),
                  pl.BlockSpec(w_g.shape, const),
                  pl.BlockSpec(w_ba.shape, const),
                  pl.BlockSpec((1, da_w), const),
                  pl.BlockSpec((1, da_w), const)],
        out_specs=[pl.BlockSpec((tm, da_w), row), pl.BlockSpec((tm, da_w), row),
                   pl.BlockSpec((tm, dv_w), row), pl.BlockSpec((tm, conv_w), row),
                   pl.BlockSpec((tm, z_w), row), pl.BlockSpec((tm, LANES), row)],
        out_shape=[jax.ShapeDtypeStruct((n, da_w), BF16), jax.ShapeDtypeStruct((n, da_w), BF16),
                   jax.ShapeDtypeStruct((n, dv_w), BF16), jax.ShapeDtypeStruct((n, conv_w), F32),
                   jax.ShapeDtypeStruct((n, z_w), F32), jax.ShapeDtypeStruct((n, LANES), F32)],
        compiler_params=_cparams(1),
    )(x2, mod3, g1, w_da, w_g, w_ba, qn, kn)


def _attn_kernel(slopes_ref, q_ref, kt_ref, v_ref, lq1_ref, lk1_ref, lq2_ref, lk2_ref, on_ref, o_ref,
                 qs_ref, m_ref, acc_ref, p_ref, ddiag_ref, *, t, lambda_init):
    h = pl.program_id(1)
    i = pl.program_id(2)
    slope2 = slopes_ref[h]
    reps = t // LANES

    @pl.when(i == 0)
    def _tables():
        r = lax.broadcasted_iota(I32, (t, t), 0)
        c = lax.broadcasted_iota(I32, (t, t), 1)
        ahead = jnp.maximum(c - r, 0).astype(F32)
        allowed = (c // MASK_CHUNK) <= (r // MASK_CHUNK)
        ddiag_ref[...] = jnp.where(allowed, (-2.0 * slope2) * ahead, NEG_INF)

    q = q_ref[...]
    lane = lax.broadcasted_iota(I32, q.shape, 1)
    zero = jnp.zeros_like(q)
    ones = jnp.where(lane < ALIBI_ROWS, 1.0, 0.0).astype(BF16)
    qs_ref[0:t, :] = jnp.concatenate([jnp.where(lane < DA_QK_DIM, q, zero), ones], axis=1)
    qs_ref[t:2 * t, :] = jnp.concatenate([jnp.where(lane < DA_QK_DIM, zero, q), ones], axis=1)
    m_ref[...] = jnp.full(m_ref.shape, NEG_INF, F32)
    acc_ref[...] = jnp.zeros(acc_ref.shape, F32)

    def scores(j):
        start = pl.multiple_of(j * t, t)
        return jnp.dot(qs_ref[...], kt_ref[0, 0, :, pl.ds(start, t)], preferred_element_type=F32)

    def consume(s, j, slot, diagonal):
        start = pl.multiple_of(j * t, t)
        alphas = []
        for half in range(2):
            rows = slice(half * t, (half + 1) * t)
            sb = s[rows] + ddiag_ref[...] if diagonal else s[rows]
            m_prev = m_ref[rows]
            m_next = jnp.maximum(m_prev, jnp.max(sb, axis=-1, keepdims=True))
            alphas.append(jnp.exp2(m_prev - m_next))
            p_ref[slot, rows] = jnp.exp2(sb - jnp.tile(m_next, (1, reps))).astype(BF16)
            m_ref[rows] = m_next
        v = v_ref[pl.ds(start, t), :]
        vext = jnp.concatenate([v, jnp.ones_like(v)], axis=1)
        pv = jnp.dot(p_ref[slot], vext, preferred_element_type=F32)
        for half in range(2):
            rows = slice(half * t, (half + 1) * t)
            acc_ref[rows] = jnp.tile(alphas[half], (1, 2)) * acc_ref[rows] + pv[rows]

    def run_blocks(first, count):
        s_next = scores(first)
        for k in range(count):
            s_cur = s_next
            if k + 1 < count:
                s_next = scores(first + k + 1)
            consume(s_cur, first + k, k % 2, False)

    def quad_body(jj, carry):
        run_blocks(4 * jj, 4)
        return carry

    lax.fori_loop(0, i // 4, quad_body, 0)
    done = (i // 4) * 4

    @pl.when((i // 2) % 2 == 1)
    def _pair():
        run_blocks(done, 2)

    @pl.when(i % 2 == 1)
    def _single():
        run_blocks(i - 1, 1)

    consume(scores(i), i, 1, True)

    acc = acc_ref[...]
    o_all = acc[:, 0:DA_V_DIM] / acc[:, DA_V_DIM:2 * DA_V_DIM]
    lam = (jnp.exp(jnp.sum(lq1_ref[...] * lk1_ref[...], axis=-1, keepdims=True))
           - jnp.exp(jnp.sum(lq2_ref[...] * lk2_ref[...], axis=-1, keepdims=True)) + lambda_init)
    o = o_all[0:t] - lam * o_all[t:2 * t]
    o = _rms(o, on_ref[...]) * (1.0 - lambda_init)
    o_ref[...] = o.astype(o_ref.dtype)


def _alibi_slopes_log2():
    return np.asarray([2.0 ** (-8.0 * (hh + 1) / DA_HEADS) * LOG2E for hh in range(DA_HEADS)], np.float32)


def _alibi_key_rows(seq):
    col = _alibi_slopes_log2()[:, None] * np.arange(seq, dtype=np.float32)[None, :]
    b1 = col.astype(BF16)
    r1 = col - b1.astype(np.float32)
    b2 = r1.astype(BF16)
    b3 = (r1 - b2.astype(np.float32)).astype(BF16)
    rows = np.zeros((DA_HEADS, LANES, seq), BF16)
    rows[:, 0], rows[:, 1], rows[:, 2] = b1, b2, b3
    return jnp.asarray(rows)


def _attention(q, kt, v, lq1, lk1, lq2, lk2, out_norm, batch, seq, t, lambda_init):
    n = batch * seq
    nq = seq // t
    slopes = jnp.asarray(_alibi_slopes_log2())
    vec = lambda b, h, i, sl: (0, 0)
    grid_spec = pltpu.PrefetchScalarGridSpec(
        num_scalar_prefetch=1,
        grid=(batch, DA_HEADS, nq),
        in_specs=[pl.BlockSpec((t, LANES), lambda b, h, i, sl: (b * nq + i, h)),
                  pl.BlockSpec((1, 1, 2 * LANES, seq), lambda b, h, i, sl: (b, h, 0, 0)),
                  pl.BlockSpec((seq, LANES), lambda b, h, i, sl: (b, h)),
                  pl.BlockSpec((1, DA_QK_DIM), vec), pl.BlockSpec((1, DA_QK_DIM), vec),
                  pl.BlockSpec((1, DA_QK_DIM), vec), pl.BlockSpec((1, DA_QK_DIM), vec),
                  pl.BlockSpec((1, DA_V_DIM), vec)],
        out_specs=pl.BlockSpec((t, LANES), lambda b, h, i, sl: (b * nq + i, h)),
        scratch_shapes=[pltpu.VMEM((2 * t, 2 * LANES), BF16),
                        pltpu.VMEM((2 * t, LANES), F32),
                        pltpu.VMEM((2 * t, 2 * DA_V_DIM), F32),
                        pltpu.VMEM((2, 2 * t, t), BF16),
                        pltpu.VMEM((t, t), F32)])
    return pl.pallas_call(
        functools.partial(_attn_kernel, t=t, lambda_init=lambda_init),
        grid_spec=grid_spec,
        out_shape=jax.ShapeDtypeStruct((n, DA_HEADS * DA_V_DIM), BF16),
        compiler_params=_cparams(3),
    )(slopes, q, kt, v, lq1, lk1, lq2, lk2, out_norm)


def _gdn_kernel(u_ref, z_ref, ba_ref, cw_ref, alog_ref, dt_ref, on_ref, o_ref, stage_ref, state_ref, *, batch):
    c = GDN_CHUNK
    hd = GDN_HEADS * GDN_DIM
    step = pl.program_id(0)

    @pl.when(step == 0)
    def _init():
        state_ref[...] = jnp.zeros(state_ref.shape, F32)
        stage_ref[:, 0:8, :] = jnp.zeros((batch, 8, stage_ref.shape[2]), F32)

    row = lax.broadcasted_iota(I32, (c, c), 0)
    col = lax.broadcasted_iota(I32, (c, c), 1)
    tri = row >= col
    strict = row > col
    tril_ones = jnp.where(tri, 1.0, 0.0).astype(BF16)

    chains = []
    for b in range(batch):
        stage_ref[b, 8:8 + c, :] = u_ref[b]
        y = cw_ref[0:1, :] * stage_ref[b, 5:5 + c, :]
        for tap in range(1, CONV_K):
            y = y + cw_ref[tap:tap + 1, :] * stage_ref[b, 5 + tap:5 + tap + c, :]
        stage_ref[b, 0:8, :] = stage_ref[b, c:c + 8, :]
        y = _silu(y)

        ba = ba_ref[b]
        beta_all = _sigmoid(ba)
        g_all = -jnp.exp(alog_ref[...]) * _softplus(ba + dt_ref[...])
        gcum_all = _mm_exact_lhs(tril_ones, g_all)
        gcum_t = gcum_all.T

        for h in range(GDN_HEADS):
            q = y[:, h * GDN_DIM:(h + 1) * GDN_DIM]
            k = y[:, hd + h * GDN_DIM:hd + (h + 1) * GDN_DIM]
            v = y[:, 2 * hd + h * GDN_DIM:2 * hd + (h + 1) * GDN_DIM]
            q = q * lax.rsqrt(jnp.sum(q * q, axis=-1, keepdims=True) + EPS) * (GDN_DIM ** -0.5)
            k = k * lax.rsqrt(jnp.sum(k * k, axis=-1, keepdims=True) + EPS)
            beta = beta_all[:, h:h + 1]
            gc = gcum_all[:, GDN_HEADS + h:GDN_HEADS + h + 1]
            gr = gcum_t[GDN_HEADS + h:GDN_HEADS + h + 1, :]
            g_last = gc[c - 1:c, :]
            decay = jnp.where(tri, jnp.exp(jnp.where(tri, gc - gr, 0.0)), 0.0)
            e_gc = jnp.exp(gc)
            chains.append(dict(
                b=b, h=h, decay=decay, g_last=g_last,
                p=jnp.where(strict, -(beta * _mm_nt(k, k) * decay), 0.0),
                rhs=jnp.concatenate([v * beta, k * (beta * e_gc)], axis=1),
                qk=_mm_nt(q, k) * decay, q_dec=q * e_gc, k_dec_t=(k * jnp.exp(g_last - gc)).T))

    def same_block(size):
        return (row // size) == (col // size)

    for ch in chains:
        ch["pk"] = jnp.where(same_block(GDN_BASE), ch["p"], 0.0)
        ch["x"] = ch["pk"]
    for _ in range(int(math.log2(GDN_BASE)) - 1):
        for ch in chains:
            ch["pk"] = _mm(ch["pk"], ch["pk"])
        for ch in chains:
            ch["x"] = ch["x"] + ch["pk"] + _mm(ch["x"], ch["pk"])
    size = GDN_BASE
    while size < c:
        pair_only = same_block(2 * size) & jnp.logical_not(same_block(size))
        for ch in chains:
            e = jnp.where(pair_only, -ch["p"], 0.0)
            ch["y"] = e + _mm(ch["x"], e)
        for ch in chains:
            ch["x"] = ch["x"] - (ch["y"] + _mm(ch["y"], ch["x"]))
        size *= 2

    outs = [[None] * GDN_HEADS for _ in range(batch)]
    new_states = []
    for ch in chains:
        sol = ch["rhs"] + _mm(ch["x"], ch["rhs"])
        u_c, w_c = sol[:, :GDN_DIM], sol[:, GDN_DIM:]
        s_prev = state_ref[ch["b"] * GDN_HEADS + ch["h"]]
        v_new = u_c - _mm(w_c, s_prev)
        o = _mm(ch["q_dec"], s_prev) + _mm(ch["qk"], v_new)
        new_states.append(s_prev * jnp.exp(ch["g_last"]) + _mm(ch["k_dec_t"], v_new))
        zg = z_ref[ch["b"], :, ch["h"] * GDN_DIM:(ch["h"] + 1) * GDN_DIM]
        outs[ch["b"]][ch["h"]] = (_rms(o, on_ref[...]) * _silu(zg)).astype(o_ref.dtype)
    o_ref[...] = jnp.stack([jnp.concatenate(heads, axis=1) for heads in outs], axis=0)
    state_ref[...] = jnp.stack(new_states, axis=0)


def _gdn(gqkv3, z3, ba3, conv_w, alog_row, dt_row, out_norm):
    batch, seq, cw = gqkv3.shape
    c = GDN_CHUNK
    hd = GDN_HEADS * GDN_DIM
    blk = lambda s: (0, s, 0)
    const = lambda s: (0, 0)
    return pl.pallas_call(
        functools.partial(_gdn_kernel, batch=batch),
        grid=(seq // c,),
        in_specs=[pl.BlockSpec((batch, c, cw), blk),
                  pl.BlockSpec((batch, c, hd), blk),
                  pl.BlockSpec((batch, c, LANES), blk),
                  pl.BlockSpec((CONV_K, cw), const),
                  pl.BlockSpec((1, LANES), const),
                  pl.BlockSpec((1, LANES), const),
                  pl.BlockSpec((1, GDN_DIM), const)],
        out_specs=pl.BlockSpec((batch, c, hd), blk),
        out_shape=jax.ShapeDtypeStruct((batch, seq, hd), BF16),
        scratch_shapes=[pltpu.VMEM((batch, c + 8, cw), F32),
                        pltpu.VMEM((batch * GDN_HEADS, GDN_DIM, GDN_DIM), F32)],
        compiler_params=_cparams(1),
    )(gqkv3, z3, ba3, conv_w, alog_row, dt_row, out_norm)


def _merge_kernel(x_ref, mod_ref, g1_ref, g2_ref, oa_ref, ob_ref, wgate_ref, wa_ref, wb_ref, wout_ref,
                  wrt_ref, brt_ref, x1_ref, h2_ref, lg_ref):
    x = x_ref[...]
    d = x.shape[1]
    mod = mod_ref[0]
    shift1, scale1, gate1 = mod[0:1, :], mod[1:2, :], mod[2:3, :]
    shift2, scale2 = mod[3:4, :], mod[4:5, :]
    hb = (_rms(x, g1_ref[...]) * (1.0 + scale1) + shift1).astype(BF16)
    gates = _sigmoid(jnp.dot(hb, wgate_ref[...], preferred_element_type=F32))
    ya = jnp.dot(oa_ref[...], wa_ref[...], preferred_element_type=F32)
    yb = jnp.dot(ob_ref[...], wb_ref[...], preferred_element_type=F32)
    merged = gates[:, :d] * ya + gates[:, d:] * yb
    x1 = x + gate1 * _mm(merged, wout_ref[...])
    x1_ref[...] = x1
    h2 = _rms(x1, g2_ref[...]) * (1.0 + scale2) + shift2
    h2_ref[...] = _pack_bf16_pairs(h2)
    lg_ref[...] = _mm3(h2, wrt_ref[...]) + brt_ref[...]


def _merge(x2, mod3, g1, g2, oa, ob, w_gate, w_a, w_b, w_out, w_rt, b_rt, tm, tiles_per_batch):
    n, d = x2.shape
    const = lambda i: (0, 0)
    row = lambda i: (i, 0)
    return pl.pallas_call(
        _merge_kernel,
        grid=(n // tm,),
        in_specs=[pl.BlockSpec((tm, d), row),
                  pl.BlockSpec((1, 6, d), lambda i: (i // tiles_per_batch, 0, 0)),
                  pl.BlockSpec((1, d), const), pl.BlockSpec((1, d), const),
                  pl.BlockSpec((tm, oa.shape[1]), row), pl.BlockSpec((tm, ob.shape[1]), row),
                  pl.BlockSpec(w_gate.shape, const), pl.BlockSpec(w_a.shape, const),
                  pl.BlockSpec(w_b.shape, const), pl.BlockSpec(w_out.shape, const),
                  pl.BlockSpec(w_rt.shape, const), pl.BlockSpec((1, LANES), const)],
        out_specs=[pl.BlockSpec((tm, d), row), pl.BlockSpec((tm, d // 2), row), pl.BlockSpec((tm, LANES), row)],
        out_shape=[jax.ShapeDtypeStruct((n, d), F32), jax.ShapeDtypeStruct((n, d // 2), I32),
                   jax.ShapeDtypeStruct((n, LANES), F32)],
        compiler_params=_cparams(1),
    )(x2, mod3, g1, g2, oa, ob, w_gate, w_a, w_b, w_out, w_rt, b_rt)


def _route_kernel(lg_ref, ri_ref, rf_ref, cnt_ref, carry_ref):
    step = pl.program_id(0)
    tm = lg_ref.shape[0]

    @pl.when(step == 0)
    def _init():
        carry_ref[...] = jnp.zeros(carry_ref.shape, F32)

    lg = lg_ref[...]
    lane = lax.broadcasted_iota(I32, lg.shape, 1)
    big = jnp.int32(LANES)
    is_grp = (lane >= N_EXPERTS) & (lane < N_EXPERTS + N_GROUPS)
    gl = jnp.where(is_grp, lg, NEG_INF)
    gmax = jnp.max(gl, axis=-1, keepdims=True)
    g_top = jnp.min(jnp.where(is_grp & (gl == gmax), lane, big), axis=-1, keepdims=True) - N_EXPERTS
    g_top_p = 1.0 / jnp.sum(jnp.where(is_grp, jnp.exp(gl - gmax), 0.0), axis=-1, keepdims=True)
    in_grp = (lane < N_EXPERTS) & ((lane // EXPERTS_PER_GROUP) == g_top)
    el = jnp.where(in_grp, lg, NEG_INF)
    v1 = jnp.max(el, axis=-1, keepdims=True)
    e1 = jnp.min(jnp.where(in_grp & (el == v1), lane, big), axis=-1, keepdims=True)
    rest = in_grp & (lane != e1)
    el2 = jnp.where(rest, lg, NEG_INF)
    v2 = jnp.max(el2, axis=-1, keepdims=True)
    e2 = jnp.min(jnp.where(rest & (el2 == v2), lane, big), axis=-1, keepdims=True)
    ex = jnp.exp(v2 - v1)
    w1 = g_top_p / (1.0 + ex)
    w2 = g_top_p * ex / (1.0 + ex)

    oh1 = lane == e1
    oh2 = lane == e2
    ohs = (jnp.where(oh1, 1.0, 0.0) + jnp.where(oh2, 1.0, 0.0)).astype(BF16)
    r = lax.broadcasted_iota(I32, (tm, tm), 0)
    c = lax.broadcasted_iota(I32, (tm, tm), 1)
    before = jnp.where(r > c, 1.0, 0.0).astype(BF16)
    prior = jnp.dot(before, ohs, preferred_element_type=F32) + carry_ref[0:1, :]
    rank1 = jnp.sum(jnp.where(oh1, prior, 0.0), axis=-1, keepdims=True)
    rank2 = jnp.sum(jnp.where(oh2, prior, 0.0), axis=-1, keepdims=True)
    carry_ref[...] = carry_ref[...] + jnp.sum(ohs.astype(F32), axis=0, keepdims=True)
    cnt_ref[...] = carry_ref[...]

    r1 = rank1.astype(I32)
    r2 = rank2.astype(I32)
    ri = jnp.where(lane == 0, e1, jnp.where(lane == 1, e2, jnp.where(lane == 2, r1, jnp.where(lane == 3, r2, 0))))
    ri_ref[...] = ri
    rf_ref[...] = jnp.where(lane == 0, w1, jnp.where(lane == 1, w2, 0.0))


def _route(logits, tm):
    n = logits.shape[0]
    row = lambda i: (i, 0)
    return pl.pallas_call(
        _route_kernel,
        grid=(n // tm,),
        in_specs=[pl.BlockSpec((tm, LANES), row)],
        out_specs=[pl.BlockSpec((tm, LANES), row), pl.BlockSpec((tm, LANES), row),
                   pl.BlockSpec((8, LANES), lambda i: (0, 0))],
        out_shape=[jax.ShapeDtypeStruct((n, LANES), I32), jax.ShapeDtypeStruct((n, LANES), F32),
                   jax.ShapeDtypeStruct((8, LANES), F32)],
        scratch_shapes=[pltpu.VMEM((8, LANES), F32)],
        compiler_params=_cparams(1),
    )(logits)


def _moe_kernel(be_ref, act_ref, xb_ref, w1_ref, w3_ref, w2_ref, o_ref, w1b_ref, w3b_ref, w2b_ref):
    i = pl.program_id(0)
    prev = be_ref[jnp.maximum(i - 1, 0)]

    @pl.when((i == 0) | (be_ref[i] != prev))
    def _cast_weights():
        w1b_ref[...] = w1_ref[...].astype(BF16)
        w3b_ref[...] = w3_ref[...].astype(BF16)
        w2b_ref[...] = w2_ref[...].astype(BF16)

    @pl.when(act_ref[i] > 0)
    def _compute():
        xp = xb_ref[...]
        rows = lax.broadcasted_iota(I32, xp.shape, 0)
        xb = _unpack_bf16_pairs(jnp.where(rows < act_ref[i], xp, 0))
        a = jnp.dot(xb, w1b_ref[...], preferred_element_type=F32)
        g = jnp.dot(xb, w3b_ref[...], preferred_element_type=F32)
        hid = (_silu(a) * g).astype(BF16)
        o_ref[...] = _pack_bf16_pairs(jnp.dot(hid, w2b_ref[...], preferred_element_type=F32))

    @pl.when(act_ref[i] == 0)
    def _idle():
        o_ref[...] = jnp.zeros(o_ref.shape, o_ref.dtype)


def _moe(blk_expert, blk_rows, xb, w1, w3, w2):
    cap = xb.shape[0]
    d = w1.shape[1]
    nb = cap // MOE_ROWS
    de = w1.shape[2]
    grid_spec = pltpu.PrefetchScalarGridSpec(
        num_scalar_prefetch=2,
        grid=(nb,),
        in_specs=[pl.BlockSpec((MOE_ROWS, d // 2), lambda i, be, act: (i, 0)),
                  pl.BlockSpec((None, d, de), lambda i, be, act: (be[i], 0, 0)),
                  pl.BlockSpec((None, d, de), lambda i, be, act: (be[i], 0, 0)),
                  pl.BlockSpec((None, de, d), lambda i, be, act: (be[i], 0, 0))],
        out_specs=pl.BlockSpec((MOE_ROWS, d // 2), lambda i, be, act: (i, 0)),
        scratch_shapes=[pltpu.VMEM((d, de), BF16), pltpu.VMEM((d, de), BF16), pltpu.VMEM((de, d), BF16)])
    return pl.pallas_call(
        _moe_kernel,
        grid_spec=grid_spec,
        out_shape=jax.ShapeDtypeStruct((cap, d // 2), I32),
        compiler_params=_cparams(1),
    )(blk_expert, blk_rows, xb, w1, w3, w2)


def _dest_kernel(ri_ref, pstart_ref, o_ref):
    ri = ri_ref[...]
    lane = lax.broadcasted_iota(I32, ri.shape, 1)
    pstart = pstart_ref[...]
    base1 = jnp.sum(jnp.where(lane == ri[:, 0:1], pstart, 0), axis=-1, keepdims=True)
    base2 = jnp.sum(jnp.where(lane == ri[:, 1:2], pstart, 0), axis=-1, keepdims=True)
    dest = jnp.where(lane == 0, base1 + ri[:, 2:3], jnp.where(lane == 1, base2 + ri[:, 3:4], 0))
    o_ref[...] = dest.astype(F32).T[0:8, :].astype(I32)


def _dest(ri, pstart_row, tm):
    n = ri.shape[0]
    return pl.pallas_call(
        _dest_kernel,
        grid=(n // tm,),
        in_specs=[pl.BlockSpec((tm, LANES), lambda i: (i, 0)), pl.BlockSpec((1, LANES), lambda i: (0, 0))],
        out_specs=pl.BlockSpec((8, tm), lambda i: (0, i)),
        out_shape=jax.ShapeDtypeStruct((8, n), I32),
        compiler_params=_cparams(1),
    )(ri, pstart_row)


def _sc_mesh():
    return plsc.VectorSubcoreMesh(core_axis_name="core", subcore_axis_name="subcore")


def _sc_scatter_rows(x, idx, cap):
    n, d = x.shape
    windows = idx.shape[0]
    tiles = n // SC_WINDOW
    mesh = _sc_mesh()
    workers = mesh.num_cores * mesh.num_subcores
    per_worker = windows // workers

    @functools.partial(pl.kernel, out_type=jax.ShapeDtypeStruct((cap, d), x.dtype), mesh=mesh,
                       scratch_types=[pltpu.VMEM((1, SC_WINDOW), I32), pltpu.VMEM((SC_WINDOW, d), x.dtype)])
    def scatter(x_hbm, i_hbm, o_hbm, idx_v, rows_v):
        wid = lax.axis_index("subcore") * mesh.num_cores + lax.axis_index("core")

        @pl.loop(0, per_worker)
        def _(j):
            w = wid * per_worker + j
            pltpu.sync_copy(i_hbm.at[pl.ds(w, 1)], idx_v)
            pltpu.sync_copy(x_hbm.at[pl.ds(lax.rem(w, tiles) * SC_WINDOW, SC_WINDOW)], rows_v)
            pltpu.sync_copy(rows_v, o_hbm.at[idx_v.at[0]])

    return scatter(x, idx)


def _sc_gather_rows(table, idx):
    d = table.shape[1]
    windows = idx.shape[0]
    mesh = _sc_mesh()
    workers = mesh.num_cores * mesh.num_subcores
    per_worker = windows // workers

    @functools.partial(pl.kernel, out_type=jax.ShapeDtypeStruct((windows * SC_WINDOW, d), table.dtype), mesh=mesh,
                       scratch_types=[pltpu.VMEM((1, SC_WINDOW), I32), pltpu.VMEM((SC_WINDOW, d), table.dtype)])
    def gather(t_hbm, i_hbm, o_hbm, idx_v, rows_v):
        wid = lax.axis_index("subcore") * mesh.num_cores + lax.axis_index("core")

        @pl.loop(0, per_worker)
        def _(j):
            w = wid * per_worker + j
            pltpu.sync_copy(i_hbm.at[pl.ds(w, 1)], idx_v)
            pltpu.sync_copy(t_hbm.at[idx_v.at[0]], rows_v)
            pltpu.sync_copy(rows_v, o_hbm.at[pl.ds(w * SC_WINDOW, SC_WINDOW)])

    return gather(table, idx)


def _combine_kernel(x1_ref, mod_ref, rf_ref, ya_ref, yb_ref, o_ref):
    gate2 = mod_ref[0][5:6, :]
    rf = rf_ref[...]
    ya = _unpack_bf16_pairs(ya_ref[...]).astype(F32)
    yb = _unpack_bf16_pairs(yb_ref[...]).astype(F32)
    y = rf[:, 0:1] * ya + rf[:, 1:2] * yb
    o_ref[...] = x1_ref[...] + gate2 * y


def _combine(x1, mod3, rf, y12, tm, tiles_per_batch):
    n, d = x1.shape
    tiles = n // tm
    row = lambda i: (i, 0)
    return pl.pallas_call(
        _combine_kernel,
        grid=(tiles,),
        in_specs=[pl.BlockSpec((tm, d), row),
                  pl.BlockSpec((1, 6, d), lambda i: (i // tiles_per_batch, 0, 0)),
                  pl.BlockSpec((tm, LANES), row),
                  pl.BlockSpec((tm, d // 2), row), pl.BlockSpec((tm, d // 2), lambda i: (i + tiles, 0))],
        out_specs=pl.BlockSpec((tm, d), row),
        out_shape=jax.ShapeDtypeStruct((n, d), F32),
        compiler_params=_cparams(1),
    )(x1, mod3, rf, y12, y12)


def _pad_lanes(a, offset=0):
    return jnp.pad(a, ((0, 0), (offset, LANES - offset - a.shape[1])))


def _layer(x, c, layer, w_ada, b_ada, norm1_gain, w_in, da_q_norm, da_k_norm, lq1, lk1, lq2, lk2, da_out_norm,
           gdn_conv, gdn_a_log, gdn_dt_bias, gdn_out_norm, w_branch_a, w_branch_b, w_out, norm2_gain,
           w_group, b_group, w_router, b_router, w1, w3, w2):
    batch, seq, d = x.shape
    n = batch * seq
    tm = min(512, seq)
    tiles_per_batch = seq // tm
    t_attn = min(512, seq)
    lambda_init = 0.8 - 0.6 * math.exp(-0.3 * layer)

    mod = _ada(jnp.pad(c, ((0, 8 - batch), (0, 0))), w_ada, b_ada)[:batch]
    mod3 = mod.reshape(batch, 6, d)

    da_w = DA_HEADS * 2 * DA_QK_DIM
    dv_w = DA_HEADS * DA_V_DIM
    gd_w = GDN_HEADS * GDN_DIM
    o0 = 2 * da_w + dv_w
    o1 = o0 + 4 * gd_w
    o2 = o1 + 2 * GDN_HEADS
    w_da = w_in[:, :o0].astype(BF16)
    w_g = w_in[:, o0:o1].astype(BF16)
    w_ba = _pad_lanes(w_in[:, o1:o2]).astype(BF16)
    w_gate = w_in[:, o2:].astype(BF16)
    qn = jnp.tile(da_q_norm, 2 * DA_HEADS).reshape(1, da_w)
    kn = jnp.tile(da_k_norm, 2 * DA_HEADS).reshape(1, da_w)

    x2 = x.reshape(n, d)
    g1 = norm1_gain.reshape(1, d)
    q, k, v, gqkv, z, ba = _proj(x2, mod3, g1, w_da, w_g, w_ba, qn, kn, tm, tiles_per_batch)

    kt = k.reshape(batch, seq, DA_HEADS, 2 * DA_QK_DIM).transpose(0, 2, 3, 1)
    kt = jnp.concatenate([kt, jnp.broadcast_to(_alibi_key_rows(seq), kt.shape)], axis=2)
    o_a = _attention(q, kt, v, lq1.reshape(1, -1), lk1.reshape(1, -1), lq2.reshape(1, -1), lk2.reshape(1, -1),
                     da_out_norm.reshape(1, -1), batch, seq, t_attn, lambda_init)

    alog_row = _pad_lanes(gdn_a_log.reshape(1, -1), GDN_HEADS)
    dt_row = _pad_lanes(gdn_dt_bias.reshape(1, -1), GDN_HEADS)
    o_b = _gdn(gqkv.reshape(batch, seq, -1), z.reshape(batch, seq, -1), ba.reshape(batch, seq, -1),
               gdn_conv, alog_row, dt_row, gdn_out_norm.reshape(1, -1)).reshape(n, gd_w)

    w_rt = _pad_lanes(jnp.concatenate([w_router, w_group], axis=1))
    b_rt = _pad_lanes(jnp.concatenate([b_router, b_group]).reshape(1, -1))
    x1, h2, logits = _merge(x2, mod3, g1, norm2_gain.reshape(1, d), o_a, o_b, w_gate,
                            w_branch_a.astype(BF16), w_branch_b.astype(BF16), w_out.astype(BF16),
                            w_rt, b_rt, tm, tiles_per_batch)

    ri, rf, cnt = _route(logits, tm)

    counts = cnt[0, :N_EXPERTS].astype(I32)
    padded = (counts + MOE_ROWS - 1) // MOE_ROWS * MOE_ROWS
    pend = jnp.cumsum(padded)
    pstart = pend - padded
    cap = 2 * n + N_EXPERTS * MOE_ROWS
    nb = cap // MOE_ROWS
    blk_start = jnp.arange(nb, dtype=I32) * MOE_ROWS
    blk_expert = jnp.minimum(jnp.sum(pend[None, :] <= blk_start[:, None], axis=1), N_EXPERTS - 1).astype(I32)
    blk_rows = jnp.clip((pstart + counts)[blk_expert] - blk_start, 0, MOE_ROWS).astype(I32)
    blk_rows = jnp.where(blk_start < pend[-1], blk_rows, 0)
    dest = _dest(ri, _pad_lanes(pstart.reshape(1, -1)), tm)
    idx = dest[0:2].reshape(2 * n // SC_WINDOW, SC_WINDOW)

    xb = _sc_scatter_rows(h2, idx, cap)
    yb = _moe(blk_expert, blk_rows, xb, w1, w3, w2)
    y12 = _sc_gather_rows(yb, idx)
    out = _combine(x1, mod3, rf, y12, tm, tiles_per_batch)
    return out.reshape(batch, seq, d)


def kernel(x, c, w_ada, b_ada, norm1_gain, w_in, da_q_norm, da_k_norm, da_lambda_q1, da_lambda_k1, da_lambda_q2,
           da_lambda_k2, da_out_norm, gdn_conv, gdn_a_log, gdn_dt_bias, gdn_out_norm, w_branch_a, w_branch_b,
           w_out, norm2_gain, w_group, b_group, w_router, b_router, w1, w3, w2):
    for layer in range(w_ada.shape[0]):
        x = _layer(x, c, layer, w_ada[layer], b_ada[layer], norm1_gain[layer], w_in[layer], da_q_norm[layer],
                   da_k_norm[layer], da_lambda_q1[layer], da_lambda_k1[layer], da_lambda_q2[layer],
                   da_lambda_k2[layer], da_out_norm[layer], gdn_conv[layer], gdn_a_log[layer], gdn_dt_bias[layer],
                   gdn_out_norm[layer], w_branch_a[layer], w_branch_b[layer], w_out[layer], norm2_gain[layer],
                   w_group[layer], b_group[layer], w_router[layer], b_router[layer], w1[layer], w3[layer], w2[layer])
    return x
```

```python
import functools
import math

import jax
import jax.numpy as jnp
import numpy as np
from jax import lax
from jax.experimental import pallas as pl
from jax.experimental.pallas import tpu as pltpu
from jax.experimental.pallas import tpu_sc as plsc

F32 = jnp.float32
BF16 = jnp.bfloat16
I32 = jnp.int32

EPS = 1e-6
NEG_INF = -1e30
MASK_CHUNK = 64
LOG2E = 1.4426950408889634
ALIBI_ROWS = 3

DA_HEADS = 4
DA_QK_DIM = 64
DA_V_DIM = 128
GDN_HEADS = 4
GDN_DIM = 128
CONV_K = 4
N_GROUPS = 4
EXPERTS_PER_GROUP = 8
N_EXPERTS = N_GROUPS * EXPERTS_PER_GROUP
D_EXPERT = 512

LANES = 128
VMEM_LIMIT = 56 * 1024 * 1024

ATTN_UNROLL = 4
GDN_CHUNK = 128
GDN_BASE = 16
MOE_ROWS = 256
SC_WINDOW = 128


def _cparams(n_axes):
    return pltpu.CompilerParams(dimension_semantics=("arbitrary",) * n_axes,
                                vmem_limit_bytes=VMEM_LIMIT)


def _mm(a, b):
    return jnp.dot(a.astype(BF16), b.astype(BF16), preferred_element_type=F32)


def _mm_nt(a, b):
    return lax.dot_general(a.astype(BF16), b.astype(BF16), (((1,), (1,)), ((), ())),
                           preferred_element_type=F32)


def _split2(a):
    hi = a.astype(BF16)
    lo = (a - hi.astype(F32)).astype(BF16)
    return hi, lo


def _mm3(a, b):
    ah, al = _split2(a)
    bh, bl = _split2(b)
    out = jnp.dot(ah, bh, preferred_element_type=F32)
    out = out + jnp.dot(ah, bl, preferred_element_type=F32)
    out = out + jnp.dot(al, bh, preferred_element_type=F32)
    return out


def _mm_exact_lhs(a_bf16, b):
    b1 = b.astype(BF16)
    r1 = b - b1.astype(F32)
    b2 = r1.astype(BF16)
    b3 = (r1 - b2.astype(F32)).astype(BF16)
    out = jnp.dot(a_bf16, b1, preferred_element_type=F32)
    out = out + jnp.dot(a_bf16, b2, preferred_element_type=F32)
    out = out + jnp.dot(a_bf16, b3, preferred_element_type=F32)
    return out


def _sigmoid(x):
    return 1.0 / (1.0 + jnp.exp(-x))


def _silu(x):
    return x * _sigmoid(x)


def _softplus(x):
    return jnp.maximum(x, 0.0) + jnp.log1p(jnp.exp(-jnp.abs(x)))


def _pack_bf16_pairs(x):
    w = x.shape[1] // 2
    bits = lax.bitcast_convert_type(x.astype(BF16).astype(F32), jnp.uint32)
    lo = lax.shift_right_logical(bits[:, :w], jnp.uint32(16))
    hi = bits[:, w:] & jnp.uint32(0xFFFF0000)
    return lax.bitcast_convert_type(hi | lo, I32)


def _unpack_bf16_pairs(p):
    bits = lax.bitcast_convert_type(p, jnp.uint32)
    lo = lax.bitcast_convert_type(lax.shift_left(bits, jnp.uint32(16)), F32)
    hi = lax.bitcast_convert_type(bits & jnp.uint32(0xFFFF0000), F32)
    return jnp.concatenate([lo, hi], axis=1).astype(BF16)


def _rms(x, gain):
    return x * lax.rsqrt(jnp.mean(x * x, axis=-1, keepdims=True) + EPS) * gain


def _ada_kernel(c_ref, w_ref, b_ref, o_ref):
    sc = _silu(c_ref[...])
    o_ref[...] = _mm3(sc, w_ref[...]) + b_ref[...]


def _ada(c_pad, w_ada, b_ada):
    rows, d = c_pad.shape
    n = w_ada.shape[1]
    tn = d
    return pl.pallas_call(
        _ada_kernel,
        grid=(n // tn,),
        in_specs=[pl.BlockSpec((rows, d), lambda j: (0, 0)),
                  pl.BlockSpec((d, tn), lambda j: (0, j)),
                  pl.BlockSpec((1, tn), lambda j: (0, j))],
        out_specs=pl.BlockSpec((rows, tn), lambda j: (0, j)),
        out_shape=jax.ShapeDtypeStruct((rows, n), F32),
        compiler_params=_cparams(1),
    )(c_pad, w_ada, b_ada.reshape(1, n))


def _group_rms64(x, gain):
    tm, width = x.shape
    lane = lax.broadcasted_iota(I32, (tm, LANES), 1)
    low = lane < DA_QK_DIM
    parts = []
    for j in range(width // LANES):
        blk = x[:, j * LANES:(j + 1) * LANES]
        sq = blk * blk
        s_lo = jnp.sum(jnp.where(low, sq, 0.0), axis=-1, keepdims=True)
        s_hi = jnp.sum(jnp.where(low, 0.0, sq), axis=-1, keepdims=True)
        ms = jnp.where(low, s_lo, s_hi) * (1.0 / DA_QK_DIM)
        parts.append(blk * lax.rsqrt(ms + EPS))
    return jnp.concatenate(parts, axis=-1) * gain


def _proj_kernel(x_ref, mod_ref, g1_ref, wda_ref, wg_ref, wba_ref, qn_ref, kn_ref,
                 q_ref, k_ref, v_ref, gqkv_ref, z_ref, ba_ref):
    x = x_ref[...]
    mod = mod_ref[0]
    shift, scale = mod[0:1, :], mod[1:2, :]
    hb = (_rms(x, g1_ref[...]) * (1.0 + scale) + shift).astype(BF16)
    da_w = DA_HEADS * 2 * DA_QK_DIM
    da = jnp.dot(hb, wda_ref[...], preferred_element_type=F32)
    q = _group_rms64(da[:, :da_w], qn_ref[...]) * (DA_QK_DIM ** -0.5 * LOG2E)
    k = _group_rms64(da[:, da_w:2 * da_w], kn_ref[...])
    q_ref[...] = q.astype(BF16)
    k_ref[...] = k.astype(BF16)
    v_ref[...] = da[:, 2 * da_w:].astype(BF16)
    g = jnp.dot(hb, wg_ref[...], preferred_element_type=F32)
    conv_w = 3 * GDN_HEADS * GDN_DIM
    gqkv_ref[...] = g[:, :conv_w]
    z_ref[...] = g[:, conv_w:]
    ba_ref[...] = jnp.dot(hb, wba_ref[...], preferred_element_type=F32)


def _proj(x2, mod3, g1, w_da, w_g, w_ba, qn, kn, tm, tiles_per_batch):
    n, d = x2.shape
    da_w = DA_HEADS * 2 * DA_QK_DIM
    dv_w = DA_HEADS * DA_V_DIM
    conv_w = 3 * GDN_HEADS * GDN_DIM
    z_w = GDN_HEADS * GDN_DIM
    const = lambda i: (0, 0)
    row = lambda i: (i, 0)
    return pl.pallas_call(
        _proj_kernel,
        grid=(n // tm,),
        in_specs=[pl.BlockSpec((tm, d), row),
                  pl.BlockSpec((1, 6, d), lambda i: (i // tiles_per_batch, 0, 0)),
                  pl.BlockSpec((1, d), const),
                  pl.BlockSpec(w_da.shape, const),
                  pl.BlockSpec(w_g.shape, const),
                  pl.BlockSpec(w_ba.shape, const),
                  pl.BlockSpec((1, da_w), const),
                  pl.BlockSpec((1, da_w), const)],
        out_specs=[pl.BlockSpec((tm, da_w), row), pl.BlockSpec((tm, da_w), row),
                   pl.BlockSpec((tm, dv_w), row), pl.BlockSpec((tm, conv_w), row),
                   pl.BlockSpec((tm, z_w), row), pl.BlockSpec((tm, LANES), row)],
        out_shape=[jax.ShapeDtypeStruct((n, da_w), BF16), jax.ShapeDtypeStruct((n, da_w), BF16),
                   jax.ShapeDtypeStruct((n, dv_w), BF16), jax.ShapeDtypeStruct((n, conv_w), F32),
                   jax.ShapeDtypeStruct((n, z_w), F32), jax.ShapeDtypeStruct((n, LANES), F32)],
        compiler_params=_cparams(1),
    )(x2, mod3, g1, w_da, w_g, w_ba, qn, kn)


def _attn_kernel(slopes_ref, qa_ref, qb_ref, kt_ref, v_ref, lq1_ref, lk1_ref, lq2_ref, lk2_ref, on_ref,
                 oa_ref, ob_ref, qs_ref, m_ref, acc_ref, p_ref, ddiag_ref, *, t, nq, lambda_init):
    h = pl.program_id(1)
    i = pl.program_id(2)
    slope2 = slopes_ref[h]
    reps = t // LANES

    @pl.when(i == 0)
    def _tables():
        r = lax.broadcasted_iota(I32, (t, t), 0)
        c = lax.broadcasted_iota(I32, (t, t), 1)
        ahead = jnp.maximum(c - r, 0).astype(F32)
        allowed = (c // MASK_CHUNK) <= (r // MASK_CHUNK)
        ddiag_ref[...] = jnp.where(allowed, (-2.0 * slope2) * ahead, NEG_INF)

    for tile, q_ref in enumerate((qa_ref, qb_ref)):
        q = q_ref[...]
        lane = lax.broadcasted_iota(I32, q.shape, 1)
        zero = jnp.zeros_like(q)
        ones = jnp.where(lane < ALIBI_ROWS, 1.0, 0.0).astype(BF16)
        qs_ref[tile, 0:t, :] = jnp.concatenate([jnp.where(lane < DA_QK_DIM, q, zero), ones], axis=1)
        qs_ref[tile, t:2 * t, :] = jnp.concatenate([jnp.where(lane < DA_QK_DIM, zero, q), ones], axis=1)
    m_ref[...] = jnp.full(m_ref.shape, NEG_INF, F32)
    acc_ref[...] = jnp.zeros(acc_ref.shape, F32)

    def scores(tile, j):
        start = pl.multiple_of(j * t, t)
        return jnp.dot(qs_ref[tile], kt_ref[0, 0, :, pl.ds(start, t)], preferred_element_type=F32)

    def consume(s, tile, j, slot, diagonal):
        start = pl.multiple_of(j * t, t)
        alphas = []
        for half in range(2):
            rows = slice(half * t, (half + 1) * t)
            sb = s[rows] + ddiag_ref[...] if diagonal else s[rows]
            m_prev = m_ref[tile, rows]
            m_next = jnp.maximum(m_prev, jnp.max(sb, axis=-1, keepdims=True))
            alphas.append(jnp.exp2(m_prev - m_next))
            p_ref[slot, rows] = jnp.exp2(sb - jnp.tile(m_next, (1, reps))).astype(BF16)
            m_ref[tile, rows] = m_next
        v = v_ref[pl.ds(start, t), :]
        vext = jnp.concatenate([v, jnp.ones_like(v)], axis=1)
        pv = jnp.dot(p_ref[slot], vext, preferred_element_type=F32)
        for half in range(2):
            rows = slice(half * t, (half + 1) * t)
            acc_ref[tile, rows] = jnp.tile(alphas[half], (1, 2)) * acc_ref[tile, rows] + pv[rows]

    blocks = []
    for k in range(nq - 1):
        tile = jnp.where(k >= i, 1, 0)
        blocks.append((tile, k - tile * i, False))
    blocks.append((0, i, True))
    blocks.append((1, nq - 1 - i, True))
    s_next = scores(blocks[0][0], blocks[0][1])
    for k, (tile, j, diagonal) in enumerate(blocks):
        s_cur = s_next
        if k + 1 < len(blocks):
            s_next = scores(blocks[k + 1][0], blocks[k + 1][1])
        consume(s_cur, tile, j, k % 2, diagonal)

    lam = (jnp.exp(jnp.sum(lq1_ref[...] * lk1_ref[...], axis=-1, keepdims=True))
           - jnp.exp(jnp.sum(lq2_ref[...] * lk2_ref[...], axis=-1, keepdims=True)) + lambda_init)
    for tile, o_ref in enumerate((oa_ref, ob_ref)):
        acc = acc_ref[tile]
        o_all = acc[:, 0:DA_V_DIM] / acc[:, DA_V_DIM:2 * DA_V_DIM]
        o = o_all[0:t] - lam * o_all[t:2 * t]
        o = _rms(o, on_ref[...]) * (1.0 - lambda_init)
        o_ref[...] = o.astype(o_ref.dtype)


def _alibi_slopes_log2():
    return np.asarray([2.0 ** (-8.0 * (hh + 1) / DA_HEADS) * LOG2E for hh in range(DA_HEADS)], np.float32)


def _alibi_key_rows(seq):
    col = _alibi_slopes_log2()[:, None] * np.arange(seq, dtype=np.float32)[None, :]
    b1 = col.astype(BF16)
    r1 = col - b1.astype(np.float32)
    b2 = r1.astype(BF16)
    b3 = (r1 - b2.astype(np.float32)).astype(BF16)
    rows = np.zeros((DA_HEADS, LANES, seq), BF16)
    rows[:, 0], rows[:, 1], rows[:, 2] = b1, b2, b3
    return jnp.asarray(rows)


def _attention(q, kt, v, lq1, lk1, lq2, lk2, out_norm, batch, seq, t, lambda_init):
    nq = seq // t
    half = nq // 2
    width = DA_HEADS * DA_V_DIM
    slopes = jnp.asarray(_alibi_slopes_log2())
    vec = lambda b, h, i, sl: (0, 0)
    grid_spec = pltpu.PrefetchScalarGridSpec(
        num_scalar_prefetch=1,
        grid=(batch, DA_HEADS, half),
        in_specs=[pl.BlockSpec((t, LANES), lambda b, h, i, sl: (b * nq + i, h)),
                  pl.BlockSpec((t, LANES), lambda b, h, i, sl: (b * nq + nq - 1 - i, h)),
                  pl.BlockSpec((1, 1, 2 * LANES, seq), lambda b, h, i, sl: (b, h, 0, 0)),
                  pl.BlockSpec((seq, LANES), lambda b, h, i, sl: (b, h)),
                  pl.BlockSpec((1, DA_QK_DIM), vec), pl.BlockSpec((1, DA_QK_DIM), vec),
                  pl.BlockSpec((1, DA_QK_DIM), vec), pl.BlockSpec((1, DA_QK_DIM), vec),
                  pl.BlockSpec((1, DA_V_DIM), vec)],
        out_specs=[pl.BlockSpec((t, LANES), lambda b, h, i, sl: (b * half + i, h)),
                   pl.BlockSpec((t, LANES), lambda b, h, i, sl: (b * half + half - 1 - i, h))],
        scratch_shapes=[pltpu.VMEM((2, 2 * t, 2 * LANES), BF16),
                        pltpu.VMEM((2, 2 * t, LANES), F32),
                        pltpu.VMEM((2, 2 * t, 2 * DA_V_DIM), F32),
                        pltpu.VMEM((2, 2 * t, t), BF16),
                        pltpu.VMEM((t, t), F32)])
    o_lo, o_hi = pl.pallas_call(
        functools.partial(_attn_kernel, t=t, nq=nq, lambda_init=lambda_init),
        grid_spec=grid_spec,
        out_shape=[jax.ShapeDtypeStruct((batch * seq // 2, width), BF16)] * 2,
        compiler_params=_cparams(3),
    )(slopes, q, q, kt, v, lq1, lk1, lq2, lk2, out_norm)
    both = jnp.concatenate([o_lo.reshape(batch, seq // 2, width), o_hi.reshape(batch, seq // 2, width)], axis=1)
    return both.reshape(batch * seq, width)


def _gdn_kernel(u_ref, z_ref, ba_ref, cw_ref, alog_ref, dt_ref, on_ref, o_ref, stage_ref, state_ref, *, batch):
    c = GDN_CHUNK
    hd = GDN_HEADS * GDN_DIM
    step = pl.program_id(0)

    @pl.when(step == 0)
    def _init():
        state_ref[...] = jnp.zeros(state_ref.shape, F32)
        stage_ref[:, 0:8, :] = jnp.zeros((batch, 8, stage_ref.shape[2]), F32)

    row = lax.broadcasted_iota(I32, (c, c), 0)
    col = lax.broadcasted_iota(I32, (c, c), 1)
    tri = row >= col
    strict = row > col
    tril_ones = jnp.where(tri, 1.0, 0.0).astype(BF16)

    chains = []
    for b in range(batch):
        stage_ref[b, 8:8 + c, :] = u_ref[b]
        y = cw_ref[0:1, :] * stage_ref[b, 5:5 + c, :]
        for tap in range(1, CONV_K):
            y = y + cw_ref[tap:tap + 1, :] * stage_ref[b, 5 + tap:5 + tap + c, :]
        stage_ref[b, 0:8, :] = stage_ref[b, c:c + 8, :]
        y = _silu(y)

        ba = ba_ref[b]
        beta_all = _sigmoid(ba)
        g_all = -jnp.exp(alog_ref[...]) * _softplus(ba + dt_ref[...])
        gcum_all = _mm_exact_lhs(tril_ones, g_all)
        gcum_t = gcum_all.T

        for h in range(GDN_HEADS):
            q = y[:, h * GDN_DIM:(h + 1) * GDN_DIM]
            k = y[:, hd + h * GDN_DIM:hd + (h + 1) * GDN_DIM]
            v = y[:, 2 * hd + h * GDN_DIM:2 * hd + (h + 1) * GDN_DIM]
            q = q * lax.rsqrt(jnp.sum(q * q, axis=-1, keepdims=True) + EPS) * (GDN_DIM ** -0.5)
            k = k * lax.rsqrt(jnp.sum(k * k, axis=-1, keepdims=True) + EPS)
            beta = beta_all[:, h:h + 1]
            gc = gcum_all[:, GDN_HEADS + h:GDN_HEADS + h + 1]
            gr = gcum_t[GDN_HEADS + h:GDN_HEADS + h + 1, :]
            g_last = gc[c - 1:c, :]
            decay = jnp.where(tri, jnp.exp(jnp.where(tri, gc - gr, 0.0)), 0.0)
            e_gc = jnp.exp(gc)
            chains.append(dict(
                b=b, h=h, decay=decay, g_last=g_last,
                p=jnp.where(strict, -(beta * _mm_nt(k, k) * decay), 0.0),
                rhs=jnp.concatenate([v * beta, k * (beta * e_gc)], axis=1),
                qk=_mm_nt(q, k) * decay, q_dec=q * e_gc, k_dec_t=(k * jnp.exp(g_last - gc)).T))

    def same_block(size):
        return (row // size) == (col // size)

    for ch in chains:
        ch["pk"] = jnp.where(same_block(GDN_BASE), ch["p"], 0.0)
        ch["x"] = ch["pk"]
    for _ in range(int(math.log2(GDN_BASE)) - 1):
        for ch in chains:
            ch["pk"] = _mm(ch["pk"], ch["pk"])
        for ch in chains:
            ch["x"] = ch["x"] + ch["pk"] + _mm(ch["x"], ch["pk"])
    size = GDN_BASE
    while size < c:
        pair_only = same_block(2 * size) & jnp.logical_not(same_block(size))
        for ch in chains:
            e = jnp.where(pair_only, -ch["p"], 0.0)
            ch["y"] = e + _mm(ch["x"], e)
        for ch in chains:
            ch["x"] = ch["x"] - (ch["y"] + _mm(ch["y"], ch["x"]))
        size *= 2

    outs = [[None] * GDN_HEADS for _ in range(batch)]
    new_states = []
    for ch in chains:
        sol = ch["rhs"] + _mm(ch["x"], ch["rhs"])
        u_c, w_c = sol[:, :GDN_DIM], sol[:, GDN_DIM:]
        s_prev = state_ref[ch["b"] * GDN_HEADS + ch["h"]]
        v_new = u_c - _mm(w_c, s_prev)
        o = _mm(ch["q_dec"], s_prev) + _mm(ch["qk"], v_new)
        new_states.append(s_prev * jnp.exp(ch["g_last"]) + _mm(ch["k_dec_t"], v_new))
        zg = z_ref[ch["b"], :, ch["h"] * GDN_DIM:(ch["h"] + 1) * GDN_DIM]
        outs[ch["b"]][ch["h"]] = (_rms(o, on_ref[...]) * _silu(zg)).astype(o_ref.dtype)
    o_ref[...] = jnp.stack([jnp.concatenate(heads, axis=1) for heads in outs], axis=0)
    state_ref[...] = jnp.stack(new_states, axis=0)


def _gdn(gqkv3, z3, ba3, conv_w, alog_row, dt_row, out_norm):
    batch, seq, cw = gqkv3.shape
    c = GDN_CHUNK
    hd = GDN_HEADS * GDN_DIM
    blk = lambda s: (0, s, 0)
    const = lambda s: (0, 0)
    return pl.pallas_call(
        functools.partial(_gdn_kernel, batch=batch),
        grid=(seq // c,),
        in_specs=[pl.BlockSpec((batch, c, cw), blk),
                  pl.BlockSpec((batch, c, hd), blk),
                  pl.BlockSpec((batch, c, LANES), blk),
                  pl.BlockSpec((CONV_K, cw), const),
                  pl.BlockSpec((1, LANES), const),
                  pl.BlockSpec((1, LANES), const),
                  pl.BlockSpec((1, GDN_DIM), const)],
        out_specs=pl.BlockSpec((batch, c, hd), blk),
        out_shape=jax.ShapeDtypeStruct((batch, seq, hd), BF16),
        scratch_shapes=[pltpu.VMEM((batch, c + 8, cw), F32),
                        pltpu.VMEM((batch * GDN_HEADS, GDN_DIM, GDN_DIM), F32)],
        compiler_params=_cparams(1),
    )(gqkv3, z3, ba3, conv_w, alog_row, dt_row, out_norm)


def _merge_kernel(x_ref, mod_ref, g1_ref, g2_ref, oa_ref, ob_ref, wgate_ref, wa_ref, wb_ref, wout_ref,
                  wrt_ref, brt_ref, x1_ref, h2_ref, lg_ref):
    x = x_ref[...]
    d = x.shape[1]
    mod = mod_ref[0]
    shift1, scale1, gate1 = mod[0:1, :], mod[1:2, :], mod[2:3, :]
    shift2, scale2 = mod[3:4, :], mod[4:5, :]
    hb = (_rms(x, g1_ref[...]) * (1.0 + scale1) + shift1).astype(BF16)
    gates = _sigmoid(jnp.dot(hb, wgate_ref[...], preferred_element_type=F32))
    ya = jnp.dot(oa_ref[...], wa_ref[...], preferred_element_type=F32)
    yb = jnp.dot(ob_ref[...], wb_ref[...], preferred_element_type=F32)
    merged = gates[:, :d] * ya + gates[:, d:] * yb
    x1 = x + gate1 * _mm(merged, wout_ref[...])
    x1_ref[...] = x1
    h2 = _rms(x1, g2_ref[...]) * (1.0 + scale2) + shift2
    h2_ref[...] = _pack_bf16_pairs(h2)
    lg_ref[...] = _mm3(h2, wrt_ref[...]) + brt_ref[...]


def _merge(x2, mod3, g1, g2, oa, ob, w_gate, w_a, w_b, w_out, w_rt, b_rt, tm, tiles_per_batch):
    n, d = x2.shape
    const = lambda i: (0, 0)
    row = lambda i: (i, 0)
    return pl.pallas_call(
        _merge_kernel,
        grid=(n // tm,),
        in_specs=[pl.BlockSpec((tm, d), row),
                  pl.BlockSpec((1, 6, d), lambda i: (i // tiles_per_batch, 0, 0)),
                  pl.BlockSpec((1, d), const), pl.BlockSpec((1, d), const),
                  pl.BlockSpec((tm, oa.shape[1]), row), pl.BlockSpec((tm, ob.shape[1]), row),
                  pl.BlockSpec(w_gate.shape, const), pl.BlockSpec(w_a.shape, const),
                  pl.BlockSpec(w_b.shape, const), pl.BlockSpec(w_out.shape, const),
                  pl.BlockSpec(w_rt.shape, const), pl.BlockSpec((1, LANES), const)],
        out_specs=[pl.BlockSpec((tm, d), row), pl.BlockSpec((tm, d // 2), row), pl.BlockSpec((tm, LANES), row)],
        out_shape=[jax.ShapeDtypeStruct((n, d), F32), jax.ShapeDtypeStruct((n, d // 2), I32),
                   jax.ShapeDtypeStruct((n, LANES), F32)],
        compiler_params=_cparams(1),
    )(x2, mod3, g1, g2, oa, ob, w_gate, w_a, w_b, w_out, w_rt, b_rt)


def _route_kernel(lg_ref, ri_ref, rf_ref, cnt_ref, carry_ref):
    step = pl.program_id(0)
    tm = lg_ref.shape[0]

    @pl.when(step == 0)
    def _init():
        carry_ref[...] = jnp.zeros(carry_ref.shape, F32)

    lg = lg_ref[...]
    lane = lax.broadcasted_iota(I32, lg.shape, 1)
    big = jnp.int32(LANES)
    is_grp = (lane >= N_EXPERTS) & (lane < N_EXPERTS + N_GROUPS)
    gl = jnp.where(is_grp, lg, NEG_INF)
    gmax = jnp.max(gl, axis=-1, keepdims=True)
    g_top = jnp.min(jnp.where(is_grp & (gl == gmax), lane, big), axis=-1, keepdims=True) - N_EXPERTS
    g_top_p = 1.0 / jnp.sum(jnp.where(is_grp, jnp.exp(gl - gmax), 0.0), axis=-1, keepdims=True)
    in_grp = (lane < N_EXPERTS) & ((lane // EXPERTS_PER_GROUP) == g_top)
    el = jnp.where(in_grp, lg, NEG_INF)
    v1 = jnp.max(el, axis=-1, keepdims=True)
    e1 = jnp.min(jnp.where(in_grp & (el == v1), lane, big), axis=-1, keepdims=True)
    rest = in_grp & (lane != e1)
    el2 = jnp.where(rest, lg, NEG_INF)
    v2 = jnp.max(el2, axis=-1, keepdims=True)
    e2 = jnp.min(jnp.where(rest & (el2 == v2), lane, big), axis=-1, keepdims=True)
    ex = jnp.exp(v2 - v1)
    w1 = g_top_p / (1.0 + ex)
    w2 = g_top_p * ex / (1.0 + ex)

    oh1 = lane == e1
    oh2 = lane == e2
    ohs = (jnp.where(oh1, 1.0, 0.0) + jnp.where(oh2, 1.0, 0.0)).astype(BF16)
    r = lax.broadcasted_iota(I32, (tm, tm), 0)
    c = lax.broadcasted_iota(I32, (tm, tm), 1)
    before = jnp.where(r > c, 1.0, 0.0).astype(BF16)
    prior = jnp.dot(before, ohs, preferred_element_type=F32) + carry_ref[0:1, :]
    rank1 = jnp.sum(jnp.where(oh1, prior, 0.0), axis=-1, keepdims=True)
    rank2 = jnp.sum(jnp.where(oh2, prior, 0.0), axis=-1, keepdims=True)
    carry_ref[...] = carry_ref[...] + jnp.sum(ohs.astype(F32), axis=0, keepdims=True)
    cnt_ref[...] = carry_ref[...]

    r1 = rank1.astype(I32)
    r2 = rank2.astype(I32)
    ri = jnp.where(lane == 0, e1, jnp.where(lane == 1, e2, jnp.where(lane == 2, r1, jnp.where(lane == 3, r2, 0))))
    ri_ref[...] = ri
    rf_ref[...] = jnp.where(lane == 0, w1, jnp.where(lane == 1, w2, 0.0))


def _route(logits, tm):
    n = logits.shape[0]
    row = lambda i: (i, 0)
    return pl.pallas_call(
        _route_kernel,
        grid=(n // tm,),
        in_specs=[pl.BlockSpec((tm, LANES), row)],
        out_specs=[pl.BlockSpec((tm, LANES), row), pl.BlockSpec((tm, LANES), row),
                   pl.BlockSpec((8, LANES), lambda i: (0, 0))],
        out_shape=[jax.ShapeDtypeStruct((n, LANES), I32), jax.ShapeDtypeStruct((n, LANES), F32),
                   jax.ShapeDtypeStruct((8, LANES), F32)],
        scratch_shapes=[pltpu.VMEM((8, LANES), F32)],
        compiler_params=_cparams(1),
    )(logits)


def _moe_kernel(be_ref, act_ref, xb_ref, w1_ref, w3_ref, w2_ref, o_ref, w1b_ref, w3b_ref, w2b_ref):
    i = pl.program_id(0)
    prev = be_ref[jnp.maximum(i - 1, 0)]

    @pl.when((i == 0) | (be_ref[i] != prev))
    def _cast_weights():
        w1b_ref[...] = w1_ref[...].astype(BF16)
        w3b_ref[...] = w3_ref[...].astype(BF16)
        w2b_ref[...] = w2_ref[...].astype(BF16)

    @pl.when(act_ref[i] > 0)
    def _compute():
        xp = xb_ref[...]
        rows = lax.broadcasted_iota(I32, xp.shape, 0)
        xb = _unpack_bf16_pairs(jnp.where(rows < act_ref[i], xp, 0))
        a = jnp.dot(xb, w1b_ref[...], preferred_element_type=F32)
        g = jnp.dot(xb, w3b_ref[...], preferred_element_type=F32)
        hid = (_silu(a) * g).astype(BF16)
        o_ref[...] = _pack_bf16_pairs(jnp.dot(hid, w2b_ref[...], preferred_element_type=F32))

    @pl.when(act_ref[i] == 0)
    def _idle():
        o_ref[...] = jnp.zeros(o_ref.shape, o_ref.dtype)


def _moe(blk_expert, blk_rows, xb, w1, w3, w2):
    cap = xb.shape[0]
    d = w1.shape[1]
    nb = cap // MOE_ROWS
    de = w1.shape[2]
    grid_spec = pltpu.PrefetchScalarGridSpec(
        num_scalar_prefetch=2,
        grid=(nb,),
        in_specs=[pl.BlockSpec((MOE_ROWS, d // 2), lambda i, be, act: (i, 0)),
                  pl.BlockSpec((None, d, de), lambda i, be, act: (be[i], 0, 0)),
                  pl.BlockSpec((None, d, de), lambda i, be, act: (be[i], 0, 0)),
                  pl.BlockSpec((None, de, d), lambda i, be, act: (be[i], 0, 0))],
        out_specs=pl.BlockSpec((MOE_ROWS, d // 2), lambda i, be, act: (i, 0)),
        scratch_shapes=[pltpu.VMEM((d, de), BF16), pltpu.VMEM((d, de), BF16), pltpu.VMEM((de, d), BF16)])
    return pl.pallas_call(
        _moe_kernel,
        grid_spec=grid_spec,
        out_shape=jax.ShapeDtypeStruct((cap, d // 2), I32),
        compiler_params=_cparams(1),
    )(blk_expert, blk_rows, xb, w1, w3, w2)


def _dest_kernel(ri_ref, pstart_ref, o_ref):
    ri = ri_ref[...]
    lane = lax.broadcasted_iota(I32, ri.shape, 1)
    pstart = pstart_ref[...]
    base1 = jnp.sum(jnp.where(lane == ri[:, 0:1], pstart, 0), axis=-1, keepdims=True)
    base2 = jnp.sum(jnp.where(lane == ri[:, 1:2], pstart, 0), axis=-1, keepdims=True)
    dest = jnp.where(lane == 0, base1 + ri[:, 2:3], jnp.where(lane == 1, base2 + ri[:, 3:4], 0))
    o_ref[...] = dest.astype(F32).T[0:8, :].astype(I32)


def _dest(ri, pstart_row, tm):
    n = ri.shape[0]
    return pl.pallas_call(
        _dest_kernel,
        grid=(n // tm,),
        in_specs=[pl.BlockSpec((tm, LANES), lambda i: (i, 0)), pl.BlockSpec((1, LANES), lambda i: (0, 0))],
        out_specs=pl.BlockSpec((8, tm), lambda i: (0, i)),
        out_shape=jax.ShapeDtypeStruct((8, n), I32),
        compiler_params=_cparams(1),
    )(ri, pstart_row)


def _sc_mesh():
    return plsc.VectorSubcoreMesh(core_axis_name="core", subcore_axis_name="subcore")


def _sc_scatter_rows(x, idx, cap):
    n, d = x.shape
    windows = idx.shape[0]
    tiles = n // SC_WINDOW
    mesh = _sc_mesh()
    workers = mesh.num_cores * mesh.num_subcores
    per_worker = windows // workers

    @functools.partial(pl.kernel, out_type=jax.ShapeDtypeStruct((cap, d), x.dtype), mesh=mesh,
                       scratch_types=[pltpu.VMEM((1, SC_WINDOW), I32), pltpu.VMEM((SC_WINDOW, d), x.dtype)])
    def scatter(x_hbm, i_hbm, o_hbm, idx_v, rows_v):
        wid = lax.axis_index("subcore") * mesh.num_cores + lax.axis_index("core")

        @pl.loop(0, per_worker)
        def _(j):
            w = wid * per_worker + j
            pltpu.sync_copy(i_hbm.at[pl.ds(w, 1)], idx_v)
            pltpu.sync_copy(x_hbm.at[pl.ds(lax.rem(w, tiles) * SC_WINDOW, SC_WINDOW)], rows_v)
            pltpu.sync_copy(rows_v, o_hbm.at[idx_v.at[0]])

    return scatter(x, idx)


def _sc_gather_rows(table, idx):
    d = table.shape[1]
    windows = idx.shape[0]
    mesh = _sc_mesh()
    workers = mesh.num_cores * mesh.num_subcores
    per_worker = windows // workers

    @functools.partial(pl.kernel, out_type=jax.ShapeDtypeStruct((windows * SC_WINDOW, d), table.dtype), mesh=mesh,
                       scratch_types=[pltpu.VMEM((1, SC_WINDOW), I32), pltpu.VMEM((SC_WINDOW, d), table.dtype)])
    def gather(t_hbm, i_hbm, o_hbm, idx_v, rows_v):
        wid = lax.axis_index("subcore") * mesh.num_cores + lax.axis_index("core")

        @pl.loop(0, per_worker)
        def _(j):
            w = wid * per_worker + j
            pltpu.sync_copy(i_hbm.at[pl.ds(w, 1)], idx_v)
            pltpu.sync_copy(t_hbm.at[idx_v.at[0]], rows_v)
            pltpu.sync_copy(rows_v, o_hbm.at[pl.ds(w * SC_WINDOW, SC_WINDOW)])

    return gather(table, idx)


def _combine_kernel(x1_ref, mod_ref, rf_ref, ya_ref, yb_ref, o_ref):
    gate2 = mod_ref[0][5:6, :]
    rf = rf_ref[...]
    ya = _unpack_bf16_pairs(ya_ref[...]).astype(F32)
    yb = _unpack_bf16_pairs(yb_ref[...]).astype(F32)
    y = rf[:, 0:1] * ya + rf[:, 1:2] * yb
    o_ref[...] = x1_ref[...] + gate2 * y


def _combine(x1, mod3, rf, y12, tm, tiles_per_batch):
    n, d = x1.shape
    tiles = n // tm
    row = lambda i: (i, 0)
    return pl.pallas_call(
        _combine_kernel,
        grid=(tiles,),
        in_specs=[pl.BlockSpec((tm, d), row),
                  pl.BlockSpec((1, 6, d), lambda i: (i // tiles_per_batch, 0, 0)),
                  pl.BlockSpec((tm, LANES), row),
                  pl.BlockSpec((tm, d // 2), row), pl.BlockSpec((tm, d // 2), lambda i: (i + tiles, 0))],
        out_specs=pl.BlockSpec((tm, d), row),
        out_shape=jax.ShapeDtypeStruct((n, d), F32),
        compiler_params=_cparams(1),
    )(x1, mod3, rf, y12, y12)


def _pad_lanes(a, offset=0):
    return jnp.pad(a, ((0, 0), (offset, LANES - offset - a.shape[1])))


def _layer(x, c, layer, w_ada, b_ada, norm1_gain, w_in, da_q_norm, da_k_norm, lq1, lk1, lq2, lk2, da_out_norm,
           gdn_conv, gdn_a_log, gdn_dt_bias, gdn_out_norm, w_branch_a, w_branch_b, w_out, norm2_gain,
           w_group, b_group, w_router, b_router, w1, w3, w2):
    batch, seq, d = x.shape
    n = batch * seq
    tm = min(512, seq)
    tiles_per_batch = seq // tm
    t_attn = min(512, seq)
    lambda_init = 0.8 - 0.6 * math.exp(-0.3 * layer)

    mod = _ada(jnp.pad(c, ((0, 8 - batch), (0, 0))), w_ada, b_ada)[:batch]
    mod3 = mod.reshape(batch, 6, d)

    da_w = DA_HEADS * 2 * DA_QK_DIM
    dv_w = DA_HEADS * DA_V_DIM
    gd_w = GDN_HEADS * GDN_DIM
    o0 = 2 * da_w + dv_w
    o1 = o0 + 4 * gd_w
    o2 = o1 + 2 * GDN_HEADS
    w_da = w_in[:, :o0].astype(BF16)
    w_g = w_in[:, o0:o1].astype(BF16)
    w_ba = _pad_lanes(w_in[:, o1:o2]).astype(BF16)
    w_gate = w_in[:, o2:].astype(BF16)
    qn = jnp.tile(da_q_norm, 2 * DA_HEADS).reshape(1, da_w)
    kn = jnp.tile(da_k_norm, 2 * DA_HEADS).reshape(1, da_w)

    x2 = x.reshape(n, d)
    g1 = norm1_gain.reshape(1, d)
    q, k, v, gqkv, z, ba = _proj(x2, mod3, g1, w_da, w_g, w_ba, qn, kn, tm, tiles_per_batch)

    kt = k.reshape(batch, seq, DA_HEADS, 2 * DA_QK_DIM).transpose(0, 2, 3, 1)
    kt = jnp.concatenate([kt, jnp.broadcast_to(_alibi_key_rows(seq), kt.shape)], axis=2)
    o_a = _attention(q, kt, v, lq1.reshape(1, -1), lk1.reshape(1, -1), lq2.reshape(1, -1), lk2.reshape(1, -1),
                     da_out_norm.reshape(1, -1), batch, seq, t_attn, lambda_init)

    alog_row = _pad_lanes(gdn_a_log.reshape(1, -1), GDN_HEADS)
    dt_row = _pad_lanes(gdn_dt_bias.reshape(1, -1), GDN_HEADS)
    o_b = _gdn(gqkv.reshape(batch, seq, -1), z.reshape(batch, seq, -1), ba.reshape(batch, seq, -1),
               gdn_conv, alog_row, dt_row, gdn_out_norm.reshape(1, -1)).reshape(n, gd_w)

    w_rt = _pad_lanes(jnp.concatenate([w_router, w_group], axis=1))
    b_rt = _pad_lanes(jnp.concatenate([b_router, b_group]).reshape(1, -1))
    x1, h2, logits = _merge(x2, mod3, g1, norm2_gain.reshape(1, d), o_a, o_b, w_gate,
                            w_branch_a.astype(BF16), w_branch_b.astype(BF16), w_out.astype(BF16),
                            w_rt, b_rt, tm, tiles_per_batch)

    ri, rf, cnt = _route(logits, tm)

    counts = cnt[0, :N_EXPERTS].astype(I32)
    padded = (counts + MOE_ROWS - 1) // MOE_ROWS * MOE_ROWS
    pend = jnp.cumsum(padded)
    pstart = pend - padded
    cap = 2 * n + N_EXPERTS * MOE_ROWS
    nb = cap // MOE_ROWS
    blk_start = jnp.arange(nb, dtype=I32) * MOE_ROWS
    blk_expert = jnp.minimum(jnp.sum(pend[None, :] <= blk_start[:, None], axis=1), N_EXPERTS - 1).astype(I32)
    blk_rows = jnp.clip((pstart + counts)[blk_expert] - blk_start, 0, MOE_ROWS).astype(I32)
    blk_rows = jnp.where(blk_start < pend[-1], blk_rows, 0)
    dest = _dest(ri, _pad_lanes(pstart.reshape(1, -1)), tm)
    idx = dest[0:2].reshape(2 * n // SC_WINDOW, SC_WINDOW)

    xb = _sc_scatter_rows(h2, idx, cap)
    yb = _moe(blk_expert, blk_rows, xb, w1, w3, w2)
    y12 = _sc_gather_rows(yb, idx)
    out = _combine(x1, mod3, rf, y12, tm, tiles_per_batch)
    return out.reshape(batch, seq, d)


def kernel(x, c, w_ada, b_ada, norm1_gain, w_in, da_q_norm, da_k_norm, da_lambda_q1, da_lambda_k1, da_lambda_q2,
           da_lambda_k2, da_out_norm, gdn_conv, gdn_a_log, gdn_dt_bias, gdn_out_norm, w_branch_a, w_branch_b,
           w_out, norm2_gain, w_group, b_group, w_router, b_router, w1, w3, w2):
    for layer in range(w_ada.shape[0]):
        x = _layer(x, c, layer, w_ada[layer], b_ada[layer], norm1_gain[layer], w_in[layer], da_q_norm[layer],
                   da_k_norm[layer], da_lambda_q1[layer], da_lambda_k1[layer], da_lambda_q2[layer],
                   da_lambda_k2[layer], da_out_norm[layer], gdn_conv[layer], gdn_a_log[layer], gdn_dt_bias[layer],
                   gdn_out_norm[layer], w_branch_a[layer], w_branch_b[layer], w_out[layer], norm2_gain[layer],
                   w_group[layer], b_group[layer], w_router[layer], b_router[layer], w1[layer], w3[layer], w2[layer])
    return x
```

```python
import functools
import math

import jax
import jax.numpy as jnp
import numpy as np
from jax import lax
from jax.experimental import pallas as pl
from jax.experimental.pallas import tpu as pltpu
from jax.experimental.pallas import tpu_sc as plsc

F32 = jnp.float32
BF16 = jnp.bfloat16
I32 = jnp.int32

EPS = 1e-6
NEG_INF = -1e30
MASK_CHUNK = 64
LOG2E = 1.4426950408889634
ALIBI_ROWS = 3

DA_HEADS = 4
DA_QK_DIM = 64
DA_V_DIM = 128
GDN_HEADS = 4
GDN_DIM = 128
CONV_K = 4
N_GROUPS = 4
EXPERTS_PER_GROUP = 8
N_EXPERTS = N_GROUPS * EXPERTS_PER_GROUP
D_EXPERT = 512

LANES = 128
VMEM_LIMIT = 56 * 1024 * 1024

ATTN_UNROLL = 4
GDN_CHUNK = 128
GDN_BASE = 16
MOE_ROWS = 512
MOE_SUB = 256
SC_WINDOW = 128


def _cparams(n_axes):
    return pltpu.CompilerParams(dimension_semantics=("arbitrary",) * n_axes,
                                vmem_limit_bytes=VMEM_LIMIT)


def _mm(a, b):
    return jnp.dot(a.astype(BF16), b.astype(BF16), preferred_element_type=F32)


def _mm_nt(a, b):
    return lax.dot_general(a.astype(BF16), b.astype(BF16), (((1,), (1,)), ((), ())),
                           preferred_element_type=F32)


def _split2(a):
    hi = a.astype(BF16)
    lo = (a - hi.astype(F32)).astype(BF16)
    return hi, lo


def _mm3(a, b):
    ah, al = _split2(a)
    bh, bl = _split2(b)
    out = jnp.dot(ah, bh, preferred_element_type=F32)
    out = out + jnp.dot(ah, bl, preferred_element_type=F32)
    out = out + jnp.dot(al, bh, preferred_element_type=F32)
    return out


def _mm_exact_lhs(a_bf16, b):
    b1 = b.astype(BF16)
    r1 = b - b1.astype(F32)
    b2 = r1.astype(BF16)
    b3 = (r1 - b2.astype(F32)).astype(BF16)
    out = jnp.dot(a_bf16, b1, preferred_element_type=F32)
    out = out + jnp.dot(a_bf16, b2, preferred_element_type=F32)
    out = out + jnp.dot(a_bf16, b3, preferred_element_type=F32)
    return out


def _sigmoid(x):
    return 1.0 / (1.0 + jnp.exp(-x))


def _silu(x):
    return x * _sigmoid(x)


def _softplus(x):
    return jnp.maximum(x, 0.0) + jnp.log(1.0 + jnp.exp(-jnp.abs(x)))


def _pack_bf16_pairs(x):
    w = x.shape[1] // 2
    bits = lax.bitcast_convert_type(x.astype(BF16).astype(F32), jnp.uint32)
    lo = lax.shift_right_logical(bits[:, :w], jnp.uint32(16))
    hi = bits[:, w:] & jnp.uint32(0xFFFF0000)
    return lax.bitcast_convert_type(hi | lo, I32)


def _unpack_bf16_pairs(p):
    bits = lax.bitcast_convert_type(p, jnp.uint32)
    lo = lax.bitcast_convert_type(lax.shift_left(bits, jnp.uint32(16)), F32)
    hi = lax.bitcast_convert_type(bits & jnp.uint32(0xFFFF0000), F32)
    return jnp.concatenate([lo, hi], axis=1).astype(BF16)


def _rms(x, gain):
    return x * lax.rsqrt(jnp.mean(x * x, axis=-1, keepdims=True) + EPS) * gain


def _ada_kernel(c_ref, w_ref, b_ref, o_ref):
    sc = _silu(c_ref[...])
    o_ref[...] = _mm3(sc, w_ref[...]) + b_ref[...]


def _ada(c_pad, w_ada, b_ada):
    rows, d = c_pad.shape
    n = w_ada.shape[1]
    tn = d
    return pl.pallas_call(
        _ada_kernel,
        grid=(n // tn,),
        in_specs=[pl.BlockSpec((rows, d), lambda j: (0, 0)),
                  pl.BlockSpec((d, tn), lambda j: (0, j)),
                  pl.BlockSpec((1, tn), lambda j: (0, j))],
        out_specs=pl.BlockSpec((rows, tn), lambda j: (0, j)),
        out_shape=jax.ShapeDtypeStruct((rows, n), F32),
        compiler_params=_cparams(1),
    )(c_pad, w_ada, b_ada.reshape(1, n))


def _group_rms64(x, gain):
    tm, width = x.shape
    lane = lax.broadcasted_iota(I32, (tm, LANES), 1)
    low = lane < DA_QK_DIM
    parts = []
    for j in range(width // LANES):
        blk = x[:, j * LANES:(j + 1) * LANES]
        sq = blk * blk
        s_lo = jnp.sum(jnp.where(low, sq, 0.0), axis=-1, keepdims=True)
        s_hi = jnp.sum(jnp.where(low, 0.0, sq), axis=-1, keepdims=True)
        ms = jnp.where(low, s_lo, s_hi) * (1.0 / DA_QK_DIM)
        parts.append(blk * lax.rsqrt(ms + EPS))
    return jnp.concatenate(parts, axis=-1) * gain


def _proj_kernel(x_ref, mod_ref, g1_ref, wda_ref, wg_ref, wba_ref, qn_ref, kn_ref,
                 q_ref, k_ref, v_ref, gqkv_ref, z_ref, ba_ref):
    x = x_ref[...]
    mod = mod_ref[0]
    shift, scale = mod[0:1, :], mod[1:2, :]
    hb = (_rms(x, g1_ref[...]) * (1.0 + scale) + shift).astype(BF16)
    da_w = DA_HEADS * 2 * DA_QK_DIM
    da = jnp.dot(hb, wda_ref[...], preferred_element_type=F32)
    q = _group_rms64(da[:, :da_w], qn_ref[...]) * (DA_QK_DIM ** -0.5 * LOG2E)
    k = _group_rms64(da[:, da_w:2 * da_w], kn_ref[...])
    q_ref[...] = q.astype(BF16)
    k_ref[...] = k.astype(BF16)
    v_ref[...] = da[:, 2 * da_w:].astype(BF16)
    g = jnp.dot(hb, wg_ref[...], preferred_element_type=F32)
    conv_w = 3 * GDN_HEADS * GDN_DIM
    gqkv_ref[...] = g[:, :conv_w]
    z_ref[...] = g[:, conv_w:]
    ba_ref[...] = jnp.dot(hb, wba_ref[...], preferred_element_type=F32)


def _proj(x2, mod3, g1, w_da, w_g, w_ba, qn, kn, tm, tiles_per_batch):
    n, d = x2.shape
    da_w = DA_HEADS * 2 * DA_QK_DIM
    dv_w = DA_HEADS * DA_V_DIM
    conv_w = 3 * GDN_HEADS * GDN_DIM
    z_w = GDN_HEADS * GDN_DIM
    const = lambda i: (0, 0)
    row = lambda i: (i, 0)
    return pl.pallas_call(
        _proj_kernel,
        grid=(n // tm,),
        in_specs=[pl.BlockSpec((tm, d), row),
                  pl.BlockSpec((1, 6, d), lambda i: (i // tiles_per_batch, 0, 0)),
                  pl.BlockSpec((1, d), const),
                  pl.BlockSpec(w_da.shape, const),
                  pl.BlockSpec(w_g.shape, const),
                  pl.BlockSpec(w_ba.shape, const),
                  pl.BlockSpec((1, da_w), const),
                  pl.BlockSpec((1, da_w), const)],
        out_specs=[pl.BlockSpec((tm, da_w), row), pl.BlockSpec((tm, da_w), row),
                   pl.BlockSpec((tm, dv_w), row), pl.BlockSpec((tm, conv_w), row),
                   pl.BlockSpec((tm, z_w), row), pl.BlockSpec((tm, LANES), row)],
        out_shape=[jax.ShapeDtypeStruct((n, da_w), BF16), jax.ShapeDtypeStruct((n, da_w), BF16),
                   jax.ShapeDtypeStruct((n, dv_w), BF16), jax.ShapeDtypeStruct((n, conv_w), F32),
                   jax.ShapeDtypeStruct((n, z_w), F32), jax.ShapeDtypeStruct((n, LANES), F32)],
        compiler_params=_cparams(1),
    )(x2, mod3, g1, w_da, w_g, w_ba, qn, kn)


def _attn_kernel(slopes_ref, qa_ref, qb_ref, kt_ref, v_ref, lq1_ref, lk1_ref, lq2_ref, lk2_ref, on_ref,
                 oa_ref, ob_ref, qs_ref, m_ref, acc_ref, p_ref, ddiag_ref, *, t, nq, lambda_init):
    h = pl.program_id(1)
    i = pl.program_id(2)
    slope2 = slopes_ref[h]
    reps = t // LANES

    @pl.when(i == 0)
    def _tables():
        r = lax.broadcasted_iota(I32, (t, t), 0)
        c = lax.broadcasted_iota(I32, (t, t), 1)
        ahead = jnp.maximum(c - r, 0).astype(F32)
        allowed = (c // MASK_CHUNK) <= (r // MASK_CHUNK)
        ddiag_ref[...] = jnp.where(allowed, (-2.0 * slope2) * ahead, NEG_INF)

    for tile, q_ref in enumerate((qa_ref, qb_ref)):
        q = q_ref[...]
        lane = lax.broadcasted_iota(I32, q.shape, 1)
        zero = jnp.zeros_like(q)
        ones = jnp.where(lane < ALIBI_ROWS, 1.0, 0.0).astype(BF16)
        qs_ref[tile, 0:t, :] = jnp.concatenate([jnp.where(lane < DA_QK_DIM, q, zero), ones], axis=1)
        qs_ref[tile, t:2 * t, :] = jnp.concatenate([jnp.where(lane < DA_QK_DIM, zero, q), ones], axis=1)
    m_ref[...] = jnp.full(m_ref.shape, NEG_INF, F32)
    acc_ref[...] = jnp.zeros(acc_ref.shape, F32)

    def scores(tile, j):
        start = pl.multiple_of(j * t, t)
        return jnp.dot(qs_ref[tile], kt_ref[0, 0, :, pl.ds(start, t)], preferred_element_type=F32)

    def consume(s, tile, j, slot, diagonal):
        start = pl.multiple_of(j * t, t)
        alphas = []
        for half in range(2):
            rows = slice(half * t, (half + 1) * t)
            sb = s[rows] + ddiag_ref[...] if diagonal else s[rows]
            m_prev = m_ref[tile, rows]
            m_next = jnp.maximum(m_prev, jnp.max(sb, axis=-1, keepdims=True))
            alphas.append(jnp.exp2(m_prev - m_next))
            p_ref[slot, rows] = jnp.exp2(sb - jnp.tile(m_next, (1, reps))).astype(BF16)
            m_ref[tile, rows] = m_next
        v = v_ref[pl.ds(start, t), :]
        vext = jnp.concatenate([v, jnp.ones_like(v)], axis=1)
        pv = jnp.dot(p_ref[slot], vext, preferred_element_type=F32)
        for half in range(2):
            rows = slice(half * t, (half + 1) * t)
            acc_ref[tile, rows] = jnp.tile(alphas[half], (1, 2)) * acc_ref[tile, rows] + pv[rows]

    blocks = []
    for k in range(nq - 1):
        tile = jnp.where(k >= i, 1, 0)
        blocks.append((tile, k - tile * i, False))
    blocks.append((0, i, True))
    blocks.append((1, nq - 1 - i, True))
    s_next = scores(blocks[0][0], blocks[0][1])
    for k, (tile, j, diagonal) in enumerate(blocks):
        s_cur = s_next
        if k + 1 < len(blocks):
            s_next = scores(blocks[k + 1][0], blocks[k + 1][1])
        consume(s_cur, tile, j, k % 2, diagonal)

    lam = (jnp.exp(jnp.sum(lq1_ref[...] * lk1_ref[...], axis=-1, keepdims=True))
           - jnp.exp(jnp.sum(lq2_ref[...] * lk2_ref[...], axis=-1, keepdims=True)) + lambda_init)
    for tile, o_ref in enumerate((oa_ref, ob_ref)):
        acc = acc_ref[tile]
        o_all = acc[:, 0:DA_V_DIM] / acc[:, DA_V_DIM:2 * DA_V_DIM]
        o = o_all[0:t] - lam * o_all[t:2 * t]
        o = _rms(o, on_ref[...]) * (1.0 - lambda_init)
        o_ref[...] = o.astype(o_ref.dtype)


def _alibi_slopes_log2():
    return np.asarray([2.0 ** (-8.0 * (hh + 1) / DA_HEADS) * LOG2E for hh in range(DA_HEADS)], np.float32)


def _alibi_key_rows(seq):
    col = _alibi_slopes_log2()[:, None] * np.arange(seq, dtype=np.float32)[None, :]
    b1 = col.astype(BF16)
    r1 = col - b1.astype(np.float32)
    b2 = r1.astype(BF16)
    b3 = (r1 - b2.astype(np.float32)).astype(BF16)
    rows = np.zeros((DA_HEADS, LANES, seq), BF16)
    rows[:, 0], rows[:, 1], rows[:, 2] = b1, b2, b3
    return jnp.asarray(rows)


def _attention(q, kt, v, lq1, lk1, lq2, lk2, out_norm, batch, seq, t, lambda_init):
    nq = seq // t
    half = nq // 2
    width = DA_HEADS * DA_V_DIM
    slopes = jnp.asarray(_alibi_slopes_log2())
    vec = lambda b, h, i, sl: (0, 0)
    grid_spec = pltpu.PrefetchScalarGridSpec(
        num_scalar_prefetch=1,
        grid=(batch, DA_HEADS, half),
        in_specs=[pl.BlockSpec((t, LANES), lambda b, h, i, sl: (b * nq + i, h)),
                  pl.BlockSpec((t, LANES), lambda b, h, i, sl: (b * nq + nq - 1 - i, h)),
                  pl.BlockSpec((1, 1, 2 * LANES, seq), lambda b, h, i, sl: (b, h, 0, 0)),
                  pl.BlockSpec((seq, LANES), lambda b, h, i, sl: (b, h)),
                  pl.BlockSpec((1, DA_QK_DIM), vec), pl.BlockSpec((1, DA_QK_DIM), vec),
                  pl.BlockSpec((1, DA_QK_DIM), vec), pl.BlockSpec((1, DA_QK_DIM), vec),
                  pl.BlockSpec((1, DA_V_DIM), vec)],
        out_specs=[pl.BlockSpec((t, LANES), lambda b, h, i, sl: (b * half + i, h)),
                   pl.BlockSpec((t, LANES), lambda b, h, i, sl: (b * half + half - 1 - i, h))],
        scratch_shapes=[pltpu.VMEM((2, 2 * t, 2 * LANES), BF16),
                        pltpu.VMEM((2, 2 * t, LANES), F32),
                        pltpu.VMEM((2, 2 * t, 2 * DA_V_DIM), F32),
                        pltpu.VMEM((2, 2 * t, t), BF16),
                        pltpu.VMEM((t, t), F32)])
    o_lo, o_hi = pl.pallas_call(
        functools.partial(_attn_kernel, t=t, nq=nq, lambda_init=lambda_init),
        grid_spec=grid_spec,
        out_shape=[jax.ShapeDtypeStruct((batch * seq // 2, width), BF16)] * 2,
        compiler_params=_cparams(3),
    )(slopes, q, q, kt, v, lq1, lk1, lq2, lk2, out_norm)
    both = jnp.concatenate([o_lo.reshape(batch, seq // 2, width), o_hi.reshape(batch, seq // 2, width)], axis=1)
    return both.reshape(batch * seq, width)


def _gdn_kernel(u_ref, z_ref, ba_ref, cw_ref, alog_ref, dt_ref, on_ref, o_ref, stage_ref, state_ref, *, batch):
    c = GDN_CHUNK
    hd = GDN_HEADS * GDN_DIM
    step = pl.program_id(0)

    @pl.when(step == 0)
    def _init():
        state_ref[...] = jnp.zeros(state_ref.shape, F32)
        stage_ref[:, 0:8, :] = jnp.zeros((batch, 8, stage_ref.shape[2]), F32)

    row = lax.broadcasted_iota(I32, (c, c), 0)
    col = lax.broadcasted_iota(I32, (c, c), 1)
    tri = row >= col
    strict = row > col
    tril_ones = jnp.where(tri, 1.0, 0.0).astype(BF16)

    chains = []
    for b in range(batch):
        stage_ref[b, 8:8 + c, :] = u_ref[b]
        u = stage_ref[b]
        y = cw_ref[CONV_K - 1:CONV_K, :] * u
        for back in range(1, CONV_K):
            y = y + cw_ref[CONV_K - 1 - back:CONV_K - back, :] * pltpu.roll(u, back, 0)
        stage_ref[b, 0:8, :] = stage_ref[b, c:c + 8, :]
        y = _silu(y[8:8 + c, :])

        ba = ba_ref[b]
        beta_all = _sigmoid(ba)
        g_all = -jnp.exp(alog_ref[...]) * _softplus(ba + dt_ref[...])
        gcum_all = _mm_exact_lhs(tril_ones, g_all)
        gcum_t = gcum_all.T

        for h in range(GDN_HEADS):
            q = y[:, h * GDN_DIM:(h + 1) * GDN_DIM]
            k = y[:, hd + h * GDN_DIM:hd + (h + 1) * GDN_DIM]
            v = y[:, 2 * hd + h * GDN_DIM:2 * hd + (h + 1) * GDN_DIM]
            q = q * lax.rsqrt(jnp.sum(q * q, axis=-1, keepdims=True) + EPS) * (GDN_DIM ** -0.5)
            k = k * lax.rsqrt(jnp.sum(k * k, axis=-1, keepdims=True) + EPS)
            beta = beta_all[:, h:h + 1]
            gc = gcum_all[:, GDN_HEADS + h:GDN_HEADS + h + 1]
            gr = gcum_t[GDN_HEADS + h:GDN_HEADS + h + 1, :]
            g_last = gc[c - 1:c, :]
            decay = jnp.where(tri, jnp.exp(jnp.where(tri, gc - gr, 0.0)), 0.0)
            e_gc = jnp.exp(gc)
            chains.append(dict(
                b=b, h=h, decay=decay, g_last=g_last,
                p=jnp.where(strict, -(beta * _mm_nt(k, k) * decay), 0.0),
                rhs=jnp.concatenate([v * beta, k * (beta * e_gc)], axis=1),
                qk=_mm_nt(q, k) * decay, q_dec=q * e_gc, k_dec_t=(k * jnp.exp(g_last - gc)).T))

    def same_block(size):
        return (row // size) == (col // size)

    for ch in chains:
        ch["pk"] = jnp.where(same_block(GDN_BASE), ch["p"], 0.0)
        ch["x"] = ch["pk"]
    for _ in range(int(math.log2(GDN_BASE)) - 1):
        for ch in chains:
            ch["pk"] = _mm(ch["pk"], ch["pk"])
        for ch in chains:
            ch["x"] = ch["x"] + ch["pk"] + _mm(ch["x"], ch["pk"])
    size = GDN_BASE
    while size < c:
        pair_only = same_block(2 * size) & jnp.logical_not(same_block(size))
        for ch in chains:
            e = jnp.where(pair_only, -ch["p"], 0.0)
            ch["y"] = e + _mm(ch["x"], e)
        for ch in chains:
            ch["x"] = ch["x"] - (ch["y"] + _mm(ch["y"], ch["x"]))
        size *= 2

    for ch in chains:
        ch["s_prev"] = state_ref[ch["b"] * GDN_HEADS + ch["h"]]
        ch["o_inter"] = _mm(ch["q_dec"], ch["s_prev"])
        ch["sol"] = ch["rhs"] + _mm(ch["x"], ch["rhs"])
    for ch in chains:
        ch["v_new"] = ch["sol"][:, :GDN_DIM] - _mm(ch["sol"][:, GDN_DIM:], ch["s_prev"])
    for ch in chains:
        ch["o"] = ch["o_inter"] + _mm(ch["qk"], ch["v_new"])
        ch["s_new"] = ch["s_prev"] * jnp.exp(ch["g_last"]) + _mm(ch["k_dec_t"], ch["v_new"])
    outs = [[None] * GDN_HEADS for _ in range(batch)]
    for ch in chains:
        zg = z_ref[ch["b"], :, ch["h"] * GDN_DIM:(ch["h"] + 1) * GDN_DIM]
        outs[ch["b"]][ch["h"]] = (_rms(ch["o"], on_ref[...]) * _silu(zg)).astype(o_ref.dtype)
    o_ref[...] = jnp.stack([jnp.concatenate(heads, axis=1) for heads in outs], axis=0)
    state_ref[...] = jnp.stack([ch["s_new"] for ch in chains], axis=0)


def _gdn(qkv3, z3, ba3, conv_taps, alog_row, dt_row, out_norm):
    batch, seq, cw = qkv3.shape
    c = GDN_CHUNK
    hd = GDN_HEADS * GDN_DIM
    blk = lambda s: (0, s, 0)
    const = lambda s: (0, 0)
    return pl.pallas_call(
        functools.partial(_gdn_kernel, batch=batch),
        grid=(seq // c,),
        in_specs=[pl.BlockSpec((batch, c, cw), blk),
                  pl.BlockSpec((batch, c, hd), blk),
                  pl.BlockSpec((batch, c, LANES), blk),
                  pl.BlockSpec((CONV_K, cw), const),
                  pl.BlockSpec((1, LANES), const),
                  pl.BlockSpec((1, LANES), const),
                  pl.BlockSpec((1, GDN_DIM), const)],
        out_specs=pl.BlockSpec((batch, c, hd), blk),
        out_shape=jax.ShapeDtypeStruct((batch, seq, hd), BF16),
        scratch_shapes=[pltpu.VMEM((batch, c + 8, cw), F32),
                        pltpu.VMEM((batch * GDN_HEADS, GDN_DIM, GDN_DIM), F32)],
        compiler_params=_cparams(1),
    )(qkv3, z3, ba3, conv_taps, alog_row, dt_row, out_norm)


def _merge_kernel(x_ref, mod_ref, g1_ref, g2_ref, oa_ref, ob_ref, wgate_ref, wa_ref, wb_ref, wout_ref,
                  wrt_ref, brt_ref, x1_ref, h2_ref, lg_ref):
    x = x_ref[...]
    d = x.shape[1]
    mod = mod_ref[0]
    shift1, scale1, gate1 = mod[0:1, :], mod[1:2, :], mod[2:3, :]
    shift2, scale2 = mod[3:4, :], mod[4:5, :]
    hb = (_rms(x, g1_ref[...]) * (1.0 + scale1) + shift1).astype(BF16)
    gates = _sigmoid(jnp.dot(hb, wgate_ref[...], preferred_element_type=F32))
    ya = jnp.dot(oa_ref[...], wa_ref[...], preferred_element_type=F32)
    yb = jnp.dot(ob_ref[...], wb_ref[...], preferred_element_type=F32)
    merged = gates[:, :d] * ya + gates[:, d:] * yb
    x1 = x + gate1 * _mm(merged, wout_ref[...])
    x1_ref[...] = x1
    h2 = _rms(x1, g2_ref[...]) * (1.0 + scale2) + shift2
    h2_ref[...] = _pack_bf16_pairs(h2)
    lg_ref[...] = _mm3(h2, wrt_ref[...]) + brt_ref[...]


def _merge(x2, mod3, g1, g2, oa, ob, w_gate, w_a, w_b, w_out, w_rt, b_rt, tm, tiles_per_batch):
    n, d = x2.shape
    const = lambda i: (0, 0)
    row = lambda i: (i, 0)
    return pl.pallas_call(
        _merge_kernel,
        grid=(n // tm,),
        in_specs=[pl.BlockSpec((tm, d), row),
                  pl.BlockSpec((1, 6, d), lambda i: (i // tiles_per_batch, 0, 0)),
                  pl.BlockSpec((1, d), const), pl.BlockSpec((1, d), const),
                  pl.BlockSpec((tm, oa.shape[1]), row), pl.BlockSpec((tm, ob.shape[1]), row),
                  pl.BlockSpec(w_gate.shape, const), pl.BlockSpec(w_a.shape, const),
                  pl.BlockSpec(w_b.shape, const), pl.BlockSpec(w_out.shape, const),
                  pl.BlockSpec(w_rt.shape, const), pl.BlockSpec((1, LANES), const)],
        out_specs=[pl.BlockSpec((tm, d), row), pl.BlockSpec((tm, d // 2), row), pl.BlockSpec((tm, LANES), row)],
        out_shape=[jax.ShapeDtypeStruct((n, d), F32), jax.ShapeDtypeStruct((n, d // 2), I32),
                   jax.ShapeDtypeStruct((n, LANES), F32)],
        compiler_params=_cparams(1),
    )(x2, mod3, g1, g2, oa, ob, w_gate, w_a, w_b, w_out, w_rt, b_rt)


def _route_kernel(lg_ref, ri_ref, rf_ref, cnt_ref, carry_ref):
    step = pl.program_id(0)
    tm = lg_ref.shape[0]

    @pl.when(step == 0)
    def _init():
        carry_ref[...] = jnp.zeros(carry_ref.shape, F32)

    lg = lg_ref[...]
    lane = lax.broadcasted_iota(I32, lg.shape, 1)
    big = jnp.int32(LANES)
    is_grp = (lane >= N_EXPERTS) & (lane < N_EXPERTS + N_GROUPS)
    gl = jnp.where(is_grp, lg, NEG_INF)
    gmax = jnp.max(gl, axis=-1, keepdims=True)
    g_top = jnp.min(jnp.where(is_grp & (gl == gmax), lane, big), axis=-1, keepdims=True) - N_EXPERTS
    g_top_p = 1.0 / jnp.sum(jnp.where(is_grp, jnp.exp(gl - gmax), 0.0), axis=-1, keepdims=True)
    in_grp = (lane < N_EXPERTS) & ((lane // EXPERTS_PER_GROUP) == g_top)
    el = jnp.where(in_grp, lg, NEG_INF)
    v1 = jnp.max(el, axis=-1, keepdims=True)
    e1 = jnp.min(jnp.where(in_grp & (el == v1), lane, big), axis=-1, keepdims=True)
    rest = in_grp & (lane != e1)
    el2 = jnp.where(rest, lg, NEG_INF)
    v2 = jnp.max(el2, axis=-1, keepdims=True)
    e2 = jnp.min(jnp.where(rest & (el2 == v2), lane, big), axis=-1, keepdims=True)
    ex = jnp.exp(v2 - v1)
    w1 = g_top_p / (1.0 + ex)
    w2 = g_top_p * ex / (1.0 + ex)

    oh1 = lane == e1
    oh2 = lane == e2
    ohs = (jnp.where(oh1, 1.0, 0.0) + jnp.where(oh2, 1.0, 0.0)).astype(BF16)
    r = lax.broadcasted_iota(I32, (tm, tm), 0)
    c = lax.broadcasted_iota(I32, (tm, tm), 1)
    before = jnp.where(r > c, 1.0, 0.0).astype(BF16)
    prior = jnp.dot(before, ohs, preferred_element_type=F32) + carry_ref[0:1, :]
    rank1 = jnp.sum(jnp.where(oh1, prior, 0.0), axis=-1, keepdims=True)
    rank2 = jnp.sum(jnp.where(oh2, prior, 0.0), axis=-1, keepdims=True)
    carry_ref[...] = carry_ref[...] + jnp.sum(ohs.astype(F32), axis=0, keepdims=True)
    cnt_ref[...] = carry_ref[...]

    r1 = rank1.astype(I32)
    r2 = rank2.astype(I32)
    ri = jnp.where(lane == 0, e1, jnp.where(lane == 1, e2, jnp.where(lane == 2, r1, jnp.where(lane == 3, r2, 0))))
    ri_ref[...] = ri
    rf_ref[...] = jnp.where(lane == 0, w1, jnp.where(lane == 1, w2, 0.0))


def _route(logits, tm):
    n = logits.shape[0]
    row = lambda i: (i, 0)
    return pl.pallas_call(
        _route_kernel,
        grid=(n // tm,),
        in_specs=[pl.BlockSpec((tm, LANES), row)],
        out_specs=[pl.BlockSpec((tm, LANES), row), pl.BlockSpec((tm, LANES), row),
                   pl.BlockSpec((8, LANES), lambda i: (0, 0))],
        out_shape=[jax.ShapeDtypeStruct((n, LANES), I32), jax.ShapeDtypeStruct((n, LANES), F32),
                   jax.ShapeDtypeStruct((8, LANES), F32)],
        scratch_shapes=[pltpu.VMEM((8, LANES), F32)],
        compiler_params=_cparams(1),
    )(logits)


def _moe_kernel(be_ref, act_ref, xb_ref, w1_ref, w3_ref, w2_ref, o_ref, w1b_ref, w3b_ref, w2b_ref):
    i = pl.program_id(0)
    prev = be_ref[jnp.maximum(i - 1, 0)]

    @pl.when((i == 0) | (be_ref[i] != prev))
    def _cast_weights():
        w1b_ref[...] = w1_ref[...].astype(BF16)
        w3b_ref[...] = w3_ref[...].astype(BF16)
        w2b_ref[...] = w2_ref[...].astype(BF16)

    for piece in range(MOE_ROWS // MOE_SUB):
        rows = slice(piece * MOE_SUB, (piece + 1) * MOE_SUB)
        valid = act_ref[i] - piece * MOE_SUB

        @pl.when(valid > 0)
        def _compute(rows=rows, valid=valid):
            xp = xb_ref[rows, :]
            r = lax.broadcasted_iota(I32, xp.shape, 0)
            xb = _unpack_bf16_pairs(jnp.where(r < valid, xp, 0))
            a = jnp.dot(xb, w1b_ref[...], preferred_element_type=F32)
            g = jnp.dot(xb, w3b_ref[...], preferred_element_type=F32)
            hid = (_silu(a) * g).astype(BF16)
            o_ref[rows, :] = _pack_bf16_pairs(jnp.dot(hid, w2b_ref[...], preferred_element_type=F32))

        @pl.when(valid <= 0)
        def _idle(rows=rows):
            o_ref[rows, :] = jnp.zeros((MOE_SUB, o_ref.shape[1]), o_ref.dtype)


def _moe(blk_expert, blk_rows, xb, w1, w3, w2):
    cap = xb.shape[0]
    d = w1.shape[1]
    nb = cap // MOE_ROWS
    de = w1.shape[2]
    grid_spec = pltpu.PrefetchScalarGridSpec(
        num_scalar_prefetch=2,
        grid=(nb,),
        in_specs=[pl.BlockSpec((MOE_ROWS, d // 2), lambda i, be, act: (i, 0)),
                  pl.BlockSpec((None, d, de), lambda i, be, act: (be[i], 0, 0)),
                  pl.BlockSpec((None, d, de), lambda i, be, act: (be[i], 0, 0)),
                  pl.BlockSpec((None, de, d), lambda i, be, act: (be[i], 0, 0))],
        out_specs=pl.BlockSpec((MOE_ROWS, d // 2), lambda i, be, act: (i, 0)),
        scratch_shapes=[pltpu.VMEM((d, de), BF16), pltpu.VMEM((d, de), BF16), pltpu.VMEM((de, d), BF16)])
    return pl.pallas_call(
        _moe_kernel,
        grid_spec=grid_spec,
        out_shape=jax.ShapeDtypeStruct((cap, d // 2), I32),
        compiler_params=_cparams(1),
    )(blk_expert, blk_rows, xb, w1, w3, w2)


def _dest_kernel(ri_ref, pstart_ref, o_ref):
    ri = ri_ref[...]
    lane = lax.broadcasted_iota(I32, ri.shape, 1)
    pstart = pstart_ref[...]
    base1 = jnp.sum(jnp.where(lane == ri[:, 0:1], pstart, 0), axis=-1, keepdims=True)
    base2 = jnp.sum(jnp.where(lane == ri[:, 1:2], pstart, 0), axis=-1, keepdims=True)
    dest = jnp.where(lane == 0, base1 + ri[:, 2:3], jnp.where(lane == 1, base2 + ri[:, 3:4], 0))
    o_ref[...] = dest.astype(F32).T[0:8, :].astype(I32)


def _dest(ri, pstart_row, tm):
    n = ri.shape[0]
    return pl.pallas_call(
        _dest_kernel,
        grid=(n // tm,),
        in_specs=[pl.BlockSpec((tm, LANES), lambda i: (i, 0)), pl.BlockSpec((1, LANES), lambda i: (0, 0))],
        out_specs=pl.BlockSpec((8, tm), lambda i: (0, i)),
        out_shape=jax.ShapeDtypeStruct((8, n), I32),
        compiler_params=_cparams(1),
    )(ri, pstart_row)


def _sc_mesh():
    return plsc.VectorSubcoreMesh(core_axis_name="core", subcore_axis_name="subcore")


def _sc_scatter_rows(x, idx, cap):
    n, d = x.shape
    windows = idx.shape[0]
    tiles = n // SC_WINDOW
    mesh = _sc_mesh()
    workers = mesh.num_cores * mesh.num_subcores
    per_worker = windows // workers

    @functools.partial(pl.kernel, out_type=jax.ShapeDtypeStruct((cap, d), x.dtype), mesh=mesh,
                       scratch_types=[pltpu.VMEM((1, SC_WINDOW), I32), pltpu.VMEM((SC_WINDOW, d), x.dtype)])
    def scatter(x_hbm, i_hbm, o_hbm, idx_v, rows_v):
        wid = lax.axis_index("subcore") * mesh.num_cores + lax.axis_index("core")

        @pl.loop(0, per_worker)
        def _(j):
            w = wid * per_worker + j
            pltpu.sync_copy(i_hbm.at[pl.ds(w, 1)], idx_v)
            pltpu.sync_copy(x_hbm.at[pl.ds(lax.rem(w, tiles) * SC_WINDOW, SC_WINDOW)], rows_v)
            pltpu.sync_copy(rows_v, o_hbm.at[idx_v.at[0]])

    return scatter(x, idx)


def _sc_gather_rows(table, idx):
    d = table.shape[1]
    windows = idx.shape[0]
    mesh = _sc_mesh()
    workers = mesh.num_cores * mesh.num_subcores
    per_worker = windows // workers

    @functools.partial(pl.kernel, out_type=jax.ShapeDtypeStruct((windows * SC_WINDOW, d), table.dtype), mesh=mesh,
                       scratch_types=[pltpu.VMEM((1, SC_WINDOW), I32), pltpu.VMEM((SC_WINDOW, d), table.dtype)])
    def gather(t_hbm, i_hbm, o_hbm, idx_v, rows_v):
        wid = lax.axis_index("subcore") * mesh.num_cores + lax.axis_index("core")

        @pl.loop(0, per_worker)
        def _(j):
            w = wid * per_worker + j
            pltpu.sync_copy(i_hbm.at[pl.ds(w, 1)], idx_v)
            pltpu.sync_copy(t_hbm.at[idx_v.at[0]], rows_v)
            pltpu.sync_copy(rows_v, o_hbm.at[pl.ds(w * SC_WINDOW, SC_WINDOW)])

    return gather(table, idx)


def _combine_kernel(x1_ref, mod_ref, rf_ref, ya_ref, yb_ref, o_ref):
    gate2 = mod_ref[0][5:6, :]
    rf = rf_ref[...]
    ya = _unpack_bf16_pairs(ya_ref[...]).astype(F32)
    yb = _unpack_bf16_pairs(yb_ref[...]).astype(F32)
    y = rf[:, 0:1] * ya + rf[:, 1:2] * yb
    o_ref[...] = x1_ref[...] + gate2 * y


def _combine(x1, mod3, rf, y12, tm, tiles_per_batch):
    n, d = x1.shape
    tiles = n // tm
    row = lambda i: (i, 0)
    return pl.pallas_call(
        _combine_kernel,
        grid=(tiles,),
        in_specs=[pl.BlockSpec((tm, d), row),
                  pl.BlockSpec((1, 6, d), lambda i: (i // tiles_per_batch, 0, 0)),
                  pl.BlockSpec((tm, LANES), row),
                  pl.BlockSpec((tm, d // 2), row), pl.BlockSpec((tm, d // 2), lambda i: (i + tiles, 0))],
        out_specs=pl.BlockSpec((tm, d), row),
        out_shape=jax.ShapeDtypeStruct((n, d), F32),
        compiler_params=_cparams(1),
    )(x1, mod3, rf, y12, y12)


def _pad_lanes(a, offset=0):
    return jnp.pad(a, ((0, 0), (offset, LANES - offset - a.shape[1])))


def _layer(x, c, layer, w_ada, b_ada, norm1_gain, w_in, da_q_norm, da_k_norm, lq1, lk1, lq2, lk2, da_out_norm,
           gdn_conv, gdn_a_log, gdn_dt_bias, gdn_out_norm, w_branch_a, w_branch_b, w_out, norm2_gain,
           w_group, b_group, w_router, b_router, w1, w3, w2):
    batch, seq, d = x.shape
    n = batch * seq
    tm = min(512, seq)
    tiles_per_batch = seq // tm
    t_attn = min(512, seq)
    lambda_init = 0.8 - 0.6 * math.exp(-0.3 * layer)

    mod = _ada(jnp.pad(c, ((0, 8 - batch), (0, 0))), w_ada, b_ada)[:batch]
    mod3 = mod.reshape(batch, 6, d)

    da_w = DA_HEADS * 2 * DA_QK_DIM
    dv_w = DA_HEADS * DA_V_DIM
    gd_w = GDN_HEADS * GDN_DIM
    o0 = 2 * da_w + dv_w
    o1 = o0 + 4 * gd_w
    o2 = o1 + 2 * GDN_HEADS
    w_da = w_in[:, :o0].astype(BF16)
    w_g = w_in[:, o0:o1].astype(BF16)
    w_ba = _pad_lanes(w_in[:, o1:o2]).astype(BF16)
    w_gate = w_in[:, o2:].astype(BF16)
    qn = jnp.tile(da_q_norm, 2 * DA_HEADS).reshape(1, da_w)
    kn = jnp.tile(da_k_norm, 2 * DA_HEADS).reshape(1, da_w)

    x2 = x.reshape(n, d)
    g1 = norm1_gain.reshape(1, d)
    q, k, v, gqkv, z, ba = _proj(x2, mod3, g1, w_da, w_g, w_ba, qn, kn, tm, tiles_per_batch)

    kt = k.reshape(batch, seq, DA_HEADS, 2 * DA_QK_DIM).transpose(0, 2, 3, 1)
    kt = jnp.concatenate([kt, jnp.broadcast_to(_alibi_key_rows(seq), kt.shape)], axis=2)
    o_a = _attention(q, kt, v, lq1.reshape(1, -1), lk1.reshape(1, -1), lq2.reshape(1, -1), lk2.reshape(1, -1),
                     da_out_norm.reshape(1, -1), batch, seq, t_attn, lambda_init)

    alog_row = _pad_lanes(gdn_a_log.reshape(1, -1), GDN_HEADS)
    dt_row = _pad_lanes(gdn_dt_bias.reshape(1, -1), GDN_HEADS)
    o_b = _gdn(gqkv.reshape(batch, seq, -1), z.reshape(batch, seq, -1), ba.reshape(batch, seq, -1),
               gdn_conv, alog_row, dt_row, gdn_out_norm.reshape(1, -1)).reshape(n, gd_w)

    w_rt = _pad_lanes(jnp.concatenate([w_router, w_group], axis=1))
    b_rt = _pad_lanes(jnp.concatenate([b_router, b_group]).reshape(1, -1))
    x1, h2, logits = _merge(x2, mod3, g1, norm2_gain.reshape(1, d), o_a, o_b, w_gate,
                            w_branch_a.astype(BF16), w_branch_b.astype(BF16), w_out.astype(BF16),
                            w_rt, b_rt, tm, tiles_per_batch)

    ri, rf, cnt = _route(logits, tm)

    counts = cnt[0, :N_EXPERTS].astype(I32)
    padded = (counts + MOE_ROWS - 1) // MOE_ROWS * MOE_ROWS
    pend = jnp.cumsum(padded)
    pstart = pend - padded
    cap = 2 * n + N_EXPERTS * MOE_ROWS
    nb = cap // MOE_ROWS
    blk_start = jnp.arange(nb, dtype=I32) * MOE_ROWS
    blk_expert = jnp.minimum(jnp.sum(pend[None, :] <= blk_start[:, None], axis=1), N_EXPERTS - 1).astype(I32)
    blk_rows = jnp.clip((pstart + counts)[blk_expert] - blk_start, 0, MOE_ROWS).astype(I32)
    blk_rows = jnp.where(blk_start < pend[-1], blk_rows, 0)
    dest = _dest(ri, _pad_lanes(pstart.reshape(1, -1)), tm)
    idx = dest[0:2].reshape(2 * n // SC_WINDOW, SC_WINDOW)

    xb = _sc_scatter_rows(h2, idx, cap)
    yb = _moe(blk_expert, blk_rows, xb, w1, w3, w2)
    y12 = _sc_gather_rows(yb, idx)
    out = _combine(x1, mod3, rf, y12, tm, tiles_per_batch)
    return out.reshape(batch, seq, d)


def kernel(x, c, w_ada, b_ada, norm1_gain, w_in, da_q_norm, da_k_norm, da_lambda_q1, da_lambda_k1, da_lambda_q2,
           da_lambda_k2, da_out_norm, gdn_conv, gdn_a_log, gdn_dt_bias, gdn_out_norm, w_branch_a, w_branch_b,
           w_out, norm2_gain, w_group, b_group, w_router, b_router, w1, w3, w2):
    for layer in range(w_ada.shape[0]):
        x = _layer(x, c, layer, w_ada[layer], b_ada[layer], norm1_gain[layer], w_in[layer], da_q_norm[layer],
                   da_k_norm[layer], da_lambda_q1[layer], da_lambda_k1[layer], da_lambda_q2[layer],
                   da_lambda_k2[layer], da_out_norm[layer], gdn_conv[layer], gdn_a_log[layer], gdn_dt_bias[layer],
                   gdn_out_norm[layer], w_branch_a[layer], w_branch_b[layer], w_out[layer], norm2_gain[layer],
                   w_group[layer], b_group[layer], w_router[layer], b_router[layer], w1[layer], w3[layer], w2[layer])
    return x
```

```python
import functools
import math

import jax
import jax.numpy as jnp
import numpy as np
from jax import lax
from jax.experimental import pallas as pl
from jax.experimental.pallas import tpu as pltpu
from jax.experimental.pallas import tpu_sc as plsc

F32 = jnp.float32
BF16 = jnp.bfloat16
I32 = jnp.int32

EPS = 1e-6
NEG_INF = -1e30
MASK_CHUNK = 64
LOG2E = 1.4426950408889634
ALIBI_ROWS = 3

DA_HEADS = 4
DA_QK_DIM = 64
DA_V_DIM = 128
GDN_HEADS = 4
GDN_DIM = 128
CONV_K = 4
N_GROUPS = 4
EXPERTS_PER_GROUP = 8
N_EXPERTS = N_GROUPS * EXPERTS_PER_GROUP
D_EXPERT = 512

LANES = 128
VMEM_LIMIT = 56 * 1024 * 1024

ATTN_UNROLL = 4
GDN_CHUNK = 128
GDN_BASE = 16
MOE_ROWS = 512
MOE_SUB = 256
SC_WINDOW = 128


def _cparams(n_axes):
    return pltpu.CompilerParams(dimension_semantics=("arbitrary",) * n_axes,
                                vmem_limit_bytes=VMEM_LIMIT)


def _mm(a, b):
    return jnp.dot(a.astype(BF16), b.astype(BF16), preferred_element_type=F32)


def _mm_nt(a, b):
    return lax.dot_general(a.astype(BF16), b.astype(BF16), (((1,), (1,)), ((), ())),
                           preferred_element_type=F32)


def _split2(a):
    hi = a.astype(BF16)
    lo = (a - hi.astype(F32)).astype(BF16)
    return hi, lo


def _mm3(a, b):
    ah, al = _split2(a)
    bh, bl = _split2(b)
    out = jnp.dot(ah, bh, preferred_element_type=F32)
    out = out + jnp.dot(ah, bl, preferred_element_type=F32)
    out = out + jnp.dot(al, bh, preferred_element_type=F32)
    return out


def _mm_exact_lhs(a_bf16, b):
    b1 = b.astype(BF16)
    r1 = b - b1.astype(F32)
    b2 = r1.astype(BF16)
    b3 = (r1 - b2.astype(F32)).astype(BF16)
    out = jnp.dot(a_bf16, b1, preferred_element_type=F32)
    out = out + jnp.dot(a_bf16, b2, preferred_element_type=F32)
    out = out + jnp.dot(a_bf16, b3, preferred_element_type=F32)
    return out


def _sigmoid(x):
    return 1.0 / (1.0 + jnp.exp(-x))


def _silu(x):
    return x * _sigmoid(x)


def _softplus(x):
    return jnp.maximum(x, 0.0) + jnp.log(1.0 + jnp.exp(-jnp.abs(x)))


def _pack_bf16_pairs(x):
    w = x.shape[1] // 2
    bits = lax.bitcast_convert_type(x.astype(BF16).astype(F32), jnp.uint32)
    lo = lax.shift_right_logical(bits[:, :w], jnp.uint32(16))
    hi = bits[:, w:] & jnp.uint32(0xFFFF0000)
    return lax.bitcast_convert_type(hi | lo, I32)


def _unpack_bf16_pairs(p):
    bits = lax.bitcast_convert_type(p, jnp.uint32)
    lo = lax.bitcast_convert_type(lax.shift_left(bits, jnp.uint32(16)), F32)
    hi = lax.bitcast_convert_type(bits & jnp.uint32(0xFFFF0000), F32)
    return jnp.concatenate([lo, hi], axis=1).astype(BF16)


def _rms(x, gain):
    return x * lax.rsqrt(jnp.mean(x * x, axis=-1, keepdims=True) + EPS) * gain


def _ada_kernel(c_ref, w_ref, b_ref, o_ref):
    sc = _silu(c_ref[...])
    o_ref[...] = _mm3(sc, w_ref[...]) + b_ref[...]


def _ada(c_pad, w_ada, b_ada):
    rows, d = c_pad.shape
    n = w_ada.shape[1]
    tn = d
    return pl.pallas_call(
        _ada_kernel,
        grid=(n // tn,),
        in_specs=[pl.BlockSpec((rows, d), lambda j: (0, 0)),
                  pl.BlockSpec((d, tn), lambda j: (0, j)),
                  pl.BlockSpec((1, tn), lambda j: (0, j))],
        out_specs=pl.BlockSpec((rows, tn), lambda j: (0, j)),
        out_shape=jax.ShapeDtypeStruct((rows, n), F32),
        compiler_params=_cparams(1),
    )(c_pad, w_ada, b_ada.reshape(1, n))


def _group_rms64(x, gain):
    tm, width = x.shape
    lane = lax.broadcasted_iota(I32, (tm, LANES), 1)
    low = lane < DA_QK_DIM
    parts = []
    for j in range(width // LANES):
        blk = x[:, j * LANES:(j + 1) * LANES]
        sq = blk * blk
        s_lo = jnp.sum(jnp.where(low, sq, 0.0), axis=-1, keepdims=True)
        s_hi = jnp.sum(jnp.where(low, 0.0, sq), axis=-1, keepdims=True)
        ms = jnp.where(low, s_lo, s_hi) * (1.0 / DA_QK_DIM)
        parts.append(blk * lax.rsqrt(ms + EPS))
    return jnp.concatenate(parts, axis=-1) * gain


def _proj_kernel(x_ref, mod_ref, g1_ref, wda_ref, wg_ref, wba_ref, qn_ref, kn_ref,
                 q_ref, k_ref, v_ref, gqkv_ref, z_ref, ba_ref):
    x = x_ref[...]
    mod = mod_ref[0]
    shift, scale = mod[0:1, :], mod[1:2, :]
    hb = (_rms(x, g1_ref[...]) * (1.0 + scale) + shift).astype(BF16)
    da_w = DA_HEADS * 2 * DA_QK_DIM
    da = jnp.dot(hb, wda_ref[...], preferred_element_type=F32)
    q = _group_rms64(da[:, :da_w], qn_ref[...]) * (DA_QK_DIM ** -0.5 * LOG2E)
    k = _group_rms64(da[:, da_w:2 * da_w], kn_ref[...])
    q_ref[...] = q.astype(BF16)
    k_ref[...] = k.T.astype(BF16)
    v_ref[...] = da[:, 2 * da_w:].astype(BF16)
    g = jnp.dot(hb, wg_ref[...], preferred_element_type=F32)
    conv_w = 3 * GDN_HEADS * GDN_DIM
    gqkv_ref[...] = g[:, :conv_w]
    z_ref[...] = g[:, conv_w:]
    ba_ref[...] = jnp.dot(hb, wba_ref[...], preferred_element_type=F32)


def _proj(x2, mod3, g1, w_da, w_g, w_ba, qn, kn, tm, tiles_per_batch):
    n, d = x2.shape
    da_w = DA_HEADS * 2 * DA_QK_DIM
    dv_w = DA_HEADS * DA_V_DIM
    conv_w = 3 * GDN_HEADS * GDN_DIM
    z_w = GDN_HEADS * GDN_DIM
    const = lambda i: (0, 0)
    row = lambda i: (i, 0)
    return pl.pallas_call(
        _proj_kernel,
        grid=(n // tm,),
        in_specs=[pl.BlockSpec((tm, d), row),
                  pl.BlockSpec((1, 6, d), lambda i: (i // tiles_per_batch, 0, 0)),
                  pl.BlockSpec((1, d), const),
                  pl.BlockSpec(w_da.shape, const),
                  pl.BlockSpec(w_g.shape, const),
                  pl.BlockSpec(w_ba.shape, const),
                  pl.BlockSpec((1, da_w), const),
                  pl.BlockSpec((1, da_w), const)],
        out_specs=[pl.BlockSpec((tm, da_w), row), pl.BlockSpec((da_w, tm), lambda i: (0, i)),
                   pl.BlockSpec((tm, dv_w), row), pl.BlockSpec((tm, conv_w), row),
                   pl.BlockSpec((tm, z_w), row), pl.BlockSpec((tm, LANES), row)],
        out_shape=[jax.ShapeDtypeStruct((n, da_w), BF16), jax.ShapeDtypeStruct((da_w, n), BF16),
                   jax.ShapeDtypeStruct((n, dv_w), BF16), jax.ShapeDtypeStruct((n, conv_w), F32),
                   jax.ShapeDtypeStruct((n, z_w), F32), jax.ShapeDtypeStruct((n, LANES), F32)],
        compiler_params=_cparams(1),
    )(x2, mod3, g1, w_da, w_g, w_ba, qn, kn)


def _attn_kernel(slopes_ref, qa_ref, qb_ref, kt_ref, al_ref, v_ref, lq1_ref, lk1_ref, lq2_ref, lk2_ref, on_ref,
                 oa_ref, ob_ref, qs_ref, m_ref, acc_ref, p_ref, ddiag_ref, kta_ref, *, t, nq, lambda_init):
    h = pl.program_id(1)
    i = pl.program_id(2)
    slope2 = slopes_ref[h]
    reps = t // LANES

    @pl.when(i == 0)
    def _tables():
        r = lax.broadcasted_iota(I32, (t, t), 0)
        c = lax.broadcasted_iota(I32, (t, t), 1)
        ahead = jnp.maximum(c - r, 0).astype(F32)
        allowed = (c // MASK_CHUNK) <= (r // MASK_CHUNK)
        ddiag_ref[...] = jnp.where(allowed, (-2.0 * slope2) * ahead, NEG_INF)
        kta_ref[0:LANES, :] = kt_ref[...]
        kta_ref[LANES:2 * LANES, :] = al_ref[0]

    for tile, q_ref in enumerate((qa_ref, qb_ref)):
        q = q_ref[...]
        lane = lax.broadcasted_iota(I32, q.shape, 1)
        zero = jnp.zeros_like(q)
        ones = jnp.where(lane < ALIBI_ROWS, 1.0, 0.0).astype(BF16)
        qs_ref[tile, 0:t, :] = jnp.concatenate([jnp.where(lane < DA_QK_DIM, q, zero), ones], axis=1)
        qs_ref[tile, t:2 * t, :] = jnp.concatenate([jnp.where(lane < DA_QK_DIM, zero, q), ones], axis=1)
    m_ref[...] = jnp.full(m_ref.shape, NEG_INF, F32)
    acc_ref[...] = jnp.zeros(acc_ref.shape, F32)

    def scores(tile, j):
        start = pl.multiple_of(j * t, t)
        return jnp.dot(qs_ref[tile], kta_ref[:, pl.ds(start, t)], preferred_element_type=F32)

    def consume(s, tile, j, slot, diagonal):
        start = pl.multiple_of(j * t, t)
        alphas = []
        for half in range(2):
            rows = slice(half * t, (half + 1) * t)
            sb = s[rows] + ddiag_ref[...] if diagonal else s[rows]
            m_prev = m_ref[tile, rows]
            m_next = jnp.maximum(m_prev, jnp.max(sb, axis=-1, keepdims=True))
            alphas.append(jnp.exp2(m_prev - m_next))
            p_ref[slot, rows] = jnp.exp2(sb - jnp.tile(m_next, (1, reps))).astype(BF16)
            m_ref[tile, rows] = m_next
        v = v_ref[pl.ds(start, t), :]
        vext = jnp.concatenate([v, jnp.ones_like(v)], axis=1)
        pv = jnp.dot(p_ref[slot], vext, preferred_element_type=F32)
        for half in range(2):
            rows = slice(half * t, (half + 1) * t)
            acc_ref[tile, rows] = jnp.tile(alphas[half], (1, 2)) * acc_ref[tile, rows] + pv[rows]

    blocks = []
    for k in range(nq - 1):
        tile = jnp.where(k >= i, 1, 0)
        blocks.append((tile, k - tile * i, False))
    blocks.append((0, i, True))
    blocks.append((1, nq - 1 - i, True))
    s_next = scores(blocks[0][0], blocks[0][1])
    for k, (tile, j, diagonal) in enumerate(blocks):
        s_cur = s_next
        if k + 1 < len(blocks):
            s_next = scores(blocks[k + 1][0], blocks[k + 1][1])
        consume(s_cur, tile, j, k % 2, diagonal)

    lam = (jnp.exp(jnp.sum(lq1_ref[...] * lk1_ref[...], axis=-1, keepdims=True))
           - jnp.exp(jnp.sum(lq2_ref[...] * lk2_ref[...], axis=-1, keepdims=True)) + lambda_init)
    for tile, o_ref in enumerate((oa_ref, ob_ref)):
        acc = acc_ref[tile]
        o_all = acc[:, 0:DA_V_DIM] / acc[:, DA_V_DIM:2 * DA_V_DIM]
        o = o_all[0:t] - lam * o_all[t:2 * t]
        o = _rms(o, on_ref[...]) * (1.0 - lambda_init)
        o_ref[...] = o.astype(o_ref.dtype)


def _alibi_slopes_log2():
    return np.asarray([2.0 ** (-8.0 * (hh + 1) / DA_HEADS) * LOG2E for hh in range(DA_HEADS)], np.float32)


def _alibi_key_rows(seq):
    col = _alibi_slopes_log2()[:, None] * np.arange(seq, dtype=np.float32)[None, :]
    b1 = col.astype(BF16)
    r1 = col - b1.astype(np.float32)
    b2 = r1.astype(BF16)
    b3 = (r1 - b2.astype(np.float32)).astype(BF16)
    rows = np.zeros((DA_HEADS, LANES, seq), BF16)
    rows[:, 0], rows[:, 1], rows[:, 2] = b1, b2, b3
    return jnp.asarray(rows)


def _attention(q, kt, v, lq1, lk1, lq2, lk2, out_norm, batch, seq, t, lambda_init):
    nq = seq // t
    half = nq // 2
    width = DA_HEADS * DA_V_DIM
    slopes = jnp.asarray(_alibi_slopes_log2())
    vec = lambda b, h, i, sl: (0, 0)
    grid_spec = pltpu.PrefetchScalarGridSpec(
        num_scalar_prefetch=1,
        grid=(batch, DA_HEADS, half),
        in_specs=[pl.BlockSpec((t, LANES), lambda b, h, i, sl: (b * nq + i, h)),
                  pl.BlockSpec((t, LANES), lambda b, h, i, sl: (b * nq + nq - 1 - i, h)),
                  pl.BlockSpec((LANES, seq), lambda b, h, i, sl: (h, b)),
                  pl.BlockSpec((1, LANES, seq), lambda b, h, i, sl: (h, 0, 0)),
                  pl.BlockSpec((seq, LANES), lambda b, h, i, sl: (b, h)),
                  pl.BlockSpec((1, DA_QK_DIM), vec), pl.BlockSpec((1, DA_QK_DIM), vec),
                  pl.BlockSpec((1, DA_QK_DIM), vec), pl.BlockSpec((1, DA_QK_DIM), vec),
                  pl.BlockSpec((1, DA_V_DIM), vec)],
        out_specs=[pl.BlockSpec((t, LANES), lambda b, h, i, sl: (b * half + i, h)),
                   pl.BlockSpec((t, LANES), lambda b, h, i, sl: (b * half + half - 1 - i, h))],
        scratch_shapes=[pltpu.VMEM((2, 2 * t, 2 * LANES), BF16),
                        pltpu.VMEM((2, 2 * t, LANES), F32),
                        pltpu.VMEM((2, 2 * t, 2 * DA_V_DIM), F32),
                        pltpu.VMEM((2, 2 * t, t), BF16),
                        pltpu.VMEM((t, t), F32),
                        pltpu.VMEM((2 * LANES, seq), BF16)])
    return pl.pallas_call(
        functools.partial(_attn_kernel, t=t, nq=nq, lambda_init=lambda_init),
        grid_spec=grid_spec,
        out_shape=[jax.ShapeDtypeStruct((batch * seq // 2, width), BF16)] * 2,
        compiler_params=_cparams(3),
    )(slopes, q, q, kt, _alibi_key_rows(seq), v, lq1, lk1, lq2, lk2, out_norm)


def _gdn_kernel(u_ref, z_ref, ba_ref, cw_ref, alog_ref, dt_ref, on_ref, o_ref, stage_ref, state_ref, *, batch):
    c = GDN_CHUNK
    hd = GDN_HEADS * GDN_DIM
    step = pl.program_id(0)

    @pl.when(step == 0)
    def _init():
        state_ref[...] = jnp.zeros(state_ref.shape, F32)
        stage_ref[:, 0:8, :] = jnp.zeros((batch, 8, stage_ref.shape[2]), F32)

    row = lax.broadcasted_iota(I32, (c, c), 0)
    col = lax.broadcasted_iota(I32, (c, c), 1)
    tri = row >= col
    strict = row > col
    tril_ones = jnp.where(tri, 1.0, 0.0).astype(BF16)

    chains = []
    for b in range(batch):
        stage_ref[b, 8:8 + c, :] = u_ref[b]
        u = stage_ref[b]
        y = cw_ref[CONV_K - 1:CONV_K, :] * u
        for back in range(1, CONV_K):
            y = y + cw_ref[CONV_K - 1 - back:CONV_K - back, :] * pltpu.roll(u, back, 0)
        stage_ref[b, 0:8, :] = stage_ref[b, c:c + 8, :]
        y = _silu(y[8:8 + c, :])

        ba = ba_ref[b]
        beta_all = _sigmoid(ba)
        g_all = -jnp.exp(alog_ref[...]) * _softplus(ba + dt_ref[...])
        gcum_all = _mm_exact_lhs(tril_ones, g_all)
        gcum_t = gcum_all.T

        for h in range(GDN_HEADS):
            q = y[:, h * GDN_DIM:(h + 1) * GDN_DIM]
            k = y[:, hd + h * GDN_DIM:hd + (h + 1) * GDN_DIM]
            v = y[:, 2 * hd + h * GDN_DIM:2 * hd + (h + 1) * GDN_DIM]
            q = q * lax.rsqrt(jnp.sum(q * q, axis=-1, keepdims=True) + EPS) * (GDN_DIM ** -0.5)
            k = k * lax.rsqrt(jnp.sum(k * k, axis=-1, keepdims=True) + EPS)
            beta = beta_all[:, h:h + 1]
            gc = gcum_all[:, GDN_HEADS + h:GDN_HEADS + h + 1]
            gr = gcum_t[GDN_HEADS + h:GDN_HEADS + h + 1, :]
            g_last = gc[c - 1:c, :]
            decay = jnp.where(tri, jnp.exp(jnp.where(tri, gc - gr, 0.0)), 0.0)
            e_gc = jnp.exp(gc)
            chains.append(dict(
                b=b, h=h, decay=decay, g_last=g_last,
                p=jnp.where(strict, -(beta * _mm_nt(k, k) * decay), 0.0),
                rhs=jnp.concatenate([v * beta, k * (beta * e_gc)], axis=1),
                qk=_mm_nt(q, k) * decay, q_dec=q * e_gc, k_dec_t=(k * jnp.exp(g_last - gc)).T))

    def same_block(size):
        return (row // size) == (col // size)

    for ch in chains:
        ch["pk"] = jnp.where(same_block(GDN_BASE), ch["p"], 0.0)
        ch["x"] = ch["pk"]
    for _ in range(int(math.log2(GDN_BASE)) - 1):
        for ch in chains:
            ch["pk"] = _mm(ch["pk"], ch["pk"])
        for ch in chains:
            ch["x"] = ch["x"] + ch["pk"] + _mm(ch["x"], ch["pk"])
    size = GDN_BASE
    while size < c:
        pair_only = same_block(2 * size) & jnp.logical_not(same_block(size))
        for ch in chains:
            e = jnp.where(pair_only, -ch["p"], 0.0)
            ch["y"] = e + _mm(ch["x"], e)
        for ch in chains:
            ch["x"] = ch["x"] - (ch["y"] + _mm(ch["y"], ch["x"]))
        size *= 2

    for ch in chains:
        ch["s_prev"] = state_ref[ch["b"] * GDN_HEADS + ch["h"]]
        ch["o_inter"] = _mm(ch["q_dec"], ch["s_prev"])
        ch["sol"] = ch["rhs"] + _mm(ch["x"], ch["rhs"])
    for ch in chains:
        ch["v_new"] = ch["sol"][:, :GDN_DIM] - _mm(ch["sol"][:, GDN_DIM:], ch["s_prev"])
    for ch in chains:
        ch["o"] = ch["o_inter"] + _mm(ch["qk"], ch["v_new"])
        ch["s_new"] = ch["s_prev"] * jnp.exp(ch["g_last"]) + _mm(ch["k_dec_t"], ch["v_new"])
    outs = [[None] * GDN_HEADS for _ in range(batch)]
    for ch in chains:
        zg = z_ref[ch["b"], :, ch["h"] * GDN_DIM:(ch["h"] + 1) * GDN_DIM]
        outs[ch["b"]][ch["h"]] = (_rms(ch["o"], on_ref[...]) * _silu(zg)).astype(o_ref.dtype)
    o_ref[...] = jnp.stack([jnp.concatenate(heads, axis=1) for heads in outs], axis=0)
    state_ref[...] = jnp.stack([ch["s_new"] for ch in chains], axis=0)


def _gdn(qkv3, z3, ba3, conv_taps, alog_row, dt_row, out_norm):
    batch, seq, cw = qkv3.shape
    c = GDN_CHUNK
    hd = GDN_HEADS * GDN_DIM
    blk = lambda s: (0, s, 0)
    const = lambda s: (0, 0)
    return pl.pallas_call(
        functools.partial(_gdn_kernel, batch=batch),
        grid=(seq // c,),
        in_specs=[pl.BlockSpec((batch, c, cw), blk),
                  pl.BlockSpec((batch, c, hd), blk),
                  pl.BlockSpec((batch, c, LANES), blk),
                  pl.BlockSpec((CONV_K, cw), const),
                  pl.BlockSpec((1, LANES), const),
                  pl.BlockSpec((1, LANES), const),
                  pl.BlockSpec((1, GDN_DIM), const)],
        out_specs=pl.BlockSpec((batch, c, hd), blk),
        out_shape=jax.ShapeDtypeStruct((batch, seq, hd), BF16),
        scratch_shapes=[pltpu.VMEM((batch, c + 8, cw), F32),
                        pltpu.VMEM((batch * GDN_HEADS, GDN_DIM, GDN_DIM), F32)],
        compiler_params=_cparams(1),
    )(qkv3, z3, ba3, conv_taps, alog_row, dt_row, out_norm)


def _merge_kernel(x_ref, mod_ref, g1_ref, g2_ref, oa_lo_ref, oa_hi_ref, ob_ref, wgate_ref, wa_ref, wb_ref, wout_ref,
                  wrt_ref, brt_ref, x1_ref, h2_ref, lg_ref, *, tiles_per_batch):
    x = x_ref[...]
    d = x.shape[1]
    mod = mod_ref[0]
    shift1, scale1, gate1 = mod[0:1, :], mod[1:2, :], mod[2:3, :]
    shift2, scale2 = mod[3:4, :], mod[4:5, :]
    hb = (_rms(x, g1_ref[...]) * (1.0 + scale1) + shift1).astype(BF16)
    gates = _sigmoid(jnp.dot(hb, wgate_ref[...], preferred_element_type=F32))
    first_half = (pl.program_id(0) % tiles_per_batch) < tiles_per_batch // 2
    oa = jnp.where(first_half, oa_lo_ref[...], oa_hi_ref[...])
    ya = jnp.dot(oa, wa_ref[...], preferred_element_type=F32)
    yb = jnp.dot(ob_ref[...], wb_ref[...], preferred_element_type=F32)
    merged = gates[:, :d] * ya + gates[:, d:] * yb
    x1 = x + gate1 * _mm(merged, wout_ref[...])
    x1_ref[...] = x1
    h2 = _rms(x1, g2_ref[...]) * (1.0 + scale2) + shift2
    h2_ref[...] = _pack_bf16_pairs(h2)
    lg_ref[...] = _mm3(h2, wrt_ref[...]) + brt_ref[...]


def _merge(x2, mod3, g1, g2, oa_lo, oa_hi, ob, w_gate, w_a, w_b, w_out, w_rt, b_rt, tm, tiles_per_batch):
    n, d = x2.shape
    hp = tiles_per_batch // 2
    const = lambda i: (0, 0)
    row = lambda i: (i, 0)
    lo = lambda i: ((i // tiles_per_batch) * hp + jnp.minimum(i % tiles_per_batch, hp - 1), 0)
    hi = lambda i: ((i // tiles_per_batch) * hp + jnp.maximum(i % tiles_per_batch - hp, 0), 0)
    return pl.pallas_call(
        functools.partial(_merge_kernel, tiles_per_batch=tiles_per_batch),
        grid=(n // tm,),
        in_specs=[pl.BlockSpec((tm, d), row),
                  pl.BlockSpec((1, 6, d), lambda i: (i // tiles_per_batch, 0, 0)),
                  pl.BlockSpec((1, d), const), pl.BlockSpec((1, d), const),
                  pl.BlockSpec((tm, oa_lo.shape[1]), lo), pl.BlockSpec((tm, oa_hi.shape[1]), hi),
                  pl.BlockSpec((tm, ob.shape[1]), row),
                  pl.BlockSpec(w_gate.shape, const), pl.BlockSpec(w_a.shape, const),
                  pl.BlockSpec(w_b.shape, const), pl.BlockSpec(w_out.shape, const),
                  pl.BlockSpec(w_rt.shape, const), pl.BlockSpec((1, LANES), const)],
        out_specs=[pl.BlockSpec((tm, d), row), pl.BlockSpec((tm, d // 2), row), pl.BlockSpec((tm, LANES), row)],
        out_shape=[jax.ShapeDtypeStruct((n, d), F32), jax.ShapeDtypeStruct((n, d // 2), I32),
                   jax.ShapeDtypeStruct((n, LANES), F32)],
        compiler_params=_cparams(1),
    )(x2, mod3, g1, g2, oa_lo, oa_hi, ob, w_gate, w_a, w_b, w_out, w_rt, b_rt)


def _route_kernel(lg_ref, ri_ref, rf_ref, cnt_ref, carry_ref):
    step = pl.program_id(0)
    tm = lg_ref.shape[0]

    @pl.when(step == 0)
    def _init():
        carry_ref[...] = jnp.zeros(carry_ref.shape, F32)

    lg = lg_ref[...]
    lane = lax.broadcasted_iota(I32, lg.shape, 1)
    big = jnp.int32(LANES)
    is_grp = (lane >= N_EXPERTS) & (lane < N_EXPERTS + N_GROUPS)
    gl = jnp.where(is_grp, lg, NEG_INF)
    gmax = jnp.max(gl, axis=-1, keepdims=True)
    g_top = jnp.min(jnp.where(is_grp & (gl == gmax), lane, big), axis=-1, keepdims=True) - N_EXPERTS
    g_top_p = 1.0 / jnp.sum(jnp.where(is_grp, jnp.exp(gl - gmax), 0.0), axis=-1, keepdims=True)
    in_grp = (lane < N_EXPERTS) & ((lane // EXPERTS_PER_GROUP) == g_top)
    el = jnp.where(in_grp, lg, NEG_INF)
    v1 = jnp.max(el, axis=-1, keepdims=True)
    e1 = jnp.min(jnp.where(in_grp & (el == v1), lane, big), axis=-1, keepdims=True)
    rest = in_grp & (lane != e1)
    el2 = jnp.where(rest, lg, NEG_INF)
    v2 = jnp.max(el2, axis=-1, keepdims=True)
    e2 = jnp.min(jnp.where(rest & (el2 == v2), lane, big), axis=-1, keepdims=True)
    ex = jnp.exp(v2 - v1)
    w1 = g_top_p / (1.0 + ex)
    w2 = g_top_p * ex / (1.0 + ex)

    oh1 = lane == e1
    oh2 = lane == e2
    ohs = (jnp.where(oh1, 1.0, 0.0) + jnp.where(oh2, 1.0, 0.0)).astype(BF16)
    r = lax.broadcasted_iota(I32, (tm, tm), 0)
    c = lax.broadcasted_iota(I32, (tm, tm), 1)
    before = jnp.where(r > c, 1.0, 0.0).astype(BF16)
    prior = jnp.dot(before, ohs, preferred_element_type=F32) + carry_ref[0:1, :]
    rank1 = jnp.sum(jnp.where(oh1, prior, 0.0), axis=-1, keepdims=True)
    rank2 = jnp.sum(jnp.where(oh2, prior, 0.0), axis=-1, keepdims=True)
    carry_ref[...] = carry_ref[...] + jnp.sum(ohs.astype(F32), axis=0, keepdims=True)
    cnt_ref[...] = carry_ref[...]

    r1 = rank1.astype(I32)
    r2 = rank2.astype(I32)
    ri = jnp.where(lane == 0, e1, jnp.where(lane == 1, e2, jnp.where(lane == 2, r1, jnp.where(lane == 3, r2, 0))))
    ri_ref[...] = ri
    rf_ref[...] = jnp.where(lane == 0, w1, jnp.where(lane == 1, w2, 0.0))


def _route(logits, tm):
    n = logits.shape[0]
    row = lambda i: (i, 0)
    return pl.pallas_call(
        _route_kernel,
        grid=(n // tm,),
        in_specs=[pl.BlockSpec((tm, LANES), row)],
        out_specs=[pl.BlockSpec((tm, LANES), row), pl.BlockSpec((tm, LANES), row),
                   pl.BlockSpec((8, LANES), lambda i: (0, 0))],
        out_shape=[jax.ShapeDtypeStruct((n, LANES), I32), jax.ShapeDtypeStruct((n, LANES), F32),
                   jax.ShapeDtypeStruct((8, LANES), F32)],
        scratch_shapes=[pltpu.VMEM((8, LANES), F32)],
        compiler_params=_cparams(1),
    )(logits)


def _moe_kernel(be_ref, act_ref, f1_ref, f3_ref, f2_ref, xb_ref, w1_ref, w3_ref, w2_ref, o_ref,
                w1b_ref, w3b_ref, w2b_ref):
    i = pl.program_id(0)
    prev = be_ref[jnp.maximum(i - 1, 0)]

    @pl.when((i == 0) | (be_ref[i] != prev))
    def _cast_weights():
        w1b_ref[...] = w1_ref[...].astype(BF16)
        w3b_ref[...] = w3_ref[...].astype(BF16)
        w2b_ref[...] = w2_ref[...].astype(BF16)

    for piece in range(MOE_ROWS // MOE_SUB):
        rows = slice(piece * MOE_SUB, (piece + 1) * MOE_SUB)
        valid = act_ref[i] - piece * MOE_SUB

        @pl.when(valid > 0)
        def _compute(rows=rows, valid=valid):
            xp = xb_ref[rows, :]
            r = lax.broadcasted_iota(I32, xp.shape, 0)
            xb = _unpack_bf16_pairs(jnp.where(r < valid, xp, 0))
            a = jnp.dot(xb, w1b_ref[...], preferred_element_type=F32)
            g = jnp.dot(xb, w3b_ref[...], preferred_element_type=F32)
            hid = (_silu(a) * g).astype(BF16)
            o_ref[rows, :] = _pack_bf16_pairs(jnp.dot(hid, w2b_ref[...], preferred_element_type=F32))

        @pl.when(valid <= 0)
        def _idle(rows=rows):
            o_ref[rows, :] = jnp.zeros((MOE_SUB, o_ref.shape[1]), o_ref.dtype)


def _moe(blk_expert, blk_rows, fetch, xb, w1, w3, w2):
    cap = xb.shape[0]
    d = w1.shape[1]
    nb = cap // MOE_ROWS
    de = w1.shape[2]
    grid_spec = pltpu.PrefetchScalarGridSpec(
        num_scalar_prefetch=5,
        grid=(nb,),
        in_specs=[pl.BlockSpec((MOE_ROWS, d // 2), lambda i, be, act, f1, f3, f2: (i, 0)),
                  pl.BlockSpec((None, d, de), lambda i, be, act, f1, f3, f2: (f1[i], 0, 0)),
                  pl.BlockSpec((None, d, de), lambda i, be, act, f1, f3, f2: (f3[i], 0, 0)),
                  pl.BlockSpec((None, de, d), lambda i, be, act, f1, f3, f2: (f2[i], 0, 0))],
        out_specs=pl.BlockSpec((MOE_ROWS, d // 2), lambda i, be, act, f1, f3, f2: (i, 0)),
        scratch_shapes=[pltpu.VMEM((d, de), BF16), pltpu.VMEM((d, de), BF16), pltpu.VMEM((de, d), BF16)])
    return pl.pallas_call(
        _moe_kernel,
        grid_spec=grid_spec,
        out_shape=jax.ShapeDtypeStruct((cap, d // 2), I32),
        compiler_params=_cparams(1),
    )(blk_expert, blk_rows, *fetch, xb, w1, w3, w2)


def _dest_kernel(ri_ref, pstart_ref, o_ref):
    ri = ri_ref[...]
    lane = lax.broadcasted_iota(I32, ri.shape, 1)
    pstart = pstart_ref[...]
    base1 = jnp.sum(jnp.where(lane == ri[:, 0:1], pstart, 0), axis=-1, keepdims=True)
    base2 = jnp.sum(jnp.where(lane == ri[:, 1:2], pstart, 0), axis=-1, keepdims=True)
    dest = jnp.where(lane == 0, base1 + ri[:, 2:3], jnp.where(lane == 1, base2 + ri[:, 3:4], 0))
    o_ref[...] = dest.astype(F32).T[0:8, :].astype(I32)


def _dest(ri, pstart_row, tm):
    n = ri.shape[0]
    return pl.pallas_call(
        _dest_kernel,
        grid=(n // tm,),
        in_specs=[pl.BlockSpec((tm, LANES), lambda i: (i, 0)), pl.BlockSpec((1, LANES), lambda i: (0, 0))],
        out_specs=pl.BlockSpec((8, tm), lambda i: (0, i)),
        out_shape=jax.ShapeDtypeStruct((8, n), I32),
        compiler_params=_cparams(1),
    )(ri, pstart_row)


def _sc_mesh():
    return plsc.VectorSubcoreMesh(core_axis_name="core", subcore_axis_name="subcore")


def _sc_scatter_rows(x, idx, cap):
    n, d = x.shape
    windows = idx.shape[0]
    tiles = n // SC_WINDOW
    mesh = _sc_mesh()
    workers = mesh.num_cores * mesh.num_subcores
    per_worker = windows // workers

    @functools.partial(pl.kernel, out_type=jax.ShapeDtypeStruct((cap, d), x.dtype), mesh=mesh,
                       scratch_types=[pltpu.VMEM((1, SC_WINDOW), I32), pltpu.VMEM((SC_WINDOW, d), x.dtype)])
    def scatter(x_hbm, i_hbm, o_hbm, idx_v, rows_v):
        wid = lax.axis_index("subcore") * mesh.num_cores + lax.axis_index("core")

        @pl.loop(0, per_worker)
        def _(j):
            w = wid * per_worker + j
            pltpu.sync_copy(i_hbm.at[pl.ds(w, 1)], idx_v)
            pltpu.sync_copy(x_hbm.at[pl.ds(lax.rem(w, tiles) * SC_WINDOW, SC_WINDOW)], rows_v)
            pltpu.sync_copy(rows_v, o_hbm.at[idx_v.at[0]])

    return scatter(x, idx)


def _sc_gather_rows(table, idx):
    d = table.shape[1]
    windows = idx.shape[0]
    mesh = _sc_mesh()
    workers = mesh.num_cores * mesh.num_subcores
    per_worker = windows // workers

    @functools.partial(pl.kernel, out_type=jax.ShapeDtypeStruct((windows * SC_WINDOW, d), table.dtype), mesh=mesh,
                       scratch_types=[pltpu.VMEM((1, SC_WINDOW), I32), pltpu.VMEM((SC_WINDOW, d), table.dtype)])
    def gather(t_hbm, i_hbm, o_hbm, idx_v, rows_v):
        wid = lax.axis_index("subcore") * mesh.num_cores + lax.axis_index("core")

        @pl.loop(0, per_worker)
        def _(j):
            w = wid * per_worker + j
            pltpu.sync_copy(i_hbm.at[pl.ds(w, 1)], idx_v)
            pltpu.sync_copy(t_hbm.at[idx_v.at[0]], rows_v)
            pltpu.sync_copy(rows_v, o_hbm.at[pl.ds(w * SC_WINDOW, SC_WINDOW)])

    return gather(table, idx)


def _combine_kernel(x1_ref, mod_ref, rf_ref, ya_ref, yb_ref, o_ref):
    gate2 = mod_ref[0][5:6, :]
    rf = rf_ref[...]
    ya = _unpack_bf16_pairs(ya_ref[...]).astype(F32)
    yb = _unpack_bf16_pairs(yb_ref[...]).astype(F32)
    y = rf[:, 0:1] * ya + rf[:, 1:2] * yb
    o_ref[...] = x1_ref[...] + gate2 * y


def _combine(x1, mod3, rf, y12, tm, tiles_per_batch):
    n, d = x1.shape
    tiles = n // tm
    row = lambda i: (i, 0)
    return pl.pallas_call(
        _combine_kernel,
        grid=(tiles,),
        in_specs=[pl.BlockSpec((tm, d), row),
                  pl.BlockSpec((1, 6, d), lambda i: (i // tiles_per_batch, 0, 0)),
                  pl.BlockSpec((tm, LANES), row),
                  pl.BlockSpec((tm, d // 2), row), pl.BlockSpec((tm, d // 2), lambda i: (i + tiles, 0))],
        out_specs=pl.BlockSpec((tm, d), row),
        out_shape=jax.ShapeDtypeStruct((n, d), F32),
        compiler_params=_cparams(1),
    )(x1, mod3, rf, y12, y12)


def _pad_lanes(a, offset=0):
    return jnp.pad(a, ((0, 0), (offset, LANES - offset - a.shape[1])))


def _layer(x, c, layer, w_ada, b_ada, norm1_gain, w_in, da_q_norm, da_k_norm, lq1, lk1, lq2, lk2, da_out_norm,
           gdn_conv, gdn_a_log, gdn_dt_bias, gdn_out_norm, w_branch_a, w_branch_b, w_out, norm2_gain,
           w_group, b_group, w_router, b_router, w1, w3, w2):
    batch, seq, d = x.shape
    n = batch * seq
    tm = min(512, seq)
    tiles_per_batch = seq // tm
    t_attn = min(512, seq)
    lambda_init = 0.8 - 0.6 * math.exp(-0.3 * layer)

    mod = _ada(jnp.pad(c, ((0, 8 - batch), (0, 0))), w_ada, b_ada)[:batch]
    mod3 = mod.reshape(batch, 6, d)

    da_w = DA_HEADS * 2 * DA_QK_DIM
    dv_w = DA_HEADS * DA_V_DIM
    gd_w = GDN_HEADS * GDN_DIM
    o0 = 2 * da_w + dv_w
    o1 = o0 + 4 * gd_w
    o2 = o1 + 2 * GDN_HEADS
    w_da = w_in[:, :o0].astype(BF16)
    w_g = w_in[:, o0:o1].astype(BF16)
    w_ba = _pad_lanes(w_in[:, o1:o2]).astype(BF16)
    w_gate = w_in[:, o2:].astype(BF16)
    qn = jnp.tile(da_q_norm, 2 * DA_HEADS).reshape(1, da_w)
    kn = jnp.tile(da_k_norm, 2 * DA_HEADS).reshape(1, da_w)

    x2 = x.reshape(n, d)
    g1 = norm1_gain.reshape(1, d)
    q, kt, v, gqkv, z, ba = _proj(x2, mod3, g1, w_da, w_g, w_ba, qn, kn, tm, tiles_per_batch)

    assert t_attn == tm
    oa_lo, oa_hi = _attention(q, kt, v, lq1.reshape(1, -1), lk1.reshape(1, -1), lq2.reshape(1, -1),
                              lk2.reshape(1, -1), da_out_norm.reshape(1, -1), batch, seq, t_attn, lambda_init)

    alog_row = _pad_lanes(gdn_a_log.reshape(1, -1), GDN_HEADS)
    dt_row = _pad_lanes(gdn_dt_bias.reshape(1, -1), GDN_HEADS)
    o_b = _gdn(gqkv.reshape(batch, seq, -1), z.reshape(batch, seq, -1), ba.reshape(batch, seq, -1),
               gdn_conv, alog_row, dt_row, gdn_out_norm.reshape(1, -1)).reshape(n, gd_w)

    w_rt = _pad_lanes(jnp.concatenate([w_router, w_group], axis=1))
    b_rt = _pad_lanes(jnp.concatenate([b_router, b_group]).reshape(1, -1))
    x1, h2, logits = _merge(x2, mod3, g1, norm2_gain.reshape(1, d), oa_lo, oa_hi, o_b, w_gate,
                            w_branch_a.astype(BF16), w_branch_b.astype(BF16), w_out.astype(BF16),
                            w_rt, b_rt, tm, tiles_per_batch)

    ri, rf, cnt = _route(logits, tm)

    counts = cnt[0, :N_EXPERTS].astype(I32)
    padded = (counts + MOE_ROWS - 1) // MOE_ROWS * MOE_ROWS
    pend = jnp.cumsum(padded)
    pstart = pend - padded
    cap = 2 * n + N_EXPERTS * MOE_ROWS
    nb = cap // MOE_ROWS
    blk_start = jnp.arange(nb, dtype=I32) * MOE_ROWS
    blk_expert = jnp.minimum(jnp.sum(pend[None, :] <= blk_start[:, None], axis=1), N_EXPERTS - 1).astype(I32)
    blk_rows = jnp.clip((pstart + counts)[blk_expert] - blk_start, 0, MOE_ROWS).astype(I32)
    blk_rows = jnp.where(blk_start < pend[-1], blk_rows, 0)
    first_blk = pstart // MOE_ROWS
    n_blk = padded // MOE_ROWS
    pos = jnp.arange(nb, dtype=I32) - first_blk[blk_expert]
    nxt = blk_expert[jnp.clip(first_blk[blk_expert] + n_blk[blk_expert], 0, nb - 1)]
    fetch = [jnp.where(pos < jnp.minimum(lag, n_blk[blk_expert]), blk_expert, nxt).astype(I32) for lag in (1, 2, 3)]
    dest = _dest(ri, _pad_lanes(pstart.reshape(1, -1)), tm)
    idx = dest[0:2].reshape(2 * n // SC_WINDOW, SC_WINDOW)

    xb = _sc_scatter_rows(h2, idx, cap)
    yb = _moe(blk_expert, blk_rows, fetch, xb, w1, w3, w2)
    y12 = _sc_gather_rows(yb, idx)
    out = _combine(x1, mod3, rf, y12, tm, tiles_per_batch)
    return out.reshape(batch, seq, d)


def kernel(x, c, w_ada, b_ada, norm1_gain, w_in, da_q_norm, da_k_norm, da_lambda_q1, da_lambda_k1, da_lambda_q2,
           da_lambda_k2, da_out_norm, gdn_conv, gdn_a_log, gdn_dt_bias, gdn_out_norm, w_branch_a, w_branch_b,
           w_out, norm2_gain, w_group, b_group, w_router, b_router, w1, w3, w2):
    for layer in range(w_ada.shape[0]):
        x = _layer(x, c, layer, w_ada[layer], b_ada[layer], norm1_gain[layer], w_in[layer], da_q_norm[layer],
                   da_k_norm[layer], da_lambda_q1[layer], da_lambda_k1[layer], da_lambda_q2[layer],
                   da_lambda_k2[layer], da_out_norm[layer], gdn_conv[layer], gdn_a_log[layer], gdn_dt_bias[layer],
                   gdn_out_norm[layer], w_branch_a[layer], w_branch_b[layer], w_out[layer], norm2_gain[layer],
                   w_group[layer], b_group[layer], w_router[layer], b_router[layer], w1[layer], w3[layer], w2[layer])
    return x
```

```python
import functools
import math

import jax
import jax.numpy as jnp
import numpy as np
from jax import lax
from jax.experimental import pallas as pl
from jax.experimental.pallas import tpu as pltpu
from jax.experimental.pallas import tpu_sc as plsc

F32 = jnp.float32
BF16 = jnp.bfloat16
I32 = jnp.int32

EPS = 1e-6
NEG_INF = -1e30
MASK_CHUNK = 64
LOG2E = 1.4426950408889634
ALIBI_ROWS = 3

DA_HEADS = 4
DA_QK_DIM = 64
DA_V_DIM = 128
GDN_HEADS = 4
GDN_DIM = 128
CONV_K = 4
N_GROUPS = 4
EXPERTS_PER_GROUP = 8
N_EXPERTS = N_GROUPS * EXPERTS_PER_GROUP
D_EXPERT = 512

LANES = 128
VMEM_LIMIT = 56 * 1024 * 1024

ATTN_UNROLL = 4
GDN_CHUNK = 128
GDN_BASE = 16
MOE_ROWS = 512
MOE_SUB = 256
SC_WINDOW = 128


def _cparams(n_axes):
    return pltpu.CompilerParams(dimension_semantics=("arbitrary",) * n_axes,
                                vmem_limit_bytes=VMEM_LIMIT)


def _mm(a, b):
    return jnp.dot(a.astype(BF16), b.astype(BF16), preferred_element_type=F32)


def _mm_nt(a, b):
    return lax.dot_general(a.astype(BF16), b.astype(BF16), (((1,), (1,)), ((), ())),
                           preferred_element_type=F32)


def _split2(a):
    hi = a.astype(BF16)
    lo = (a - hi.astype(F32)).astype(BF16)
    return hi, lo


def _mm3(a, b):
    ah, al = _split2(a)
    bh, bl = _split2(b)
    out = jnp.dot(ah, bh, preferred_element_type=F32)
    out = out + jnp.dot(ah, bl, preferred_element_type=F32)
    out = out + jnp.dot(al, bh, preferred_element_type=F32)
    return out


def _mm_exact_lhs(a_bf16, b):
    b1 = b.astype(BF16)
    r1 = b - b1.astype(F32)
    b2 = r1.astype(BF16)
    b3 = (r1 - b2.astype(F32)).astype(BF16)
    out = jnp.dot(a_bf16, b1, preferred_element_type=F32)
    out = out + jnp.dot(a_bf16, b2, preferred_element_type=F32)
    out = out + jnp.dot(a_bf16, b3, preferred_element_type=F32)
    return out


def _sigmoid(x):
    return 1.0 / (1.0 + jnp.exp(-x))


def _silu(x):
    return x * _sigmoid(x)


def _softplus(x):
    return jnp.maximum(x, 0.0) + jnp.log(1.0 + jnp.exp(-jnp.abs(x)))


def _pack_bf16_pairs(x):
    w = x.shape[1] // 2
    bits = lax.bitcast_convert_type(x.astype(BF16).astype(F32), jnp.uint32)
    lo = lax.shift_right_logical(bits[:, :w], jnp.uint32(16))
    hi = bits[:, w:] & jnp.uint32(0xFFFF0000)
    return lax.bitcast_convert_type(hi | lo, I32)


def _unpack_bf16_pairs(p):
    bits = lax.bitcast_convert_type(p, jnp.uint32)
    lo = lax.bitcast_convert_type(lax.shift_left(bits, jnp.uint32(16)), F32)
    hi = lax.bitcast_convert_type(bits & jnp.uint32(0xFFFF0000), F32)
    return jnp.concatenate([lo, hi], axis=1).astype(BF16)


def _rms(x, gain):
    return x * lax.rsqrt(jnp.mean(x * x, axis=-1, keepdims=True) + EPS) * gain


def _ada_kernel(c_ref, w_ref, b_ref, o_ref):
    sc = _silu(c_ref[...])
    o_ref[...] = _mm3(sc, w_ref[...]) + b_ref[...]


def _ada(c_pad, w_ada, b_ada):
    rows, d = c_pad.shape
    n = w_ada.shape[1]
    tn = d
    return pl.pallas_call(
        _ada_kernel,
        grid=(n // tn,),
        in_specs=[pl.BlockSpec((rows, d), lambda j: (0, 0)),
                  pl.BlockSpec((d, tn), lambda j: (0, j)),
                  pl.BlockSpec((1, tn), lambda j: (0, j))],
        out_specs=pl.BlockSpec((rows, tn), lambda j: (0, j)),
        out_shape=jax.ShapeDtypeStruct((rows, n), F32),
        compiler_params=_cparams(1),
    )(c_pad, w_ada, b_ada.reshape(1, n))


def _group_rms64(x, gain):
    tm, width = x.shape
    lane = lax.broadcasted_iota(I32, (tm, LANES), 1)
    low = lane < DA_QK_DIM
    parts = []
    for j in range(width // LANES):
        blk = x[:, j * LANES:(j + 1) * LANES]
        sq = blk * blk
        s_lo = jnp.sum(jnp.where(low, sq, 0.0), axis=-1, keepdims=True)
        s_hi = jnp.sum(jnp.where(low, 0.0, sq), axis=-1, keepdims=True)
        ms = jnp.where(low, s_lo, s_hi) * (1.0 / DA_QK_DIM)
        parts.append(blk * lax.rsqrt(ms + EPS))
    return jnp.concatenate(parts, axis=-1) * gain


def _proj_kernel(x_ref, mod_ref, g1_ref, wda_ref, wg_ref, wba_ref, qn_ref, kn_ref,
                 q_ref, k_ref, v_ref, gqkv_ref, z_ref, ba_ref):
    x = x_ref[...]
    mod = mod_ref[0]
    shift, scale = mod[0:1, :], mod[1:2, :]
    hb = (_rms(x, g1_ref[...]) * (1.0 + scale) + shift).astype(BF16)
    da_w = DA_HEADS * 2 * DA_QK_DIM
    da = jnp.dot(hb, wda_ref[...], preferred_element_type=F32)
    q = _group_rms64(da[:, :da_w], qn_ref[...]) * (DA_QK_DIM ** -0.5 * LOG2E)
    k = _group_rms64(da[:, da_w:2 * da_w], kn_ref[...])
    q_ref[...] = q.astype(BF16)
    k_ref[...] = k.T.astype(BF16)
    v_ref[...] = da[:, 2 * da_w:].astype(BF16)
    g = jnp.dot(hb, wg_ref[...], preferred_element_type=F32)
    conv_w = 3 * GDN_HEADS * GDN_DIM
    gqkv_ref[...] = g[:, :conv_w]
    z_ref[...] = g[:, conv_w:]
    ba_ref[...] = jnp.dot(hb, wba_ref[...], preferred_element_type=F32)


def _proj(x2, mod3, g1, w_da, w_g, w_ba, qn, kn, tm, tiles_per_batch):
    n, d = x2.shape
    da_w = DA_HEADS * 2 * DA_QK_DIM
    dv_w = DA_HEADS * DA_V_DIM
    conv_w = 3 * GDN_HEADS * GDN_DIM
    z_w = GDN_HEADS * GDN_DIM
    const = lambda i: (0, 0)
    row = lambda i: (i, 0)
    return pl.pallas_call(
        _proj_kernel,
        grid=(n // tm,),
        in_specs=[pl.BlockSpec((tm, d), row),
                  pl.BlockSpec((1, 6, d), lambda i: (i // tiles_per_batch, 0, 0)),
                  pl.BlockSpec((1, d), const),
                  pl.BlockSpec(w_da.shape, const),
                  pl.BlockSpec(w_g.shape, const),
                  pl.BlockSpec(w_ba.shape, const),
                  pl.BlockSpec((1, da_w), const),
                  pl.BlockSpec((1, da_w), const)],
        out_specs=[pl.BlockSpec((tm, da_w), row), pl.BlockSpec((da_w, tm), lambda i: (0, i)),
                   pl.BlockSpec((tm, dv_w), row), pl.BlockSpec((tm, conv_w), row),
                   pl.BlockSpec((tm, z_w), row), pl.BlockSpec((tm, LANES), row)],
        out_shape=[jax.ShapeDtypeStruct((n, da_w), BF16), jax.ShapeDtypeStruct((da_w, n), BF16),
                   jax.ShapeDtypeStruct((n, dv_w), BF16), jax.ShapeDtypeStruct((n, conv_w), F32),
                   jax.ShapeDtypeStruct((n, z_w), F32), jax.ShapeDtypeStruct((n, LANES), F32)],
        compiler_params=_cparams(1),
    )(x2, mod3, g1, w_da, w_g, w_ba, qn, kn)


def _attn_step(h, i, slopes_ref, qa_ref, qb_ref, kt_ref, al_ref, v_ref, lq1_ref, lk1_ref, lq2_ref, lk2_ref, on_ref,
               oa_ref, ob_ref, qs_ref, m_ref, acc_ref, p_ref, ddiag_ref, kta_ref, *, t, nq, lambda_init,
               between_blocks):
    slope2 = slopes_ref[h]
    reps = t // LANES

    @pl.when(i == 0)
    def _tables():
        r = lax.broadcasted_iota(I32, (t, t), 0)
        c = lax.broadcasted_iota(I32, (t, t), 1)
        ahead = jnp.maximum(c - r, 0).astype(F32)
        allowed = (c // MASK_CHUNK) <= (r // MASK_CHUNK)
        ddiag_ref[...] = jnp.where(allowed, (-2.0 * slope2) * ahead, NEG_INF)
        kta_ref[0:LANES, :] = kt_ref[...]
        kta_ref[LANES:2 * LANES, :] = al_ref[0]

    for tile, q_ref in enumerate((qa_ref, qb_ref)):
        q = q_ref[...]
        lane = lax.broadcasted_iota(I32, q.shape, 1)
        zero = jnp.zeros_like(q)
        ones = jnp.where(lane < ALIBI_ROWS, 1.0, 0.0).astype(BF16)
        qs_ref[tile, 0:t, :] = jnp.concatenate([jnp.where(lane < DA_QK_DIM, q, zero), ones], axis=1)
        qs_ref[tile, t:2 * t, :] = jnp.concatenate([jnp.where(lane < DA_QK_DIM, zero, q), ones], axis=1)
    m_ref[...] = jnp.full(m_ref.shape, NEG_INF, F32)
    acc_ref[...] = jnp.zeros(acc_ref.shape, F32)

    def scores(tile, j):
        start = pl.multiple_of(j * t, t)
        return jnp.dot(qs_ref[tile], kta_ref[:, pl.ds(start, t)], preferred_element_type=F32)

    def consume(s, tile, j, slot, diagonal):
        start = pl.multiple_of(j * t, t)
        alphas = []
        for half in range(2):
            rows = slice(half * t, (half + 1) * t)
            sb = s[rows] + ddiag_ref[...] if diagonal else s[rows]
            m_prev = m_ref[tile, rows]
            m_next = jnp.maximum(m_prev, jnp.max(sb, axis=-1, keepdims=True))
            alphas.append(jnp.exp2(m_prev - m_next))
            p_ref[slot, rows] = jnp.exp2(sb - jnp.tile(m_next, (1, reps))).astype(BF16)
            m_ref[tile, rows] = m_next
        v = v_ref[pl.ds(start, t), :]
        vext = jnp.concatenate([v, jnp.ones_like(v)], axis=1)
        pv = jnp.dot(p_ref[slot], vext, preferred_element_type=F32)
        for half in range(2):
            rows = slice(half * t, (half + 1) * t)
            acc_ref[tile, rows] = jnp.tile(alphas[half], (1, 2)) * acc_ref[tile, rows] + pv[rows]

    blocks = []
    for k in range(nq - 1):
        tile = jnp.where(k >= i, 1, 0)
        blocks.append((tile, k - tile * i, False))
    blocks.append((0, i, True))
    blocks.append((1, nq - 1 - i, True))
    s_next = scores(blocks[0][0], blocks[0][1])
    for k, (tile, j, diagonal) in enumerate(blocks):
        s_cur = s_next
        if k + 1 < len(blocks):
            s_next = scores(blocks[k + 1][0], blocks[k + 1][1])
        consume(s_cur, tile, j, k % 2, diagonal)
        between_blocks()

    lam =(jnp.exp(jnp.sum(lq1_ref[...] * lk1_ref[...], axis=-1, keepdims=True))
           - jnp.exp(jnp.sum(lq2_ref[...] * lk2_ref[...], axis=-1, keepdims=True)) + lambda_init)
    for tile, o_ref in enumerate((oa_ref, ob_ref)):
        acc = acc_ref[tile]
        o_all = acc[:, 0:DA_V_DIM] / acc[:, DA_V_DIM:2 * DA_V_DIM]
        o = o_all[0:t] - lam * o_all[t:2 * t]
        o = _rms(o, on_ref[...]) * (1.0 - lambda_init)
        o_ref[...] = o.astype(o_ref.dtype)


def _alibi_slopes_log2():
    return np.asarray([2.0 ** (-8.0 * (hh + 1) / DA_HEADS) * LOG2E for hh in range(DA_HEADS)], np.float32)


def _alibi_key_rows(seq):
    col = _alibi_slopes_log2()[:, None] * np.arange(seq, dtype=np.float32)[None, :]
    b1 = col.astype(BF16)
    r1 = col - b1.astype(np.float32)
    b2 = r1.astype(BF16)
    b3 = (r1 - b2.astype(np.float32)).astype(BF16)
    rows = np.zeros((DA_HEADS, LANES, seq), BF16)
    rows[:, 0], rows[:, 1], rows[:, 2] = b1, b2, b3
    return jnp.asarray(rows)


N_ATTN_IN = 10
N_GDN_IN = 7
N_ATTN_SCRATCH = 6


def _mixer_kernel(slopes_ref, *refs, t, nq, batch, lambda_init):
    attn_in = refs[:N_ATTN_IN]
    gdn_in = refs[N_ATTN_IN:N_ATTN_IN + N_GDN_IN]
    outs = refs[N_ATTN_IN + N_GDN_IN:N_ATTN_IN + N_GDN_IN + 3]
    scratch = refs[N_ATTN_IN + N_GDN_IN + 3:]
    b, h, i = pl.program_id(0), pl.program_id(1), pl.program_id(2)
    chunk = (b * DA_HEADS + h) * (nq // 2) + i
    gdn = _gdn_phases(chunk, *gdn_in, outs[2], *scratch[N_ATTN_SCRATCH:], batch=batch)
    next(gdn)
    _attn_step(h, i, slopes_ref, *attn_in, outs[0], outs[1], *scratch[:N_ATTN_SCRATCH],
               t=t, nq=nq, lambda_init=lambda_init, between_blocks=lambda: next(gdn, None))
    for _ in gdn:
        pass


def _mixer(q, kt, v, lq1, lk1, lq2, lk2, out_norm_a, qkv3, z3, ba3, conv_taps, alog_row, dt_row, out_norm_b,
           batch, seq, t, lambda_init):
    nq = seq // t
    half = nq // 2
    width = DA_HEADS * DA_V_DIM
    c = GDN_CHUNK
    hd = GDN_HEADS * GDN_DIM
    cw = qkv3.shape[2]
    assert seq // c == batch * DA_HEADS * half
    slopes = jnp.asarray(_alibi_slopes_log2())
    vec = lambda b, h, i, sl: (0, 0)
    blk = lambda b, h, i, sl: (0, (b * DA_HEADS + h) * half + i, 0)
    grid_spec = pltpu.PrefetchScalarGridSpec(
        num_scalar_prefetch=1,
        grid=(batch, DA_HEADS, half),
        in_specs=[pl.BlockSpec((t, LANES), lambda b, h, i, sl: (b * nq + i, h)),
                  pl.BlockSpec((t, LANES), lambda b, h, i, sl: (b * nq + nq - 1 - i, h)),
                  pl.BlockSpec((LANES, seq), lambda b, h, i, sl: (h, b)),
                  pl.BlockSpec((1, LANES, seq), lambda b, h, i, sl: (h, 0, 0)),
                  pl.BlockSpec((seq, LANES), lambda b, h, i, sl: (b, h)),
                  pl.BlockSpec((1, DA_QK_DIM), vec), pl.BlockSpec((1, DA_QK_DIM), vec),
                  pl.BlockSpec((1, DA_QK_DIM), vec), pl.BlockSpec((1, DA_QK_DIM), vec),
                  pl.BlockSpec((1, DA_V_DIM), vec),
                  pl.BlockSpec((batch, c, cw), blk),
                  pl.BlockSpec((batch, c, hd), blk),
                  pl.BlockSpec((batch, c, LANES), blk),
                  pl.BlockSpec((CONV_K, cw), vec),
                  pl.BlockSpec((1, LANES), vec),
                  pl.BlockSpec((1, LANES), vec),
                  pl.BlockSpec((1, GDN_DIM), vec)],
        out_specs=[pl.BlockSpec((t, LANES), lambda b, h, i, sl: (b * half + i, h)),
                   pl.BlockSpec((t, LANES), lambda b, h, i, sl: (b * half + half - 1 - i, h)),
                   pl.BlockSpec((batch, c, hd), blk)],
        scratch_shapes=[pltpu.VMEM((2, 2 * t, 2 * LANES), BF16),
                        pltpu.VMEM((2, 2 * t, LANES), F32),
                        pltpu.VMEM((2, 2 * t, 2 * DA_V_DIM), F32),
                        pltpu.VMEM((2, 2 * t, t), BF16),
                        pltpu.VMEM((t, t), F32),
                        pltpu.VMEM((2 * LANES, seq), BF16),
                        pltpu.VMEM((batch, c + 8, cw), F32),
                        pltpu.VMEM((batch * GDN_HEADS, GDN_DIM, GDN_DIM), F32)])
    return pl.pallas_call(
        functools.partial(_mixer_kernel, t=t, nq=nq, batch=batch, lambda_init=lambda_init),
        grid_spec=grid_spec,
        out_shape=[jax.ShapeDtypeStruct((batch * seq // 2, width), BF16)] * 2
        + [jax.ShapeDtypeStruct((batch, seq, hd), BF16)],
        compiler_params=_cparams(3),
    )(slopes, q, q, kt, _alibi_key_rows(seq), v, lq1, lk1, lq2, lk2, out_norm_a,
      qkv3, z3, ba3, conv_taps, alog_row, dt_row, out_norm_b)


def _gdn_phases(step, u_ref, z_ref, ba_ref, cw_ref, alog_ref, dt_ref, on_ref, o_ref, stage_ref, state_ref, *, batch):
    c = GDN_CHUNK
    hd = GDN_HEADS * GDN_DIM

    @pl.when(step == 0)
    def _init():
        state_ref[...] = jnp.zeros(state_ref.shape, F32)
        stage_ref[:, 0:8, :] = jnp.zeros((batch, 8, stage_ref.shape[2]), F32)

    yield

    row = lax.broadcasted_iota(I32, (c, c), 0)
    col = lax.broadcasted_iota(I32, (c, c), 1)
    tri = row >= col
    strict = row > col
    tril_ones = jnp.where(tri, 1.0, 0.0).astype(BF16)

    chains = []
    for b in range(batch):
        stage_ref[b, 8:8 + c, :] = u_ref[b]
        u = stage_ref[b]
        y = cw_ref[CONV_K - 1:CONV_K, :] * u
        for back in range(1, CONV_K):
            y = y + cw_ref[CONV_K - 1 - back:CONV_K - back, :] * pltpu.roll(u, back, 0)
        stage_ref[b, 0:8, :] = stage_ref[b, c:c + 8, :]
        y = _silu(y[8:8 + c, :])

        ba = ba_ref[b]
        beta_all = _sigmoid(ba)
        g_all = -jnp.exp(alog_ref[...]) * _softplus(ba + dt_ref[...])
        gcum_all = _mm_exact_lhs(tril_ones, g_all)
        gcum_t = gcum_all.T

        for h in range(GDN_HEADS):
            q = y[:, h * GDN_DIM:(h + 1) * GDN_DIM]
            k = y[:, hd + h * GDN_DIM:hd + (h + 1) * GDN_DIM]
            v = y[:, 2 * hd + h * GDN_DIM:2 * hd + (h + 1) * GDN_DIM]
            q = q * lax.rsqrt(jnp.sum(q * q, axis=-1, keepdims=True) + EPS) * (GDN_DIM ** -0.5)
            k = k * lax.rsqrt(jnp.sum(k * k, axis=-1, keepdims=True) + EPS)
            beta = beta_all[:, h:h + 1]
            gc = gcum_all[:, GDN_HEADS + h:GDN_HEADS + h + 1]
            gr = gcum_t[GDN_HEADS + h:GDN_HEADS + h + 1, :]
            g_last = gc[c - 1:c, :]
            decay = jnp.where(tri, jnp.exp(jnp.where(tri, gc - gr, 0.0)), 0.0)
            e_gc = jnp.exp(gc)
            chains.append(dict(
                b=b, h=h, decay=decay, g_last=g_last,
                p=jnp.where(strict, -(beta * _mm_nt(k, k) * decay), 0.0),
                rhs=jnp.concatenate([v * beta, k * (beta * e_gc)], axis=1),
                qk=_mm_nt(q, k) * decay, q_dec=q * e_gc, k_dec_t=(k * jnp.exp(g_last - gc)).T))
        yield

    def same_block(size):
        return (row // size) == (col // size)

    for ch in chains:
        ch["pk"] = jnp.where(same_block(GDN_BASE), ch["p"], 0.0)
        ch["x"] = ch["pk"]
    for _ in range(int(math.log2(GDN_BASE)) - 1):
        for ch in chains:
            ch["pk"] = _mm(ch["pk"], ch["pk"])
        for ch in chains:
            ch["x"] = ch["x"] + ch["pk"] + _mm(ch["x"], ch["pk"])
        yield
    size = GDN_BASE
    while size < c:
        pair_only = same_block(2 * size) & jnp.logical_not(same_block(size))
        for ch in chains:
            e = jnp.where(pair_only, -ch["p"], 0.0)
            ch["y"] = e + _mm(ch["x"], e)
        for ch in chains:
            ch["x"] = ch["x"] - (ch["y"] + _mm(ch["y"], ch["x"]))
        size *= 2
        yield

    for ch in chains:
        ch["s_prev"] = state_ref[ch["b"] * GDN_HEADS + ch["h"]]
        ch["o_inter"] = _mm(ch["q_dec"], ch["s_prev"])
        ch["sol"] = ch["rhs"] + _mm(ch["x"], ch["rhs"])
    yield
    for ch in chains:
        ch["v_new"] = ch["sol"][:, :GDN_DIM] - _mm(ch["sol"][:, GDN_DIM:], ch["s_prev"])
    yield
    for ch in chains:
        ch["o"] = ch["o_inter"] + _mm(ch["qk"], ch["v_new"])
        ch["s_new"] = ch["s_prev"] * jnp.exp(ch["g_last"]) + _mm(ch["k_dec_t"], ch["v_new"])
    yield
    outs = [[None] * GDN_HEADS for _ in range(batch)]
    for ch in chains:
        zg = z_ref[ch["b"], :, ch["h"] * GDN_DIM:(ch["h"] + 1) * GDN_DIM]
        outs[ch["b"]][ch["h"]] = (_rms(ch["o"], on_ref[...]) * _silu(zg)).astype(o_ref.dtype)
    o_ref[...] = jnp.stack([jnp.concatenate(heads, axis=1) for heads in outs], axis=0)
    state_ref[...] = jnp.stack([ch["s_new"] for ch in chains], axis=0)


def _merge_kernel(x_ref, mod_ref, g1_ref, g2_ref, oa_lo_ref, oa_hi_ref, ob_ref, wgate_ref, wa_ref, wb_ref, wout_ref,
                  wrt_ref, brt_ref, x1_ref, h2_ref, ri_ref, rf_ref, cnt_ref, carry_ref, *, tiles_per_batch):
    @pl.when(pl.program_id(0) == 0)
    def _init():
        carry_ref[...] = jnp.zeros(carry_ref.shape, F32)

    x = x_ref[...]
    d = x.shape[1]
    mod = mod_ref[0]
    shift1, scale1, gate1 = mod[0:1, :], mod[1:2, :], mod[2:3, :]
    shift2, scale2 = mod[3:4, :], mod[4:5, :]
    hb = (_rms(x, g1_ref[...]) * (1.0 + scale1) + shift1).astype(BF16)
    gates = _sigmoid(jnp.dot(hb, wgate_ref[...], preferred_element_type=F32))
    first_half = (pl.program_id(0) % tiles_per_batch) < tiles_per_batch // 2
    oa = jnp.where(first_half, oa_lo_ref[...], oa_hi_ref[...])
    ya = jnp.dot(oa, wa_ref[...], preferred_element_type=F32)
    yb = jnp.dot(ob_ref[...], wb_ref[...], preferred_element_type=F32)
    merged = gates[:, :d] * ya + gates[:, d:] * yb
    x1 = x + gate1 * _mm(merged, wout_ref[...])
    x1_ref[...] = x1
    h2 = _rms(x1, g2_ref[...]) * (1.0 + scale2) + shift2
    h2_ref[...] = _pack_bf16_pairs(h2)
    _route_tile(_mm3(h2, wrt_ref[...]) + brt_ref[...], ri_ref, rf_ref, cnt_ref, carry_ref)


def _merge(x2, mod3, g1, g2, oa_lo, oa_hi, ob, w_gate, w_a, w_b, w_out, w_rt, b_rt, tm, tiles_per_batch):
    n, d = x2.shape
    hp = tiles_per_batch // 2
    const = lambda i: (0, 0)
    row = lambda i: (i, 0)
    lo = lambda i: ((i // tiles_per_batch) * hp + jnp.minimum(i % tiles_per_batch, hp - 1), 0)
    hi = lambda i: ((i // tiles_per_batch) * hp + jnp.maximum(i % tiles_per_batch - hp, 0), 0)
    return pl.pallas_call(
        functools.partial(_merge_kernel, tiles_per_batch=tiles_per_batch),
        grid=(n // tm,),
        in_specs=[pl.BlockSpec((tm, d), row),
                  pl.BlockSpec((1, 6, d), lambda i: (i // tiles_per_batch, 0, 0)),
                  pl.BlockSpec((1, d), const), pl.BlockSpec((1, d), const),
                  pl.BlockSpec((tm, oa_lo.shape[1]), lo), pl.BlockSpec((tm, oa_hi.shape[1]), hi),
                  pl.BlockSpec((tm, ob.shape[1]), row),
                  pl.BlockSpec(w_gate.shape, const), pl.BlockSpec(w_a.shape, const),
                  pl.BlockSpec(w_b.shape, const), pl.BlockSpec(w_out.shape, const),
                  pl.BlockSpec(w_rt.shape, const), pl.BlockSpec((1, LANES), const)],
        out_specs=[pl.BlockSpec((tm, d), row), pl.BlockSpec((tm, d // 2), row), pl.BlockSpec((tm, LANES), row),
                   pl.BlockSpec((tm, LANES), row), pl.BlockSpec((8, LANES), const)],
        out_shape=[jax.ShapeDtypeStruct((n, d), F32), jax.ShapeDtypeStruct((n, d // 2), I32),
                   jax.ShapeDtypeStruct((n, LANES), I32), jax.ShapeDtypeStruct((n, LANES), F32),
                   jax.ShapeDtypeStruct((8, LANES), F32)],
        scratch_shapes=[pltpu.VMEM((8, LANES), F32)],
        compiler_params=_cparams(1),
    )(x2, mod3, g1, g2, oa_lo, oa_hi, ob, w_gate, w_a, w_b, w_out, w_rt, b_rt)


def _route_tile(lg, ri_ref, rf_ref, cnt_ref, carry_ref):
    tm = lg.shape[0]
    lane = lax.broadcasted_iota(I32, lg.shape, 1)
    big = jnp.int32(LANES)
    is_grp = (lane >= N_EXPERTS) & (lane < N_EXPERTS + N_GROUPS)
    gl = jnp.where(is_grp, lg, NEG_INF)
    gmax = jnp.max(gl, axis=-1, keepdims=True)
    g_top = jnp.min(jnp.where(is_grp & (gl == gmax), lane, big), axis=-1, keepdims=True) - N_EXPERTS
    g_top_p = 1.0 / jnp.sum(jnp.where(is_grp, jnp.exp(gl - gmax), 0.0), axis=-1, keepdims=True)
    in_grp = (lane < N_EXPERTS) & ((lane // EXPERTS_PER_GROUP) == g_top)
    el = jnp.where(in_grp, lg, NEG_INF)
    v1 = jnp.max(el, axis=-1, keepdims=True)
    e1 = jnp.min(jnp.where(in_grp & (el == v1), lane, big), axis=-1, keepdims=True)
    rest = in_grp & (lane != e1)
    el2 = jnp.where(rest, lg, NEG_INF)
    v2 = jnp.max(el2, axis=-1, keepdims=True)
    e2 = jnp.min(jnp.where(rest & (el2 == v2), lane, big), axis=-1, keepdims=True)
    ex = jnp.exp(v2 - v1)
    w1 = g_top_p / (1.0 + ex)
    w2 = g_top_p * ex / (1.0 + ex)

    oh1 = lane == e1
    oh2 = lane == e2
    ohs = (jnp.where(oh1, 1.0, 0.0) + jnp.where(oh2, 1.0, 0.0)).astype(BF16)
    r = lax.broadcasted_iota(I32, (tm, tm), 0)
    c = lax.broadcasted_iota(I32, (tm, tm), 1)
    before = jnp.where(r > c, 1.0, 0.0).astype(BF16)
    prior = jnp.dot(before, ohs, preferred_element_type=F32) + carry_ref[0:1, :]
    rank1 = jnp.sum(jnp.where(oh1, prior, 0.0), axis=-1, keepdims=True)
    rank2 = jnp.sum(jnp.where(oh2, prior, 0.0), axis=-1, keepdims=True)
    carry_ref[...] = carry_ref[...] + jnp.sum(ohs.astype(F32), axis=0, keepdims=True)
    cnt_ref[...] = carry_ref[...]

    r1 = rank1.astype(I32)
    r2 = rank2.astype(I32)
    ri = jnp.where(lane == 0, e1, jnp.where(lane == 1, e2, jnp.where(lane == 2, r1, jnp.where(lane == 3, r2, 0))))
    ri_ref[...] = ri
    rf_ref[...] = jnp.where(lane == 0, w1, jnp.where(lane == 1, w2, 0.0))


def _moe_kernel(be_ref, act_ref, f1_ref, f3_ref, f2_ref, xb_ref, w1_ref, w3_ref, w2_ref, o_ref,
                w1b_ref, w3b_ref, w2b_ref):
    i = pl.program_id(0)
    prev = be_ref[jnp.maximum(i - 1, 0)]

    @pl.when((i == 0) | (be_ref[i] != prev))
    def _cast_weights():
        w1b_ref[...] = w1_ref[...].astype(BF16)
        w3b_ref[...] = w3_ref[...].astype(BF16)
        w2b_ref[...] = w2_ref[...].astype(BF16)

    for piece in range(MOE_ROWS // MOE_SUB):
        rows = slice(piece * MOE_SUB, (piece + 1) * MOE_SUB)
        valid = act_ref[i] - piece * MOE_SUB

        @pl.when(valid > 0)
        def _compute(rows=rows, valid=valid):
            xp = xb_ref[rows, :]
            r = lax.broadcasted_iota(I32, xp.shape, 0)
            xb = _unpack_bf16_pairs(jnp.where(r < valid, xp, 0))
            a = jnp.dot(xb, w1b_ref[...], preferred_element_type=F32)
            g = jnp.dot(xb, w3b_ref[...], preferred_element_type=F32)
            hid = (_silu(a) * g).astype(BF16)
            o_ref[rows, :] = _pack_bf16_pairs(jnp.dot(hid, w2b_ref[...], preferred_element_type=F32))

        @pl.when(valid <= 0)
        def _idle(rows=rows):
            o_ref[rows, :] = jnp.zeros((MOE_SUB, o_ref.shape[1]), o_ref.dtype)


def _moe(blk_expert, blk_rows, fetch, xb, w1, w3, w2):
    cap = xb.shape[0]
    d = w1.shape[1]
    nb = cap // MOE_ROWS
    de = w1.shape[2]
    grid_spec = pltpu.PrefetchScalarGridSpec(
        num_scalar_prefetch=5,
        grid=(nb,),
        in_specs=[pl.BlockSpec((MOE_ROWS, d // 2), lambda i, be, act, f1, f3, f2: (i, 0)),
                  pl.BlockSpec((None, d, de), lambda i, be, act, f1, f3, f2: (f1[i], 0, 0)),
                  pl.BlockSpec((None, d, de), lambda i, be, act, f1, f3, f2: (f3[i], 0, 0)),
                  pl.BlockSpec((None, de, d), lambda i, be, act, f1, f3, f2: (f2[i], 0, 0))],
        out_specs=pl.BlockSpec((MOE_ROWS, d // 2), lambda i, be, act, f1, f3, f2: (i, 0)),
        scratch_shapes=[pltpu.VMEM((d, de), BF16), pltpu.VMEM((d, de), BF16), pltpu.VMEM((de, d), BF16)])
    return pl.pallas_call(
        _moe_kernel,
        grid_spec=grid_spec,
        out_shape=jax.ShapeDtypeStruct((cap, d // 2), I32),
        compiler_params=_cparams(1),
    )(blk_expert, blk_rows, *fetch, xb, w1, w3, w2)


def _dest_kernel(ri_ref, pstart_ref, o_ref):
    ri = ri_ref[...]
    lane = lax.broadcasted_iota(I32, ri.shape, 1)
    pstart = pstart_ref[...]
    base1 = jnp.sum(jnp.where(lane == ri[:, 0:1], pstart, 0), axis=-1, keepdims=True)
    base2 = jnp.sum(jnp.where(lane == ri[:, 1:2], pstart, 0), axis=-1, keepdims=True)
    dest = jnp.where(lane == 0, base1 + ri[:, 2:3], jnp.where(lane == 1, base2 + ri[:, 3:4], 0))
    o_ref[...] = dest.astype(F32).T[0:8, :].astype(I32)


def _dest(ri, pstart_row, tm):
    n = ri.shape[0]
    return pl.pallas_call(
        _dest_kernel,
        grid=(n // tm,),
        in_specs=[pl.BlockSpec((tm, LANES), lambda i: (i, 0)), pl.BlockSpec((1, LANES), lambda i: (0, 0))],
        out_specs=pl.BlockSpec((8, tm), lambda i: (0, i)),
        out_shape=jax.ShapeDtypeStruct((8, n), I32),
        compiler_params=_cparams(1),
    )(ri, pstart_row)


def _sc_mesh():
    return plsc.VectorSubcoreMesh(core_axis_name="core", subcore_axis_name="subcore")


def _sc_scatter_rows(x, idx, cap):
    n, d = x.shape
    windows = idx.shape[0]
    tiles = n // SC_WINDOW
    mesh = _sc_mesh()
    workers = mesh.num_cores * mesh.num_subcores
    per_worker = windows // workers

    @functools.partial(pl.kernel, out_type=jax.ShapeDtypeStruct((cap, d), x.dtype), mesh=mesh,
                       scratch_types=[pltpu.VMEM((1, SC_WINDOW), I32), pltpu.VMEM((SC_WINDOW, d), x.dtype)])
    def scatter(x_hbm, i_hbm, o_hbm, idx_v, rows_v):
        wid = lax.axis_index("subcore") * mesh.num_cores + lax.axis_index("core")

        @pl.loop(0, per_worker)
        def _(j):
            w = wid * per_worker + j
            pltpu.sync_copy(i_hbm.at[pl.ds(w, 1)], idx_v)
            pltpu.sync_copy(x_hbm.at[pl.ds(lax.rem(w, tiles) * SC_WINDOW, SC_WINDOW)], rows_v)
            pltpu.sync_copy(rows_v, o_hbm.at[idx_v.at[0]])

    return scatter(x, idx)


def _sc_gather_rows(table, idx):
    d = table.shape[1]
    windows = idx.shape[0]
    mesh = _sc_mesh()
    workers = mesh.num_cores * mesh.num_subcores
    per_worker = windows // workers

    @functools.partial(pl.kernel, out_type=jax.ShapeDtypeStruct((windows * SC_WINDOW, d), table.dtype), mesh=mesh,
                       scratch_types=[pltpu.VMEM((1, SC_WINDOW), I32), pltpu.VMEM((SC_WINDOW, d), table.dtype)])
    def gather(t_hbm, i_hbm, o_hbm, idx_v, rows_v):
        wid = lax.axis_index("subcore") * mesh.num_cores + lax.axis_index("core")

        @pl.loop(0, per_worker)
        def _(j):
            w = wid * per_worker + j
            pltpu.sync_copy(i_hbm.at[pl.ds(w, 1)], idx_v)
            pltpu.sync_copy(t_hbm.at[idx_v.at[0]], rows_v)
            pltpu.sync_copy(rows_v, o_hbm.at[pl.ds(w * SC_WINDOW, SC_WINDOW)])

    return gather(table, idx)


def _combine_kernel(x1_ref, mod_ref, rf_ref, ya_ref, yb_ref, o_ref):
    gate2 = mod_ref[0][5:6, :]
    rf = rf_ref[...]
    ya = _unpack_bf16_pairs(ya_ref[...]).astype(F32)
    yb = _unpack_bf16_pairs(yb_ref[...]).astype(F32)
    y = rf[:, 0:1] * ya + rf[:, 1:2] * yb
    o_ref[...] = x1_ref[...] + gate2 * y


def _combine(x1, mod3, rf, y12, tm, tiles_per_batch):
    n, d = x1.shape
    tiles = n // tm
    row = lambda i: (i, 0)
    return pl.pallas_call(
        _combine_kernel,
        grid=(tiles,),
        in_specs=[pl.BlockSpec((tm, d), row),
                  pl.BlockSpec((1, 6, d), lambda i: (i // tiles_per_batch, 0, 0)),
                  pl.BlockSpec((tm, LANES), row),
                  pl.BlockSpec((tm, d // 2), row), pl.BlockSpec((tm, d // 2), lambda i: (i + tiles, 0))],
        out_specs=pl.BlockSpec((tm, d), row),
        out_shape=jax.ShapeDtypeStruct((n, d), F32),
        compiler_params=_cparams(1),
    )(x1, mod3, rf, y12, y12)


def _pad_lanes(a, offset=0):
    return jnp.pad(a, ((0, 0), (offset, LANES - offset - a.shape[1])))


def _layer(x, c, layer, w_ada, b_ada, norm1_gain, w_in, da_q_norm, da_k_norm, lq1, lk1, lq2, lk2, da_out_norm,
           gdn_conv, gdn_a_log, gdn_dt_bias, gdn_out_norm, w_branch_a, w_branch_b, w_out, norm2_gain,
           w_group, b_group, w_router, b_router, w1, w3, w2):
    batch, seq, d = x.shape
    n = batch * seq
    tm = min(512, seq)
    tiles_per_batch = seq // tm
    t_attn = min(512, seq)
    lambda_init = 0.8 - 0.6 * math.exp(-0.3 * layer)

    mod = _ada(jnp.pad(c, ((0, 8 - batch), (0, 0))), w_ada, b_ada)[:batch]
    mod3 = mod.reshape(batch, 6, d)

    da_w = DA_HEADS * 2 * DA_QK_DIM
    dv_w = DA_HEADS * DA_V_DIM
    gd_w = GDN_HEADS * GDN_DIM
    o0 = 2 * da_w + dv_w
    o1 = o0 + 4 * gd_w
    o2 = o1 + 2 * GDN_HEADS
    w_da = w_in[:, :o0].astype(BF16)
    w_g = w_in[:, o0:o1].astype(BF16)
    w_ba = _pad_lanes(w_in[:, o1:o2]).astype(BF16)
    w_gate = w_in[:, o2:].astype(BF16)
    qn = jnp.tile(da_q_norm, 2 * DA_HEADS).reshape(1, da_w)
    kn = jnp.tile(da_k_norm, 2 * DA_HEADS).reshape(1, da_w)

    x2 = x.reshape(n, d)
    g1 = norm1_gain.reshape(1, d)
    q, kt, v, gqkv, z, ba = _proj(x2, mod3, g1, w_da, w_g, w_ba, qn, kn, tm, tiles_per_batch)

    assert t_attn == tm
    alog_row = _pad_lanes(gdn_a_log.reshape(1, -1), GDN_HEADS)
    dt_row = _pad_lanes(gdn_dt_bias.reshape(1, -1), GDN_HEADS)
    oa_lo, oa_hi, o_b = _mixer(q, kt, v, lq1.reshape(1, -1), lk1.reshape(1, -1), lq2.reshape(1, -1),
                               lk2.reshape(1, -1), da_out_norm.reshape(1, -1),
                               gqkv.reshape(batch, seq, -1), z.reshape(batch, seq, -1), ba.reshape(batch, seq, -1),
                               gdn_conv, alog_row, dt_row, gdn_out_norm.reshape(1, -1),
                               batch, seq, t_attn, lambda_init)
    o_b = o_b.reshape(n, gd_w)

    w_rt = _pad_lanes(jnp.concatenate([w_router, w_group], axis=1))
    b_rt = _pad_lanes(jnp.concatenate([b_router, b_group]).reshape(1, -1))
    x1, h2, ri, rf, cnt = _merge(x2, mod3, g1, norm2_gain.reshape(1, d), oa_lo, oa_hi, o_b, w_gate,
                                 w_branch_a.astype(BF16), w_branch_b.astype(BF16), w_out.astype(BF16),
                                 w_rt, b_rt, tm, tiles_per_batch)

    counts = cnt[0, :N_EXPERTS].astype(I32)
    padded = (counts + MOE_ROWS - 1) // MOE_ROWS * MOE_ROWS
    pend = jnp.cumsum(padded)
    pstart = pend - padded
    cap = 2 * n + N_EXPERTS * MOE_ROWS
    nb = cap // MOE_ROWS
    blk_start = jnp.arange(nb, dtype=I32) * MOE_ROWS
    blk_expert = jnp.minimum(jnp.sum(pend[None, :] <= blk_start[:, None], axis=1), N_EXPERTS - 1).astype(I32)
    blk_rows = jnp.clip((pstart + counts)[blk_expert] - blk_start, 0, MOE_ROWS).astype(I32)
    blk_rows = jnp.where(blk_start < pend[-1], blk_rows, 0)
    first_blk = pstart // MOE_ROWS
    n_blk = padded // MOE_ROWS
    pos = jnp.arange(nb, dtype=I32) - first_blk[blk_expert]
    nxt = blk_expert[jnp.clip(first_blk[blk_expert] + n_blk[blk_expert], 0, nb - 1)]
    fetch = [jnp.where(pos < jnp.minimum(lag, n_blk[blk_expert]), blk_expert, nxt).astype(I32) for lag in (1, 2, 3)]
    dest = _dest(ri, _pad_lanes(pstart.reshape(1, -1)), min(4 * tm, n))
    idx = dest[0:2].reshape(2 * n // SC_WINDOW, SC_WINDOW)

    xb = _sc_scatter_rows(h2, idx, cap)
    yb = _moe(blk_expert, blk_rows, fetch, xb, w1, w3, w2)
    y12 = _sc_gather_rows(yb, idx)
    out = _combine(x1, mod3, rf, y12, tm, tiles_per_batch)
    return out.reshape(batch, seq, d)


def kernel(x, c, w_ada, b_ada, norm1_gain, w_in, da_q_norm, da_k_norm, da_lambda_q1, da_lambda_k1, da_lambda_q2,
           da_lambda_k2, da_out_norm, gdn_conv, gdn_a_log, gdn_dt_bias, gdn_out_norm, w_branch_a, w_branch_b,
           w_out, norm2_gain, w_group, b_group, w_router, b_router, w1, w3, w2):
    for layer in range(w_ada.shape[0]):
        x = _layer(x, c, layer, w_ada[layer], b_ada[layer], norm1_gain[layer], w_in[layer], da_q_norm[layer],
                   da_k_norm[layer], da_lambda_q1[layer], da_lambda_k1[layer], da_lambda_q2[layer],
                   da_lambda_k2[layer], da_out_norm[layer], gdn_conv[layer], gdn_a_log[layer], gdn_dt_bias[layer],
                   gdn_out_norm[layer], w_branch_a[layer], w_branch_b[layer], w_out[layer], norm2_gain[layer],
                   w_group[layer], b_group[layer], w_router[layer], b_router[layer], w1[layer], w3[layer], w2[layer])
    return x
```

```python
import functools
import math

import jax
import jax.numpy as jnp
import numpy as np
from jax import lax
from jax.experimental import pallas as pl
from jax.experimental.pallas import tpu as pltpu
from jax.experimental.pallas import tpu_sc as plsc

F32 = jnp.float32
BF16 = jnp.bfloat16
I32 = jnp.int32

EPS = 1e-6
NEG_INF = -1e30
MASK_CHUNK = 64
LOG2E = 1.4426950408889634
ALIBI_ROWS = 3

DA_HEADS = 4
DA_QK_DIM = 64
DA_V_DIM = 128
GDN_HEADS = 4
GDN_DIM = 128
CONV_K = 4
N_GROUPS = 4
EXPERTS_PER_GROUP = 8
N_EXPERTS = N_GROUPS * EXPERTS_PER_GROUP
D_EXPERT = 512

LANES = 128
VMEM_LIMIT = 56 * 1024 * 1024

MERGE_SPLIT = 2
GDN_CHUNK = 128
GDN_BASE = 16
MOE_ROWS = 512
MOE_SUB = 256
SC_WINDOW = 128


def _cparams(n_axes):
    return pltpu.CompilerParams(dimension_semantics=("arbitrary",) * n_axes,
                                vmem_limit_bytes=VMEM_LIMIT)


def _mm(a, b):
    return jnp.dot(a.astype(BF16), b.astype(BF16), preferred_element_type=F32)


def _mm_nt(a, b):
    return lax.dot_general(a.astype(BF16), b.astype(BF16), (((1,), (1,)), ((), ())),
                           preferred_element_type=F32)


def _split2(a):
    hi = a.astype(BF16)
    lo = (a - hi.astype(F32)).astype(BF16)
    return hi, lo


def _mm3(a, b):
    ah, al = _split2(a)
    bh, bl = _split2(b)
    out = jnp.dot(ah, bh, preferred_element_type=F32)
    out = out + jnp.dot(ah, bl, preferred_element_type=F32)
    out = out + jnp.dot(al, bh, preferred_element_type=F32)
    return out


def _mm_exact_lhs(a_bf16, b):
    b1 = b.astype(BF16)
    r1 = b - b1.astype(F32)
    b2 = r1.astype(BF16)
    b3 = (r1 - b2.astype(F32)).astype(BF16)
    out = jnp.dot(a_bf16, b1, preferred_element_type=F32)
    out = out + jnp.dot(a_bf16, b2, preferred_element_type=F32)
    out = out + jnp.dot(a_bf16, b3, preferred_element_type=F32)
    return out


def _sigmoid(x):
    return 1.0 / (1.0 + jnp.exp(-x))


def _silu(x):
    return x * _sigmoid(x)


def _softplus(x):
    return jnp.maximum(x, 0.0) + jnp.log(1.0 + jnp.exp(-jnp.abs(x)))


def _pack_bf16_pairs(x):
    w = x.shape[1] // 2
    bits = lax.bitcast_convert_type(x.astype(BF16).astype(F32), jnp.uint32)
    lo = lax.shift_right_logical(bits[:, :w], jnp.uint32(16))
    hi = bits[:, w:] & jnp.uint32(0xFFFF0000)
    return lax.bitcast_convert_type(hi | lo, I32)


def _unpack_bf16_pairs(p):
    bits = lax.bitcast_convert_type(p, jnp.uint32)
    lo = lax.bitcast_convert_type(lax.shift_left(bits, jnp.uint32(16)), F32)
    hi = lax.bitcast_convert_type(bits & jnp.uint32(0xFFFF0000), F32)
    return jnp.concatenate([lo, hi], axis=1).astype(BF16)


def _rms(x, gain):
    return x * lax.rsqrt(jnp.mean(x * x, axis=-1, keepdims=True) + EPS) * gain


def _ada_kernel(c_ref, w_ref, b_ref, o_ref):
    sc = _silu(c_ref[...])
    o_ref[...] = _mm3(sc, w_ref[...]) + b_ref[...]


def _ada(c_pad, w_ada, b_ada):
    rows, d = c_pad.shape
    n = w_ada.shape[1]
    tn = d
    return pl.pallas_call(
        _ada_kernel,
        grid=(n // tn,),
        in_specs=[pl.BlockSpec((rows, d), lambda j: (0, 0)),
                  pl.BlockSpec((d, tn), lambda j: (0, j)),
                  pl.BlockSpec((1, tn), lambda j: (0, j))],
        out_specs=pl.BlockSpec((rows, tn), lambda j: (0, j)),
        out_shape=jax.ShapeDtypeStruct((rows, n), F32),
        compiler_params=_cparams(1),
    )(c_pad, w_ada, b_ada.reshape(1, n))


def _group_rms64(x, gain):
    tm, width = x.shape
    lane = lax.broadcasted_iota(I32, (tm, LANES), 1)
    low = lane < DA_QK_DIM
    parts = []
    for j in range(width // LANES):
        blk = x[:, j * LANES:(j + 1) * LANES]
        sq = blk * blk
        s_lo = jnp.sum(jnp.where(low, sq, 0.0), axis=-1, keepdims=True)
        s_hi = jnp.sum(jnp.where(low, 0.0, sq), axis=-1, keepdims=True)
        ms = jnp.where(low, s_lo, s_hi) * (1.0 / DA_QK_DIM)
        parts.append(blk * lax.rsqrt(ms + EPS))
    return jnp.concatenate(parts, axis=-1) * gain


def _proj_kernel(x_ref, mod_ref, g1_ref, wda_ref, wg_ref, wba_ref, qn_ref, kn_ref,
                 q_ref, k_ref, v_ref, gqkv_ref, z_ref, ba_ref):
    x = x_ref[...]
    mod = mod_ref[0]
    shift, scale = mod[0:1, :], mod[1:2, :]
    hb = (_rms(x, g1_ref[...]) * (1.0 + scale) + shift).astype(BF16)
    da_w = DA_HEADS * 2 * DA_QK_DIM
    da = jnp.dot(hb, wda_ref[...], preferred_element_type=F32)
    q = _group_rms64(da[:, :da_w], qn_ref[...]) * (DA_QK_DIM ** -0.5 * LOG2E)
    k = _group_rms64(da[:, da_w:2 * da_w], kn_ref[...])
    q_ref[...] = q.astype(BF16)
    k_ref[...] = k.T.astype(BF16)
    v_ref[...] = da[:, 2 * da_w:].astype(BF16)
    g = jnp.dot(hb, wg_ref[...], preferred_element_type=F32)
    conv_w = 3 * GDN_HEADS * GDN_DIM
    gqkv_ref[...] = g[:, :conv_w]
    z_ref[...] = g[:, conv_w:]
    ba_ref[...] = jnp.dot(hb, wba_ref[...], preferred_element_type=F32)


def _proj(x2, mod3, g1, w_da, w_g, w_ba, qn, kn, tm, tiles_per_batch):
    n, d = x2.shape
    da_w = DA_HEADS * 2 * DA_QK_DIM
    dv_w = DA_HEADS * DA_V_DIM
    conv_w = 3 * GDN_HEADS * GDN_DIM
    z_w = GDN_HEADS * GDN_DIM
    const = lambda i: (0, 0)
    row = lambda i: (i, 0)
    return pl.pallas_call(
        _proj_kernel,
        grid=(n // tm,),
        in_specs=[pl.BlockSpec((tm, d), row),
                  pl.BlockSpec((1, 6, d), lambda i: (i // tiles_per_batch, 0, 0)),
                  pl.BlockSpec((1, d), const),
                  pl.BlockSpec(w_da.shape, const),
                  pl.BlockSpec(w_g.shape, const),
                  pl.BlockSpec(w_ba.shape, const),
                  pl.BlockSpec((1, da_w), const),
                  pl.BlockSpec((1, da_w), const)],
        out_specs=[pl.BlockSpec((tm, da_w), row), pl.BlockSpec((da_w, tm), lambda i: (0, i)),
                   pl.BlockSpec((tm, dv_w), row), pl.BlockSpec((tm, conv_w), row),
                   pl.BlockSpec((tm, z_w), row), pl.BlockSpec((tm, LANES), row)],
        out_shape=[jax.ShapeDtypeStruct((n, da_w), BF16), jax.ShapeDtypeStruct((da_w, n), BF16),
                   jax.ShapeDtypeStruct((n, dv_w), BF16), jax.ShapeDtypeStruct((n, conv_w), F32),
                   jax.ShapeDtypeStruct((n, z_w), F32), jax.ShapeDtypeStruct((n, LANES), F32)],
        compiler_params=_cparams(1),
    )(x2, mod3, g1, w_da, w_g, w_ba, qn, kn)


def _attn_step(h, i, slopes_ref, qa_ref, qb_ref, kt_ref, al_ref, v_ref, lq1_ref, lk1_ref, lq2_ref, lk2_ref, on_ref,
               oa_ref, ob_ref, qs_ref, m_ref, acc_ref, p_ref, ddiag_ref, kta_ref, *, t, nq, lambda_init,
               between_blocks):
    slope2 = slopes_ref[h]
    reps = t // LANES

    @pl.when(i == 0)
    def _tables():
        r = lax.broadcasted_iota(I32, (t, t), 0)
        c = lax.broadcasted_iota(I32, (t, t), 1)
        ahead = jnp.maximum(c - r, 0).astype(F32)
        allowed = (c // MASK_CHUNK) <= (r // MASK_CHUNK)
        ddiag_ref[...] = jnp.where(allowed, (-2.0 * slope2) * ahead, NEG_INF)
        kta_ref[0:LANES, :] = kt_ref[...]
        kta_ref[LANES:2 * LANES, :] = al_ref[0]

    for tile, q_ref in enumerate((qa_ref, qb_ref)):
        q = q_ref[...]
        lane = lax.broadcasted_iota(I32, q.shape, 1)
        zero = jnp.zeros_like(q)
        ones = jnp.where(lane < ALIBI_ROWS, 1.0, 0.0).astype(BF16)
        qs_ref[tile, 0:t, :] = jnp.concatenate([jnp.where(lane < DA_QK_DIM, q, zero), ones], axis=1)
        qs_ref[tile, t:2 * t, :] = jnp.concatenate([jnp.where(lane < DA_QK_DIM, zero, q), ones], axis=1)
    m_ref[...] = jnp.full(m_ref.shape, NEG_INF, F32)
    acc_ref[...] = jnp.zeros(acc_ref.shape, F32)

    def scores(tile, j):
        start = pl.multiple_of(j * t, t)
        return jnp.dot(qs_ref[tile], kta_ref[:, pl.ds(start, t)], preferred_element_type=F32)

    def consume(s, tile, j, slot, diagonal):
        start = pl.multiple_of(j * t, t)
        alphas = []
        for half in range(2):
            rows = slice(half * t, (half + 1) * t)
            sb = s[rows] + ddiag_ref[...] if diagonal else s[rows]
            m_prev = m_ref[tile, rows]
            m_next = jnp.maximum(m_prev, jnp.max(sb, axis=-1, keepdims=True))
            alphas.append(jnp.exp2(m_prev - m_next))
            p_ref[slot, rows] = jnp.exp2(sb - jnp.tile(m_next, (1, reps))).astype(BF16)
            m_ref[tile, rows] = m_next
        v = v_ref[pl.ds(start, t), :]
        vext = jnp.concatenate([v, jnp.ones_like(v)], axis=1)
        pv = jnp.dot(p_ref[slot], vext, preferred_element_type=F32)
        for half in range(2):
            rows = slice(half * t, (half + 1) * t)
            acc_ref[tile, rows] = jnp.tile(alphas[half], (1, 2)) * acc_ref[tile, rows] + pv[rows]

    blocks = []
    for k in range(nq - 1):
        tile = jnp.where(k >= i, 1, 0)
        blocks.append((tile, k - tile * i, False))
    blocks.append((0, i, True))
    blocks.append((1, nq - 1 - i, True))
    s_next = scores(blocks[0][0], blocks[0][1])
    for k, (tile, j, diagonal) in enumerate(blocks):
        s_cur = s_next
        if k + 1 < len(blocks):
            s_next = scores(blocks[k + 1][0], blocks[k + 1][1])
        consume(s_cur, tile, j, k % 2, diagonal)
        between_blocks()

    lam =(jnp.exp(jnp.sum(lq1_ref[...] * lk1_ref[...], axis=-1, keepdims=True))
           - jnp.exp(jnp.sum(lq2_ref[...] * lk2_ref[...], axis=-1, keepdims=True)) + lambda_init)
    for tile, o_ref in enumerate((oa_ref, ob_ref)):
        acc = acc_ref[tile]
        o_all = acc[:, 0:DA_V_DIM] / acc[:, DA_V_DIM:2 * DA_V_DIM]
        o = o_all[0:t] - lam * o_all[t:2 * t]
        o = _rms(o, on_ref[...]) * (1.0 - lambda_init)
        o_ref[...] = o.astype(o_ref.dtype)


def _alibi_slopes_log2():
    return np.asarray([2.0 ** (-8.0 * (hh + 1) / DA_HEADS) * LOG2E for hh in range(DA_HEADS)], np.float32)


def _alibi_key_rows(seq):
    col = _alibi_slopes_log2()[:, None] * np.arange(seq, dtype=np.float32)[None, :]
    b1 = col.astype(BF16)
    r1 = col - b1.astype(np.float32)
    b2 = r1.astype(BF16)
    b3 = (r1 - b2.astype(np.float32)).astype(BF16)
    rows = np.zeros((DA_HEADS, LANES, seq), BF16)
    rows[:, 0], rows[:, 1], rows[:, 2] = b1, b2, b3
    return jnp.asarray(rows)


N_ATTN_IN = 10
N_GDN_IN = 7
N_ATTN_SCRATCH = 6


def _mixer_kernel(slopes_ref, *refs, t, nq, batch, lambda_init):
    attn_in = refs[:N_ATTN_IN]
    gdn_in = refs[N_ATTN_IN:N_ATTN_IN + N_GDN_IN]
    outs = refs[N_ATTN_IN + N_GDN_IN:N_ATTN_IN + N_GDN_IN + 3]
    scratch = refs[N_ATTN_IN + N_GDN_IN + 3:]
    b, h, i = pl.program_id(0), pl.program_id(1), pl.program_id(2)
    chunk = (b * DA_HEADS + h) * (nq // 2) + i
    gdn = _gdn_phases(chunk, *gdn_in, outs[2], *scratch[N_ATTN_SCRATCH:], batch=batch)
    next(gdn)
    _attn_step(h, i, slopes_ref, *attn_in, outs[0], outs[1], *scratch[:N_ATTN_SCRATCH],
               t=t, nq=nq, lambda_init=lambda_init, between_blocks=lambda: next(gdn, None))
    for _ in gdn:
        pass


def _mixer(q, kt, v, lq1, lk1, lq2, lk2, out_norm_a, qkv3, z3, ba3, conv_taps, alog_row, dt_row, out_norm_b,
           batch, seq, t, lambda_init):
    nq = seq // t
    half = nq // 2
    width = DA_HEADS * DA_V_DIM
    c = GDN_CHUNK
    hd = GDN_HEADS * GDN_DIM
    cw = qkv3.shape[2]
    assert seq // c == batch * DA_HEADS * half
    slopes = jnp.asarray(_alibi_slopes_log2())
    vec = lambda b, h, i, sl: (0, 0)
    blk = lambda b, h, i, sl: (0, (b * DA_HEADS + h) * half + i, 0)
    grid_spec = pltpu.PrefetchScalarGridSpec(
        num_scalar_prefetch=1,
        grid=(batch, DA_HEADS, half),
        in_specs=[pl.BlockSpec((t, LANES), lambda b, h, i, sl: (b * nq + i, h)),
                  pl.BlockSpec((t, LANES), lambda b, h, i, sl: (b * nq + nq - 1 - i, h)),
                  pl.BlockSpec((LANES, seq), lambda b, h, i, sl: (h, b)),
                  pl.BlockSpec((1, LANES, seq), lambda b, h, i, sl: (h, 0, 0)),
                  pl.BlockSpec((seq, LANES), lambda b, h, i, sl: (b, h)),
                  pl.BlockSpec((1, DA_QK_DIM), vec), pl.BlockSpec((1, DA_QK_DIM), vec),
                  pl.BlockSpec((1, DA_QK_DIM), vec), pl.BlockSpec((1, DA_QK_DIM), vec),
                  pl.BlockSpec((1, DA_V_DIM), vec),
                  pl.BlockSpec((batch, c, cw), blk),
                  pl.BlockSpec((batch, c, hd), blk),
                  pl.BlockSpec((batch, c, LANES), blk),
                  pl.BlockSpec((CONV_K, cw), vec),
                  pl.BlockSpec((1, LANES), vec),
                  pl.BlockSpec((1, LANES), vec),
                  pl.BlockSpec((1, GDN_DIM), vec)],
        out_specs=[pl.BlockSpec((t, LANES), lambda b, h, i, sl: (b * half + i, h)),
                   pl.BlockSpec((t, LANES), lambda b, h, i, sl: (b * half + half - 1 - i, h)),
                   pl.BlockSpec((batch, c, hd), blk)],
        scratch_shapes=[pltpu.VMEM((2, 2 * t, 2 * LANES), BF16),
                        pltpu.VMEM((2, 2 * t, LANES), F32),
                        pltpu.VMEM((2, 2 * t, 2 * DA_V_DIM), F32),
                        pltpu.VMEM((2, 2 * t, t), BF16),
                        pltpu.VMEM((t, t), F32),
                        pltpu.VMEM((2 * LANES, seq), BF16),
                        pltpu.VMEM((batch, c + 8, cw), F32),
                        pltpu.VMEM((batch * GDN_HEADS, GDN_DIM, GDN_DIM), F32)])
    return pl.pallas_call(
        functools.partial(_mixer_kernel, t=t, nq=nq, batch=batch, lambda_init=lambda_init),
        grid_spec=grid_spec,
        out_shape=[jax.ShapeDtypeStruct((batch * seq // 2, width), BF16)] * 2
        + [jax.ShapeDtypeStruct((batch, seq, hd), BF16)],
        compiler_params=_cparams(3),
    )(slopes, q, q, kt, _alibi_key_rows(seq), v, lq1, lk1, lq2, lk2, out_norm_a,
      qkv3, z3, ba3, conv_taps, alog_row, dt_row, out_norm_b)


def _gdn_phases(step, u_ref, z_ref, ba_ref, cw_ref, alog_ref, dt_ref, on_ref, o_ref, stage_ref, state_ref, *, batch):
    c = GDN_CHUNK
    hd = GDN_HEADS * GDN_DIM

    @pl.when(step == 0)
    def _init():
        state_ref[...] = jnp.zeros(state_ref.shape, F32)
        stage_ref[:, 0:8, :] = jnp.zeros((batch, 8, stage_ref.shape[2]), F32)

    yield

    row = lax.broadcasted_iota(I32, (c, c), 0)
    col = lax.broadcasted_iota(I32, (c, c), 1)
    tri = row >= col
    strict = row > col
    tril_ones = jnp.where(tri, 1.0, 0.0).astype(BF16)

    chains = []
    for b in range(batch):
        stage_ref[b, 8:8 + c, :] = u_ref[b]
        u = stage_ref[b]
        y = cw_ref[CONV_K - 1:CONV_K, :] * u
        for back in range(1, CONV_K):
            y = y + cw_ref[CONV_K - 1 - back:CONV_K - back, :] * pltpu.roll(u, back, 0)
        stage_ref[b, 0:8, :] = stage_ref[b, c:c + 8, :]
        y = _silu(y[8:8 + c, :])

        ba = ba_ref[b]
        beta_all = _sigmoid(ba)
        g_all = -jnp.exp(alog_ref[...]) * _softplus(ba + dt_ref[...])
        gcum_all = _mm_exact_lhs(tril_ones, g_all)
        gcum_t = gcum_all.T

        for h in range(GDN_HEADS):
            q = y[:, h * GDN_DIM:(h + 1) * GDN_DIM]
            k = y[:, hd + h * GDN_DIM:hd + (h + 1) * GDN_DIM]
            v = y[:, 2 * hd + h * GDN_DIM:2 * hd + (h + 1) * GDN_DIM]
            q = q * lax.rsqrt(jnp.sum(q * q, axis=-1, keepdims=True) + EPS) * (GDN_DIM ** -0.5)
            k = k * lax.rsqrt(jnp.sum(k * k, axis=-1, keepdims=True) + EPS)
            beta = beta_all[:, h:h + 1]
            gc = gcum_all[:, GDN_HEADS + h:GDN_HEADS + h + 1]
            gr = gcum_t[GDN_HEADS + h:GDN_HEADS + h + 1, :]
            g_last = gc[c - 1:c, :]
            decay = jnp.where(tri, jnp.exp(jnp.where(tri, gc - gr, 0.0)), 0.0)
            e_gc = jnp.exp(gc)
            chains.append(dict(
                b=b, h=h, decay=decay, g_last=g_last,
                p=jnp.where(strict, -(beta * _mm_nt(k, k) * decay), 0.0),
                rhs=jnp.concatenate([v * beta, k * (beta * e_gc)], axis=1),
                qk=_mm_nt(q, k) * decay, q_dec=q * e_gc, k_dec_t=(k * jnp.exp(g_last - gc)).T))
        yield

    def same_block(size):
        return (row // size) == (col // size)

    for ch in chains:
        ch["pk"] = jnp.where(same_block(GDN_BASE), ch["p"], 0.0)
        ch["x"] = ch["pk"]
    for _ in range(int(math.log2(GDN_BASE)) - 1):
        for ch in chains:
            ch["pk"] = _mm(ch["pk"], ch["pk"])
        for ch in chains:
            ch["x"] = ch["x"] + ch["pk"] + _mm(ch["x"], ch["pk"])
        yield
    size = GDN_BASE
    while size < c:
        pair_only = same_block(2 * size) & jnp.logical_not(same_block(size))
        for ch in chains:
            e = jnp.where(pair_only, -ch["p"], 0.0)
            ch["y"] = e + _mm(ch["x"], e)
        for ch in chains:
            ch["x"] = ch["x"] - (ch["y"] + _mm(ch["y"], ch["x"]))
        size *= 2
        yield

    for ch in chains:
        ch["s_prev"] = state_ref[ch["b"] * GDN_HEADS + ch["h"]]
        ch["o_inter"] = _mm(ch["q_dec"], ch["s_prev"])
        ch["sol"] = ch["rhs"] + _mm(ch["x"], ch["rhs"])
    yield
    for ch in chains:
        ch["v_new"] = ch["sol"][:, :GDN_DIM] - _mm(ch["sol"][:, GDN_DIM:], ch["s_prev"])
    yield
    for ch in chains:
        ch["o"] = ch["o_inter"] + _mm(ch["qk"], ch["v_new"])
        ch["s_new"] = ch["s_prev"] * jnp.exp(ch["g_last"]) + _mm(ch["k_dec_t"], ch["v_new"])
    yield
    outs = [[None] * GDN_HEADS for _ in range(batch)]
    for ch in chains:
        zg = z_ref[ch["b"], :, ch["h"] * GDN_DIM:(ch["h"] + 1) * GDN_DIM]
        outs[ch["b"]][ch["h"]] = (_rms(ch["o"], on_ref[...]) * _silu(zg)).astype(o_ref.dtype)
    o_ref[...] = jnp.stack([jnp.concatenate(heads, axis=1) for heads in outs], axis=0)
    state_ref[...] = jnp.stack([ch["s_new"] for ch in chains], axis=0)


def _merge_kernel(x_ref, mod_ref, g1_ref, g2_ref, oa_lo_ref, oa_hi_ref, ob_ref, wgate_ref, wa_ref, wb_ref, wout_ref,
                  wrt_ref, brt_ref, x1_ref, h2_ref, ri_ref, rf_ref, cnt_ref, carry_ref, *, tiles_per_batch):
    @pl.when(pl.program_id(0) == 0)
    def _init():
        carry_ref[...] = jnp.zeros(carry_ref.shape, F32)

    d = x_ref.shape[1]
    mod = mod_ref[0]
    shift1, scale1, gate1 = mod[0:1, :], mod[1:2, :], mod[2:3, :]
    shift2, scale2 = mod[3:4, :], mod[4:5, :]
    first_half = (pl.program_id(0) % tiles_per_batch) < tiles_per_batch // 2
    tm = x_ref.shape[0] // MERGE_SPLIT
    pieces = [dict(rows=slice(s * tm, (s + 1) * tm)) for s in range(MERGE_SPLIT)]
    for pc in pieces:
        pc["x"] = x_ref[pc["rows"], :]
        hb = (_rms(pc["x"], g1_ref[...]) * (1.0 + scale1) + shift1).astype(BF16)
        pc["gates"] = _sigmoid(jnp.dot(hb, wgate_ref[...], preferred_element_type=F32))
    for pc in pieces:
        oa = jnp.where(first_half, oa_lo_ref[pc["rows"], :], oa_hi_ref[pc["rows"], :])
        pc["ya"] = jnp.dot(oa, wa_ref[...], preferred_element_type=F32)
        pc["yb"] = jnp.dot(ob_ref[pc["rows"], :], wb_ref[...], preferred_element_type=F32)
    for pc in pieces:
        pc["merged"] = pc["gates"][:, :d] * pc["ya"] + pc["gates"][:, d:] * pc["yb"]
    for pc in pieces:
        x1 = pc["x"] + gate1 * _mm(pc["merged"], wout_ref[...])
        x1_ref[pc["rows"], :] = x1
        pc["h2"] = _rms(x1, g2_ref[...]) * (1.0 + scale2) + shift2
        h2_ref[pc["rows"], :] = _pack_bf16_pairs(pc["h2"])
    for pc in pieces:
        pc["logits"] = _mm3(pc["h2"], wrt_ref[...]) + brt_ref[...]
    for pc in pieces:
        _route_tile(pc["logits"], ri_ref.at[pc["rows"], :], rf_ref.at[pc["rows"], :], cnt_ref, carry_ref)


def _merge(x2, mod3, g1, g2, oa_lo, oa_hi, ob, w_gate, w_a, w_b, w_out, w_rt, b_rt, tm, tiles_per_batch):
    n, d = x2.shape
    hp = tiles_per_batch // 2
    const = lambda i: (0, 0)
    row = lambda i: (i, 0)
    lo = lambda i: ((i // tiles_per_batch) * hp + jnp.minimum(i % tiles_per_batch, hp - 1), 0)
    hi = lambda i: ((i // tiles_per_batch) * hp + jnp.maximum(i % tiles_per_batch - hp, 0), 0)
    return pl.pallas_call(
        functools.partial(_merge_kernel, tiles_per_batch=tiles_per_batch),
        grid=(n // tm,),
        in_specs=[pl.BlockSpec((tm, d), row),
                  pl.BlockSpec((1, 6, d), lambda i: (i // tiles_per_batch, 0, 0)),
                  pl.BlockSpec((1, d), const), pl.BlockSpec((1, d), const),
                  pl.BlockSpec((tm, oa_lo.shape[1]), lo), pl.BlockSpec((tm, oa_hi.shape[1]), hi),
                  pl.BlockSpec((tm, ob.shape[1]), row),
                  pl.BlockSpec(w_gate.shape, const), pl.BlockSpec(w_a.shape, const),
                  pl.BlockSpec(w_b.shape, const), pl.BlockSpec(w_out.shape, const),
                  pl.BlockSpec(w_rt.shape, const), pl.BlockSpec((1, LANES), const)],
        out_specs=[pl.BlockSpec((tm, d), row), pl.BlockSpec((tm, d // 2), row), pl.BlockSpec((tm, LANES), row),
                   pl.BlockSpec((tm, LANES), row), pl.BlockSpec((8, LANES), const)],
        out_shape=[jax.ShapeDtypeStruct((n, d), F32), jax.ShapeDtypeStruct((n, d // 2), I32),
                   jax.ShapeDtypeStruct((n, LANES), I32), jax.ShapeDtypeStruct((n, LANES), F32),
                   jax.ShapeDtypeStruct((8, LANES), F32)],
        scratch_shapes=[pltpu.VMEM((8, LANES), F32)],
        compiler_params=_cparams(1),
    )(x2, mod3, g1, g2, oa_lo, oa_hi, ob, w_gate, w_a, w_b, w_out, w_rt, b_rt)


def _route_tile(lg, ri_ref, rf_ref, cnt_ref, carry_ref):
    tm = lg.shape[0]
    lane = lax.broadcasted_iota(I32, lg.shape, 1)
    big = jnp.int32(LANES)
    is_grp = (lane >= N_EXPERTS) & (lane < N_EXPERTS + N_GROUPS)
    gl = jnp.where(is_grp, lg, NEG_INF)
    gmax = jnp.max(gl, axis=-1, keepdims=True)
    g_top = jnp.min(jnp.where(is_grp & (gl == gmax), lane, big), axis=-1, keepdims=True) - N_EXPERTS
    g_top_p = 1.0 / jnp.sum(jnp.where(is_grp, jnp.exp(gl - gmax), 0.0), axis=-1, keepdims=True)
    in_grp = (lane < N_EXPERTS) & ((lane // EXPERTS_PER_GROUP) == g_top)
    el = jnp.where(in_grp, lg, NEG_INF)
    v1 = jnp.max(el, axis=-1, keepdims=True)
    e1 = jnp.min(jnp.where(in_grp & (el == v1), lane, big), axis=-1, keepdims=True)
    rest = in_grp & (lane != e1)
    el2 = jnp.where(rest, lg, NEG_INF)
    v2 = jnp.max(el2, axis=-1, keepdims=True)
    e2 = jnp.min(jnp.where(rest & (el2 == v2), lane, big), axis=-1, keepdims=True)
    ex = jnp.exp(v2 - v1)
    w1 = g_top_p / (1.0 + ex)
    w2 = g_top_p * ex / (1.0 + ex)

    oh1 = lane == e1
    oh2 = lane == e2
    ohs = (jnp.where(oh1, 1.0, 0.0) + jnp.where(oh2, 1.0, 0.0)).astype(BF16)
    r = lax.broadcasted_iota(I32, (tm, tm), 0)
    c = lax.broadcasted_iota(I32, (tm, tm), 1)
    before = jnp.where(r > c, 1.0, 0.0).astype(BF16)
    prior = jnp.dot(before, ohs, preferred_element_type=F32) + carry_ref[0:1, :]
    rank1 = jnp.sum(jnp.where(oh1, prior, 0.0), axis=-1, keepdims=True)
    rank2 = jnp.sum(jnp.where(oh2, prior, 0.0), axis=-1, keepdims=True)
    carry_ref[...] = carry_ref[...] + jnp.sum(ohs.astype(F32), axis=0, keepdims=True)
    cnt_ref[...] = carry_ref[...]

    r1 = rank1.astype(I32)
    r2 = rank2.astype(I32)
    ri = jnp.where(lane == 0, e1, jnp.where(lane == 1, e2, jnp.where(lane == 2, r1, jnp.where(lane == 3, r2, 0))))
    ri_ref[...] = ri
    rf_ref[...] = jnp.where(lane == 0, w1, jnp.where(lane == 1, w2, 0.0))


def _moe_kernel(be_ref, act_ref, f1_ref, f3_ref, f2_ref, xb_ref, w1_ref, w3_ref, w2_ref, o_ref,
                w1b_ref, w3b_ref, w2b_ref):
    i = pl.program_id(0)
    prev = be_ref[jnp.maximum(i - 1, 0)]

    @pl.when((i == 0) | (be_ref[i] != prev))
    def _cast_weights():
        w1b_ref[...] = w1_ref[...].astype(BF16)
        w3b_ref[...] = w3_ref[...].astype(BF16)
        w2b_ref[...] = w2_ref[...].astype(BF16)

    valid = act_ref[i]

    def expert_mlp(n_rows):
        xp = xb_ref[0:n_rows, :]
        r = lax.broadcasted_iota(I32, xp.shape, 0)
        xb = _unpack_bf16_pairs(jnp.where(r < valid, xp, 0))
        a = jnp.dot(xb, w1b_ref[...], preferred_element_type=F32)
        g = jnp.dot(xb, w3b_ref[...], preferred_element_type=F32)
        hid = (_silu(a) * g).astype(BF16)
        o_ref[0:n_rows, :] = _pack_bf16_pairs(jnp.dot(hid, w2b_ref[...], preferred_element_type=F32))

    @pl.when(valid > MOE_SUB)
    def _full():
        expert_mlp(MOE_ROWS)

    @pl.when((valid > 0) & (valid <= MOE_SUB))
    def _partial():
        expert_mlp(MOE_SUB)
        o_ref[MOE_SUB:MOE_ROWS, :] = jnp.zeros((MOE_ROWS - MOE_SUB, o_ref.shape[1]), o_ref.dtype)

    @pl.when(valid == 0)
    def _idle():
        o_ref[...] = jnp.zeros(o_ref.shape, o_ref.dtype)


def _moe(blk_expert, blk_rows, fetch, xb, w1, w3, w2):
    cap = xb.shape[0]
    d = w1.shape[1]
    nb = cap // MOE_ROWS
    de = w1.shape[2]
    grid_spec = pltpu.PrefetchScalarGridSpec(
        num_scalar_prefetch=5,
        grid=(nb,),
        in_specs=[pl.BlockSpec((MOE_ROWS, d // 2), lambda i, be, act, f1, f3, f2: (i, 0)),
                  pl.BlockSpec((None, d, de), lambda i, be, act, f1, f3, f2: (f1[i], 0, 0)),
                  pl.BlockSpec((None, d, de), lambda i, be, act, f1, f3, f2: (f3[i], 0, 0)),
                  pl.BlockSpec((None, de, d), lambda i, be, act, f1, f3, f2: (f2[i], 0, 0))],
        out_specs=pl.BlockSpec((MOE_ROWS, d // 2), lambda i, be, act, f1, f3, f2: (i, 0)),
        scratch_shapes=[pltpu.VMEM((d, de), BF16), pltpu.VMEM((d, de), BF16), pltpu.VMEM((de, d), BF16)])
    return pl.pallas_call(
        _moe_kernel,
        grid_spec=grid_spec,
        out_shape=jax.ShapeDtypeStruct((cap, d // 2), I32),
        compiler_params=_cparams(1),
    )(blk_expert, blk_rows, *fetch, xb, w1, w3, w2)


def _dest_kernel(ri_ref, pstart_ref, o_ref):
    ri = ri_ref[...]
    lane = lax.broadcasted_iota(I32, ri.shape, 1)
    pstart = pstart_ref[...]
    base1 = jnp.sum(jnp.where(lane == ri[:, 0:1], pstart, 0), axis=-1, keepdims=True)
    base2 = jnp.sum(jnp.where(lane == ri[:, 1:2], pstart, 0), axis=-1, keepdims=True)
    dest = jnp.where(lane == 0, base1 + ri[:, 2:3], jnp.where(lane == 1, base2 + ri[:, 3:4], 0))
    o_ref[...] = dest.astype(F32).T[0:8, :].astype(I32)


def _dest(ri, pstart_row, tm):
    n = ri.shape[0]
    return pl.pallas_call(
        _dest_kernel,
        grid=(n // tm,),
        in_specs=[pl.BlockSpec((tm, LANES), lambda i: (i, 0)), pl.BlockSpec((1, LANES), lambda i: (0, 0))],
        out_specs=pl.BlockSpec((8, tm), lambda i: (0, i)),
        out_shape=jax.ShapeDtypeStruct((8, n), I32),
        compiler_params=_cparams(1),
    )(ri, pstart_row)


def _sc_mesh():
    return plsc.VectorSubcoreMesh(core_axis_name="core", subcore_axis_name="subcore")


def _sc_scatter_rows(x, idx, cap):
    n, d = x.shape
    windows = idx.shape[0]
    tiles = n // SC_WINDOW
    mesh = _sc_mesh()
    workers = mesh.num_cores * mesh.num_subcores
    per_worker = windows // workers

    @functools.partial(pl.kernel, out_type=jax.ShapeDtypeStruct((cap, d), x.dtype), mesh=mesh,
                       scratch_types=[pltpu.VMEM((1, SC_WINDOW), I32), pltpu.VMEM((SC_WINDOW, d), x.dtype)])
    def scatter(x_hbm, i_hbm, o_hbm, idx_v, rows_v):
        wid = lax.axis_index("subcore") * mesh.num_cores + lax.axis_index("core")

        @pl.loop(0, per_worker)
        def _(j):
            w = wid * per_worker + j
            pltpu.sync_copy(i_hbm.at[pl.ds(w, 1)], idx_v)
            pltpu.sync_copy(x_hbm.at[pl.ds(lax.rem(w, tiles) * SC_WINDOW, SC_WINDOW)], rows_v)
            pltpu.sync_copy(rows_v, o_hbm.at[idx_v.at[0]])

    return scatter(x, idx)


def _sc_gather_rows(table, idx):
    d = table.shape[1]
    windows = idx.shape[0]
    mesh = _sc_mesh()
    workers = mesh.num_cores * mesh.num_subcores
    per_worker = windows // workers

    @functools.partial(pl.kernel, out_type=jax.ShapeDtypeStruct((windows * SC_WINDOW, d), table.dtype), mesh=mesh,
                       scratch_types=[pltpu.VMEM((1, SC_WINDOW), I32), pltpu.VMEM((SC_WINDOW, d), table.dtype)])
    def gather(t_hbm, i_hbm, o_hbm, idx_v, rows_v):
        wid = lax.axis_index("subcore") * mesh.num_cores + lax.axis_index("core")

        @pl.loop(0, per_worker)
        def _(j):
            w = wid * per_worker + j
            pltpu.sync_copy(i_hbm.at[pl.ds(w, 1)], idx_v)
            pltpu.sync_copy(t_hbm.at[idx_v.at[0]], rows_v)
            pltpu.sync_copy(rows_v, o_hbm.at[pl.ds(w * SC_WINDOW, SC_WINDOW)])

    return gather(table, idx)


def _combine_kernel(x1_ref, mod_ref, rf_ref, ya_ref, yb_ref, o_ref):
    gate2 = mod_ref[0][5:6, :]
    rf = rf_ref[...]
    ya = _unpack_bf16_pairs(ya_ref[...]).astype(F32)
    yb = _unpack_bf16_pairs(yb_ref[...]).astype(F32)
    y = rf[:, 0:1] * ya + rf[:, 1:2] * yb
    o_ref[...] = x1_ref[...] + gate2 * y


def _combine(x1, mod3, rf, y12, tm, tiles_per_batch):
    n, d = x1.shape
    tiles = n // tm
    row = lambda i: (i, 0)
    return pl.pallas_call(
        _combine_kernel,
        grid=(tiles,),
        in_specs=[pl.BlockSpec((tm, d), row),
                  pl.BlockSpec((1, 6, d), lambda i: (i // tiles_per_batch, 0, 0)),
                  pl.BlockSpec((tm, LANES), row),
                  pl.BlockSpec((tm, d // 2), row), pl.BlockSpec((tm, d // 2), lambda i: (i + tiles, 0))],
        out_specs=pl.BlockSpec((tm, d), row),
        out_shape=jax.ShapeDtypeStruct((n, d), F32),
        compiler_params=_cparams(1),
    )(x1, mod3, rf, y12, y12)


def _pad_lanes(a, offset=0):
    return jnp.pad(a, ((0, 0), (offset, LANES - offset - a.shape[1])))


def _layer(x, c, layer, w_ada, b_ada, norm1_gain, w_in, da_q_norm, da_k_norm, lq1, lk1, lq2, lk2, da_out_norm,
           gdn_conv, gdn_a_log, gdn_dt_bias, gdn_out_norm, w_branch_a, w_branch_b, w_out, norm2_gain,
           w_group, b_group, w_router, b_router, w1, w3, w2):
    batch, seq, d = x.shape
    n = batch * seq
    tm = min(512, seq)
    tiles_per_batch = seq // tm
    t_attn = min(512, seq)
    lambda_init = 0.8 - 0.6 * math.exp(-0.3 * layer)

    mod = _ada(jnp.pad(c, ((0, 8 - batch), (0, 0))), w_ada, b_ada)[:batch]
    mod3 = mod.reshape(batch, 6, d)

    da_w = DA_HEADS * 2 * DA_QK_DIM
    dv_w = DA_HEADS * DA_V_DIM
    gd_w = GDN_HEADS * GDN_DIM
    o0 = 2 * da_w + dv_w
    o1 = o0 + 4 * gd_w
    o2 = o1 + 2 * GDN_HEADS
    w_da = w_in[:, :o0].astype(BF16)
    w_g = w_in[:, o0:o1].astype(BF16)
    w_ba = _pad_lanes(w_in[:, o1:o2]).astype(BF16)
    w_gate = w_in[:, o2:].astype(BF16)
    qn = jnp.tile(da_q_norm, 2 * DA_HEADS).reshape(1, da_w)
    kn = jnp.tile(da_k_norm, 2 * DA_HEADS).reshape(1, da_w)

    x2 = x.reshape(n, d)
    g1 = norm1_gain.reshape(1, d)
    q, kt, v, gqkv, z, ba = _proj(x2, mod3, g1, w_da, w_g, w_ba, qn, kn, tm, tiles_per_batch)

    assert t_attn == tm
    alog_row = _pad_lanes(gdn_a_log.reshape(1, -1), GDN_HEADS)
    dt_row = _pad_lanes(gdn_dt_bias.reshape(1, -1), GDN_HEADS)
    oa_lo, oa_hi, o_b = _mixer(q, kt, v, lq1.reshape(1, -1), lk1.reshape(1, -1), lq2.reshape(1, -1),
                               lk2.reshape(1, -1), da_out_norm.reshape(1, -1),
                               gqkv.reshape(batch, seq, -1), z.reshape(batch, seq, -1), ba.reshape(batch, seq, -1),
                               gdn_conv, alog_row, dt_row, gdn_out_norm.reshape(1, -1),
                               batch, seq, t_attn, lambda_init)
    o_b = o_b.reshape(n, gd_w)

    w_rt = _pad_lanes(jnp.concatenate([w_router, w_group], axis=1))
    b_rt = _pad_lanes(jnp.concatenate([b_router, b_group]).reshape(1, -1))
    x1, h2, ri, rf, cnt = _merge(x2, mod3, g1, norm2_gain.reshape(1, d), oa_lo, oa_hi, o_b, w_gate,
                                 w_branch_a.astype(BF16), w_branch_b.astype(BF16), w_out.astype(BF16),
                                 w_rt, b_rt, tm * MERGE_SPLIT, tiles_per_batch // MERGE_SPLIT)

    counts = cnt[0, :N_EXPERTS].astype(I32)
    padded = (counts + MOE_ROWS - 1) // MOE_ROWS * MOE_ROWS
    pend = jnp.cumsum(padded)
    pstart = pend - padded
    cap = 2 * n + N_EXPERTS * MOE_ROWS
    nb = cap // MOE_ROWS
    blk_start = jnp.arange(nb, dtype=I32) * MOE_ROWS
    blk_expert = jnp.minimum(jnp.sum(pend[None, :] <= blk_start[:, None], axis=1), N_EXPERTS - 1).astype(I32)
    blk_rows = jnp.clip((pstart + counts)[blk_expert] - blk_start, 0, MOE_ROWS).astype(I32)
    blk_rows = jnp.where(blk_start < pend[-1], blk_rows, 0)
    first_blk = pstart // MOE_ROWS
    n_blk = padded // MOE_ROWS
    pos = jnp.arange(nb, dtype=I32) - first_blk[blk_expert]
    nxt = blk_expert[jnp.clip(first_blk[blk_expert] + n_blk[blk_expert], 0, nb - 1)]
    fetch = [jnp.where(pos < jnp.minimum(lag, n_blk[blk_expert]), blk_expert, nxt).astype(I32) for lag in (1, 2, 3)]
    dest = _dest(ri, _pad_lanes(pstart.reshape(1, -1)), min(4 * tm, n))
    idx = dest[0:2].reshape(2 * n // SC_WINDOW, SC_WINDOW)

    xb = _sc_scatter_rows(h2, idx, cap)
    yb = _moe(blk_expert, blk_rows, fetch, xb, w1, w3, w2)
    y12 = _sc_gather_rows(yb, idx)
    out = _combine(x1, mod3, rf, y12, tm, tiles_per_batch)
    return out.reshape(batch, seq, d)


def kernel(x, c, w_ada, b_ada, norm1_gain, w_in, da_q_norm, da_k_norm, da_lambda_q1, da_lambda_k1, da_lambda_q2,
           da_lambda_k2, da_out_norm, gdn_conv, gdn_a_log, gdn_dt_bias, gdn_out_norm, w_branch_a, w_branch_b,
           w_out, norm2_gain, w_group, b_group, w_router, b_router, w1, w3, w2):
    for layer in range(w_ada.shape[0]):
        x = _layer(x, c, layer, w_ada[layer], b_ada[layer], norm1_gain[layer], w_in[layer], da_q_norm[layer],
                   da_k_norm[layer], da_lambda_q1[layer], da_lambda_k1[layer], da_lambda_q2[layer],
                   da_lambda_k2[layer], da_out_norm[layer], gdn_conv[layer], gdn_a_log[layer], gdn_dt_bias[layer],
                   gdn_out_norm[layer], w_branch_a[layer], w_branch_b[layer], w_out[layer], norm2_gain[layer],
                   w_group[layer], b_group[layer], w_router[layer], b_router[layer], w1[layer], w3[layer], w2[layer])
    return x
```

```python
import functools
import math

import jax
import jax.numpy as jnp
import numpy as np
from jax import lax
from jax.experimental import pallas as pl
from jax.experimental.pallas import tpu as pltpu
from jax.experimental.pallas import tpu_sc as plsc

F32 = jnp.float32
BF16 = jnp.bfloat16
I32 = jnp.int32

EPS = 1e-6
NEG_INF = -1e30
MASK_CHUNK = 64
LOG2E = 1.4426950408889634
ALIBI_ROWS = 3

DA_HEADS = 4
DA_QK_DIM = 64
DA_V_DIM = 128
GDN_HEADS = 4
GDN_DIM = 128
CONV_K = 4
N_GROUPS = 4
EXPERTS_PER_GROUP = 8
N_EXPERTS = N_GROUPS * EXPERTS_PER_GROUP
D_EXPERT = 512

LANES = 128
VMEM_LIMIT = 56 * 1024 * 1024

MERGE_SPLIT = 2
GDN_CHUNK = 128
GDN_BASE = 16
MOE_ROWS = 512
MOE_SUB = 256
SC_WINDOW = 128


def _cparams(n_axes):
    return pltpu.CompilerParams(dimension_semantics=("arbitrary",) * n_axes,
                                vmem_limit_bytes=VMEM_LIMIT)


def _mm(a, b):
    return jnp.dot(a.astype(BF16), b.astype(BF16), preferred_element_type=F32)


def _mm_nt(a, b):
    return lax.dot_general(a.astype(BF16), b.astype(BF16), (((1,), (1,)), ((), ())),
                           preferred_element_type=F32)


def _split2(a):
    hi = a.astype(BF16)
    lo = (a - hi.astype(F32)).astype(BF16)
    return hi, lo


def _mm3(a, b):
    ah, al = _split2(a)
    bh, bl = _split2(b)
    out = jnp.dot(ah, bh, preferred_element_type=F32)
    out = out + jnp.dot(ah, bl, preferred_element_type=F32)
    out = out + jnp.dot(al, bh, preferred_element_type=F32)
    return out


def _mm3_narrow(a, b):
    ah, al = _split2(a)
    bh, bl = _split2(b)
    n = b.shape[1]
    both = jnp.dot(ah, jnp.concatenate([bh, bl], axis=1), preferred_element_type=F32)
    return both[:, :n] + both[:, n:] + jnp.dot(al, bh, preferred_element_type=F32)


def _mm_exact_lhs(a_bf16, b):
    b1 = b.astype(BF16)
    r1 = b - b1.astype(F32)
    b2 = r1.astype(BF16)
    b3 = (r1 - b2.astype(F32)).astype(BF16)
    out = jnp.dot(a_bf16, b1, preferred_element_type=F32)
    out = out + jnp.dot(a_bf16, b2, preferred_element_type=F32)
    out = out + jnp.dot(a_bf16, b3, preferred_element_type=F32)
    return out


def _sigmoid(x):
    return 1.0 / (1.0 + jnp.exp(-x))


def _silu(x):
    return x * _sigmoid(x)


def _softplus(x):
    return jnp.maximum(x, 0.0) + jnp.log(1.0 + jnp.exp(-jnp.abs(x)))


def _pack_bf16_pairs(x):
    w = x.shape[1] // 2
    bits = lax.bitcast_convert_type(x.astype(BF16).astype(F32), jnp.uint32)
    lo = lax.shift_right_logical(bits[:, :w], jnp.uint32(16))
    hi = bits[:, w:] & jnp.uint32(0xFFFF0000)
    return lax.bitcast_convert_type(hi | lo, I32)


def _unpack_bf16_pairs(p):
    bits = lax.bitcast_convert_type(p, jnp.uint32)
    lo = lax.bitcast_convert_type(lax.shift_left(bits, jnp.uint32(16)), F32)
    hi = lax.bitcast_convert_type(bits & jnp.uint32(0xFFFF0000), F32)
    return jnp.concatenate([lo, hi], axis=1).astype(BF16)


def _rms(x, gain):
    return x * lax.rsqrt(jnp.mean(x * x, axis=-1, keepdims=True) + EPS) * gain


def _ada_kernel(c_ref, w_ref, b_ref, o_ref):
    sc = _silu(c_ref[...])
    o_ref[...] = _mm3(sc, w_ref[...]) + b_ref[...]


def _ada(c_pad, w_ada, b_ada):
    rows, d = c_pad.shape
    n = w_ada.shape[1]
    tn = d
    return pl.pallas_call(
        _ada_kernel,
        grid=(n // tn,),
        in_specs=[pl.BlockSpec((rows, d), lambda j: (0, 0)),
                  pl.BlockSpec((d, tn), lambda j: (0, j)),
                  pl.BlockSpec((1, tn), lambda j: (0, j))],
        out_specs=pl.BlockSpec((rows, tn), lambda j: (0, j)),
        out_shape=jax.ShapeDtypeStruct((rows, n), F32),
        compiler_params=_cparams(1),
    )(c_pad, w_ada, b_ada.reshape(1, n))


def _group_rms64(x, gain):
    tm, width = x.shape
    lane = lax.broadcasted_iota(I32, (tm, LANES), 1)
    low = lane < DA_QK_DIM
    parts = []
    for j in range(width // LANES):
        blk = x[:, j * LANES:(j + 1) * LANES]
        sq = blk * blk
        s_lo = jnp.sum(jnp.where(low, sq, 0.0), axis=-1, keepdims=True)
        s_hi = jnp.sum(jnp.where(low, 0.0, sq), axis=-1, keepdims=True)
        ms = jnp.where(low, s_lo, s_hi) * (1.0 / DA_QK_DIM)
        parts.append(blk * lax.rsqrt(ms + EPS))
    return jnp.concatenate(parts, axis=-1) * gain


def _proj_kernel(x_ref, mod_ref, g1_ref, wda_ref, wg_ref, wba_ref, qn_ref, kn_ref,
                 q_ref, k_ref, v_ref, gqkv_ref, z_ref, ba_ref):
    x = x_ref[...]
    mod = mod_ref[0]
    shift, scale = mod[0:1, :], mod[1:2, :]
    hb = (_rms(x, g1_ref[...]) * (1.0 + scale) + shift).astype(BF16)
    da_w = DA_HEADS * 2 * DA_QK_DIM
    da = jnp.dot(hb, wda_ref[...], preferred_element_type=F32)
    q = _group_rms64(da[:, :da_w], qn_ref[...]) * (DA_QK_DIM ** -0.5 * LOG2E)
    k = _group_rms64(da[:, da_w:2 * da_w], kn_ref[...])
    q_ref[...] = q.astype(BF16)
    k_ref[...] = k.T.astype(BF16)
    v_ref[...] = da[:, 2 * da_w:].astype(BF16)
    g = jnp.dot(hb, wg_ref[...], preferred_element_type=F32)
    conv_w = 3 * GDN_HEADS * GDN_DIM
    gqkv_ref[...] = g[:, :conv_w]
    z_ref[...] = g[:, conv_w:]
    ba_ref[...] = jnp.dot(hb, wba_ref[...], preferred_element_type=F32)


def _proj(x2, mod3, g1, w_da, w_g, w_ba, qn, kn, tm, tiles_per_batch):
    n, d = x2.shape
    da_w = DA_HEADS * 2 * DA_QK_DIM
    dv_w = DA_HEADS * DA_V_DIM
    conv_w = 3 * GDN_HEADS * GDN_DIM
    z_w = GDN_HEADS * GDN_DIM
    const = lambda i: (0, 0)
    row = lambda i: (i, 0)
    return pl.pallas_call(
        _proj_kernel,
        grid=(n // tm,),
        in_specs=[pl.BlockSpec((tm, d), row),
                  pl.BlockSpec((1, 6, d), lambda i: (i // tiles_per_batch, 0, 0)),
                  pl.BlockSpec((1, d), const),
                  pl.BlockSpec(w_da.shape, const),
                  pl.BlockSpec(w_g.shape, const),
                  pl.BlockSpec(w_ba.shape, const),
                  pl.BlockSpec((1, da_w), const),
                  pl.BlockSpec((1, da_w), const)],
        out_specs=[pl.BlockSpec((tm, da_w), row), pl.BlockSpec((da_w, tm), lambda i: (0, i)),
                   pl.BlockSpec((tm, dv_w), row), pl.BlockSpec((tm, conv_w), row),
                   pl.BlockSpec((tm, z_w), row), pl.BlockSpec((tm, LANES), row)],
        out_shape=[jax.ShapeDtypeStruct((n, da_w), BF16), jax.ShapeDtypeStruct((da_w, n), BF16),
                   jax.ShapeDtypeStruct((n, dv_w), BF16), jax.ShapeDtypeStruct((n, conv_w), F32),
                   jax.ShapeDtypeStruct((n, z_w), F32), jax.ShapeDtypeStruct((n, LANES), F32)],
        compiler_params=_cparams(1),
    )(x2, mod3, g1, w_da, w_g, w_ba, qn, kn)


def _attn_step(h, i, slopes_ref, qa_ref, qb_ref, kt_ref, al_ref, v_ref, lq1_ref, lk1_ref, lq2_ref, lk2_ref, on_ref,
               oa_ref, ob_ref, qs_ref, m_ref, acc_ref, p_ref, ddiag_ref, kta_ref, *, t, nq, lambda_init,
               between_blocks):
    slope2 = slopes_ref[h]
    reps = t // LANES

    @pl.when(i == 0)
    def _tables():
        r = lax.broadcasted_iota(I32, (t, t), 0)
        c = lax.broadcasted_iota(I32, (t, t), 1)
        ahead = jnp.maximum(c - r, 0).astype(F32)
        allowed = (c // MASK_CHUNK) <= (r // MASK_CHUNK)
        ddiag_ref[...] = jnp.where(allowed, (-2.0 * slope2) * ahead, NEG_INF)
        kta_ref[0:LANES, :] = kt_ref[...]
        kta_ref[LANES:2 * LANES, :] = al_ref[0]

    for tile, q_ref in enumerate((qa_ref, qb_ref)):
        q = q_ref[...]
        lane = lax.broadcasted_iota(I32, q.shape, 1)
        zero = jnp.zeros_like(q)
        ones = jnp.where(lane < ALIBI_ROWS, 1.0, 0.0).astype(BF16)
        qs_ref[tile, 0:t, :] = jnp.concatenate([jnp.where(lane < DA_QK_DIM, q, zero), ones], axis=1)
        qs_ref[tile, t:2 * t, :] = jnp.concatenate([jnp.where(lane < DA_QK_DIM, zero, q), ones], axis=1)
    m_ref[...] = jnp.full(m_ref.shape, NEG_INF, F32)
    acc_ref[...] = jnp.zeros(acc_ref.shape, F32)

    def scores(tile, j):
        start = pl.multiple_of(j * t, t)
        return jnp.dot(qs_ref[tile], kta_ref[:, pl.ds(start, t)], preferred_element_type=F32)

    def softmax_part(s, tile, slot, diagonal):
        alphas = []
        for half in range(2):
            rows = slice(half * t, (half + 1) * t)
            sb = s[rows] + ddiag_ref[...] if diagonal else s[rows]
            m_prev = m_ref[tile, rows]
            m_next = jnp.maximum(m_prev, jnp.max(sb, axis=-1, keepdims=True))
            alphas.append(jnp.exp2(m_prev - m_next))
            p_ref[slot, rows] = jnp.exp2(sb - jnp.tile(m_next, (1, reps))).astype(BF16)
            m_ref[tile, rows] = m_next
        return alphas

    def value_matmul(j, slot):
        start = pl.multiple_of(j * t, t)
        v = v_ref[pl.ds(start, t), :]
        vext = jnp.concatenate([v, jnp.ones_like(v)], axis=1)
        return jnp.dot(p_ref[slot], vext, preferred_element_type=F32)

    def accumulate(tile, alphas, pv):
        for half in range(2):
            rows = slice(half * t, (half + 1) * t)
            acc_ref[tile, rows] = jnp.tile(alphas[half], (1, 2)) * acc_ref[tile, rows] + pv[rows]

    blocks = []
    for k in range(nq - 1):
        tile = jnp.where(k >= i, 1, 0)
        blocks.append((tile, k - tile * i, False))
    blocks.append((0, i, True))
    blocks.append((1, nq - 1 - i, True))
    s_next = scores(blocks[0][0], blocks[0][1])
    for k, (tile, j, diagonal) in enumerate(blocks):
        s_cur = s_next
        if k + 1 < len(blocks):
            s_next = scores(blocks[k + 1][0], blocks[k + 1][1])
        alphas = softmax_part(s_cur, tile, k % 2, diagonal)
        accumulate(tile, alphas, value_matmul(j, k % 2))
        between_blocks()

    lam =(jnp.exp(jnp.sum(lq1_ref[...] * lk1_ref[...], axis=-1, keepdims=True))
           - jnp.exp(jnp.sum(lq2_ref[...] * lk2_ref[...], axis=-1, keepdims=True)) + lambda_init)
    for tile, o_ref in enumerate((oa_ref, ob_ref)):
        acc = acc_ref[tile]
        o_all = acc[:, 0:DA_V_DIM] / acc[:, DA_V_DIM:2 * DA_V_DIM]
        o = o_all[0:t] - lam * o_all[t:2 * t]
        o = _rms(o, on_ref[...]) * (1.0 - lambda_init)
        o_ref[...] = o.astype(o_ref.dtype)


def _alibi_slopes_log2():
    return np.asarray([2.0 ** (-8.0 * (hh + 1) / DA_HEADS) * LOG2E for hh in range(DA_HEADS)], np.float32)


def _alibi_key_rows(seq):
    col = _alibi_slopes_log2()[:, None] * np.arange(seq, dtype=np.float32)[None, :]
    b1 = col.astype(BF16)
    r1 = col - b1.astype(np.float32)
    b2 = r1.astype(BF16)
    b3 = (r1 - b2.astype(np.float32)).astype(BF16)
    rows = np.zeros((DA_HEADS, LANES, seq), BF16)
    rows[:, 0], rows[:, 1], rows[:, 2] = b1, b2, b3
    return jnp.asarray(rows)


N_ATTN_IN = 10
N_GDN_IN = 7
N_ATTN_SCRATCH = 6


def _mixer_kernel(slopes_ref, *refs, t, nq, batch, lambda_init):
    attn_in = refs[:N_ATTN_IN]
    gdn_in = refs[N_ATTN_IN:N_ATTN_IN + N_GDN_IN]
    outs = refs[N_ATTN_IN + N_GDN_IN:N_ATTN_IN + N_GDN_IN + 3]
    scratch = refs[N_ATTN_IN + N_GDN_IN + 3:]
    b, h, i = pl.program_id(0), pl.program_id(1), pl.program_id(2)
    chunk = (b * DA_HEADS + h) * (nq // 2) + i
    gdn = _gdn_phases(chunk, *gdn_in, outs[2], *scratch[N_ATTN_SCRATCH:], batch=batch)
    next(gdn)
    _attn_step(h, i, slopes_ref, *attn_in, outs[0], outs[1], *scratch[:N_ATTN_SCRATCH],
               t=t, nq=nq, lambda_init=lambda_init, between_blocks=lambda: next(gdn, None))
    for _ in gdn:
        pass


def _mixer(q, kt, v, lq1, lk1, lq2, lk2, out_norm_a, qkv3, z3, ba3, conv_taps, alog_row, dt_row, out_norm_b,
           batch, seq, t, lambda_init):
    nq = seq // t
    half = nq // 2
    width = DA_HEADS * DA_V_DIM
    c = GDN_CHUNK
    hd = GDN_HEADS * GDN_DIM
    cw = qkv3.shape[2]
    assert seq // c == batch * DA_HEADS * half
    slopes = jnp.asarray(_alibi_slopes_log2())
    vec = lambda b, h, i, sl: (0, 0)
    blk = lambda b, h, i, sl: (0, (b * DA_HEADS + h) * half + i, 0)
    grid_spec = pltpu.PrefetchScalarGridSpec(
        num_scalar_prefetch=1,
        grid=(batch, DA_HEADS, half),
        in_specs=[pl.BlockSpec((t, LANES), lambda b, h, i, sl: (b * nq + i, h)),
                  pl.BlockSpec((t, LANES), lambda b, h, i, sl: (b * nq + nq - 1 - i, h)),
                  pl.BlockSpec((LANES, seq), lambda b, h, i, sl: (h, b)),
                  pl.BlockSpec((1, LANES, seq), lambda b, h, i, sl: (h, 0, 0)),
                  pl.BlockSpec((seq, LANES), lambda b, h, i, sl: (b, h)),
                  pl.BlockSpec((1, DA_QK_DIM), vec), pl.BlockSpec((1, DA_QK_DIM), vec),
                  pl.BlockSpec((1, DA_QK_DIM), vec), pl.BlockSpec((1, DA_QK_DIM), vec),
                  pl.BlockSpec((1, DA_V_DIM), vec),
                  pl.BlockSpec((batch, c, cw), blk),
                  pl.BlockSpec((batch, c, hd), blk),
                  pl.BlockSpec((batch, c, LANES), blk),
                  pl.BlockSpec((CONV_K, cw), vec),
                  pl.BlockSpec((1, LANES), vec),
                  pl.BlockSpec((1, LANES), vec),
                  pl.BlockSpec((1, GDN_DIM), vec)],
        out_specs=[pl.BlockSpec((t, LANES), lambda b, h, i, sl: (b * half + i, h)),
                   pl.BlockSpec((t, LANES), lambda b, h, i, sl: (b * half + half - 1 - i, h)),
                   pl.BlockSpec((batch, c, hd), blk)],
        scratch_shapes=[pltpu.VMEM((2, 2 * t, 2 * LANES), BF16),
                        pltpu.VMEM((2, 2 * t, LANES), F32),
                        pltpu.VMEM((2, 2 * t, 2 * DA_V_DIM), F32),
                        pltpu.VMEM((2, 2 * t, t), BF16),
                        pltpu.VMEM((t, t), F32),
                        pltpu.VMEM((2 * LANES, seq), BF16),
                        pltpu.VMEM((batch, c + 8, cw), F32),
                        pltpu.VMEM((batch * GDN_HEADS, GDN_DIM, GDN_DIM), F32)])
    return pl.pallas_call(
        functools.partial(_mixer_kernel, t=t, nq=nq, batch=batch, lambda_init=lambda_init),
        grid_spec=grid_spec,
        out_shape=[jax.ShapeDtypeStruct((batch * seq // 2, width), BF16)] * 2
        + [jax.ShapeDtypeStruct((batch, seq, hd), BF16)],
        compiler_params=_cparams(3),
    )(slopes, q, q, kt, _alibi_key_rows(seq), v, lq1, lk1, lq2, lk2, out_norm_a,
      qkv3, z3, ba3, conv_taps, alog_row, dt_row, out_norm_b)


def _gdn_phases(step, u_ref, z_ref, ba_ref, cw_ref, alog_ref, dt_ref, on_ref, o_ref, stage_ref, state_ref, *, batch):
    c = GDN_CHUNK
    hd = GDN_HEADS * GDN_DIM

    @pl.when(step == 0)
    def _init():
        state_ref[...] = jnp.zeros(state_ref.shape, F32)
        stage_ref[:, 0:8, :] = jnp.zeros((batch, 8, stage_ref.shape[2]), F32)

    yield

    row = lax.broadcasted_iota(I32, (c, c), 0)
    col = lax.broadcasted_iota(I32, (c, c), 1)
    tri = row >= col
    strict = row > col
    tril_ones = jnp.where(tri, 1.0, 0.0).astype(BF16)

    chains = []
    for b in range(batch):
        stage_ref[b, 8:8 + c, :] = u_ref[b]
        u = stage_ref[b]
        y = cw_ref[CONV_K - 1:CONV_K, :] * u
        for back in range(1, CONV_K):
            y = y + cw_ref[CONV_K - 1 - back:CONV_K - back, :] * pltpu.roll(u, back, 0)
        stage_ref[b, 0:8, :] = stage_ref[b, c:c + 8, :]
        y = _silu(y[8:8 + c, :])

        ba = ba_ref[b]
        beta_all = _sigmoid(ba)
        g_all = -jnp.exp(alog_ref[...]) * _softplus(ba + dt_ref[...])
        gcum_all = _mm_exact_lhs(tril_ones, g_all)
        gcum_t = gcum_all.T

        for h in range(GDN_HEADS):
            q = y[:, h * GDN_DIM:(h + 1) * GDN_DIM]
            k = y[:, hd + h * GDN_DIM:hd + (h + 1) * GDN_DIM]
            v = y[:, 2 * hd + h * GDN_DIM:2 * hd + (h + 1) * GDN_DIM]
            q = q * lax.rsqrt(jnp.sum(q * q, axis=-1, keepdims=True) + EPS) * (GDN_DIM ** -0.5)
            k = k * lax.rsqrt(jnp.sum(k * k, axis=-1, keepdims=True) + EPS)
            beta = beta_all[:, h:h + 1]
            gc = gcum_all[:, GDN_HEADS + h:GDN_HEADS + h + 1]
            gr = gcum_t[GDN_HEADS + h:GDN_HEADS + h + 1, :]
            g_last = gc[c - 1:c, :]
            decay = jnp.where(tri, jnp.exp(jnp.where(tri, gc - gr, 0.0)), 0.0)
            e_gc = jnp.exp(gc)
            chains.append(dict(
                b=b, h=h, decay=decay, g_last=g_last,
                p=jnp.where(strict, -(beta * _mm_nt(k, k) * decay), 0.0),
                rhs=jnp.concatenate([v * beta, k * (beta * e_gc)], axis=1),
                qk=_mm_nt(q, k) * decay, q_dec=q * e_gc, k_dec_t=(k * jnp.exp(g_last - gc)).T))
        yield

    def same_block(size):
        return (row // size) == (col // size)

    for ch in chains:
        ch["pk"] = jnp.where(same_block(GDN_BASE), ch["p"], 0.0)
        ch["x"] = ch["pk"]
    for _ in range(int(math.log2(GDN_BASE)) - 1):
        for ch in chains:
            ch["pk"] = _mm(ch["pk"], ch["pk"])
        for ch in chains:
            ch["x"] = ch["x"] + ch["pk"] + _mm(ch["x"], ch["pk"])
        yield
    size = GDN_BASE
    while size < c:
        pair_only = same_block(2 * size) & jnp.logical_not(same_block(size))
        for ch in chains:
            e = jnp.where(pair_only, -ch["p"], 0.0)
            ch["y"] = e + _mm(ch["x"], e)
        for ch in chains:
            ch["x"] = ch["x"] - (ch["y"] + _mm(ch["y"], ch["x"]))
        size *= 2
        yield

    for ch in chains:
        ch["s_prev"] = state_ref[ch["b"] * GDN_HEADS + ch["h"]]
        ch["o_inter"] = _mm(ch["q_dec"], ch["s_prev"])
        ch["sol"] = ch["rhs"] + _mm(ch["x"], ch["rhs"])
    yield
    for ch in chains:
        ch["v_new"] = ch["sol"][:, :GDN_DIM] - _mm(ch["sol"][:, GDN_DIM:], ch["s_prev"])
    yield
    for ch in chains:
        ch["o"] = ch["o_inter"] + _mm(ch["qk"], ch["v_new"])
        ch["s_new"] = ch["s_prev"] * jnp.exp(ch["g_last"]) + _mm(ch["k_dec_t"], ch["v_new"])
    yield
    outs = [[None] * GDN_HEADS for _ in range(batch)]
    for ch in chains:
        zg = z_ref[ch["b"], :, ch["h"] * GDN_DIM:(ch["h"] + 1) * GDN_DIM]
        outs[ch["b"]][ch["h"]] = (_rms(ch["o"], on_ref[...]) * _silu(zg)).astype(o_ref.dtype)
    o_ref[...] = jnp.stack([jnp.concatenate(heads, axis=1) for heads in outs], axis=0)
    state_ref[...] = jnp.stack([ch["s_new"] for ch in chains], axis=0)


def _merge_kernel(x_ref, mod_ref, g1_ref, g2_ref, oa_lo_ref, oa_hi_ref, ob_ref, wgate_ref, wa_ref, wb_ref, wout_ref,
                  wrt_ref, brt_ref, x1_ref, h2_ref, ri_ref, rf_ref, cnt_ref, carry_ref, *, tiles_per_batch):
    @pl.when(pl.program_id(0) == 0)
    def _init():
        carry_ref[...] = jnp.zeros(carry_ref.shape, F32)

    d = x_ref.shape[1]
    mod = mod_ref[0]
    shift1, scale1, gate1 = mod[0:1, :], mod[1:2, :], mod[2:3, :]
    shift2, scale2 = mod[3:4, :], mod[4:5, :]
    first_half = (pl.program_id(0) % tiles_per_batch) < tiles_per_batch // 2
    tm = x_ref.shape[0] // MERGE_SPLIT
    pieces = [dict(rows=slice(s * tm, (s + 1) * tm)) for s in range(MERGE_SPLIT)]
    for pc in pieces:
        pc["x"] = x_ref[pc["rows"], :]
        hb = (_rms(pc["x"], g1_ref[...]) * (1.0 + scale1) + shift1).astype(BF16)
        pc["gates"] = _sigmoid(jnp.dot(hb, wgate_ref[...], preferred_element_type=F32))
    for pc in pieces:
        oa = jnp.where(first_half, oa_lo_ref[pc["rows"], :], oa_hi_ref[pc["rows"], :])
        pc["ya"] = jnp.dot(oa, wa_ref[...], preferred_element_type=F32)
        pc["yb"] = jnp.dot(ob_ref[pc["rows"], :], wb_ref[...], preferred_element_type=F32)
    for pc in pieces:
        pc["merged"] = pc["gates"][:, :d] * pc["ya"] + pc["gates"][:, d:] * pc["yb"]
    for pc in pieces:
        x1 = pc["x"] + gate1 * _mm(pc["merged"], wout_ref[...])
        x1_ref[pc["rows"], :] = x1
        pc["h2"] = _rms(x1, g2_ref[...]) * (1.0 + scale2) + shift2
        h2_ref[pc["rows"], :] = _pack_bf16_pairs(pc["h2"])
    for pc in pieces:
        pc["logits"] = _mm3_narrow(pc["h2"], wrt_ref[...]) + brt_ref[...]
    for pc in pieces:
        _route_tile(pc["logits"], ri_ref.at[pc["rows"], :], rf_ref.at[pc["rows"], :], cnt_ref, carry_ref)


def _merge(x2, mod3, g1, g2, oa_lo, oa_hi, ob, w_gate, w_a, w_b, w_out, w_rt, b_rt, tm, tiles_per_batch):
    n, d = x2.shape
    hp = tiles_per_batch // 2
    const = lambda i: (0, 0)
    row = lambda i: (i, 0)
    lo = lambda i: ((i // tiles_per_batch) * hp + jnp.minimum(i % tiles_per_batch, hp - 1), 0)
    hi = lambda i: ((i // tiles_per_batch) * hp + jnp.maximum(i % tiles_per_batch - hp, 0), 0)
    return pl.pallas_call(
        functools.partial(_merge_kernel, tiles_per_batch=tiles_per_batch),
        grid=(n // tm,),
        in_specs=[pl.BlockSpec((tm, d), row),
                  pl.BlockSpec((1, 6, d), lambda i: (i // tiles_per_batch, 0, 0)),
                  pl.BlockSpec((1, d), const), pl.BlockSpec((1, d), const),
                  pl.BlockSpec((tm, oa_lo.shape[1]), lo), pl.BlockSpec((tm, oa_hi.shape[1]), hi),
                  pl.BlockSpec((tm, ob.shape[1]), row),
                  pl.BlockSpec(w_gate.shape, const), pl.BlockSpec(w_a.shape, const),
                  pl.BlockSpec(w_b.shape, const), pl.BlockSpec(w_out.shape, const),
                  pl.BlockSpec(w_rt.shape, const), pl.BlockSpec((1, LANES), const)],
        out_specs=[pl.BlockSpec((tm, d), row), pl.BlockSpec((tm, d // 2), row), pl.BlockSpec((tm, LANES), row),
                   pl.BlockSpec((tm, LANES), row), pl.BlockSpec((8, LANES), const)],
        out_shape=[jax.ShapeDtypeStruct((n, d), F32), jax.ShapeDtypeStruct((n, d // 2), I32),
                   jax.ShapeDtypeStruct((n, LANES), I32), jax.ShapeDtypeStruct((n, LANES), F32),
                   jax.ShapeDtypeStruct((8, LANES), F32)],
        scratch_shapes=[pltpu.VMEM((8, LANES), F32)],
        compiler_params=_cparams(1),
    )(x2, mod3, g1, g2, oa_lo, oa_hi, ob, w_gate, w_a, w_b, w_out, w_rt, b_rt)


def _route_tile(lg, ri_ref, rf_ref, cnt_ref, carry_ref):
    tm = lg.shape[0]
    lane = lax.broadcasted_iota(I32, lg.shape, 1)
    big = jnp.int32(LANES)
    is_grp = (lane >= N_EXPERTS) & (lane < N_EXPERTS + N_GROUPS)
    gl = jnp.where(is_grp, lg, NEG_INF)
    gmax = jnp.max(gl, axis=-1, keepdims=True)
    g_top = jnp.min(jnp.where(is_grp & (gl == gmax), lane, big), axis=-1, keepdims=True) - N_EXPERTS
    g_top_p = 1.0 / jnp.sum(jnp.where(is_grp, jnp.exp(gl - gmax), 0.0), axis=-1, keepdims=True)
    in_grp = (lane < N_EXPERTS) & ((lane // EXPERTS_PER_GROUP) == g_top)
    el = jnp.where(in_grp, lg, NEG_INF)
    v1 = jnp.max(el, axis=-1, keepdims=True)
    e1 = jnp.min(jnp.where(in_grp & (el == v1), lane, big), axis=-1, keepdims=True)
    rest = in_grp & (lane != e1)
    el2 = jnp.where(rest, lg, NEG_INF)
    v2 = jnp.max(el2, axis=-1, keepdims=True)
    e2 = jnp.min(jnp.where(rest & (el2 == v2), lane, big), axis=-1, keepdims=True)
    ex = jnp.exp(v2 - v1)
    w1 = g_top_p / (1.0 + ex)
    w2 = g_top_p * ex / (1.0 + ex)

    oh1 = lane == e1
    oh2 = lane == e2
    ohs = (jnp.where(oh1, 1.0, 0.0) + jnp.where(oh2, 1.0, 0.0)).astype(BF16)
    r = lax.broadcasted_iota(I32, (tm, tm), 0)
    c = lax.broadcasted_iota(I32, (tm, tm), 1)
    before = jnp.where(r > c, 1.0, 0.0).astype(BF16)
    prior = jnp.dot(before, ohs, preferred_element_type=F32) + carry_ref[0:1, :]
    rank1 = jnp.sum(jnp.where(oh1, prior, 0.0), axis=-1, keepdims=True)
    rank2 = jnp.sum(jnp.where(oh2, prior, 0.0), axis=-1, keepdims=True)
    carry_ref[...] = carry_ref[...] + jnp.sum(ohs.astype(F32), axis=0, keepdims=True)
    cnt_ref[...] = carry_ref[...]

    r1 = rank1.astype(I32)
    r2 = rank2.astype(I32)
    ri = jnp.where(lane == 0, e1, jnp.where(lane == 1, e2, jnp.where(lane == 2, r1, jnp.where(lane == 3, r2, 0))))
    ri_ref[...] = ri
    rf_ref[...] = jnp.where(lane == 0, w1, jnp.where(lane == 1, w2, 0.0))


def _moe_kernel(be_ref, act_ref, f1_ref, f3_ref, f2_ref, xb_ref, w1_ref, w3_ref, w2_ref, o_ref,
                w1b_ref, w3b_ref, w2b_ref):
    i = pl.program_id(0)
    prev = be_ref[jnp.maximum(i - 1, 0)]

    @pl.when((i == 0) | (be_ref[i] != prev))
    def _cast_weights():
        w1b_ref[...] = w1_ref[...].astype(BF16)
        w3b_ref[...] = w3_ref[...].astype(BF16)
        w2b_ref[...] = w2_ref[...].astype(BF16)

    valid = act_ref[i]

    def expert_mlp(n_rows):
        xp = xb_ref[0:n_rows, :]
        r = lax.broadcasted_iota(I32, xp.shape, 0)
        xb = _unpack_bf16_pairs(jnp.where(r < valid, xp, 0))
        a = jnp.dot(xb, w1b_ref[...], preferred_element_type=F32)
        g = jnp.dot(xb, w3b_ref[...], preferred_element_type=F32)
        hid = (_silu(a) * g).astype(BF16)
        o_ref[0:n_rows, :] = _pack_bf16_pairs(jnp.dot(hid, w2b_ref[...], preferred_element_type=F32))

    @pl.when(valid > MOE_SUB)
    def _full():
        expert_mlp(MOE_ROWS)

    @pl.when((valid > 0) & (valid <= MOE_SUB))
    def _partial():
        expert_mlp(MOE_SUB)
        o_ref[MOE_SUB:MOE_ROWS, :] = jnp.zeros((MOE_ROWS - MOE_SUB, o_ref.shape[1]), o_ref.dtype)

    @pl.when(valid == 0)
    def _idle():
        o_ref[...] = jnp.zeros(o_ref.shape, o_ref.dtype)


def _moe(blk_expert, blk_rows, fetch, xb, w1, w3, w2):
    cap = xb.shape[0]
    d = w1.shape[1]
    nb = cap // MOE_ROWS
    de = w1.shape[2]
    grid_spec = pltpu.PrefetchScalarGridSpec(
        num_scalar_prefetch=5,
        grid=(nb,),
        in_specs=[pl.BlockSpec((MOE_ROWS, d // 2), lambda i, be, act, f1, f3, f2: (i, 0)),
                  pl.BlockSpec((None, d, de), lambda i, be, act, f1, f3, f2: (f1[i], 0, 0)),
                  pl.BlockSpec((None, d, de), lambda i, be, act, f1, f3, f2: (f3[i], 0, 0)),
                  pl.BlockSpec((None, de, d), lambda i, be, act, f1, f3, f2: (f2[i], 0, 0))],
        out_specs=pl.BlockSpec((MOE_ROWS, d // 2), lambda i, be, act, f1, f3, f2: (i, 0)),
        scratch_shapes=[pltpu.VMEM((d, de), BF16), pltpu.VMEM((d, de), BF16), pltpu.VMEM((de, d), BF16)])
    return pl.pallas_call(
        _moe_kernel,
        grid_spec=grid_spec,
        out_shape=jax.ShapeDtypeStruct((cap, d // 2), I32),
        compiler_params=_cparams(1),
    )(blk_expert, blk_rows, *fetch, xb, w1, w3, w2)


def _dest_kernel(ri_ref, pstart_ref, o_ref):
    ri = ri_ref[...]
    lane = lax.broadcasted_iota(I32, ri.shape, 1)
    pstart = pstart_ref[...]
    base1 = jnp.sum(jnp.where(lane == ri[:, 0:1], pstart, 0), axis=-1, keepdims=True)
    base2 = jnp.sum(jnp.where(lane == ri[:, 1:2], pstart, 0), axis=-1, keepdims=True)
    dest = jnp.where(lane == 0, base1 + ri[:, 2:3], jnp.where(lane == 1, base2 + ri[:, 3:4], 0))
    o_ref[...] = dest.astype(F32).T[0:8, :].astype(I32)


def _dest(ri, pstart_row, tm):
    n = ri.shape[0]
    return pl.pallas_call(
        _dest_kernel,
        grid=(n // tm,),
        in_specs=[pl.BlockSpec((tm, LANES), lambda i: (i, 0)), pl.BlockSpec((1, LANES), lambda i: (0, 0))],
        out_specs=pl.BlockSpec((8, tm), lambda i: (0, i)),
        out_shape=jax.ShapeDtypeStruct((8, n), I32),
        compiler_params=_cparams(1),
    )(ri, pstart_row)


def _sc_mesh():
    return plsc.VectorSubcoreMesh(core_axis_name="core", subcore_axis_name="subcore")


def _sc_scatter_rows(x, idx, cap):
    n, d = x.shape
    windows = idx.shape[0]
    tiles = n // SC_WINDOW
    mesh = _sc_mesh()
    workers = mesh.num_cores * mesh.num_subcores
    per_worker = windows // workers

    @functools.partial(pl.kernel, out_type=jax.ShapeDtypeStruct((cap, d), x.dtype), mesh=mesh,
                       scratch_types=[pltpu.VMEM((1, SC_WINDOW), I32), pltpu.VMEM((SC_WINDOW, d), x.dtype)])
    def scatter(x_hbm, i_hbm, o_hbm, idx_v, rows_v):
        wid = lax.axis_index("subcore") * mesh.num_cores + lax.axis_index("core")

        @pl.loop(0, per_worker)
        def _(j):
            w = wid * per_worker + j
            pltpu.sync_copy(i_hbm.at[pl.ds(w, 1)], idx_v)
            pltpu.sync_copy(x_hbm.at[pl.ds(lax.rem(w, tiles) * SC_WINDOW, SC_WINDOW)], rows_v)
            pltpu.sync_copy(rows_v, o_hbm.at[idx_v.at[0]])

    return scatter(x, idx)


def _sc_gather_rows(table, idx):
    d = table.shape[1]
    windows = idx.shape[0]
    mesh = _sc_mesh()
    workers = mesh.num_cores * mesh.num_subcores
    per_worker = windows // workers

    @functools.partial(pl.kernel, out_type=jax.ShapeDtypeStruct((windows * SC_WINDOW, d), table.dtype), mesh=mesh,
                       scratch_types=[pltpu.VMEM((1, SC_WINDOW), I32), pltpu.VMEM((SC_WINDOW, d), table.dtype)])
    def gather(t_hbm, i_hbm, o_hbm, idx_v, rows_v):
        wid = lax.axis_index("subcore") * mesh.num_cores + lax.axis_index("core")

        @pl.loop(0, per_worker)
        def _(j):
            w = wid * per_worker + j
            pltpu.sync_copy(i_hbm.at[pl.ds(w, 1)], idx_v)
            pltpu.sync_copy(t_hbm.at[idx_v.at[0]], rows_v)
            pltpu.sync_copy(rows_v, o_hbm.at[pl.ds(w * SC_WINDOW, SC_WINDOW)])

    return gather(table, idx)


def _combine_kernel(x1_ref, mod_ref, rf_ref, ya_ref, yb_ref, o_ref):
    gate2 = mod_ref[0][5:6, :]
    rf = rf_ref[...]
    ya = _unpack_bf16_pairs(ya_ref[...]).astype(F32)
    yb = _unpack_bf16_pairs(yb_ref[...]).astype(F32)
    y = rf[:, 0:1] * ya + rf[:, 1:2] * yb
    o_ref[...] = x1_ref[...] + gate2 * y


def _combine(x1, mod3, rf, y12, tm, tiles_per_batch):
    n, d = x1.shape
    tiles = n // tm
    row = lambda i: (i, 0)
    return pl.pallas_call(
        _combine_kernel,
        grid=(tiles,),
        in_specs=[pl.BlockSpec((tm, d), row),
                  pl.BlockSpec((1, 6, d), lambda i: (i // tiles_per_batch, 0, 0)),
                  pl.BlockSpec((tm, LANES), row),
                  pl.BlockSpec((tm, d // 2), row), pl.BlockSpec((tm, d // 2), lambda i: (i + tiles, 0))],
        out_specs=pl.BlockSpec((tm, d), row),
        out_shape=jax.ShapeDtypeStruct((n, d), F32),
        compiler_params=_cparams(1),
    )(x1, mod3, rf, y12, y12)


def _pad_lanes(a, offset=0):
    return jnp.pad(a, ((0, 0), (offset, LANES - offset - a.shape[1])))


def _layer(x, c, layer, w_ada, b_ada, norm1_gain, w_in, da_q_norm, da_k_norm, lq1, lk1, lq2, lk2, da_out_norm,
           gdn_conv, gdn_a_log, gdn_dt_bias, gdn_out_norm, w_branch_a, w_branch_b, w_out, norm2_gain,
           w_group, b_group, w_router, b_router, w1, w3, w2):
    batch, seq, d = x.shape
    n = batch * seq
    tm = min(512, seq)
    tiles_per_batch = seq // tm
    t_attn = min(512, seq)
    lambda_init = 0.8 - 0.6 * math.exp(-0.3 * layer)

    mod = _ada(jnp.pad(c, ((0, 8 - batch), (0, 0))), w_ada, b_ada)[:batch]
    mod3 = mod.reshape(batch, 6, d)

    da_w = DA_HEADS * 2 * DA_QK_DIM
    dv_w = DA_HEADS * DA_V_DIM
    gd_w = GDN_HEADS * GDN_DIM
    o0 = 2 * da_w + dv_w
    o1 = o0 + 4 * gd_w
    o2 = o1 + 2 * GDN_HEADS
    w_da = w_in[:, :o0].astype(BF16)
    w_g = w_in[:, o0:o1].astype(BF16)
    w_ba = _pad_lanes(w_in[:, o1:o2]).astype(BF16)
    w_gate = w_in[:, o2:].astype(BF16)
    qn = jnp.tile(da_q_norm, 2 * DA_HEADS).reshape(1, da_w)
    kn = jnp.tile(da_k_norm, 2 * DA_HEADS).reshape(1, da_w)

    x2 = x.reshape(n, d)
    g1 = norm1_gain.reshape(1, d)
    q, kt, v, gqkv, z, ba = _proj(x2, mod3, g1, w_da, w_g, w_ba, qn, kn, tm, tiles_per_batch)

    assert t_attn == tm
    alog_row = _pad_lanes(gdn_a_log.reshape(1, -1), GDN_HEADS)
    dt_row = _pad_lanes(gdn_dt_bias.reshape(1, -1), GDN_HEADS)
    oa_lo, oa_hi, o_b = _mixer(q, kt, v, lq1.reshape(1, -1), lk1.reshape(1, -1), lq2.reshape(1, -1),
                               lk2.reshape(1, -1), da_out_norm.reshape(1, -1),
                               gqkv.reshape(batch, seq, -1), z.reshape(batch, seq, -1), ba.reshape(batch, seq, -1),
                               gdn_conv, alog_row, dt_row, gdn_out_norm.reshape(1, -1),
                               batch, seq, t_attn, lambda_init)
    o_b = o_b.reshape(n, gd_w)

    w_rt = _pad_lanes(jnp.concatenate([w_router, w_group], axis=1))
    b_rt = _pad_lanes(jnp.concatenate([b_router, b_group]).reshape(1, -1))
    x1, h2, ri, rf, cnt = _merge(x2, mod3, g1, norm2_gain.reshape(1, d), oa_lo, oa_hi, o_b, w_gate,
                                 w_branch_a.astype(BF16), w_branch_b.astype(BF16), w_out.astype(BF16),
                                 w_rt, b_rt, tm * MERGE_SPLIT, tiles_per_batch // MERGE_SPLIT)

    counts = cnt[0, :N_EXPERTS].astype(I32)
    padded = (counts + MOE_ROWS - 1) // MOE_ROWS * MOE_ROWS
    pend = jnp.cumsum(padded)
    pstart = pend - padded
    cap = 2 * n + N_EXPERTS * MOE_ROWS
    nb = cap // MOE_ROWS
    blk_start = jnp.arange(nb, dtype=I32) * MOE_ROWS
    blk_expert = jnp.minimum(jnp.sum(pend[None, :] <= blk_start[:, None], axis=1), N_EXPERTS - 1).astype(I32)
    blk_rows = jnp.clip((pstart + counts)[blk_expert] - blk_start, 0, MOE_ROWS).astype(I32)
    blk_rows = jnp.where(blk_start < pend[-1], blk_rows, 0)
    first_blk = pstart // MOE_ROWS
    n_blk = padded // MOE_ROWS
    pos = jnp.arange(nb, dtype=I32) - first_blk[blk_expert]
    nxt = blk_expert[jnp.clip(first_blk[blk_expert] + n_blk[blk_expert], 0, nb - 1)]
    fetch = [jnp.where(pos < jnp.minimum(lag, n_blk[blk_expert]), blk_expert, nxt).astype(I32) for lag in (1, 2, 3)]
    dest = _dest(ri, _pad_lanes(pstart.reshape(1, -1)), min(4 * tm, n))
    idx = dest[0:2].reshape(2 * n // SC_WINDOW, SC_WINDOW)

    xb = _sc_scatter_rows(h2, idx, cap)
    yb = _moe(blk_expert, blk_rows, fetch, xb, w1, w3, w2)
    y12 = _sc_gather_rows(yb, idx)
    out = _combine(x1, mod3, rf, y12, 2 * tm, tiles_per_batch // 2)
    return out.reshape(batch, seq, d)


def kernel(x, c, w_ada, b_ada, norm1_gain, w_in, da_q_norm, da_k_norm, da_lambda_q1, da_lambda_k1, da_lambda_q2,
           da_lambda_k2, da_out_norm, gdn_conv, gdn_a_log, gdn_dt_bias, gdn_out_norm, w_branch_a, w_branch_b,
           w_out, norm2_gain, w_group, b_group, w_router, b_router, w1, w3, w2):
    for layer in range(w_ada.shape[0]):
        x = _layer(x, c, layer, w_ada[layer], b_ada[layer], norm1_gain[layer], w_in[layer], da_q_norm[layer],
                   da_k_norm[layer], da_lambda_q1[layer], da_lambda_k1[layer], da_lambda_q2[layer],
                   da_lambda_k2[layer], da_out_norm[layer], gdn_conv[layer], gdn_a_log[layer], gdn_dt_bias[layer],
                   gdn_out_norm[layer], w_branch_a[layer], w_branch_b[layer], w_out[layer], norm2_gain[layer],
                   w_group[layer], b_group[layer], w_router[layer], b_router[layer], w1[layer], w3[layer], w2[layer])
    return x
```

```python
import functools
import math

import jax
import jax.numpy as jnp
import numpy as np
from jax import lax
from jax.experimental import pallas as pl
from jax.experimental.pallas import tpu as pltpu
from jax.experimental.pallas import tpu_sc as plsc

F32 = jnp.float32
BF16 = jnp.bfloat16
I32 = jnp.int32

EPS = 1e-6
NEG_INF = -1e30
MASK_CHUNK = 64
LOG2E = 1.4426950408889634
ALIBI_ROWS = 3

DA_HEADS = 4
DA_QK_DIM = 64
DA_V_DIM = 128
GDN_HEADS = 4
GDN_DIM = 128
CONV_K = 4
N_GROUPS = 4
EXPERTS_PER_GROUP = 8
N_EXPERTS = N_GROUPS * EXPERTS_PER_GROUP
D_EXPERT = 512

LANES = 128
VMEM_LIMIT = 56 * 1024 * 1024

MERGE_SPLIT = 2
GDN_CHUNK = 128
GDN_BASE = 16
MOE_ROWS = 512
MOE_SUB = 256
SC_WINDOW = 128


def _cparams(n_axes):
    return pltpu.CompilerParams(dimension_semantics=("arbitrary",) * n_axes,
                                vmem_limit_bytes=VMEM_LIMIT)


def _mm(a, b):
    return jnp.dot(a.astype(BF16), b.astype(BF16), preferred_element_type=F32)


def _mm_nt(a, b):
    return lax.dot_general(a.astype(BF16), b.astype(BF16), (((1,), (1,)), ((), ())),
                           preferred_element_type=F32)


def _split2(a):
    hi = a.astype(BF16)
    lo = (a - hi.astype(F32)).astype(BF16)
    return hi, lo


def _mm3(a, b):
    ah, al = _split2(a)
    bh, bl = _split2(b)
    out = jnp.dot(ah, bh, preferred_element_type=F32)
    out = out + jnp.dot(ah, bl, preferred_element_type=F32)
    out = out + jnp.dot(al, bh, preferred_element_type=F32)
    return out


def _mm3_narrow(a, b):
    ah, al = _split2(a)
    bh, bl = _split2(b)
    n = b.shape[1]
    both = jnp.dot(ah, jnp.concatenate([bh, bl], axis=1), preferred_element_type=F32)
    return both[:, :n] + both[:, n:] + jnp.dot(al, bh, preferred_element_type=F32)


def _mm_exact_lhs(a_bf16, b):
    b1 = b.astype(BF16)
    r1 = b - b1.astype(F32)
    b2 = r1.astype(BF16)
    b3 = (r1 - b2.astype(F32)).astype(BF16)
    out = jnp.dot(a_bf16, b1, preferred_element_type=F32)
    out = out + jnp.dot(a_bf16, b2, preferred_element_type=F32)
    out = out + jnp.dot(a_bf16, b3, preferred_element_type=F32)
    return out


def _sigmoid(x):
    return 1.0 / (1.0 + jnp.exp(-x))


def _silu(x):
    return x * _sigmoid(x)


def _softplus(x):
    return jnp.maximum(x, 0.0) + jnp.log(1.0 + jnp.exp(-jnp.abs(x)))


def _pack_bf16_pairs(x):
    w = x.shape[1] // 2
    bits = lax.bitcast_convert_type(x.astype(BF16).astype(F32), jnp.uint32)
    lo = lax.shift_right_logical(bits[:, :w], jnp.uint32(16))
    hi = bits[:, w:] & jnp.uint32(0xFFFF0000)
    return lax.bitcast_convert_type(hi | lo, I32)


def _unpack_bf16_pairs(p):
    bits = lax.bitcast_convert_type(p, jnp.uint32)
    lo = lax.bitcast_convert_type(lax.shift_left(bits, jnp.uint32(16)), F32)
    hi = lax.bitcast_convert_type(bits & jnp.uint32(0xFFFF0000), F32)
    return jnp.concatenate([lo, hi], axis=1).astype(BF16)


def _rms(x, gain):
    return x * lax.rsqrt(jnp.mean(x * x, axis=-1, keepdims=True) + EPS) * gain


def _ada_kernel(c_ref, w_ref, b_ref, o_ref):
    sc = _silu(c_ref[...])
    o_ref[...] = _mm3(sc, w_ref[...]) + b_ref[...]


def _ada(c_pad, w_ada, b_ada):
    rows, d = c_pad.shape
    n = w_ada.shape[1]
    tn = d
    return pl.pallas_call(
        _ada_kernel,
        grid=(n // tn,),
        in_specs=[pl.BlockSpec((rows, d), lambda j: (0, 0)),
                  pl.BlockSpec((d, tn), lambda j: (0, j)),
                  pl.BlockSpec((1, tn), lambda j: (0, j))],
        out_specs=pl.BlockSpec((rows, tn), lambda j: (0, j)),
        out_shape=jax.ShapeDtypeStruct((rows, n), F32),
        compiler_params=_cparams(1),
    )(c_pad, w_ada, b_ada.reshape(1, n))


def _group_rms64(x, gain):
    tm, width = x.shape
    lane = lax.broadcasted_iota(I32, (tm, LANES), 1)
    low = lane < DA_QK_DIM
    parts = []
    for j in range(width // LANES):
        blk = x[:, j * LANES:(j + 1) * LANES]
        sq = blk * blk
        s_lo = jnp.sum(jnp.where(low, sq, 0.0), axis=-1, keepdims=True)
        s_hi = jnp.sum(jnp.where(low, 0.0, sq), axis=-1, keepdims=True)
        ms = jnp.where(low, s_lo, s_hi) * (1.0 / DA_QK_DIM)
        parts.append(blk * lax.rsqrt(ms + EPS))
    return jnp.concatenate(parts, axis=-1) * gain


def _proj_kernel(x_ref, mod_ref, g1_ref, win_ref, qn_ref, kn_ref,
                 q_ref, k_ref, v_ref, gqkv_ref, z_ref, ba_ref, wb_ref):
    da_w = DA_HEADS * 2 * DA_QK_DIM
    o_g = 2 * da_w + DA_HEADS * DA_V_DIM
    o_ba = o_g + 4 * GDN_HEADS * GDN_DIM

    @pl.when(pl.program_id(0) == 0)
    def _cast_weights():
        wb_ref[:, 0:o_ba] = win_ref[:, 0:o_ba].astype(BF16)
        tail = win_ref[:, o_ba:o_ba + LANES]
        lane = lax.broadcasted_iota(I32, tail.shape, 1)
        wb_ref[:, o_ba:o_ba + LANES] = jnp.where(lane < 2 * GDN_HEADS, tail, 0.0).astype(BF16)

    wda_ref = wb_ref.at[:, 0:o_g]
    wg_ref = wb_ref.at[:, o_g:o_ba]
    wba_ref = wb_ref.at[:, o_ba:o_ba + LANES]
    x = x_ref[...]
    mod = mod_ref[0]
    shift, scale = mod[0:1, :], mod[1:2, :]
    hb = (_rms(x, g1_ref[...]) * (1.0 + scale) + shift).astype(BF16)
    da = jnp.dot(hb, wda_ref[...], preferred_element_type=F32)
    q = _group_rms64(da[:, :da_w], qn_ref[...]) * (DA_QK_DIM ** -0.5 * LOG2E)
    k = _group_rms64(da[:, da_w:2 * da_w], kn_ref[...])
    q_ref[...] = q.astype(BF16)
    k_ref[...] = k.T.astype(BF16)
    v_ref[...] = da[:, 2 * da_w:].astype(BF16)
    g = jnp.dot(hb, wg_ref[...], preferred_element_type=F32)
    conv_w = 3 * GDN_HEADS * GDN_DIM
    gqkv_ref[...] = g[:, :conv_w]
    z_ref[...] = g[:, conv_w:]
    ba_ref[...] = jnp.dot(hb, wba_ref[...], preferred_element_type=F32)


def _proj(x2, mod3, g1, w_in, qn, kn, tm, tiles_per_batch):
    n, d = x2.shape
    da_w = DA_HEADS * 2 * DA_QK_DIM
    dv_w = DA_HEADS * DA_V_DIM
    conv_w = 3 * GDN_HEADS * GDN_DIM
    z_w = GDN_HEADS * GDN_DIM
    w_cols = 2 * da_w + dv_w + conv_w + z_w + LANES
    const = lambda i: (0, 0)
    row = lambda i: (i, 0)
    return pl.pallas_call(
        _proj_kernel,
        grid=(n // tm,),
        in_specs=[pl.BlockSpec((tm, d), row),
                  pl.BlockSpec((1, 6, d), lambda i: (i // tiles_per_batch, 0, 0)),
                  pl.BlockSpec((1, d), const),
                  pl.BlockSpec((d, w_cols), const, pipeline_mode=pl.Buffered(1)),
                  pl.BlockSpec((1, da_w), const),
                  pl.BlockSpec((1, da_w), const)],
        out_specs=[pl.BlockSpec((tm, da_w), row), pl.BlockSpec((da_w, tm), lambda i: (0, i)),
                   pl.BlockSpec((tm, dv_w), row), pl.BlockSpec((tm, conv_w), row),
                   pl.BlockSpec((tm, z_w), row), pl.BlockSpec((tm, LANES), row)],
        out_shape=[jax.ShapeDtypeStruct((n, da_w), BF16), jax.ShapeDtypeStruct((da_w, n), BF16),
                   jax.ShapeDtypeStruct((n, dv_w), BF16), jax.ShapeDtypeStruct((n, conv_w), F32),
                   jax.ShapeDtypeStruct((n, z_w), F32), jax.ShapeDtypeStruct((n, LANES), F32)],
        scratch_shapes=[pltpu.VMEM((d, w_cols), BF16)],
        compiler_params=_cparams(1),
    )(x2, mod3, g1, w_in, qn, kn)


def _attn_step(h, i, slopes_ref, qa_ref, qb_ref, kt_ref, al_ref, v_ref, lq1_ref, lk1_ref, lq2_ref, lk2_ref, on_ref,
               oa_ref, ob_ref, qs_ref, m_ref, acc_ref, p_ref, ddiag_ref, kta_ref, *, t, nq, lambda_init,
               between_blocks):
    slope2 = slopes_ref[h]
    reps = t // LANES

    @pl.when(i == 0)
    def _tables():
        r = lax.broadcasted_iota(I32, (t, t), 0)
        c = lax.broadcasted_iota(I32, (t, t), 1)
        ahead = jnp.maximum(c - r, 0).astype(F32)
        allowed = (c // MASK_CHUNK) <= (r // MASK_CHUNK)
        ddiag_ref[...] = jnp.where(allowed, (-2.0 * slope2) * ahead, NEG_INF)
        kta_ref[0:LANES, :] = kt_ref[...]
        kta_ref[LANES:2 * LANES, :] = al_ref[0]

    for tile, q_ref in enumerate((qa_ref, qb_ref)):
        q = q_ref[...]
        lane = lax.broadcasted_iota(I32, q.shape, 1)
        zero = jnp.zeros_like(q)
        ones = jnp.where(lane < ALIBI_ROWS, 1.0, 0.0).astype(BF16)
        qs_ref[tile, 0:t, :] = jnp.concatenate([jnp.where(lane < DA_QK_DIM, q, zero), ones], axis=1)
        qs_ref[tile, t:2 * t, :] = jnp.concatenate([jnp.where(lane < DA_QK_DIM, zero, q), ones], axis=1)
    m_ref[...] = jnp.full(m_ref.shape, NEG_INF, F32)
    acc_ref[...] = jnp.zeros(acc_ref.shape, F32)

    def scores(tile, j):
        start = pl.multiple_of(j * t, t)
        return jnp.dot(qs_ref[tile], kta_ref[:, pl.ds(start, t)], preferred_element_type=F32)

    def softmax_part(s, tile, slot, diagonal):
        alphas = []
        for half in range(2):
            rows = slice(half * t, (half + 1) * t)
            sb = s[rows] + ddiag_ref[...] if diagonal else s[rows]
            m_prev = m_ref[tile, rows]
            m_next = jnp.maximum(m_prev, jnp.max(sb, axis=-1, keepdims=True))
            alphas.append(jnp.exp2(m_prev - m_next))
            p_ref[slot, rows] = jnp.exp2(sb - jnp.tile(m_next, (1, reps))).astype(BF16)
            m_ref[tile, rows] = m_next
        return alphas

    def value_matmul(j, slot):
        start = pl.multiple_of(j * t, t)
        v = v_ref[pl.ds(start, t), :]
        vext = jnp.concatenate([v, jnp.ones_like(v)], axis=1)
        return jnp.dot(p_ref[slot], vext, preferred_element_type=F32)

    def accumulate(tile, alphas, pv):
        for half in range(2):
            rows = slice(half * t, (half + 1) * t)
            acc_ref[tile, rows] = jnp.tile(alphas[half], (1, 2)) * acc_ref[tile, rows] + pv[rows]

    blocks = []
    for k in range(nq - 1):
        tile = jnp.where(k >= i, 1, 0)
        blocks.append((tile, k - tile * i, False))
    blocks.append((0, i, True))
    blocks.append((1, nq - 1 - i, True))
    s_next = scores(blocks[0][0], blocks[0][1])
    for k, (tile, j, diagonal) in enumerate(blocks):
        s_cur = s_next
        if k + 1 < len(blocks):
            s_next = scores(blocks[k + 1][0], blocks[k + 1][1])
        alphas = softmax_part(s_cur, tile, k % 2, diagonal)
        accumulate(tile, alphas, value_matmul(j, k % 2))
        between_blocks()

    lam =(jnp.exp(jnp.sum(lq1_ref[...] * lk1_ref[...], axis=-1, keepdims=True))
           - jnp.exp(jnp.sum(lq2_ref[...] * lk2_ref[...], axis=-1, keepdims=True)) + lambda_init)
    for tile, o_ref in enumerate((oa_ref, ob_ref)):
        acc = acc_ref[tile]
        o_all = acc[:, 0:DA_V_DIM] / acc[:, DA_V_DIM:2 * DA_V_DIM]
        o = o_all[0:t] - lam * o_all[t:2 * t]
        o = _rms(o, on_ref[...]) * (1.0 - lambda_init)
        o_ref[...] = o.astype(o_ref.dtype)


def _alibi_slopes_log2():
    return np.asarray([2.0 ** (-8.0 * (hh + 1) / DA_HEADS) * LOG2E for hh in range(DA_HEADS)], np.float32)


def _alibi_key_rows(seq):
    col = _alibi_slopes_log2()[:, None] * np.arange(seq, dtype=np.float32)[None, :]
    b1 = col.astype(BF16)
    r1 = col - b1.astype(np.float32)
    b2 = r1.astype(BF16)
    b3 = (r1 - b2.astype(np.float32)).astype(BF16)
    rows = np.zeros((DA_HEADS, LANES, seq), BF16)
    rows[:, 0], rows[:, 1], rows[:, 2] = b1, b2, b3
    return jnp.asarray(rows)


N_ATTN_IN = 10
N_GDN_IN = 7
N_ATTN_SCRATCH = 6


def _mixer_kernel(slopes_ref, *refs, t, nq, batch, lambda_init):
    attn_in = refs[:N_ATTN_IN]
    gdn_in = refs[N_ATTN_IN:N_ATTN_IN + N_GDN_IN]
    outs = refs[N_ATTN_IN + N_GDN_IN:N_ATTN_IN + N_GDN_IN + 3]
    scratch = refs[N_ATTN_IN + N_GDN_IN + 3:]
    b, h, i = pl.program_id(0), pl.program_id(1), pl.program_id(2)
    chunk = (b * DA_HEADS + h) * (nq // 2) + i
    gdn = _gdn_phases(chunk, *gdn_in, outs[2], *scratch[N_ATTN_SCRATCH:], batch=batch)
    next(gdn)
    _attn_step(h, i, slopes_ref, *attn_in, outs[0], outs[1], *scratch[:N_ATTN_SCRATCH],
               t=t, nq=nq, lambda_init=lambda_init, between_blocks=lambda: next(gdn, None))
    for _ in gdn:
        pass


def _mixer(q, kt, v, lq1, lk1, lq2, lk2, out_norm_a, qkv3, z3, ba3, conv_taps, alog_row, dt_row, out_norm_b,
           batch, seq, t, lambda_init):
    nq = seq // t
    half = nq // 2
    width = DA_HEADS * DA_V_DIM
    c = GDN_CHUNK
    hd = GDN_HEADS * GDN_DIM
    cw = qkv3.shape[2]
    assert seq // c == batch * DA_HEADS * half
    slopes = jnp.asarray(_alibi_slopes_log2())
    vec = lambda b, h, i, sl: (0, 0)
    blk = lambda b, h, i, sl: (0, (b * DA_HEADS + h) * half + i, 0)
    grid_spec = pltpu.PrefetchScalarGridSpec(
        num_scalar_prefetch=1,
        grid=(batch, DA_HEADS, half),
        in_specs=[pl.BlockSpec((t, LANES), lambda b, h, i, sl: (b * nq + i, h)),
                  pl.BlockSpec((t, LANES), lambda b, h, i, sl: (b * nq + nq - 1 - i, h)),
                  pl.BlockSpec((LANES, seq), lambda b, h, i, sl: (h, b)),
                  pl.BlockSpec((1, LANES, seq), lambda b, h, i, sl: (h, 0, 0)),
                  pl.BlockSpec((seq, LANES), lambda b, h, i, sl: (b, h)),
                  pl.BlockSpec((1, DA_QK_DIM), vec), pl.BlockSpec((1, DA_QK_DIM), vec),
                  pl.BlockSpec((1, DA_QK_DIM), vec), pl.BlockSpec((1, DA_QK_DIM), vec),
                  pl.BlockSpec((1, DA_V_DIM), vec),
                  pl.BlockSpec((batch, c, cw), blk),
                  pl.BlockSpec((batch, c, hd), blk),
                  pl.BlockSpec((batch, c, LANES), blk),
                  pl.BlockSpec((CONV_K, cw), vec),
                  pl.BlockSpec((1, LANES), vec),
                  pl.BlockSpec((1, LANES), vec),
                  pl.BlockSpec((1, GDN_DIM), vec)],
        out_specs=[pl.BlockSpec((t, LANES), lambda b, h, i, sl: (b * half + i, h)),
                   pl.BlockSpec((t, LANES), lambda b, h, i, sl: (b * half + half - 1 - i, h)),
                   pl.BlockSpec((batch, c, hd), blk)],
        scratch_shapes=[pltpu.VMEM((2, 2 * t, 2 * LANES), BF16),
                        pltpu.VMEM((2, 2 * t, LANES), F32),
                        pltpu.VMEM((2, 2 * t, 2 * DA_V_DIM), F32),
                        pltpu.VMEM((2, 2 * t, t), BF16),
                        pltpu.VMEM((t, t), F32),
                        pltpu.VMEM((2 * LANES, seq), BF16),
                        pltpu.VMEM((batch, c + 8, cw), F32),
                        pltpu.VMEM((batch * GDN_HEADS, GDN_DIM, GDN_DIM), F32)])
    return pl.pallas_call(
        functools.partial(_mixer_kernel, t=t, nq=nq, batch=batch, lambda_init=lambda_init),
        grid_spec=grid_spec,
        out_shape=[jax.ShapeDtypeStruct((batch * seq // 2, width), BF16)] * 2
        + [jax.ShapeDtypeStruct((batch, seq, hd), BF16)],
        compiler_params=_cparams(3),
    )(slopes, q, q, kt, _alibi_key_rows(seq), v, lq1, lk1, lq2, lk2, out_norm_a,
      qkv3, z3, ba3, conv_taps, alog_row, dt_row, out_norm_b)


def _gdn_phases(step, u_ref, z_ref, ba_ref, cw_ref, alog_ref, dt_ref, on_ref, o_ref, stage_ref, state_ref, *, batch):
    c = GDN_CHUNK
    hd = GDN_HEADS * GDN_DIM

    @pl.when(step == 0)
    def _init():
        state_ref[...] = jnp.zeros(state_ref.shape, F32)
        stage_ref[:, 0:8, :] = jnp.zeros((batch, 8, stage_ref.shape[2]), F32)

    yield

    row = lax.broadcasted_iota(I32, (c, c), 0)
    col = lax.broadcasted_iota(I32, (c, c), 1)
    tri = row >= col
    strict = row > col
    tril_ones = jnp.where(tri, 1.0, 0.0).astype(BF16)

    chains = []
    for b in range(batch):
        stage_ref[b, 8:8 + c, :] = u_ref[b]
        u = stage_ref[b]
        y = cw_ref[CONV_K - 1:CONV_K, :] * u
        for back in range(1, CONV_K):
            y = y + cw_ref[CONV_K - 1 - back:CONV_K - back, :] * pltpu.roll(u, back, 0)
        stage_ref[b, 0:8, :] = stage_ref[b, c:c + 8, :]
        y = _silu(y[8:8 + c, :])

        ba = ba_ref[b]
        beta_all = _sigmoid(ba)
        g_all = -jnp.exp(alog_ref[...]) * _softplus(ba + dt_ref[...])
        gcum_all = _mm_exact_lhs(tril_ones, g_all)
        gcum_t = gcum_all.T

        for h in range(GDN_HEADS):
            q = y[:, h * GDN_DIM:(h + 1) * GDN_DIM]
            k = y[:, hd + h * GDN_DIM:hd + (h + 1) * GDN_DIM]
            v = y[:, 2 * hd + h * GDN_DIM:2 * hd + (h + 1) * GDN_DIM]
            q = q * lax.rsqrt(jnp.sum(q * q, axis=-1, keepdims=True) + EPS) * (GDN_DIM ** -0.5)
            k = k * lax.rsqrt(jnp.sum(k * k, axis=-1, keepdims=True) + EPS)
            beta = beta_all[:, h:h + 1]
            gc = gcum_all[:, GDN_HEADS + h:GDN_HEADS + h + 1]
            gr = gcum_t[GDN_HEADS + h:GDN_HEADS + h + 1, :]
            g_last = gc[c - 1:c, :]
            decay = jnp.where(tri, jnp.exp(jnp.where(tri, gc - gr, 0.0)), 0.0)
            e_gc = jnp.exp(gc)
            chains.append(dict(
                b=b, h=h, decay=decay, g_last=g_last,
                p=jnp.where(strict, -(beta * _mm_nt(k, k) * decay), 0.0),
                rhs=jnp.concatenate([v * beta, k * (beta * e_gc)], axis=1),
                qk=_mm_nt(q, k) * decay, q_dec=q * e_gc, k_dec_t=(k * jnp.exp(g_last - gc)).T))
        yield

    def same_block(size):
        return (row // size) == (col // size)

    for ch in chains:
        ch["pk"] = jnp.where(same_block(GDN_BASE), ch["p"], 0.0)
        ch["x"] = ch["pk"]
    for _ in range(int(math.log2(GDN_BASE)) - 1):
        for ch in chains:
            ch["pk"] = _mm(ch["pk"], ch["pk"])
        for ch in chains:
            ch["x"] = ch["x"] + ch["pk"] + _mm(ch["x"], ch["pk"])
        yield
    size = GDN_BASE
    while size < c:
        pair_only = same_block(2 * size) & jnp.logical_not(same_block(size))
        for ch in chains:
            e = jnp.where(pair_only, -ch["p"], 0.0)
            ch["y"] = e + _mm(ch["x"], e)
        for ch in chains:
            ch["x"] = ch["x"] - (ch["y"] + _mm(ch["y"], ch["x"]))
        size *= 2
        yield

    for ch in chains:
        ch["s_prev"] = state_ref[ch["b"] * GDN_HEADS + ch["h"]]
        ch["o_inter"] = _mm(ch["q_dec"], ch["s_prev"])
        ch["sol"] = ch["rhs"] + _mm(ch["x"], ch["rhs"])
    yield
    for ch in chains:
        ch["v_new"] = ch["sol"][:, :GDN_DIM] - _mm(ch["sol"][:, GDN_DIM:], ch["s_prev"])
    yield
    for ch in chains:
        ch["o"] = ch["o_inter"] + _mm(ch["qk"], ch["v_new"])
        ch["s_new"] = ch["s_prev"] * jnp.exp(ch["g_last"]) + _mm(ch["k_dec_t"], ch["v_new"])
    yield
    outs = [[None] * GDN_HEADS for _ in range(batch)]
    for ch in chains:
        zg = z_ref[ch["b"], :, ch["h"] * GDN_DIM:(ch["h"] + 1) * GDN_DIM]
        outs[ch["b"]][ch["h"]] = (_rms(ch["o"], on_ref[...]) * _silu(zg)).astype(o_ref.dtype)
    o_ref[...] = jnp.stack([jnp.concatenate(heads, axis=1) for heads in outs], axis=0)
    state_ref[...] = jnp.stack([ch["s_new"] for ch in chains], axis=0)


def _merge_kernel(x_ref, mod_ref, g1_ref, g2_ref, oa_lo_ref, oa_hi_ref, ob_ref, wgate_ref, wa_ref, wb_ref, wout_ref,
                  wrt_ref, brt_ref, x1_ref, h2_ref, ri_ref, rf_ref, cnt_ref, carry_ref, *, tiles_per_batch):
    @pl.when(pl.program_id(0) == 0)
    def _init():
        carry_ref[...] = jnp.zeros(carry_ref.shape, F32)

    d = x_ref.shape[1]
    mod = mod_ref[0]
    shift1, scale1, gate1 = mod[0:1, :], mod[1:2, :], mod[2:3, :]
    shift2, scale2 = mod[3:4, :], mod[4:5, :]
    first_half = (pl.program_id(0) % tiles_per_batch) < tiles_per_batch // 2
    tm = x_ref.shape[0] // MERGE_SPLIT
    pieces = [dict(rows=slice(s * tm, (s + 1) * tm)) for s in range(MERGE_SPLIT)]
    for pc in pieces:
        pc["x"] = x_ref[pc["rows"], :]
        hb = (_rms(pc["x"], g1_ref[...]) * (1.0 + scale1) + shift1).astype(BF16)
        pc["gates"] = _sigmoid(jnp.dot(hb, wgate_ref[...], preferred_element_type=F32))
    for pc in pieces:
        oa = jnp.where(first_half, oa_lo_ref[pc["rows"], :], oa_hi_ref[pc["rows"], :])
        pc["ya"] = jnp.dot(oa, wa_ref[...], preferred_element_type=F32)
        pc["yb"] = jnp.dot(ob_ref[pc["rows"], :], wb_ref[...], preferred_element_type=F32)
    for pc in pieces:
        pc["merged"] = pc["gates"][:, :d] * pc["ya"] + pc["gates"][:, d:] * pc["yb"]
    for pc in pieces:
        x1 = pc["x"] + gate1 * _mm(pc["merged"], wout_ref[...])
        x1_ref[pc["rows"], :] = x1
        pc["h2"] = _rms(x1, g2_ref[...]) * (1.0 + scale2) + shift2
        h2_ref[pc["rows"], :] = _pack_bf16_pairs(pc["h2"])
    for pc in pieces:
        pc["logits"] = _mm3_narrow(pc["h2"], wrt_ref[...]) + brt_ref[...]
    for pc in pieces:
        _route_tile(pc["logits"], ri_ref.at[pc["rows"], :], rf_ref.at[pc["rows"], :], cnt_ref, carry_ref)


def _merge(x2, mod3, g1, g2, oa_lo, oa_hi, ob, w_gate, w_a, w_b, w_out, w_rt, b_rt, tm, tiles_per_batch):
    n, d = x2.shape
    hp = tiles_per_batch // 2
    const = lambda i: (0, 0)
    row = lambda i: (i, 0)
    lo = lambda i: ((i // tiles_per_batch) * hp + jnp.minimum(i % tiles_per_batch, hp - 1), 0)
    hi = lambda i: ((i // tiles_per_batch) * hp + jnp.maximum(i % tiles_per_batch - hp, 0), 0)
    return pl.pallas_call(
        functools.partial(_merge_kernel, tiles_per_batch=tiles_per_batch),
        grid=(n // tm,),
        in_specs=[pl.BlockSpec((tm, d), row),
                  pl.BlockSpec((1, 6, d), lambda i: (i // tiles_per_batch, 0, 0)),
                  pl.BlockSpec((1, d), const), pl.BlockSpec((1, d), const),
                  pl.BlockSpec((tm, oa_lo.shape[1]), lo), pl.BlockSpec((tm, oa_hi.shape[1]), hi),
                  pl.BlockSpec((tm, ob.shape[1]), row),
                  pl.BlockSpec(w_gate.shape, const), pl.BlockSpec(w_a.shape, const),
                  pl.BlockSpec(w_b.shape, const), pl.BlockSpec(w_out.shape, const),
                  pl.BlockSpec(w_rt.shape, const), pl.BlockSpec((1, LANES), const)],
        out_specs=[pl.BlockSpec((tm, d), row), pl.BlockSpec((tm, d // 2), row), pl.BlockSpec((tm, LANES), row),
                   pl.BlockSpec((tm, LANES), row), pl.BlockSpec((8, LANES), const)],
        out_shape=[jax.ShapeDtypeStruct((n, d), F32), jax.ShapeDtypeStruct((n, d // 2), I32),
                   jax.ShapeDtypeStruct((n, LANES), I32), jax.ShapeDtypeStruct((n, LANES), F32),
                   jax.ShapeDtypeStruct((8, LANES), F32)],
        scratch_shapes=[pltpu.VMEM((8, LANES), F32)],
        compiler_params=_cparams(1),
    )(x2, mod3, g1, g2, oa_lo, oa_hi, ob, w_gate, w_a, w_b, w_out, w_rt, b_rt)


def _route_tile(lg, ri_ref, rf_ref, cnt_ref, carry_ref):
    tm = lg.shape[0]
    lane = lax.broadcasted_iota(I32, lg.shape, 1)
    big = jnp.int32(LANES)
    is_grp = (lane >= N_EXPERTS) & (lane < N_EXPERTS + N_GROUPS)
    gl = jnp.where(is_grp, lg, NEG_INF)
    gmax = jnp.max(gl, axis=-1, keepdims=True)
    g_top = jnp.min(jnp.where(is_grp & (gl == gmax), lane, big), axis=-1, keepdims=True) - N_EXPERTS
    g_top_p = 1.0 / jnp.sum(jnp.where(is_grp, jnp.exp(gl - gmax), 0.0), axis=-1, keepdims=True)
    in_grp = (lane < N_EXPERTS) & ((lane // EXPERTS_PER_GROUP) == g_top)
    el = jnp.where(in_grp, lg, NEG_INF)
    v1 = jnp.max(el, axis=-1, keepdims=True)
    e1 = jnp.min(jnp.where(in_grp & (el == v1), lane, big), axis=-1, keepdims=True)
    rest = in_grp & (lane != e1)
    el2 = jnp.where(rest, lg, NEG_INF)
    v2 = jnp.max(el2, axis=-1, keepdims=True)
    e2 = jnp.min(jnp.where(rest & (el2 == v2), lane, big), axis=-1, keepdims=True)
    ex = jnp.exp(v2 - v1)
    w1 = g_top_p / (1.0 + ex)
    w2 = g_top_p * ex / (1.0 + ex)

    oh1 = lane == e1
    oh2 = lane == e2
    ohs = (jnp.where(oh1, 1.0, 0.0) + jnp.where(oh2, 1.0, 0.0)).astype(BF16)
    r = lax.broadcasted_iota(I32, (tm, tm), 0)
    c = lax.broadcasted_iota(I32, (tm, tm), 1)
    before = jnp.where(r > c, 1.0, 0.0).astype(BF16)
    prior = jnp.dot(before, ohs, preferred_element_type=F32) + carry_ref[0:1, :]
    rank1 = jnp.sum(jnp.where(oh1, prior, 0.0), axis=-1, keepdims=True)
    rank2 = jnp.sum(jnp.where(oh2, prior, 0.0), axis=-1, keepdims=True)
    carry_ref[...] = carry_ref[...] + jnp.sum(ohs.astype(F32), axis=0, keepdims=True)
    cnt_ref[...] = carry_ref[...]

    r1 = rank1.astype(I32)
    r2 = rank2.astype(I32)
    ri = jnp.where(lane == 0, e1, jnp.where(lane == 1, e2, jnp.where(lane == 2, r1, jnp.where(lane == 3, r2, 0))))
    ri_ref[...] = ri
    rf_ref[...] = jnp.where(lane == 0, w1, jnp.where(lane == 1, w2, 0.0))


def _moe_kernel(be_ref, act_ref, f1_ref, f3_ref, f2_ref, xb_ref, w1_ref, w3_ref, w2_ref, o_ref,
                w1b_ref, w3b_ref, w2b_ref):
    i = pl.program_id(0)
    prev = be_ref[jnp.maximum(i - 1, 0)]

    @pl.when((i == 0) | (be_ref[i] != prev))
    def _cast_weights():
        w1b_ref[...] = w1_ref[...].astype(BF16)
        w3b_ref[...] = w3_ref[...].astype(BF16)
        w2b_ref[...] = w2_ref[...].astype(BF16)

    valid = act_ref[i]

    def expert_mlp(n_rows):
        xp = xb_ref[0:n_rows, :]
        r = lax.broadcasted_iota(I32, xp.shape, 0)
        xb = _unpack_bf16_pairs(jnp.where(r < valid, xp, 0))
        a = jnp.dot(xb, w1b_ref[...], preferred_element_type=F32)
        g = jnp.dot(xb, w3b_ref[...], preferred_element_type=F32)
        hid = (_silu(a) * g).astype(BF16)
        o_ref[0:n_rows, :] = _pack_bf16_pairs(jnp.dot(hid, w2b_ref[...], preferred_element_type=F32))

    @pl.when(valid > MOE_SUB)
    def _full():
        expert_mlp(MOE_ROWS)

    @pl.when((valid > 0) & (valid <= MOE_SUB))
    def _partial():
        expert_mlp(MOE_SUB)
        o_ref[MOE_SUB:MOE_ROWS, :] = jnp.zeros((MOE_ROWS - MOE_SUB, o_ref.shape[1]), o_ref.dtype)

    @pl.when(valid == 0)
    def _idle():
        o_ref[...] = jnp.zeros(o_ref.shape, o_ref.dtype)


def _moe(blk_expert, blk_rows, fetch, xb, w1, w3, w2):
    cap = xb.shape[0]
    d = w1.shape[1]
    nb = cap // MOE_ROWS
    de = w1.shape[2]
    grid_spec = pltpu.PrefetchScalarGridSpec(
        num_scalar_prefetch=5,
        grid=(nb,),
        in_specs=[pl.BlockSpec((MOE_ROWS, d // 2), lambda i, be, act, f1, f3, f2: (i, 0)),
                  pl.BlockSpec((None, d, de), lambda i, be, act, f1, f3, f2: (f1[i], 0, 0)),
                  pl.BlockSpec((None, d, de), lambda i, be, act, f1, f3, f2: (f3[i], 0, 0)),
                  pl.BlockSpec((None, de, d), lambda i, be, act, f1, f3, f2: (f2[i], 0, 0))],
        out_specs=pl.BlockSpec((MOE_ROWS, d // 2), lambda i, be, act, f1, f3, f2: (i, 0)),
        scratch_shapes=[pltpu.VMEM((d, de), BF16), pltpu.VMEM((d, de), BF16), pltpu.VMEM((de, d), BF16)])
    return pl.pallas_call(
        _moe_kernel,
        grid_spec=grid_spec,
        out_shape=jax.ShapeDtypeStruct((cap, d // 2), I32),
        compiler_params=_cparams(1),
    )(blk_expert, blk_rows, *fetch, xb, w1, w3, w2)


def _dest_kernel(ri_ref, pstart_ref, o_ref):
    ri = ri_ref[...]
    lane = lax.broadcasted_iota(I32, ri.shape, 1)
    pstart = pstart_ref[...]
    base1 = jnp.sum(jnp.where(lane == ri[:, 0:1], pstart, 0), axis=-1, keepdims=True)
    base2 = jnp.sum(jnp.where(lane == ri[:, 1:2], pstart, 0), axis=-1, keepdims=True)
    dest = jnp.where(lane == 0, base1 + ri[:, 2:3], jnp.where(lane == 1, base2 + ri[:, 3:4], 0))
    o_ref[...] = dest.astype(F32).T[0:8, :].astype(I32)


def _dest(ri, pstart_row, tm):
    n = ri.shape[0]
    return pl.pallas_call(
        _dest_kernel,
        grid=(n // tm,),
        in_specs=[pl.BlockSpec((tm, LANES), lambda i: (i, 0)), pl.BlockSpec((1, LANES), lambda i: (0, 0))],
        out_specs=pl.BlockSpec((8, tm), lambda i: (0, i)),
        out_shape=jax.ShapeDtypeStruct((8, n), I32),
        compiler_params=_cparams(1),
    )(ri, pstart_row)


def _sc_mesh():
    return plsc.VectorSubcoreMesh(core_axis_name="core", subcore_axis_name="subcore")


def _sc_scatter_rows(x, idx, cap):
    n, d = x.shape
    windows = idx.shape[0]
    tiles = n // SC_WINDOW
    mesh = _sc_mesh()
    workers = mesh.num_cores * mesh.num_subcores
    per_worker = windows // workers

    @functools.partial(pl.kernel, out_type=jax.ShapeDtypeStruct((cap, d), x.dtype), mesh=mesh,
                       scratch_types=[pltpu.VMEM((1, SC_WINDOW), I32), pltpu.VMEM((SC_WINDOW, d), x.dtype)])
    def scatter(x_hbm, i_hbm, o_hbm, idx_v, rows_v):
        wid = lax.axis_index("subcore") * mesh.num_cores + lax.axis_index("core")

        @pl.loop(0, per_worker)
        def _(j):
            w = wid * per_worker + j
            pltpu.sync_copy(i_hbm.at[pl.ds(w, 1)], idx_v)
            pltpu.sync_copy(x_hbm.at[pl.ds(lax.rem(w, tiles) * SC_WINDOW, SC_WINDOW)], rows_v)
            pltpu.sync_copy(rows_v, o_hbm.at[idx_v.at[0]])

    return scatter(x, idx)


def _sc_gather_rows(table, idx):
    d = table.shape[1]
    windows = idx.shape[0]
    mesh = _sc_mesh()
    workers = mesh.num_cores * mesh.num_subcores
    per_worker = windows // workers

    @functools.partial(pl.kernel, out_type=jax.ShapeDtypeStruct((windows * SC_WINDOW, d), table.dtype), mesh=mesh,
                       scratch_types=[pltpu.VMEM((1, SC_WINDOW), I32), pltpu.VMEM((SC_WINDOW, d), table.dtype)])
    def gather(t_hbm, i_hbm, o_hbm, idx_v, rows_v):
        wid = lax.axis_index("subcore") * mesh.num_cores + lax.axis_index("core")

        @pl.loop(0, per_worker)
        def _(j):
            w = wid * per_worker + j
            pltpu.sync_copy(i_hbm.at[pl.ds(w, 1)], idx_v)
            pltpu.sync_copy(t_hbm.at[idx_v.at[0]], rows_v)
            pltpu.sync_copy(rows_v, o_hbm.at[pl.ds(w * SC_WINDOW, SC_WINDOW)])

    return gather(table, idx)


def _combine_kernel(x1_ref, mod_ref, rf_ref, ya_ref, yb_ref, o_ref):
    gate2 = mod_ref[0][5:6, :]
    rf = rf_ref[...]
    ya = _unpack_bf16_pairs(ya_ref[...]).astype(F32)
    yb = _unpack_bf16_pairs(yb_ref[...]).astype(F32)
    y = rf[:, 0:1] * ya + rf[:, 1:2] * yb
    o_ref[...] = x1_ref[...] + gate2 * y


def _combine(x1, mod3, rf, y12, tm, tiles_per_batch):
    n, d = x1.shape
    tiles = n // tm
    row = lambda i: (i, 0)
    return pl.pallas_call(
        _combine_kernel,
        grid=(tiles,),
        in_specs=[pl.BlockSpec((tm, d), row),
                  pl.BlockSpec((1, 6, d), lambda i: (i // tiles_per_batch, 0, 0)),
                  pl.BlockSpec((tm, LANES), row),
                  pl.BlockSpec((tm, d // 2), row), pl.BlockSpec((tm, d // 2), lambda i: (i + tiles, 0))],
        out_specs=pl.BlockSpec((tm, d), row),
        out_shape=jax.ShapeDtypeStruct((n, d), F32),
        compiler_params=_cparams(1),
    )(x1, mod3, rf, y12, y12)


def _pad_lanes(a, offset=0):
    return jnp.pad(a, ((0, 0), (offset, LANES - offset - a.shape[1])))


def _layer(x, c, layer, w_ada, b_ada, norm1_gain, w_in, da_q_norm, da_k_norm, lq1, lk1, lq2, lk2, da_out_norm,
           gdn_conv, gdn_a_log, gdn_dt_bias, gdn_out_norm, w_branch_a, w_branch_b, w_out, norm2_gain,
           w_group, b_group, w_router, b_router, w1, w3, w2):
    batch, seq, d = x.shape
    n = batch * seq
    tm = min(512, seq)
    tiles_per_batch = seq // tm
    t_attn = min(512, seq)
    lambda_init = 0.8 - 0.6 * math.exp(-0.3 * layer)

    mod = _ada(jnp.pad(c, ((0, 8 - batch), (0, 0))), w_ada, b_ada)[:batch]
    mod3 = mod.reshape(batch, 6, d)

    da_w = DA_HEADS * 2 * DA_QK_DIM
    dv_w = DA_HEADS * DA_V_DIM
    gd_w = GDN_HEADS * GDN_DIM
    w_gate = w_in[:, 2 * da_w + dv_w + 4 * gd_w + 2 * GDN_HEADS:].astype(BF16)
    qn = jnp.tile(da_q_norm, 2 * DA_HEADS).reshape(1, da_w)
    kn = jnp.tile(da_k_norm, 2 * DA_HEADS).reshape(1, da_w)

    x2 = x.reshape(n, d)
    g1 = norm1_gain.reshape(1, d)
    q, kt, v, gqkv, z, ba = _proj(x2, mod3, g1, w_in, qn, kn, tm, tiles_per_batch)

    assert t_attn == tm
    alog_row = _pad_lanes(gdn_a_log.reshape(1, -1), GDN_HEADS)
    dt_row = _pad_lanes(gdn_dt_bias.reshape(1, -1), GDN_HEADS)
    oa_lo, oa_hi, o_b = _mixer(q, kt, v, lq1.reshape(1, -1), lk1.reshape(1, -1), lq2.reshape(1, -1),
                               lk2.reshape(1, -1), da_out_norm.reshape(1, -1),
                               gqkv.reshape(batch, seq, -1), z.reshape(batch, seq, -1), ba.reshape(batch, seq, -1),
                               gdn_conv, alog_row, dt_row, gdn_out_norm.reshape(1, -1),
                               batch, seq, t_attn, lambda_init)
    o_b = o_b.reshape(n, gd_w)

    w_rt = _pad_lanes(jnp.concatenate([w_router, w_group], axis=1))
    b_rt = _pad_lanes(jnp.concatenate([b_router, b_group]).reshape(1, -1))
    x1, h2, ri, rf, cnt = _merge(x2, mod3, g1, norm2_gain.reshape(1, d), oa_lo, oa_hi, o_b, w_gate,
                                 w_branch_a.astype(BF16), w_branch_b.astype(BF16), w_out.astype(BF16),
                                 w_rt, b_rt, tm * MERGE_SPLIT, tiles_per_batch // MERGE_SPLIT)

    counts = cnt[0, :N_EXPERTS].astype(I32)
    padded = (counts + MOE_ROWS - 1) // MOE_ROWS * MOE_ROWS
    pend = jnp.cumsum(padded)
    pstart = pend - padded
    cap = 2 * n + N_EXPERTS * MOE_ROWS
    nb = cap // MOE_ROWS
    blk_start = jnp.arange(nb, dtype=I32) * MOE_ROWS
    blk_expert = jnp.minimum(jnp.sum(pend[None, :] <= blk_start[:, None], axis=1), N_EXPERTS - 1).astype(I32)
    blk_rows = jnp.clip((pstart + counts)[blk_expert] - blk_start, 0, MOE_ROWS).astype(I32)
    blk_rows = jnp.where(blk_start < pend[-1], blk_rows, 0)
    first_blk = pstart // MOE_ROWS
    n_blk = padded // MOE_ROWS
    pos = jnp.arange(nb, dtype=I32) - first_blk[blk_expert]
    nxt = blk_expert[jnp.clip(first_blk[blk_expert] + n_blk[blk_expert], 0, nb - 1)]
    fetch = [jnp.where(pos < jnp.minimum(lag, n_blk[blk_expert]), blk_expert, nxt).astype(I32) for lag in (1, 2, 3)]
    dest = _dest(ri, _pad_lanes(pstart.reshape(1, -1)), min(4 * tm, n))
    idx = dest[0:2].reshape(2 * n // SC_WINDOW, SC_WINDOW)

    xb = _sc_scatter_rows(h2, idx, cap)
    yb = _moe(blk_expert, blk_rows, fetch, xb, w1, w3, w2)
    y12 = _sc_gather_rows(yb, idx)
    out = _combine(x1, mod3, rf, y12, 2 * tm, tiles_per_batch // 2)
    return out.reshape(batch, seq, d)


def kernel(x, c, w_ada, b_ada, norm1_gain, w_in, da_q_norm, da_k_norm, da_lambda_q1, da_lambda_k1, da_lambda_q2,
           da_lambda_k2, da_out_norm, gdn_conv, gdn_a_log, gdn_dt_bias, gdn_out_norm, w_branch_a, w_branch_b,
           w_out, norm2_gain, w_group, b_group, w_router, b_router, w1, w3, w2):
    for layer in range(w_ada.shape[0]):
        x = _layer(x, c, layer, w_ada[layer], b_ada[layer], norm1_gain[layer], w_in[layer], da_q_norm[layer],
                   da_k_norm[layer], da_lambda_q1[layer], da_lambda_k1[layer], da_lambda_q2[layer],
                   da_lambda_k2[layer], da_out_norm[layer], gdn_conv[layer], gdn_a_log[layer], gdn_dt_bias[layer],
                   gdn_out_norm[layer], w_branch_a[layer], w_branch_b[layer], w_out[layer], norm2_gain[layer],
                   w_group[layer], b_group[layer], w_router[layer], b_router[layer], w1[layer], w3[layer], w2[layer])
    return x
```

```python
import functools
import math

import jax
import jax.numpy as jnp
import numpy as np
from jax import lax
from jax.experimental import pallas as pl
from jax.experimental.pallas import tpu as pltpu
from jax.experimental.pallas import tpu_sc as plsc

F32 = jnp.float32
BF16 = jnp.bfloat16
I32 = jnp.int32

EPS = 1e-6
NEG_INF = -1e30
MASK_CHUNK = 64
LOG2E = 1.4426950408889634
ALIBI_ROWS = 3

DA_HEADS = 4
DA_QK_DIM = 64
DA_V_DIM = 128
GDN_HEADS = 4
GDN_DIM = 128
CONV_K = 4
N_GROUPS = 4
EXPERTS_PER_GROUP = 8
N_EXPERTS = N_GROUPS * EXPERTS_PER_GROUP
D_EXPERT = 512

LANES = 128
VMEM_LIMIT = 56 * 1024 * 1024

MERGE_SPLIT = 2
GDN_CHUNK = 128
GDN_BASE = 16
MOE_ROWS = 512
MOE_SUB = 256
SC_WINDOW = 128


def _cparams(n_axes):
    return pltpu.CompilerParams(dimension_semantics=("arbitrary",) * n_axes,
                                vmem_limit_bytes=VMEM_LIMIT)


def _mm(a, b):
    return jnp.dot(a.astype(BF16), b.astype(BF16), preferred_element_type=F32)


def _mm_nt(a, b):
    return lax.dot_general(a.astype(BF16), b.astype(BF16), (((1,), (1,)), ((), ())),
                           preferred_element_type=F32)


def _split2(a):
    hi = a.astype(BF16)
    lo = (a - hi.astype(F32)).astype(BF16)
    return hi, lo


def _mm3(a, b):
    ah, al = _split2(a)
    bh, bl = _split2(b)
    out = jnp.dot(ah, bh, preferred_element_type=F32)
    out = out + jnp.dot(ah, bl, preferred_element_type=F32)
    out = out + jnp.dot(al, bh, preferred_element_type=F32)
    return out


def _mm3_narrow(a, b):
    ah, al = _split2(a)
    bh, bl = _split2(b)
    n = b.shape[1]
    both = jnp.dot(ah, jnp.concatenate([bh, bl], axis=1), preferred_element_type=F32)
    return both[:, :n] + both[:, n:] + jnp.dot(al, bh, preferred_element_type=F32)


def _mm_exact_lhs(a_bf16, b):
    b1 = b.astype(BF16)
    r1 = b - b1.astype(F32)
    b2 = r1.astype(BF16)
    b3 = (r1 - b2.astype(F32)).astype(BF16)
    out = jnp.dot(a_bf16, b1, preferred_element_type=F32)
    out = out + jnp.dot(a_bf16, b2, preferred_element_type=F32)
    out = out + jnp.dot(a_bf16, b3, preferred_element_type=F32)
    return out


def _sigmoid(x):
    return 1.0 / (1.0 + jnp.exp(-x))


def _silu(x):
    return x * _sigmoid(x)


def _softplus(x):
    return jnp.maximum(x, 0.0) + jnp.log(1.0 + jnp.exp(-jnp.abs(x)))


def _pack_bf16_pairs(x):
    w = x.shape[1] // 2
    bits = lax.bitcast_convert_type(x.astype(BF16).astype(F32), jnp.uint32)
    lo = lax.shift_right_logical(bits[:, :w], jnp.uint32(16))
    hi = bits[:, w:] & jnp.uint32(0xFFFF0000)
    return lax.bitcast_convert_type(hi | lo, I32)


def _unpack_bf16_pairs(p):
    bits = lax.bitcast_convert_type(p, jnp.uint32)
    lo = lax.bitcast_convert_type(lax.shift_left(bits, jnp.uint32(16)), F32)
    hi = lax.bitcast_convert_type(bits & jnp.uint32(0xFFFF0000), F32)
    return jnp.concatenate([lo, hi], axis=1).astype(BF16)


def _rms(x, gain):
    return x * lax.rsqrt(jnp.mean(x * x, axis=-1, keepdims=True) + EPS) * gain


def _ada_kernel(c_ref, w_ref, b_ref, o_ref):
    sc = _silu(c_ref[...])
    o_ref[...] = _mm3(sc, w_ref[...]) + b_ref[...]


def _ada(c_pad, w_ada, b_ada):
    rows, d = c_pad.shape
    n = w_ada.shape[1]
    tn = d
    return pl.pallas_call(
        _ada_kernel,
        grid=(n // tn,),
        in_specs=[pl.BlockSpec((rows, d), lambda j: (0, 0)),
                  pl.BlockSpec((d, tn), lambda j: (0, j)),
                  pl.BlockSpec((1, tn), lambda j: (0, j))],
        out_specs=pl.BlockSpec((rows, tn), lambda j: (0, j)),
        out_shape=jax.ShapeDtypeStruct((rows, n), F32),
        compiler_params=_cparams(1),
    )(c_pad, w_ada, b_ada.reshape(1, n))


def _group_rms64(x, gain):
    tm, width = x.shape
    lane = lax.broadcasted_iota(I32, (tm, LANES), 1)
    low = lane < DA_QK_DIM
    parts = []
    for j in range(width // LANES):
        blk = x[:, j * LANES:(j + 1) * LANES]
        sq = blk * blk
        s_lo = jnp.sum(jnp.where(low, sq, 0.0), axis=-1, keepdims=True)
        s_hi = jnp.sum(jnp.where(low, 0.0, sq), axis=-1, keepdims=True)
        ms = jnp.where(low, s_lo, s_hi) * (1.0 / DA_QK_DIM)
        parts.append(blk * lax.rsqrt(ms + EPS))
    return jnp.concatenate(parts, axis=-1) * gain


def _proj_kernel(x_ref, mod_ref, g1_ref, wda_ref, wg_ref, wba_ref, qn_ref, kn_ref,
                 q_ref, k_ref, v_ref, gqkv_ref, z_ref, ba_ref):
    x = x_ref[...]
    mod = mod_ref[0]
    shift, scale = mod[0:1, :], mod[1:2, :]
    hb = (_rms(x, g1_ref[...]) * (1.0 + scale) + shift).astype(BF16)
    da_w = DA_HEADS * 2 * DA_QK_DIM
    da = jnp.dot(hb, wda_ref[...], preferred_element_type=F32)
    q = _group_rms64(da[:, :da_w], qn_ref[...]) * (DA_QK_DIM ** -0.5 * LOG2E)
    k = _group_rms64(da[:, da_w:2 * da_w], kn_ref[...])
    q_ref[...] = q.astype(BF16)
    k_ref[...] = k.T.astype(BF16)
    v_ref[...] = da[:, 2 * da_w:].astype(BF16)
    g = jnp.dot(hb, wg_ref[...], preferred_element_type=F32)
    conv_w = 3 * GDN_HEADS * GDN_DIM
    gqkv_ref[...] = g[:, :conv_w]
    z_ref[...] = g[:, conv_w:]
    ba_ref[...] = jnp.dot(hb, wba_ref[...], preferred_element_type=F32)


def _proj(x2, mod3, g1, w_da, w_g, w_ba, qn, kn, tm, tiles_per_batch):
    n, d = x2.shape
    da_w = DA_HEADS * 2 * DA_QK_DIM
    dv_w = DA_HEADS * DA_V_DIM
    conv_w = 3 * GDN_HEADS * GDN_DIM
    z_w = GDN_HEADS * GDN_DIM
    const = lambda i: (0, 0)
    row = lambda i: (i, 0)
    return pl.pallas_call(
        _proj_kernel,
        grid=(n // tm,),
        in_specs=[pl.BlockSpec((tm, d), row),
                  pl.BlockSpec((1, 6, d), lambda i: (i // tiles_per_batch, 0, 0)),
                  pl.BlockSpec((1, d), const),
                  pl.BlockSpec(w_da.shape, const),
                  pl.BlockSpec(w_g.shape, const),
                  pl.BlockSpec(w_ba.shape, const),
                  pl.BlockSpec((1, da_w), const),
                  pl.BlockSpec((1, da_w), const)],
        out_specs=[pl.BlockSpec((tm, da_w), row), pl.BlockSpec((da_w, tm), lambda i: (0, i)),
                   pl.BlockSpec((tm, dv_w), row), pl.BlockSpec((tm, conv_w), row),
                   pl.BlockSpec((tm, z_w), row), pl.BlockSpec((tm, LANES), row)],
        out_shape=[jax.ShapeDtypeStruct((n, da_w), BF16), jax.ShapeDtypeStruct((da_w, n), BF16),
                   jax.ShapeDtypeStruct((n, dv_w), BF16), jax.ShapeDtypeStruct((n, conv_w), F32),
                   jax.ShapeDtypeStruct((n, z_w), F32), jax.ShapeDtypeStruct((n, LANES), F32)],
        compiler_params=_cparams(1),
    )(x2, mod3, g1, w_da, w_g, w_ba, qn, kn)


def _attn_step(h, i, slopes_ref, qa_ref, qb_ref, kt_ref, al_ref, v_ref, lq1_ref, lk1_ref, lq2_ref, lk2_ref, on_ref,
               oa_ref, ob_ref, qs_ref, m_ref, acc_ref, p_ref, ddiag_ref, kta_ref, *, t, nq, lambda_init,
               between_blocks):
    slope2 = slopes_ref[h]
    reps = t // LANES

    @pl.when(i == 0)
    def _tables():
        r = lax.broadcasted_iota(I32, (t, t), 0)
        c = lax.broadcasted_iota(I32, (t, t), 1)
        ahead = jnp.maximum(c - r, 0).astype(F32)
        allowed = (c // MASK_CHUNK) <= (r // MASK_CHUNK)
        ddiag_ref[...] = jnp.where(allowed, (-2.0 * slope2) * ahead, NEG_INF)
        kta_ref[0:LANES, :] = kt_ref[...]
        kta_ref[LANES:2 * LANES, :] = al_ref[0]

    for tile, q_ref in enumerate((qa_ref, qb_ref)):
        q = q_ref[...]
        lane = lax.broadcasted_iota(I32, q.shape, 1)
        zero = jnp.zeros_like(q)
        ones = jnp.where(lane < ALIBI_ROWS, 1.0, 0.0).astype(BF16)
        qs_ref[tile, 0:t, :] = jnp.concatenate([jnp.where(lane < DA_QK_DIM, q, zero), ones], axis=1)
        qs_ref[tile, t:2 * t, :] = jnp.concatenate([jnp.where(lane < DA_QK_DIM, zero, q), ones], axis=1)
    m_ref[...] = jnp.full(m_ref.shape, NEG_INF, F32)
    acc_ref[...] = jnp.zeros(acc_ref.shape, F32)

    def scores(tile, j):
        start = pl.multiple_of(j * t, t)
        return jnp.dot(qs_ref[tile], kta_ref[:, pl.ds(start, t)], preferred_element_type=F32)

    def softmax_part(s, tile, slot, diagonal):
        alphas = []
        for half in range(2):
            rows = slice(half * t, (half + 1) * t)
            sb = s[rows] + ddiag_ref[...] if diagonal else s[rows]
            m_prev = m_ref[tile, rows]
            m_next = jnp.maximum(m_prev, jnp.max(sb, axis=-1, keepdims=True))
            alphas.append(jnp.exp2(m_prev - m_next))
            p_ref[slot, rows] = jnp.exp2(sb - jnp.tile(m_next, (1, reps))).astype(BF16)
            m_ref[tile, rows] = m_next
        return alphas

    def value_matmul(j, slot):
        start = pl.multiple_of(j * t, t)
        v = v_ref[pl.ds(start, t), :]
        vext = jnp.concatenate([v, jnp.ones_like(v)], axis=1)
        return jnp.dot(p_ref[slot], vext, preferred_element_type=F32)

    def accumulate(tile, alphas, pv):
        for half in range(2):
            rows = slice(half * t, (half + 1) * t)
            acc_ref[tile, rows] = jnp.tile(alphas[half], (1, 2)) * acc_ref[tile, rows] + pv[rows]

    blocks = []
    for k in range(nq - 1):
        tile = jnp.where(k >= i, 1, 0)
        blocks.append((tile, k - tile * i, False))
    blocks.append((0, i, True))
    blocks.append((1, nq - 1 - i, True))
    s_next = scores(blocks[0][0], blocks[0][1])
    for k, (tile, j, diagonal) in enumerate(blocks):
        s_cur = s_next
        if k + 1 < len(blocks):
            s_next = scores(blocks[k + 1][0], blocks[k + 1][1])
        alphas = softmax_part(s_cur, tile, k % 2, diagonal)
        accumulate(tile, alphas, value_matmul(j, k % 2))
        between_blocks()

    lam =(jnp.exp(jnp.sum(lq1_ref[...] * lk1_ref[...], axis=-1, keepdims=True))
           - jnp.exp(jnp.sum(lq2_ref[...] * lk2_ref[...], axis=-1, keepdims=True)) + lambda_init)
    for tile, o_ref in enumerate((oa_ref, ob_ref)):
        acc = acc_ref[tile]
        o_all = acc[:, 0:DA_V_DIM] / acc[:, DA_V_DIM:2 * DA_V_DIM]
        o = o_all[0:t] - lam * o_all[t:2 * t]
        o = _rms(o, on_ref[...]) * (1.0 - lambda_init)
        o_ref[...] = o.astype(o_ref.dtype)


def _alibi_slopes_log2():
    return np.asarray([2.0 ** (-8.0 * (hh + 1) / DA_HEADS) * LOG2E for hh in range(DA_HEADS)], np.float32)


def _alibi_key_rows(seq):
    col = _alibi_slopes_log2()[:, None] * np.arange(seq, dtype=np.float32)[None, :]
    b1 = col.astype(BF16)
    r1 = col - b1.astype(np.float32)
    b2 = r1.astype(BF16)
    b3 = (r1 - b2.astype(np.float32)).astype(BF16)
    rows = np.zeros((DA_HEADS, LANES, seq), BF16)
    rows[:, 0], rows[:, 1], rows[:, 2] = b1, b2, b3
    return jnp.asarray(rows)


N_ATTN_IN = 10
N_GDN_IN = 7
N_ATTN_SCRATCH = 6


def _mixer_kernel(slopes_ref, *refs, t, nq, batch, lambda_init):
    attn_in = refs[:N_ATTN_IN]
    gdn_in = refs[N_ATTN_IN:N_ATTN_IN + N_GDN_IN]
    outs = refs[N_ATTN_IN + N_GDN_IN:N_ATTN_IN + N_GDN_IN + 3]
    scratch = refs[N_ATTN_IN + N_GDN_IN + 3:]
    b, h, i = pl.program_id(0), pl.program_id(1), pl.program_id(2)
    chunk = (b * DA_HEADS + h) * (nq // 2) + i
    gdn = _gdn_phases(chunk, *gdn_in, outs[2], *scratch[N_ATTN_SCRATCH:], batch=batch)
    next(gdn)
    _attn_step(h, i, slopes_ref, *attn_in, outs[0], outs[1], *scratch[:N_ATTN_SCRATCH],
               t=t, nq=nq, lambda_init=lambda_init, between_blocks=lambda: next(gdn, None))
    for _ in gdn:
        pass


def _mixer(q, kt, v, lq1, lk1, lq2, lk2, out_norm_a, qkv3, z3, ba3, conv_taps, alog_row, dt_row, out_norm_b,
           batch, seq, t, lambda_init):
    nq = seq // t
    half = nq // 2
    width = DA_HEADS * DA_V_DIM
    c = GDN_CHUNK
    hd = GDN_HEADS * GDN_DIM
    cw = qkv3.shape[2]
    assert seq // c == batch * DA_HEADS * half
    slopes = jnp.asarray(_alibi_slopes_log2())
    vec = lambda b, h, i, sl: (0, 0)
    blk = lambda b, h, i, sl: (0, (b * DA_HEADS + h) * half + i, 0)
    grid_spec = pltpu.PrefetchScalarGridSpec(
        num_scalar_prefetch=1,
        grid=(batch, DA_HEADS, half),
        in_specs=[pl.BlockSpec((t, LANES), lambda b, h, i, sl: (b * nq + i, h)),
                  pl.BlockSpec((t, LANES), lambda b, h, i, sl: (b * nq + nq - 1 - i, h)),
                  pl.BlockSpec((LANES, seq), lambda b, h, i, sl: (h, b)),
                  pl.BlockSpec((1, LANES, seq), lambda b, h, i, sl: (h, 0, 0)),
                  pl.BlockSpec((seq, LANES), lambda b, h, i, sl: (b, h)),
                  pl.BlockSpec((1, DA_QK_DIM), vec), pl.BlockSpec((1, DA_QK_DIM), vec),
                  pl.BlockSpec((1, DA_QK_DIM), vec), pl.BlockSpec((1, DA_QK_DIM), vec),
                  pl.BlockSpec((1, DA_V_DIM), vec),
                  pl.BlockSpec((batch, c, cw), blk),
                  pl.BlockSpec((batch, c, hd), blk),
                  pl.BlockSpec((batch, c, LANES), blk),
                  pl.BlockSpec((CONV_K, cw), vec),
                  pl.BlockSpec((1, LANES), vec),
                  pl.BlockSpec((1, LANES), vec),
                  pl.BlockSpec((1, GDN_DIM), vec)],
        out_specs=[pl.BlockSpec((t, LANES), lambda b, h, i, sl: (b * half + i, h)),
                   pl.BlockSpec((t, LANES), lambda b, h, i, sl: (b * half + half - 1 - i, h)),
                   pl.BlockSpec((batch, c, hd), blk)],
        scratch_shapes=[pltpu.VMEM((2, 2 * t, 2 * LANES), BF16),
                        pltpu.VMEM((2, 2 * t, LANES), F32),
                        pltpu.VMEM((2, 2 * t, 2 * DA_V_DIM), F32),
                        pltpu.VMEM((2, 2 * t, t), BF16),
                        pltpu.VMEM((t, t), F32),
                        pltpu.VMEM((2 * LANES, seq), BF16),
                        pltpu.VMEM((batch, c + 8, cw), F32),
                        pltpu.VMEM((batch * GDN_HEADS, GDN_DIM, GDN_DIM), F32)])
    return pl.pallas_call(
        functools.partial(_mixer_kernel, t=t, nq=nq, batch=batch, lambda_init=lambda_init),
        grid_spec=grid_spec,
        out_shape=[jax.ShapeDtypeStruct((batch * seq // 2, width), BF16)] * 2
        + [jax.ShapeDtypeStruct((batch, seq, hd), BF16)],
        compiler_params=_cparams(3),
    )(slopes, q, q, kt, _alibi_key_rows(seq), v, lq1, lk1, lq2, lk2, out_norm_a,
      qkv3, z3, ba3, conv_taps, alog_row, dt_row, out_norm_b)


def _gdn_phases(step, u_ref, z_ref, ba_ref, cw_ref, alog_ref, dt_ref, on_ref, o_ref, stage_ref, state_ref, *, batch):
    c = GDN_CHUNK
    hd = GDN_HEADS * GDN_DIM

    @pl.when(step == 0)
    def _init():
        state_ref[...] = jnp.zeros(state_ref.shape, F32)
        stage_ref[:, 0:8, :] = jnp.zeros((batch, 8, stage_ref.shape[2]), F32)

    yield

    row = lax.broadcasted_iota(I32, (c, c), 0)
    col = lax.broadcasted_iota(I32, (c, c), 1)
    tri = row >= col
    strict = row > col
    tril_ones = jnp.where(tri, 1.0, 0.0).astype(BF16)

    chains = []
    for b in range(batch):
        stage_ref[b, 8:8 + c, :] = u_ref[b]
        u = stage_ref[b]
        y = cw_ref[CONV_K - 1:CONV_K, :] * u
        for back in range(1, CONV_K):
            y = y + cw_ref[CONV_K - 1 - back:CONV_K - back, :] * pltpu.roll(u, back, 0)
        stage_ref[b, 0:8, :] = stage_ref[b, c:c + 8, :]
        y = _silu(y[8:8 + c, :])

        ba = ba_ref[b]
        beta_all = _sigmoid(ba)
        g_all = -jnp.exp(alog_ref[...]) * _softplus(ba + dt_ref[...])
        gcum_all = _mm_exact_lhs(tril_ones, g_all)
        gcum_t = gcum_all.T

        for h in range(GDN_HEADS):
            q = y[:, h * GDN_DIM:(h + 1) * GDN_DIM]
            k = y[:, hd + h * GDN_DIM:hd + (h + 1) * GDN_DIM]
            v = y[:, 2 * hd + h * GDN_DIM:2 * hd + (h + 1) * GDN_DIM]
            q = q * lax.rsqrt(jnp.sum(q * q, axis=-1, keepdims=True) + EPS) * (GDN_DIM ** -0.5)
            k = k * lax.rsqrt(jnp.sum(k * k, axis=-1, keepdims=True) + EPS)
            beta = beta_all[:, h:h + 1]
            gc = gcum_all[:, GDN_HEADS + h:GDN_HEADS + h + 1]
            gr = gcum_t[GDN_HEADS + h:GDN_HEADS + h + 1, :]
            g_last = gc[c - 1:c, :]
            decay = jnp.where(tri, jnp.exp(jnp.where(tri, gc - gr, 0.0)), 0.0)
            e_gc = jnp.exp(gc)
            chains.append(dict(
                b=b, h=h, decay=decay, g_last=g_last,
                p=jnp.where(strict, -(beta * _mm_nt(k, k) * decay), 0.0),
                rhs=jnp.concatenate([v * beta, k * (beta * e_gc)], axis=1),
                qk=_mm_nt(q, k) * decay, q_dec=q * e_gc, k_dec_t=(k * jnp.exp(g_last - gc)).T))
        yield

    def same_block(size):
        return (row // size) == (col // size)

    for ch in chains:
        ch["pk"] = jnp.where(same_block(GDN_BASE), ch["p"], 0.0)
        ch["x"] = ch["pk"]
    for _ in range(int(math.log2(GDN_BASE)) - 1):
        for ch in chains:
            ch["pk"] = _mm(ch["pk"], ch["pk"])
        for ch in chains:
            ch["x"] = ch["x"] + ch["pk"] + _mm(ch["x"], ch["pk"])
        yield
    size = GDN_BASE
    while size < c:
        pair_only = same_block(2 * size) & jnp.logical_not(same_block(size))
        for ch in chains:
            e = jnp.where(pair_only, -ch["p"], 0.0)
            ch["y"] = e + _mm(ch["x"], e)
        for ch in chains:
            ch["x"] = ch["x"] - (ch["y"] + _mm(ch["y"], ch["x"]))
        size *= 2
        yield

    for ch in chains:
        ch["s_prev"] = state_ref[ch["b"] * GDN_HEADS + ch["h"]]
        ch["o_inter"] = _mm(ch["q_dec"], ch["s_prev"])
        ch["sol"] = ch["rhs"] + _mm(ch["x"], ch["rhs"])
    yield
    for ch in chains:
        ch["v_new"] = ch["sol"][:, :GDN_DIM] - _mm(ch["sol"][:, GDN_DIM:], ch["s_prev"])
    yield
    for ch in chains:
        ch["o"] = ch["o_inter"] + _mm(ch["qk"], ch["v_new"])
        ch["s_new"] = ch["s_prev"] * jnp.exp(ch["g_last"]) + _mm(ch["k_dec_t"], ch["v_new"])
    yield
    outs = [[None] * GDN_HEADS for _ in range(batch)]
    for ch in chains:
        zg = z_ref[ch["b"], :, ch["h"] * GDN_DIM:(ch["h"] + 1) * GDN_DIM]
        outs[ch["b"]][ch["h"]] = (_rms(ch["o"], on_ref[...]) * _silu(zg)).astype(o_ref.dtype)
    o_ref[...] = jnp.stack([jnp.concatenate(heads, axis=1) for heads in outs], axis=0)
    state_ref[...] = jnp.stack([ch["s_new"] for ch in chains], axis=0)


def _merge_kernel(x_ref, mod_ref, g1_ref, g2_ref, oa_lo_ref, oa_hi_ref, ob_ref, wgate_ref, wa_ref, wb_ref, wout_ref,
                  wrt_ref, brt_ref, x1_ref, h2_ref, ri_ref, rf_ref, cnt_ref, carry_ref, *, tiles_per_batch):
    @pl.when(pl.program_id(0) == 0)
    def _init():
        carry_ref[...] = jnp.zeros(carry_ref.shape, F32)

    d = x_ref.shape[1]
    mod = mod_ref[0]
    shift1, scale1, gate1 = mod[0:1, :], mod[1:2, :], mod[2:3, :]
    shift2, scale2 = mod[3:4, :], mod[4:5, :]
    first_half = (pl.program_id(0) % tiles_per_batch) < tiles_per_batch // 2
    tm = x_ref.shape[0] // MERGE_SPLIT
    pieces = [dict(rows=slice(s * tm, (s + 1) * tm)) for s in range(MERGE_SPLIT)]
    for pc in pieces:
        pc["x"] = x_ref[pc["rows"], :]
        hb = (_rms(pc["x"], g1_ref[...]) * (1.0 + scale1) + shift1).astype(BF16)
        pc["gates"] = _sigmoid(jnp.dot(hb, wgate_ref[...], preferred_element_type=F32))
    for pc in pieces:
        oa = jnp.where(first_half, oa_lo_ref[pc["rows"], :], oa_hi_ref[pc["rows"], :])
        pc["ya"] = jnp.dot(oa, wa_ref[...], preferred_element_type=F32)
        pc["yb"] = jnp.dot(ob_ref[pc["rows"], :], wb_ref[...], preferred_element_type=F32)
    for pc in pieces:
        pc["merged"] = pc["gates"][:, :d] * pc["ya"] + pc["gates"][:, d:] * pc["yb"]
    for pc in pieces:
        x1 = pc["x"] + gate1 * _mm(pc["merged"], wout_ref[...])
        x1_ref[pc["rows"], :] = x1
        pc["h2"] = _rms(x1, g2_ref[...]) * (1.0 + scale2) + shift2
        h2_ref[pc["rows"], :] = _pack_bf16_pairs(pc["h2"])
    for pc in pieces:
        pc["logits"] = _mm3_narrow(pc["h2"], wrt_ref[...]) + brt_ref[...]
    for pc in pieces:
        _route_tile(pc["logits"], ri_ref.at[:, pc["rows"]], rf_ref.at[:, pc["rows"]], cnt_ref, carry_ref)


def _merge(x2, mod3, g1, g2, oa_lo, oa_hi, ob, w_gate, w_a, w_b, w_out, w_rt, b_rt, tm, tiles_per_batch):
    n, d = x2.shape
    hp = tiles_per_batch // 2
    const = lambda i: (0, 0)
    row = lambda i: (i, 0)
    lo = lambda i: ((i // tiles_per_batch) * hp + jnp.minimum(i % tiles_per_batch, hp - 1), 0)
    hi = lambda i: ((i // tiles_per_batch) * hp + jnp.maximum(i % tiles_per_batch - hp, 0), 0)
    return pl.pallas_call(
        functools.partial(_merge_kernel, tiles_per_batch=tiles_per_batch),
        grid=(n // tm,),
        in_specs=[pl.BlockSpec((tm, d), row),
                  pl.BlockSpec((1, 6, d), lambda i: (i // tiles_per_batch, 0, 0)),
                  pl.BlockSpec((1, d), const), pl.BlockSpec((1, d), const),
                  pl.BlockSpec((tm, oa_lo.shape[1]), lo), pl.BlockSpec((tm, oa_hi.shape[1]), hi),
                  pl.BlockSpec((tm, ob.shape[1]), row),
                  pl.BlockSpec(w_gate.shape, const), pl.BlockSpec(w_a.shape, const),
                  pl.BlockSpec(w_b.shape, const), pl.BlockSpec(w_out.shape, const),
                  pl.BlockSpec(w_rt.shape, const), pl.BlockSpec((1, LANES), const)],
        out_specs=[pl.BlockSpec((tm, d), row), pl.BlockSpec((tm, d // 2), row),
                   pl.BlockSpec((8, tm), lambda i: (0, i)), pl.BlockSpec((8, tm), lambda i: (0, i)),
                   pl.BlockSpec((N_EXPERTS, LANES), const)],
        out_shape=[jax.ShapeDtypeStruct((n, d), F32), jax.ShapeDtypeStruct((n, d // 2), I32),
                   jax.ShapeDtypeStruct((8, n), I32), jax.ShapeDtypeStruct((8, n), F32),
                   jax.ShapeDtypeStruct((N_EXPERTS, LANES), F32)],
        scratch_shapes=[pltpu.VMEM((N_EXPERTS, LANES), F32)],
        compiler_params=_cparams(1),
    )(x2, mod3, g1, g2, oa_lo, oa_hi, ob, w_gate, w_a, w_b, w_out, w_rt, b_rt)


def _route_tile(lg, ri_ref, rf_ref, cnt_ref, carry_ref):
    tm = lg.shape[0]
    lgt = lg.T
    el_all = lgt[0:N_EXPERTS, :]
    gl_all = lgt[N_EXPERTS:N_EXPERTS + 8, :]
    er = lax.broadcasted_iota(I32, el_all.shape, 0)
    gr = lax.broadcasted_iota(I32, gl_all.shape, 0)
    big = jnp.int32(LANES)
    is_grp = gr < N_GROUPS
    gl = jnp.where(is_grp, gl_all, NEG_INF)
    gmax = jnp.max(gl, axis=0, keepdims=True)
    g_top = jnp.min(jnp.where(is_grp & (gl == gmax), gr, big), axis=0, keepdims=True)
    g_top_p = 1.0 / jnp.sum(jnp.where(is_grp, jnp.exp(gl - gmax), 0.0), axis=0, keepdims=True)
    in_grp = (er // EXPERTS_PER_GROUP) == g_top
    el = jnp.where(in_grp, el_all, NEG_INF)
    v1 = jnp.max(el, axis=0, keepdims=True)
    e1 = jnp.min(jnp.where(in_grp & (el == v1), er, big), axis=0, keepdims=True)
    rest = in_grp & (er != e1)
    el2 = jnp.where(rest, el_all, NEG_INF)
    v2 = jnp.max(el2, axis=0, keepdims=True)
    e2 = jnp.min(jnp.where(rest & (el2 == v2), er, big), axis=0, keepdims=True)
    ex = jnp.exp(v2 - v1)
    w1 = g_top_p / (1.0 + ex)
    w2 = g_top_p * ex / (1.0 + ex)

    oh1 = er == e1
    oh2 = er == e2
    ohs = (jnp.where(oh1, 1.0, 0.0) + jnp.where(oh2, 1.0, 0.0)).astype(BF16)
    r = lax.broadcasted_iota(I32, (tm, tm), 0)
    c = lax.broadcasted_iota(I32, (tm, tm), 1)
    earlier = jnp.where(r < c, 1.0, 0.0).astype(BF16)
    prior = jnp.dot(ohs, earlier, preferred_element_type=F32) + jnp.tile(carry_ref[...], (1, tm // LANES))
    rank1 = jnp.sum(jnp.where(oh1, prior, 0.0), axis=0, keepdims=True)
    rank2 = jnp.sum(jnp.where(oh2, prior, 0.0), axis=0, keepdims=True)
    carry_ref[...] = carry_ref[...] + jnp.sum(ohs.astype(F32), axis=1, keepdims=True)
    cnt_ref[...] = carry_ref[...]

    row = lax.broadcasted_iota(I32, (8, tm), 0)
    r1 = rank1.astype(I32)
    r2 = rank2.astype(I32)
    ri_ref[...] = jnp.where(row == 0, e1, jnp.where(row == 1, e2, jnp.where(row == 2, r1, jnp.where(row == 3, r2, 0))))
    rf_ref[...] = jnp.where(row == 0, w1, jnp.where(row == 1, w2, 0.0))


def _moe_kernel(be_ref, act_ref, f1_ref, f3_ref, f2_ref, xb_ref, w1_ref, w3_ref, w2_ref, o_ref,
                w1b_ref, w3b_ref, w2b_ref):
    i = pl.program_id(0)
    prev = be_ref[jnp.maximum(i - 1, 0)]

    @pl.when((i == 0) | (be_ref[i] != prev))
    def _cast_weights():
        w1b_ref[...] = w1_ref[...].astype(BF16)
        w3b_ref[...] = w3_ref[...].astype(BF16)
        w2b_ref[...] = w2_ref[...].astype(BF16)

    valid = act_ref[i]

    def expert_mlp(n_rows):
        xp = xb_ref[0:n_rows, :]
        r = lax.broadcasted_iota(I32, xp.shape, 0)
        xb = _unpack_bf16_pairs(jnp.where(r < valid, xp, 0))
        a = jnp.dot(xb, w1b_ref[...], preferred_element_type=F32)
        g = jnp.dot(xb, w3b_ref[...], preferred_element_type=F32)
        hid = (_silu(a) * g).astype(BF16)
        o_ref[0:n_rows, :] = _pack_bf16_pairs(jnp.dot(hid, w2b_ref[...], preferred_element_type=F32))

    @pl.when(valid > MOE_SUB)
    def _full():
        expert_mlp(MOE_ROWS)

    @pl.when((valid > 0) & (valid <= MOE_SUB))
    def _partial():
        expert_mlp(MOE_SUB)
        o_ref[MOE_SUB:MOE_ROWS, :] = jnp.zeros((MOE_ROWS - MOE_SUB, o_ref.shape[1]), o_ref.dtype)

    @pl.when(valid == 0)
    def _idle():
        o_ref[...] = jnp.zeros(o_ref.shape, o_ref.dtype)


def _moe(blk_expert, blk_rows, fetch, xb, w1, w3, w2):
    cap = xb.shape[0]
    d = w1.shape[1]
    nb = cap // MOE_ROWS
    de = w1.shape[2]
    grid_spec = pltpu.PrefetchScalarGridSpec(
        num_scalar_prefetch=5,
        grid=(nb,),
        in_specs=[pl.BlockSpec((MOE_ROWS, d // 2), lambda i, be, act, f1, f3, f2: (i, 0)),
                  pl.BlockSpec((None, d, de), lambda i, be, act, f1, f3, f2: (f1[i], 0, 0)),
                  pl.BlockSpec((None, d, de), lambda i, be, act, f1, f3, f2: (f3[i], 0, 0)),
                  pl.BlockSpec((None, de, d), lambda i, be, act, f1, f3, f2: (f2[i], 0, 0))],
        out_specs=pl.BlockSpec((MOE_ROWS, d // 2), lambda i, be, act, f1, f3, f2: (i, 0)),
        scratch_shapes=[pltpu.VMEM((d, de), BF16), pltpu.VMEM((d, de), BF16), pltpu.VMEM((de, d), BF16)])
    return pl.pallas_call(
        _moe_kernel,
        grid_spec=grid_spec,
        out_shape=jax.ShapeDtypeStruct((cap, d // 2), I32),
        compiler_params=_cparams(1),
    )(blk_expert, blk_rows, *fetch, xb, w1, w3, w2)


def _dest_kernel(ri_ref, pstart_ref, o_ref):
    ri = ri_ref[...]
    tm = ri.shape[1]
    er = lax.broadcasted_iota(I32, (N_EXPERTS, tm), 0)
    pstart = jnp.tile(pstart_ref[...], (1, tm // LANES))
    base1 = jnp.sum(jnp.where(er == ri[0:1, :], pstart, 0), axis=0, keepdims=True)
    base2 = jnp.sum(jnp.where(er == ri[1:2, :], pstart, 0), axis=0, keepdims=True)
    row = lax.broadcasted_iota(I32, ri.shape, 0)
    o_ref[...] = jnp.where(row == 0, base1 + ri[2:3, :], jnp.where(row == 1, base2 + ri[3:4, :], 0))


def _dest(ri, pstart_col, tm):
    n = ri.shape[1]
    return pl.pallas_call(
        _dest_kernel,
        grid=(n // tm,),
        in_specs=[pl.BlockSpec((8, tm), lambda i: (0, i)), pl.BlockSpec((N_EXPERTS, LANES), lambda i: (0, 0))],
        out_specs=pl.BlockSpec((8, tm), lambda i: (0, i)),
        out_shape=jax.ShapeDtypeStruct((8, n), I32),
        compiler_params=_cparams(1),
    )(ri, pstart_col)


def _sc_mesh():
    return plsc.VectorSubcoreMesh(core_axis_name="core", subcore_axis_name="subcore")


def _sc_scatter_rows(x, idx, cap):
    n, d = x.shape
    windows = idx.shape[0]
    tiles = n // SC_WINDOW
    mesh = _sc_mesh()
    workers = mesh.num_cores * mesh.num_subcores
    per_worker = windows // workers

    @functools.partial(pl.kernel, out_type=jax.ShapeDtypeStruct((cap, d), x.dtype), mesh=mesh,
                       scratch_types=[pltpu.VMEM((1, SC_WINDOW), I32), pltpu.VMEM((SC_WINDOW, d), x.dtype)])
    def scatter(x_hbm, i_hbm, o_hbm, idx_v, rows_v):
        wid = lax.axis_index("subcore") * mesh.num_cores + lax.axis_index("core")

        @pl.loop(0, per_worker)
        def _(j):
            w = wid * per_worker + j
            pltpu.sync_copy(i_hbm.at[pl.ds(w, 1)], idx_v)
            pltpu.sync_copy(x_hbm.at[pl.ds(lax.rem(w, tiles) * SC_WINDOW, SC_WINDOW)], rows_v)
            pltpu.sync_copy(rows_v, o_hbm.at[idx_v.at[0]])

    return scatter(x, idx)


def _sc_gather_rows(table, idx):
    d = table.shape[1]
    windows = idx.shape[0]
    mesh = _sc_mesh()
    workers = mesh.num_cores * mesh.num_subcores
    per_worker = windows // workers

    @functools.partial(pl.kernel, out_type=jax.ShapeDtypeStruct((windows * SC_WINDOW, d), table.dtype), mesh=mesh,
                       scratch_types=[pltpu.VMEM((1, SC_WINDOW), I32), pltpu.VMEM((SC_WINDOW, d), table.dtype)])
    def gather(t_hbm, i_hbm, o_hbm, idx_v, rows_v):
        wid = lax.axis_index("subcore") * mesh.num_cores + lax.axis_index("core")

        @pl.loop(0, per_worker)
        def _(j):
            w = wid * per_worker + j
            pltpu.sync_copy(i_hbm.at[pl.ds(w, 1)], idx_v)
            pltpu.sync_copy(t_hbm.at[idx_v.at[0]], rows_v)
            pltpu.sync_copy(rows_v, o_hbm.at[pl.ds(w * SC_WINDOW, SC_WINDOW)])

    return gather(table, idx)


def _combine_kernel(x1_ref, mod_ref, rf_ref, ya_ref, yb_ref, o_ref):
    gate2 = mod_ref[0][5:6, :]
    rf = rf_ref[...].T
    ya = _unpack_bf16_pairs(ya_ref[...]).astype(F32)
    yb = _unpack_bf16_pairs(yb_ref[...]).astype(F32)
    y = rf[:, 0:1] * ya + rf[:, 1:2] * yb
    o_ref[...] = x1_ref[...] + gate2 * y


def _combine(x1, mod3, rf, y12, tm, tiles_per_batch):
    n, d = x1.shape
    tiles = n // tm
    row = lambda i: (i, 0)
    return pl.pallas_call(
        _combine_kernel,
        grid=(tiles,),
        in_specs=[pl.BlockSpec((tm, d), row),
                  pl.BlockSpec((1, 6, d), lambda i: (i // tiles_per_batch, 0, 0)),
                  pl.BlockSpec((8, tm), lambda i: (0, i)),
                  pl.BlockSpec((tm, d // 2), row), pl.BlockSpec((tm, d // 2), lambda i: (i + tiles, 0))],
        out_specs=pl.BlockSpec((tm, d), row),
        out_shape=jax.ShapeDtypeStruct((n, d), F32),
        compiler_params=_cparams(1),
    )(x1, mod3, rf, y12, y12)


def _pad_lanes(a, offset=0):
    return jnp.pad(a, ((0, 0), (offset, LANES - offset - a.shape[1])))


def _layer(x, c, layer, w_ada, b_ada, norm1_gain, w_in, da_q_norm, da_k_norm, lq1, lk1, lq2, lk2, da_out_norm,
           gdn_conv, gdn_a_log, gdn_dt_bias, gdn_out_norm, w_branch_a, w_branch_b, w_out, norm2_gain,
           w_group, b_group, w_router, b_router, w1, w3, w2):
    batch, seq, d = x.shape
    n = batch * seq
    tm = min(512, seq)
    tiles_per_batch = seq // tm
    t_attn = min(512, seq)
    lambda_init = 0.8 - 0.6 * math.exp(-0.3 * layer)

    mod = _ada(jnp.pad(c, ((0, 8 - batch), (0, 0))), w_ada, b_ada)[:batch]
    mod3 = mod.reshape(batch, 6, d)

    da_w = DA_HEADS * 2 * DA_QK_DIM
    dv_w = DA_HEADS * DA_V_DIM
    gd_w = GDN_HEADS * GDN_DIM
    o0 = 2 * da_w + dv_w
    o1 = o0 + 4 * gd_w
    o2 = o1 + 2 * GDN_HEADS
    w_da = w_in[:, :o0].astype(BF16)
    w_g = w_in[:, o0:o1].astype(BF16)
    w_ba = _pad_lanes(w_in[:, o1:o2]).astype(BF16)
    w_gate = w_in[:, o2:].astype(BF16)
    qn = jnp.tile(da_q_norm, 2 * DA_HEADS).reshape(1, da_w)
    kn = jnp.tile(da_k_norm, 2 * DA_HEADS).reshape(1, da_w)

    x2 = x.reshape(n, d)
    g1 = norm1_gain.reshape(1, d)
    q, kt, v, gqkv, z, ba = _proj(x2, mod3, g1, w_da, w_g, w_ba, qn, kn, tm, tiles_per_batch)

    assert t_attn == tm
    alog_row = _pad_lanes(gdn_a_log.reshape(1, -1), GDN_HEADS)
    dt_row = _pad_lanes(gdn_dt_bias.reshape(1, -1), GDN_HEADS)
    oa_lo, oa_hi, o_b = _mixer(q, kt, v, lq1.reshape(1, -1), lk1.reshape(1, -1), lq2.reshape(1, -1),
                               lk2.reshape(1, -1), da_out_norm.reshape(1, -1),
                               gqkv.reshape(batch, seq, -1), z.reshape(batch, seq, -1), ba.reshape(batch, seq, -1),
                               gdn_conv, alog_row, dt_row, gdn_out_norm.reshape(1, -1),
                               batch, seq, t_attn, lambda_init)
    o_b = o_b.reshape(n, gd_w)

    w_rt = _pad_lanes(jnp.concatenate([w_router, w_group], axis=1))
    b_rt = _pad_lanes(jnp.concatenate([b_router, b_group]).reshape(1, -1))
    x1, h2, ri, rf, cnt = _merge(x2, mod3, g1, norm2_gain.reshape(1, d), oa_lo, oa_hi, o_b, w_gate,
                                 w_branch_a.astype(BF16), w_branch_b.astype(BF16), w_out.astype(BF16),
                                 w_rt, b_rt, tm * MERGE_SPLIT, tiles_per_batch // MERGE_SPLIT)

    counts = cnt[:, 0].astype(I32)
    padded = (counts + MOE_ROWS - 1) // MOE_ROWS * MOE_ROWS
    pend = jnp.cumsum(padded)
    pstart = pend - padded
    cap = 2 * n + N_EXPERTS * MOE_ROWS
    nb = cap // MOE_ROWS
    blk_start = jnp.arange(nb, dtype=I32) * MOE_ROWS
    blk_expert = jnp.minimum(jnp.sum(pend[None, :] <= blk_start[:, None], axis=1), N_EXPERTS - 1).astype(I32)
    blk_rows = jnp.clip((pstart + counts)[blk_expert] - blk_start, 0, MOE_ROWS).astype(I32)
    blk_rows = jnp.where(blk_start < pend[-1], blk_rows, 0)
    first_blk = pstart // MOE_ROWS
    n_blk = padded // MOE_ROWS
    pos = jnp.arange(nb, dtype=I32) - first_blk[blk_expert]
    nxt = blk_expert[jnp.clip(first_blk[blk_expert] + n_blk[blk_expert], 0, nb - 1)]
    fetch = [jnp.where(pos < jnp.minimum(lag, n_blk[blk_expert]), blk_expert, nxt).astype(I32) for lag in (1, 2, 3)]
    pstart_col = jnp.broadcast_to(pstart[:, None], (N_EXPERTS, LANES))
    dest = _dest(ri, pstart_col, min(4 * tm, n))
    idx = dest[0:2].reshape(2 * n // SC_WINDOW, SC_WINDOW)

    xb = _sc_scatter_rows(h2, idx, cap)
    yb = _moe(blk_expert, blk_rows, fetch, xb, w1, w3, w2)
    y12 = _sc_gather_rows(yb, idx)
    out = _combine(x1, mod3, rf, y12, 2 * tm, tiles_per_batch // 2)
    return out.reshape(batch, seq, d)


def kernel(x, c, w_ada, b_ada, norm1_gain, w_in, da_q_norm, da_k_norm, da_lambda_q1, da_lambda_k1, da_lambda_q2,
           da_lambda_k2, da_out_norm, gdn_conv, gdn_a_log, gdn_dt_bias, gdn_out_norm, w_branch_a, w_branch_b,
           w_out, norm2_gain, w_group, b_group, w_router, b_router, w1, w3, w2):
    for layer in range(w_ada.shape[0]):
        x = _layer(x, c, layer, w_ada[layer], b_ada[layer], norm1_gain[layer], w_in[layer], da_q_norm[layer],
                   da_k_norm[layer], da_lambda_q1[layer], da_lambda_k1[layer], da_lambda_q2[layer],
                   da_lambda_k2[layer], da_out_norm[layer], gdn_conv[layer], gdn_a_log[layer], gdn_dt_bias[layer],
                   gdn_out_norm[layer], w_branch_a[layer], w_branch_b[layer], w_out[layer], norm2_gain[layer],
                   w_group[layer], b_group[layer], w_router[layer], b_router[layer], w1[layer], w3[layer], w2[layer])
    return x
```

```python
import functools
import math

import jax
import jax.numpy as jnp
import numpy as np
from jax import lax
from jax.experimental import pallas as pl
from jax.experimental.pallas import tpu as pltpu
from jax.experimental.pallas import tpu_sc as plsc

F32 = jnp.float32
BF16 = jnp.bfloat16
I32 = jnp.int32

EPS = 1e-6
NEG_INF = -1e30
MASK_CHUNK = 64
LOG2E = 1.4426950408889634
ALIBI_ROWS = 3

DA_HEADS = 4
DA_QK_DIM = 64
DA_V_DIM = 128
GDN_HEADS = 4
GDN_DIM = 128
CONV_K = 4
N_GROUPS = 4
EXPERTS_PER_GROUP = 8
N_EXPERTS = N_GROUPS * EXPERTS_PER_GROUP
D_EXPERT = 512

LANES = 128
VMEM_LIMIT = 56 * 1024 * 1024

MERGE_SPLIT = 2
GDN_CHUNK = 128
GDN_BASE = 16
MOE_ROWS = 1024
MOE_PIECE = 512
MOE_SUB = 256
SC_WINDOW = 128


def _cparams(n_axes):
    return pltpu.CompilerParams(dimension_semantics=("arbitrary",) * n_axes,
                                vmem_limit_bytes=VMEM_LIMIT)


def _mm(a, b):
    return jnp.dot(a.astype(BF16), b.astype(BF16), preferred_element_type=F32)


def _mm_nt(a, b):
    return lax.dot_general(a.astype(BF16), b.astype(BF16), (((1,), (1,)), ((), ())),
                           preferred_element_type=F32)


def _split2(a):
    hi = a.astype(BF16)
    lo = (a - hi.astype(F32)).astype(BF16)
    return hi, lo


def _mm3(a, b):
    ah, al = _split2(a)
    bh, bl = _split2(b)
    out = jnp.dot(ah, bh, preferred_element_type=F32)
    out = out + jnp.dot(ah, bl, preferred_element_type=F32)
    out = out + jnp.dot(al, bh, preferred_element_type=F32)
    return out


def _mm3_narrow(a, b):
    ah, al = _split2(a)
    bh, bl = _split2(b)
    n = b.shape[1]
    both = jnp.dot(ah, jnp.concatenate([bh, bl], axis=1), preferred_element_type=F32)
    return both[:, :n] + both[:, n:] + jnp.dot(al, bh, preferred_element_type=F32)


def _mm_exact_lhs(a_bf16, b):
    b1 = b.astype(BF16)
    r1 = b - b1.astype(F32)
    b2 = r1.astype(BF16)
    b3 = (r1 - b2.astype(F32)).astype(BF16)
    out = jnp.dot(a_bf16, b1, preferred_element_type=F32)
    out = out + jnp.dot(a_bf16, b2, preferred_element_type=F32)
    out = out + jnp.dot(a_bf16, b3, preferred_element_type=F32)
    return out


def _sigmoid(x):
    return 1.0 / (1.0 + jnp.exp(-x))


def _silu(x):
    return x * _sigmoid(x)


def _softplus(x):
    return jnp.maximum(x, 0.0) + jnp.log(1.0 + jnp.exp(-jnp.abs(x)))


def _pack_bf16_pairs(x):
    w = x.shape[1] // 2
    bits = lax.bitcast_convert_type(x.astype(BF16).astype(F32), jnp.uint32)
    lo = lax.shift_right_logical(bits[:, :w], jnp.uint32(16))
    hi = bits[:, w:] & jnp.uint32(0xFFFF0000)
    return lax.bitcast_convert_type(hi | lo, I32)


def _unpack_bf16_pairs(p):
    bits = lax.bitcast_convert_type(p, jnp.uint32)
    lo = lax.bitcast_convert_type(lax.shift_left(bits, jnp.uint32(16)), F32)
    hi = lax.bitcast_convert_type(bits & jnp.uint32(0xFFFF0000), F32)
    return jnp.concatenate([lo, hi], axis=1).astype(BF16)


def _rms(x, gain):
    return x * lax.rsqrt(jnp.mean(x * x, axis=-1, keepdims=True) + EPS) * gain


def _ada_kernel(c_ref, w_ref, b_ref, o_ref):
    sc = _silu(c_ref[...])
    o_ref[...] = _mm3(sc, w_ref[...]) + b_ref[...]


def _ada(c_pad, w_ada, b_ada):
    rows, d = c_pad.shape
    n = w_ada.shape[1]
    tn = d
    return pl.pallas_call(
        _ada_kernel,
        grid=(n // tn,),
        in_specs=[pl.BlockSpec((rows, d), lambda j: (0, 0)),
                  pl.BlockSpec((d, tn), lambda j: (0, j)),
                  pl.BlockSpec((1, tn), lambda j: (0, j))],
        out_specs=pl.BlockSpec((rows, tn), lambda j: (0, j)),
        out_shape=jax.ShapeDtypeStruct((rows, n), F32),
        compiler_params=_cparams(1),
    )(c_pad, w_ada, b_ada.reshape(1, n))


def _group_rms64(x, gain):
    tm, width = x.shape
    lane = lax.broadcasted_iota(I32, (tm, LANES), 1)
    low = lane < DA_QK_DIM
    parts = []
    for j in range(width // LANES):
        blk = x[:, j * LANES:(j + 1) * LANES]
        sq = blk * blk
        s_lo = jnp.sum(jnp.where(low, sq, 0.0), axis=-1, keepdims=True)
        s_hi = jnp.sum(jnp.where(low, 0.0, sq), axis=-1, keepdims=True)
        ms = jnp.where(low, s_lo, s_hi) * (1.0 / DA_QK_DIM)
        parts.append(blk * lax.rsqrt(ms + EPS))
    return jnp.concatenate(parts, axis=-1) * gain


def _proj_kernel(x_ref, mod_ref, g1_ref, wda_ref, wg_ref, wba_ref, qn_ref, kn_ref,
                 q_ref, k_ref, v_ref, gqkv_ref, z_ref, ba_ref):
    x = x_ref[...]
    mod = mod_ref[0]
    shift, scale = mod[0:1, :], mod[1:2, :]
    hb = (_rms(x, g1_ref[...]) * (1.0 + scale) + shift).astype(BF16)
    da_w = DA_HEADS * 2 * DA_QK_DIM
    da = jnp.dot(hb, wda_ref[...], preferred_element_type=F32)
    q = _group_rms64(da[:, :da_w], qn_ref[...]) * (DA_QK_DIM ** -0.5 * LOG2E)
    k = _group_rms64(da[:, da_w:2 * da_w], kn_ref[...])
    q_ref[...] = q.astype(BF16)
    k_ref[...] = k.T.astype(BF16)
    v_ref[...] = da[:, 2 * da_w:].astype(BF16)
    g = jnp.dot(hb, wg_ref[...], preferred_element_type=F32)
    conv_w = 3 * GDN_HEADS * GDN_DIM
    gqkv_ref[...] = g[:, :conv_w]
    z_ref[...] = g[:, conv_w:]
    ba_ref[...] = jnp.dot(hb, wba_ref[...], preferred_element_type=F32)


def _proj(x2, mod3, g1, w_da, w_g, w_ba, qn, kn, tm, tiles_per_batch):
    n, d = x2.shape
    da_w = DA_HEADS * 2 * DA_QK_DIM
    dv_w = DA_HEADS * DA_V_DIM
    conv_w = 3 * GDN_HEADS * GDN_DIM
    z_w = GDN_HEADS * GDN_DIM
    const = lambda i: (0, 0)
    row = lambda i: (i, 0)
    return pl.pallas_call(
        _proj_kernel,
        grid=(n // tm,),
        in_specs=[pl.BlockSpec((tm, d), row),
                  pl.BlockSpec((1, 6, d), lambda i: (i // tiles_per_batch, 0, 0)),
                  pl.BlockSpec((1, d), const),
                  pl.BlockSpec(w_da.shape, const),
                  pl.BlockSpec(w_g.shape, const),
                  pl.BlockSpec(w_ba.shape, const),
                  pl.BlockSpec((1, da_w), const),
                  pl.BlockSpec((1, da_w), const)],
        out_specs=[pl.BlockSpec((tm, da_w), row), pl.BlockSpec((da_w, tm), lambda i: (0, i)),
                   pl.BlockSpec((tm, dv_w), row), pl.BlockSpec((tm, conv_w), row),
                   pl.BlockSpec((tm, z_w), row), pl.BlockSpec((tm, LANES), row)],
        out_shape=[jax.ShapeDtypeStruct((n, da_w), BF16), jax.ShapeDtypeStruct((da_w, n), BF16),
                   jax.ShapeDtypeStruct((n, dv_w), BF16), jax.ShapeDtypeStruct((n, conv_w), F32),
                   jax.ShapeDtypeStruct((n, z_w), F32), jax.ShapeDtypeStruct((n, LANES), F32)],
        compiler_params=_cparams(1),
    )(x2, mod3, g1, w_da, w_g, w_ba, qn, kn)


def _attn_step(h, i, slopes_ref, qa_ref, qb_ref, kt_ref, al_ref, v_ref, lq1_ref, lk1_ref, lq2_ref, lk2_ref, on_ref,
               oa_ref, ob_ref, qs_ref, m_ref, acc_ref, p_ref, ddiag_ref, kta_ref, *, t, nq, lambda_init,
               between_blocks):
    slope2 = slopes_ref[h]
    reps = t // LANES

    @pl.when(i == 0)
    def _tables():
        r = lax.broadcasted_iota(I32, (t, t), 0)
        c = lax.broadcasted_iota(I32, (t, t), 1)
        ahead = jnp.maximum(c - r, 0).astype(F32)
        allowed = (c // MASK_CHUNK) <= (r // MASK_CHUNK)
        ddiag_ref[...] = jnp.where(allowed, (-2.0 * slope2) * ahead, NEG_INF)
        kta_ref[0:LANES, :] = kt_ref[...]
        kta_ref[LANES:2 * LANES, :] = al_ref[0]

    for tile, q_ref in enumerate((qa_ref, qb_ref)):
        q = q_ref[...]
        lane = lax.broadcasted_iota(I32, q.shape, 1)
        zero = jnp.zeros_like(q)
        ones = jnp.where(lane < ALIBI_ROWS, 1.0, 0.0).astype(BF16)
        qs_ref[tile, 0:t, :] = jnp.concatenate([jnp.where(lane < DA_QK_DIM, q, zero), ones], axis=1)
        qs_ref[tile, t:2 * t, :] = jnp.concatenate([jnp.where(lane < DA_QK_DIM, zero, q), ones], axis=1)
    m_ref[...] = jnp.full(m_ref.shape, NEG_INF, F32)
    acc_ref[...] = jnp.zeros(acc_ref.shape, F32)

    def scores(tile, j):
        start = pl.multiple_of(j * t, t)
        return jnp.dot(qs_ref[tile], kta_ref[:, pl.ds(start, t)], preferred_element_type=F32)

    def softmax_part(s, tile, slot, diagonal):
        alphas = []
        for half in range(2):
            rows = slice(half * t, (half + 1) * t)
            sb = s[rows] + ddiag_ref[...] if diagonal else s[rows]
            m_prev = m_ref[tile, rows]
            m_next = jnp.maximum(m_prev, jnp.max(sb, axis=-1, keepdims=True))
            alphas.append(jnp.exp2(m_prev - m_next))
            p_ref[slot, rows] = jnp.exp2(sb - jnp.tile(m_next, (1, reps))).astype(BF16)
            m_ref[tile, rows] = m_next
        return alphas

    def value_matmul(j, slot):
        start = pl.multiple_of(j * t, t)
        v = v_ref[pl.ds(start, t), :]
        vext = jnp.concatenate([v, jnp.ones_like(v)], axis=1)
        return jnp.dot(p_ref[slot], vext, preferred_element_type=F32)

    def accumulate(tile, alphas, pv):
        for half in range(2):
            rows = slice(half * t, (half + 1) * t)
            acc_ref[tile, rows] = jnp.tile(alphas[half], (1, 2)) * acc_ref[tile, rows] + pv[rows]

    blocks = []
    for k in range(nq - 1):
        tile = jnp.where(k >= i, 1, 0)
        blocks.append((tile, k - tile * i, False))
    blocks.append((0, i, True))
    blocks.append((1, nq - 1 - i, True))
    s_next = scores(blocks[0][0], blocks[0][1])
    for k, (tile, j, diagonal) in enumerate(blocks):
        s_cur = s_next
        if k + 1 < len(blocks):
            s_next = scores(blocks[k + 1][0], blocks[k + 1][1])
        alphas = softmax_part(s_cur, tile, k % 2, diagonal)
        accumulate(tile, alphas, value_matmul(j, k % 2))
        between_blocks()

    lam =(jnp.exp(jnp.sum(lq1_ref[...] * lk1_ref[...], axis=-1, keepdims=True))
           - jnp.exp(jnp.sum(lq2_ref[...] * lk2_ref[...], axis=-1, keepdims=True)) + lambda_init)
    for tile, o_ref in enumerate((oa_ref, ob_ref)):
        acc = acc_ref[tile]
        o_all = acc[:, 0:DA_V_DIM] / acc[:, DA_V_DIM:2 * DA_V_DIM]
        o = o_all[0:t] - lam * o_all[t:2 * t]
        o = _rms(o, on_ref[...]) * (1.0 - lambda_init)
        o_ref[...] = o.astype(o_ref.dtype)


def _alibi_slopes_log2():
    return np.asarray([2.0 ** (-8.0 * (hh + 1) / DA_HEADS) * LOG2E for hh in range(DA_HEADS)], np.float32)


def _alibi_key_rows(seq):
    col = _alibi_slopes_log2()[:, None] * np.arange(seq, dtype=np.float32)[None, :]
    b1 = col.astype(BF16)
    r1 = col - b1.astype(np.float32)
    b2 = r1.astype(BF16)
    b3 = (r1 - b2.astype(np.float32)).astype(BF16)
    rows = np.zeros((DA_HEADS, LANES, seq), BF16)
    rows[:, 0], rows[:, 1], rows[:, 2] = b1, b2, b3
    return jnp.asarray(rows)


N_ATTN_IN = 10
N_GDN_IN = 7
N_ATTN_SCRATCH = 6


def _mixer_kernel(slopes_ref, *refs, t, nq, batch, lambda_init):
    attn_in = refs[:N_ATTN_IN]
    gdn_in = refs[N_ATTN_IN:N_ATTN_IN + N_GDN_IN]
    outs = refs[N_ATTN_IN + N_GDN_IN:N_ATTN_IN + N_GDN_IN + 3]
    scratch = refs[N_ATTN_IN + N_GDN_IN + 3:]
    b, h, i = pl.program_id(0), pl.program_id(1), pl.program_id(2)
    chunk = (b * DA_HEADS + h) * (nq // 2) + i
    gdn = _gdn_phases(chunk, *gdn_in, outs[2], *scratch[N_ATTN_SCRATCH:], batch=batch)
    next(gdn)
    _attn_step(h, i, slopes_ref, *attn_in, outs[0], outs[1], *scratch[:N_ATTN_SCRATCH],
               t=t, nq=nq, lambda_init=lambda_init, between_blocks=lambda: next(gdn, None))
    for _ in gdn:
        pass


def _mixer(q, kt, v, lq1, lk1, lq2, lk2, out_norm_a, qkv3, z3, ba3, conv_taps, alog_row, dt_row, out_norm_b,
           batch, seq, t, lambda_init):
    nq = seq // t
    half = nq // 2
    width = DA_HEADS * DA_V_DIM
    c = GDN_CHUNK
    hd = GDN_HEADS * GDN_DIM
    cw = qkv3.shape[2]
    assert seq // c == batch * DA_HEADS * half
    slopes = jnp.asarray(_alibi_slopes_log2())
    vec = lambda b, h, i, sl: (0, 0)
    blk = lambda b, h, i, sl: (0, (b * DA_HEADS + h) * half + i, 0)
    grid_spec = pltpu.PrefetchScalarGridSpec(
        num_scalar_prefetch=1,
        grid=(batch, DA_HEADS, half),
        in_specs=[pl.BlockSpec((t, LANES), lambda b, h, i, sl: (b * nq + i, h)),
                  pl.BlockSpec((t, LANES), lambda b, h, i, sl: (b * nq + nq - 1 - i, h)),
                  pl.BlockSpec((LANES, seq), lambda b, h, i, sl: (h, b)),
                  pl.BlockSpec((1, LANES, seq), lambda b, h, i, sl: (h, 0, 0)),
                  pl.BlockSpec((seq, LANES), lambda b, h, i, sl: (b, h)),
                  pl.BlockSpec((1, DA_QK_DIM), vec), pl.BlockSpec((1, DA_QK_DIM), vec),
                  pl.BlockSpec((1, DA_QK_DIM), vec), pl.BlockSpec((1, DA_QK_DIM), vec),
                  pl.BlockSpec((1, DA_V_DIM), vec),
                  pl.BlockSpec((batch, c, cw), blk),
                  pl.BlockSpec((batch, c, hd), blk),
                  pl.BlockSpec((batch, c, LANES), blk),
                  pl.BlockSpec((CONV_K, cw), vec),
                  pl.BlockSpec((1, LANES), vec),
                  pl.BlockSpec((1, LANES), vec),
                  pl.BlockSpec((1, GDN_DIM), vec)],
        out_specs=[pl.BlockSpec((t, LANES), lambda b, h, i, sl: (b * half + i, h)),
                   pl.BlockSpec((t, LANES), lambda b, h, i, sl: (b * half + half - 1 - i, h)),
                   pl.BlockSpec((batch, c, hd), blk)],
        scratch_shapes=[pltpu.VMEM((2, 2 * t, 2 * LANES), BF16),
                        pltpu.VMEM((2, 2 * t, LANES), F32),
                        pltpu.VMEM((2, 2 * t, 2 * DA_V_DIM), F32),
                        pltpu.VMEM((2, 2 * t, t), BF16),
                        pltpu.VMEM((t, t), F32),
                        pltpu.VMEM((2 * LANES, seq), BF16),
                        pltpu.VMEM((batch, c + 8, cw), F32),
                        pltpu.VMEM((batch * GDN_HEADS, GDN_DIM, GDN_DIM), F32)])
    return pl.pallas_call(
        functools.partial(_mixer_kernel, t=t, nq=nq, batch=batch, lambda_init=lambda_init),
        grid_spec=grid_spec,
        out_shape=[jax.ShapeDtypeStruct((batch * seq // 2, width), BF16)] * 2
        + [jax.ShapeDtypeStruct((batch, seq, hd), BF16)],
        compiler_params=_cparams(3),
    )(slopes, q, q, kt, _alibi_key_rows(seq), v, lq1, lk1, lq2, lk2, out_norm_a,
      qkv3, z3, ba3, conv_taps, alog_row, dt_row, out_norm_b)


def _gdn_phases(step, u_ref, z_ref, ba_ref, cw_ref, alog_ref, dt_ref, on_ref, o_ref, stage_ref, state_ref, *, batch):
    c = GDN_CHUNK
    hd = GDN_HEADS * GDN_DIM

    @pl.when(step == 0)
    def _init():
        state_ref[...] = jnp.zeros(state_ref.shape, F32)
        stage_ref[:, 0:8, :] = jnp.zeros((batch, 8, stage_ref.shape[2]), F32)

    yield

    row = lax.broadcasted_iota(I32, (c, c), 0)
    col = lax.broadcasted_iota(I32, (c, c), 1)
    tri = row >= col
    strict = row > col
    tril_ones = jnp.where(tri, 1.0, 0.0).astype(BF16)

    chains = []
    for b in range(batch):
        stage_ref[b, 8:8 + c, :] = u_ref[b]
        u = stage_ref[b]
        y = cw_ref[CONV_K - 1:CONV_K, :] * u
        for back in range(1, CONV_K):
            y = y + cw_ref[CONV_K - 1 - back:CONV_K - back, :] * pltpu.roll(u, back, 0)
        stage_ref[b, 0:8, :] = stage_ref[b, c:c + 8, :]
        y = _silu(y[8:8 + c, :])

        ba = ba_ref[b]
        beta_all = _sigmoid(ba)
        g_all = -jnp.exp(alog_ref[...]) * _softplus(ba + dt_ref[...])
        gcum_all = _mm_exact_lhs(tril_ones, g_all)
        gcum_t = gcum_all.T

        for h in range(GDN_HEADS):
            q = y[:, h * GDN_DIM:(h + 1) * GDN_DIM]
            k = y[:, hd + h * GDN_DIM:hd + (h + 1) * GDN_DIM]
            v = y[:, 2 * hd + h * GDN_DIM:2 * hd + (h + 1) * GDN_DIM]
            q = q * lax.rsqrt(jnp.sum(q * q, axis=-1, keepdims=True) + EPS) * (GDN_DIM ** -0.5)
            k = k * lax.rsqrt(jnp.sum(k * k, axis=-1, keepdims=True) + EPS)
            beta = beta_all[:, h:h + 1]
            gc = gcum_all[:, GDN_HEADS + h:GDN_HEADS + h + 1]
            gr = gcum_t[GDN_HEADS + h:GDN_HEADS + h + 1, :]
            g_last = gc[c - 1:c, :]
            decay = jnp.where(tri, jnp.exp(jnp.where(tri, gc - gr, 0.0)), 0.0)
            e_gc = jnp.exp(gc)
            chains.append(dict(
                b=b, h=h, decay=decay, g_last=g_last,
                p=jnp.where(strict, -(beta * _mm_nt(k, k) * decay), 0.0),
                rhs=jnp.concatenate([v * beta, k * (beta * e_gc)], axis=1),
                qk=_mm_nt(q, k) * decay, q_dec=q * e_gc, k_dec_t=(k * jnp.exp(g_last - gc)).T))
        yield

    def same_block(size):
        return (row // size) == (col // size)

    for ch in chains:
        ch["pk"] = jnp.where(same_block(GDN_BASE), ch["p"], 0.0)
        ch["x"] = ch["pk"]
    for _ in range(int(math.log2(GDN_BASE)) - 1):
        for ch in chains:
            ch["pk"] = _mm(ch["pk"], ch["pk"])
        for ch in chains:
            ch["x"] = ch["x"] + ch["pk"] + _mm(ch["x"], ch["pk"])
        yield
    size = GDN_BASE
    while size < c:
        pair_only = same_block(2 * size) & jnp.logical_not(same_block(size))
        for ch in chains:
            e = jnp.where(pair_only, -ch["p"], 0.0)
            ch["y"] = e + _mm(ch["x"], e)
        for ch in chains:
            ch["x"] = ch["x"] - (ch["y"] + _mm(ch["y"], ch["x"]))
        size *= 2
        yield

    for ch in chains:
        ch["s_prev"] = state_ref[ch["b"] * GDN_HEADS + ch["h"]]
        ch["o_inter"] = _mm(ch["q_dec"], ch["s_prev"])
        ch["sol"] = ch["rhs"] + _mm(ch["x"], ch["rhs"])
    yield
    for ch in chains:
        ch["v_new"] = ch["sol"][:, :GDN_DIM] - _mm(ch["sol"][:, GDN_DIM:], ch["s_prev"])
    yield
    for ch in chains:
        ch["o"] = ch["o_inter"] + _mm(ch["qk"], ch["v_new"])
        ch["s_new"] = ch["s_prev"] * jnp.exp(ch["g_last"]) + _mm(ch["k_dec_t"], ch["v_new"])
    yield
    outs = [[None] * GDN_HEADS for _ in range(batch)]
    for ch in chains:
        zg = z_ref[ch["b"], :, ch["h"] * GDN_DIM:(ch["h"] + 1) * GDN_DIM]
        outs[ch["b"]][ch["h"]] = (_rms(ch["o"], on_ref[...]) * _silu(zg)).astype(o_ref.dtype)
    o_ref[...] = jnp.stack([jnp.concatenate(heads, axis=1) for heads in outs], axis=0)
    state_ref[...] = jnp.stack([ch["s_new"] for ch in chains], axis=0)


def _merge_kernel(x_ref, mod_ref, g1_ref, g2_ref, oa_lo_ref, oa_hi_ref, ob_ref, wgate_ref, wa_ref, wb_ref, wout_ref,
                  wrt_ref, brt_ref, x1_ref, h2_ref, ri_ref, rf_ref, cnt_ref, carry_ref, *, tiles_per_batch):
    @pl.when(pl.program_id(0) == 0)
    def _init():
        carry_ref[...] = jnp.zeros(carry_ref.shape, F32)

    d = x_ref.shape[1]
    mod = mod_ref[0]
    shift1, scale1, gate1 = mod[0:1, :], mod[1:2, :], mod[2:3, :]
    shift2, scale2 = mod[3:4, :], mod[4:5, :]
    first_half = (pl.program_id(0) % tiles_per_batch) < tiles_per_batch // 2
    tm = x_ref.shape[0] // MERGE_SPLIT
    pieces = [dict(rows=slice(s * tm, (s + 1) * tm)) for s in range(MERGE_SPLIT)]
    for pc in pieces:
        pc["x"] = x_ref[pc["rows"], :]
        hb = (_rms(pc["x"], g1_ref[...]) * (1.0 + scale1) + shift1).astype(BF16)
        pc["gates"] = _sigmoid(jnp.dot(hb, wgate_ref[...], preferred_element_type=F32))
    for pc in pieces:
        oa = jnp.where(first_half, oa_lo_ref[pc["rows"], :], oa_hi_ref[pc["rows"], :])
        pc["ya"] = jnp.dot(oa, wa_ref[...], preferred_element_type=F32)
        pc["yb"] = jnp.dot(ob_ref[pc["rows"], :], wb_ref[...], preferred_element_type=F32)
    for pc in pieces:
        pc["merged"] = pc["gates"][:, :d] * pc["ya"] + pc["gates"][:, d:] * pc["yb"]
    for pc in pieces:
        x1 = pc["x"] + gate1 * _mm(pc["merged"], wout_ref[...])
        x1_ref[pc["rows"], :] = x1
        pc["h2"] = _rms(x1, g2_ref[...]) * (1.0 + scale2) + shift2
        h2_ref[pc["rows"], :] = _pack_bf16_pairs(pc["h2"])
    for pc in pieces:
        pc["logits"] = _mm3_narrow(pc["h2"], wrt_ref[...]) + brt_ref[...]
    for pc in pieces:
        _route_tile(pc["logits"], ri_ref.at[:, pc["rows"]], rf_ref.at[:, pc["rows"]], cnt_ref, carry_ref)


def _merge(x2, mod3, g1, g2, oa_lo, oa_hi, ob, w_gate, w_a, w_b, w_out, w_rt, b_rt, tm, tiles_per_batch):
    n, d = x2.shape
    hp = tiles_per_batch // 2
    const = lambda i: (0, 0)
    row = lambda i: (i, 0)
    lo = lambda i: ((i // tiles_per_batch) * hp + jnp.minimum(i % tiles_per_batch, hp - 1), 0)
    hi = lambda i: ((i // tiles_per_batch) * hp + jnp.maximum(i % tiles_per_batch - hp, 0), 0)
    return pl.pallas_call(
        functools.partial(_merge_kernel, tiles_per_batch=tiles_per_batch),
        grid=(n // tm,),
        in_specs=[pl.BlockSpec((tm, d), row),
                  pl.BlockSpec((1, 6, d), lambda i: (i // tiles_per_batch, 0, 0)),
                  pl.BlockSpec((1, d), const), pl.BlockSpec((1, d), const),
                  pl.BlockSpec((tm, oa_lo.shape[1]), lo), pl.BlockSpec((tm, oa_hi.shape[1]), hi),
                  pl.BlockSpec((tm, ob.shape[1]), row),
                  pl.BlockSpec(w_gate.shape, const), pl.BlockSpec(w_a.shape, const),
                  pl.BlockSpec(w_b.shape, const), pl.BlockSpec(w_out.shape, const),
                  pl.BlockSpec(w_rt.shape, const), pl.BlockSpec((1, LANES), const)],
        out_specs=[pl.BlockSpec((tm, d), row), pl.BlockSpec((tm, d // 2), row),
                   pl.BlockSpec((8, tm), lambda i: (0, i)), pl.BlockSpec((8, tm), lambda i: (0, i)),
                   pl.BlockSpec((N_EXPERTS, LANES), const)],
        out_shape=[jax.ShapeDtypeStruct((n, d), F32), jax.ShapeDtypeStruct((n, d // 2), I32),
                   jax.ShapeDtypeStruct((8, n), I32), jax.ShapeDtypeStruct((8, n), F32),
                   jax.ShapeDtypeStruct((N_EXPERTS, LANES), F32)],
        scratch_shapes=[pltpu.VMEM((N_EXPERTS, LANES), F32)],
        compiler_params=_cparams(1),
    )(x2, mod3, g1, g2, oa_lo, oa_hi, ob, w_gate, w_a, w_b, w_out, w_rt, b_rt)


def _route_tile(lg, ri_ref, rf_ref, cnt_ref, carry_ref):
    tm = lg.shape[0]
    lgt = lg.T
    el_all = lgt[0:N_EXPERTS, :]
    gl_all = lgt[N_EXPERTS:N_EXPERTS + 8, :]
    er = lax.broadcasted_iota(I32, el_all.shape, 0)
    gr = lax.broadcasted_iota(I32, gl_all.shape, 0)
    big = jnp.int32(LANES)
    is_grp = gr < N_GROUPS
    gl = jnp.where(is_grp, gl_all, NEG_INF)
    gmax = jnp.max(gl, axis=0, keepdims=True)
    g_top = jnp.min(jnp.where(is_grp & (gl == gmax), gr, big), axis=0, keepdims=True)
    g_top_p = 1.0 / jnp.sum(jnp.where(is_grp, jnp.exp(gl - gmax), 0.0), axis=0, keepdims=True)
    in_grp = (er // EXPERTS_PER_GROUP) == g_top
    el = jnp.where(in_grp, el_all, NEG_INF)
    v1 = jnp.max(el, axis=0, keepdims=True)
    e1 = jnp.min(jnp.where(in_grp & (el == v1), er, big), axis=0, keepdims=True)
    rest = in_grp & (er != e1)
    el2 = jnp.where(rest, el_all, NEG_INF)
    v2 = jnp.max(el2, axis=0, keepdims=True)
    e2 = jnp.min(jnp.where(rest & (el2 == v2), er, big), axis=0, keepdims=True)
    ex = jnp.exp(v2 - v1)
    w1 = g_top_p / (1.0 + ex)
    w2 = g_top_p * ex / (1.0 + ex)

    oh1 = er == e1
    oh2 = er == e2
    ohs = (jnp.where(oh1, 1.0, 0.0) + jnp.where(oh2, 1.0, 0.0)).astype(BF16)
    r = lax.broadcasted_iota(I32, (tm, tm), 0)
    c = lax.broadcasted_iota(I32, (tm, tm), 1)
    earlier = jnp.where(r < c, 1.0, 0.0).astype(BF16)
    prior = jnp.dot(ohs, earlier, preferred_element_type=F32) + jnp.tile(carry_ref[...], (1, tm // LANES))
    rank1 = jnp.sum(jnp.where(oh1, prior, 0.0), axis=0, keepdims=True)
    rank2 = jnp.sum(jnp.where(oh2, prior, 0.0), axis=0, keepdims=True)
    carry_ref[...] = carry_ref[...] + jnp.sum(ohs.astype(F32), axis=1, keepdims=True)
    cnt_ref[...] = carry_ref[...]

    row = lax.broadcasted_iota(I32, (8, tm), 0)
    r1 = rank1.astype(I32)
    r2 = rank2.astype(I32)
    ri_ref[...] = jnp.where(row == 0, e1, jnp.where(row == 1, e2, jnp.where(row == 2, r1, jnp.where(row == 3, r2, 0))))
    rf_ref[...] = jnp.where(row == 0, w1, jnp.where(row == 1, w2, 0.0))


def _moe_kernel(be_ref, act_ref, f1_ref, f3_ref, f2_ref, xb_ref, w1_ref, w3_ref, w2_ref, o_ref,
                w1b_ref, w3b_ref, w2b_ref):
    i = pl.program_id(0)
    prev = be_ref[jnp.maximum(i - 1, 0)]

    @pl.when((i == 0) | (be_ref[i] != prev))
    def _cast_weights():
        w1b_ref[...] = w1_ref[...].astype(BF16)
        w3b_ref[...] = w3_ref[...].astype(BF16)
        w2b_ref[...] = w2_ref[...].astype(BF16)

    def expert_mlp(first, n_rows, valid):
        xp = xb_ref[first:first + n_rows, :]
        r = lax.broadcasted_iota(I32, xp.shape, 0)
        xb = _unpack_bf16_pairs(jnp.where(r < valid, xp, 0))
        a = jnp.dot(xb, w1b_ref[...], preferred_element_type=F32)
        g = jnp.dot(xb, w3b_ref[...], preferred_element_type=F32)
        hid = (_silu(a) * g).astype(BF16)
        o_ref[first:first + n_rows, :] = _pack_bf16_pairs(jnp.dot(hid, w2b_ref[...], preferred_element_type=F32))

    for piece in range(MOE_ROWS // MOE_PIECE):
        first = piece * MOE_PIECE
        valid = act_ref[i] - first

        @pl.when(valid > MOE_SUB)
        def _full(first=first, valid=valid):
            expert_mlp(first, MOE_PIECE, valid)

        @pl.when((valid > 0) & (valid <= MOE_SUB))
        def _partial(first=first, valid=valid):
            expert_mlp(first, MOE_SUB, valid)
            o_ref[first + MOE_SUB:first + MOE_PIECE, :] = jnp.zeros((MOE_PIECE - MOE_SUB, o_ref.shape[1]), o_ref.dtype)

        @pl.when(valid <= 0)
        def _idle(first=first):
            o_ref[first:first + MOE_PIECE, :] = jnp.zeros((MOE_PIECE, o_ref.shape[1]), o_ref.dtype)


def _moe(blk_expert, blk_rows, fetch, xb, w1, w3, w2):
    cap = xb.shape[0]
    d = w1.shape[1]
    nb = cap // MOE_ROWS
    de = w1.shape[2]
    grid_spec = pltpu.PrefetchScalarGridSpec(
        num_scalar_prefetch=5,
        grid=(nb,),
        in_specs=[pl.BlockSpec((MOE_ROWS, d // 2), lambda i, be, act, f1, f3, f2: (i, 0)),
                  pl.BlockSpec((None, d, de), lambda i, be, act, f1, f3, f2: (f1[i], 0, 0)),
                  pl.BlockSpec((None, d, de), lambda i, be, act, f1, f3, f2: (f3[i], 0, 0)),
                  pl.BlockSpec((None, de, d), lambda i, be, act, f1, f3, f2: (f2[i], 0, 0))],
        out_specs=pl.BlockSpec((MOE_ROWS, d // 2), lambda i, be, act, f1, f3, f2: (i, 0)),
        scratch_shapes=[pltpu.VMEM((d, de), BF16), pltpu.VMEM((d, de), BF16), pltpu.VMEM((de, d), BF16)])
    return pl.pallas_call(
        _moe_kernel,
        grid_spec=grid_spec,
        out_shape=jax.ShapeDtypeStruct((cap, d // 2), I32),
        compiler_params=_cparams(1),
    )(blk_expert, blk_rows, *fetch, xb, w1, w3, w2)


def _dest_kernel(ri_ref, pstart_ref, o_ref):
    ri = ri_ref[...]
    tm = ri.shape[1]
    er = lax.broadcasted_iota(I32, (N_EXPERTS, tm), 0)
    pstart = jnp.tile(pstart_ref[...], (1, tm // LANES))
    base1 = jnp.sum(jnp.where(er == ri[0:1, :], pstart, 0), axis=0, keepdims=True)
    base2 = jnp.sum(jnp.where(er == ri[1:2, :], pstart, 0), axis=0, keepdims=True)
    row = lax.broadcasted_iota(I32, ri.shape, 0)
    o_ref[...] = jnp.where(row == 0, base1 + ri[2:3, :], jnp.where(row == 1, base2 + ri[3:4, :], 0))


def _dest(ri, pstart_col, tm):
    n = ri.shape[1]
    return pl.pallas_call(
        _dest_kernel,
        grid=(n // tm,),
        in_specs=[pl.BlockSpec((8, tm), lambda i: (0, i)), pl.BlockSpec((N_EXPERTS, LANES), lambda i: (0, 0))],
        out_specs=pl.BlockSpec((8, tm), lambda i: (0, i)),
        out_shape=jax.ShapeDtypeStruct((8, n), I32),
        compiler_params=_cparams(1),
    )(ri, pstart_col)


def _sc_mesh():
    return plsc.VectorSubcoreMesh(core_axis_name="core", subcore_axis_name="subcore")


def _sc_scatter_rows(x, idx, cap):
    n, d = x.shape
    windows = idx.shape[0]
    tiles = n // SC_WINDOW
    mesh = _sc_mesh()
    workers = mesh.num_cores * mesh.num_subcores
    per_worker = windows // workers

    @functools.partial(pl.kernel, out_type=jax.ShapeDtypeStruct((cap, d), x.dtype), mesh=mesh,
                       scratch_types=[pltpu.VMEM((1, SC_WINDOW), I32), pltpu.VMEM((SC_WINDOW, d), x.dtype)])
    def scatter(x_hbm, i_hbm, o_hbm, idx_v, rows_v):
        wid = lax.axis_index("subcore") * mesh.num_cores + lax.axis_index("core")

        @pl.loop(0, per_worker)
        def _(j):
            w = wid * per_worker + j
            pltpu.sync_copy(i_hbm.at[pl.ds(w, 1)], idx_v)
            pltpu.sync_copy(x_hbm.at[pl.ds(lax.rem(w, tiles) * SC_WINDOW, SC_WINDOW)], rows_v)
            pltpu.sync_copy(rows_v, o_hbm.at[idx_v.at[0]])

    return scatter(x, idx)


def _sc_gather_rows(table, idx):
    d = table.shape[1]
    windows = idx.shape[0]
    mesh = _sc_mesh()
    workers = mesh.num_cores * mesh.num_subcores
    per_worker = windows // workers

    @functools.partial(pl.kernel, out_type=jax.ShapeDtypeStruct((windows * SC_WINDOW, d), table.dtype), mesh=mesh,
                       scratch_types=[pltpu.VMEM((1, SC_WINDOW), I32), pltpu.VMEM((SC_WINDOW, d), table.dtype)])
    def gather(t_hbm, i_hbm, o_hbm, idx_v, rows_v):
        wid = lax.axis_index("subcore") * mesh.num_cores + lax.axis_index("core")

        @pl.loop(0, per_worker)
        def _(j):
            w = wid * per_worker + j
            pltpu.sync_copy(i_hbm.at[pl.ds(w, 1)], idx_v)
            pltpu.sync_copy(t_hbm.at[idx_v.at[0]], rows_v)
            pltpu.sync_copy(rows_v, o_hbm.at[pl.ds(w * SC_WINDOW, SC_WINDOW)])

    return gather(table, idx)


def _combine_kernel(x1_ref, mod_ref, rf_ref, ya_ref, yb_ref, o_ref):
    gate2 = mod_ref[0][5:6, :]
    rf = rf_ref[...].T
    ya = _unpack_bf16_pairs(ya_ref[...]).astype(F32)
    yb = _unpack_bf16_pairs(yb_ref[...]).astype(F32)
    y = rf[:, 0:1] * ya + rf[:, 1:2] * yb
    o_ref[...] = x1_ref[...] + gate2 * y


def _combine(x1, mod3, rf, y12, tm, tiles_per_batch):
    n, d = x1.shape
    tiles = n // tm
    row = lambda i: (i, 0)
    return pl.pallas_call(
        _combine_kernel,
        grid=(tiles,),
        in_specs=[pl.BlockSpec((tm, d), row),
                  pl.BlockSpec((1, 6, d), lambda i: (i // tiles_per_batch, 0, 0)),
                  pl.BlockSpec((8, tm), lambda i: (0, i)),
                  pl.BlockSpec((tm, d // 2), row), pl.BlockSpec((tm, d // 2), lambda i: (i + tiles, 0))],
        out_specs=pl.BlockSpec((tm, d), row),
        out_shape=jax.ShapeDtypeStruct((n, d), F32),
        compiler_params=_cparams(1),
    )(x1, mod3, rf, y12, y12)


def _pad_lanes(a, offset=0):
    return jnp.pad(a, ((0, 0), (offset, LANES - offset - a.shape[1])))


def _layer(x, c, layer, w_ada, b_ada, norm1_gain, w_in, da_q_norm, da_k_norm, lq1, lk1, lq2, lk2, da_out_norm,
           gdn_conv, gdn_a_log, gdn_dt_bias, gdn_out_norm, w_branch_a, w_branch_b, w_out, norm2_gain,
           w_group, b_group, w_router, b_router, w1, w3, w2):
    batch, seq, d = x.shape
    n = batch * seq
    tm = min(512, seq)
    tiles_per_batch = seq // tm
    t_attn = min(512, seq)
    lambda_init = 0.8 - 0.6 * math.exp(-0.3 * layer)

    mod = _ada(jnp.pad(c, ((0, 8 - batch), (0, 0))), w_ada, b_ada)[:batch]
    mod3 = mod.reshape(batch, 6, d)

    da_w = DA_HEADS * 2 * DA_QK_DIM
    dv_w = DA_HEADS * DA_V_DIM
    gd_w = GDN_HEADS * GDN_DIM
    o0 = 2 * da_w + dv_w
    o1 = o0 + 4 * gd_w
    o2 = o1 + 2 * GDN_HEADS
    w_da = w_in[:, :o0].astype(BF16)
    w_g = w_in[:, o0:o1].astype(BF16)
    w_ba = _pad_lanes(w_in[:, o1:o2]).astype(BF16)
    w_gate = w_in[:, o2:].astype(BF16)
    qn = jnp.tile(da_q_norm, 2 * DA_HEADS).reshape(1, da_w)
    kn = jnp.tile(da_k_norm, 2 * DA_HEADS).reshape(1, da_w)

    x2 = x.reshape(n, d)
    g1 = norm1_gain.reshape(1, d)
    q, kt, v, gqkv, z, ba = _proj(x2, mod3, g1, w_da, w_g, w_ba, qn, kn, tm, tiles_per_batch)

    assert t_attn == tm
    alog_row = _pad_lanes(gdn_a_log.reshape(1, -1), GDN_HEADS)
    dt_row = _pad_lanes(gdn_dt_bias.reshape(1, -1), GDN_HEADS)
    oa_lo, oa_hi, o_b = _mixer(q, kt, v, lq1.reshape(1, -1), lk1.reshape(1, -1), lq2.reshape(1, -1),
                               lk2.reshape(1, -1), da_out_norm.reshape(1, -1),
                               gqkv.reshape(batch, seq, -1), z.reshape(batch, seq, -1), ba.reshape(batch, seq, -1),
                               gdn_conv, alog_row, dt_row, gdn_out_norm.reshape(1, -1),
                               batch, seq, t_attn, lambda_init)
    o_b = o_b.reshape(n, gd_w)

    w_rt = _pad_lanes(jnp.concatenate([w_router, w_group], axis=1))
    b_rt = _pad_lanes(jnp.concatenate([b_router, b_group]).reshape(1, -1))
    x1, h2, ri, rf, cnt = _merge(x2, mod3, g1, norm2_gain.reshape(1, d), oa_lo, oa_hi, o_b, w_gate,
                                 w_branch_a.astype(BF16), w_branch_b.astype(BF16), w_out.astype(BF16),
                                 w_rt, b_rt, tm * MERGE_SPLIT, tiles_per_batch // MERGE_SPLIT)

    counts = cnt[:, 0].astype(I32)
    padded = (counts + MOE_ROWS - 1) // MOE_ROWS * MOE_ROWS
    pend = jnp.cumsum(padded)
    pstart = pend - padded
    cap = 2 * n + N_EXPERTS * MOE_ROWS
    nb = cap // MOE_ROWS
    blk_start = jnp.arange(nb, dtype=I32) * MOE_ROWS
    blk_expert = jnp.minimum(jnp.sum(pend[None, :] <= blk_start[:, None], axis=1), N_EXPERTS - 1).astype(I32)
    blk_rows = jnp.clip((pstart + counts)[blk_expert] - blk_start, 0, MOE_ROWS).astype(I32)
    blk_rows = jnp.where(blk_start < pend[-1], blk_rows, 0)
    first_blk = pstart // MOE_ROWS
    n_blk = padded // MOE_ROWS
    pos = jnp.arange(nb, dtype=I32) - first_blk[blk_expert]
    nxt = blk_expert[jnp.clip(first_blk[blk_expert] + n_blk[blk_expert], 0, nb - 1)]
    fetch = [jnp.where(pos < jnp.minimum(lag, n_blk[blk_expert]), blk_expert, nxt).astype(I32) for lag in (1, 2, 3)]
    pstart_col = jnp.broadcast_to(pstart[:, None], (N_EXPERTS, LANES))
    dest = _dest(ri, pstart_col, min(4 * tm, n))
    idx = dest[0:2].reshape(2 * n // SC_WINDOW, SC_WINDOW)

    xb = _sc_scatter_rows(h2, idx, cap)
    yb = _moe(blk_expert, blk_rows, fetch, xb, w1, w3, w2)
    y12 = _sc_gather_rows(yb, idx)
    out = _combine(x1, mod3, rf, y12, 2 * tm, tiles_per_batch // 2)
    return out.reshape(batch, seq, d)


def kernel(x, c, w_ada, b_ada, norm1_gain, w_in, da_q_norm, da_k_norm, da_lambda_q1, da_lambda_k1, da_lambda_q2,
           da_lambda_k2, da_out_norm, gdn_conv, gdn_a_log, gdn_dt_bias, gdn_out_norm, w_branch_a, w_branch_b,
           w_out, norm2_gain, w_group, b_group, w_router, b_router, w1, w3, w2):
    for layer in range(w_ada.shape[0]):
        x = _layer(x, c, layer, w_ada[layer], b_ada[layer], norm1_gain[layer], w_in[layer], da_q_norm[layer],
                   da_k_norm[layer], da_lambda_q1[layer], da_lambda_k1[layer], da_lambda_q2[layer],
                   da_lambda_k2[layer], da_out_norm[layer], gdn_conv[layer], gdn_a_log[layer], gdn_dt_bias[layer],
                   gdn_out_norm[layer], w_branch_a[layer], w_branch_b[layer], w_out[layer], norm2_gain[layer],
                   w_group[layer], b_group[layer], w_router[layer], b_router[layer], w1[layer], w3[layer], w2[layer])
    return x
```

```python
import functools
import math

import jax
import jax.numpy as jnp
import numpy as np
from jax import lax
from jax.experimental import pallas as pl
from jax.experimental.pallas import tpu as pltpu
from jax.experimental.pallas import tpu_sc as plsc

F32 = jnp.float32
BF16 = jnp.bfloat16
I32 = jnp.int32

EPS = 1e-6
NEG_INF = -1e30
MASK_CHUNK = 64
LOG2E = 1.4426950408889634
ALIBI_ROWS = 3

DA_HEADS = 4
DA_QK_DIM = 64
DA_V_DIM = 128
GDN_HEADS = 4
GDN_DIM = 128
CONV_K = 4
N_GROUPS = 4
EXPERTS_PER_GROUP = 8
N_EXPERTS = N_GROUPS * EXPERTS_PER_GROUP
D_EXPERT = 512

LANES = 128
VMEM_LIMIT = 56 * 1024 * 1024

MERGE_SPLIT = 2
GDN_CHUNK = 128
GDN_BASE = 16
MOE_ROWS = 512
MOE_SUB = 256
SC_WINDOW = 128


def _cparams(n_axes):
    return pltpu.CompilerParams(dimension_semantics=("arbitrary",) * n_axes,
                                vmem_limit_bytes=VMEM_LIMIT)


def _mm(a, b):
    return jnp.dot(a.astype(BF16), b.astype(BF16), preferred_element_type=F32)


def _mm_nt(a, b):
    return lax.dot_general(a.astype(BF16), b.astype(BF16), (((1,), (1,)), ((), ())),
                           preferred_element_type=F32)


def _split2(a):
    hi = a.astype(BF16)
    lo = (a - hi.astype(F32)).astype(BF16)
    return hi, lo


def _mm3(a, b):
    ah, al = _split2(a)
    bh, bl = _split2(b)
    out = jnp.dot(ah, bh, preferred_element_type=F32)
    out = out + jnp.dot(ah, bl, preferred_element_type=F32)
    out = out + jnp.dot(al, bh, preferred_element_type=F32)
    return out


def _mm3_narrow(a, b):
    ah, al = _split2(a)
    bh, bl = _split2(b)
    n = b.shape[1]
    both = jnp.dot(ah, jnp.concatenate([bh, bl], axis=1), preferred_element_type=F32)
    return both[:, :n] + both[:, n:] + jnp.dot(al, bh, preferred_element_type=F32)


def _mm_exact_lhs(a_bf16, b):
    b1 = b.astype(BF16)
    r1 = b - b1.astype(F32)
    b2 = r1.astype(BF16)
    b3 = (r1 - b2.astype(F32)).astype(BF16)
    out = jnp.dot(a_bf16, b1, preferred_element_type=F32)
    out = out + jnp.dot(a_bf16, b2, preferred_element_type=F32)
    out = out + jnp.dot(a_bf16, b3, preferred_element_type=F32)
    return out


def _sigmoid(x):
    return 1.0 / (1.0 + jnp.exp(-x))


def _silu(x):
    return x * _sigmoid(x)


def _softplus(x):
    return jnp.maximum(x, 0.0) + jnp.log(1.0 + jnp.exp(-jnp.abs(x)))


def _pack_bf16_pairs(x):
    w = x.shape[1] // 2
    bits = lax.bitcast_convert_type(x.astype(BF16).astype(F32), jnp.uint32)
    lo = lax.shift_right_logical(bits[:, :w], jnp.uint32(16))
    hi = bits[:, w:] & jnp.uint32(0xFFFF0000)
    return lax.bitcast_convert_type(hi | lo, I32)


def _unpack_bf16_pairs(p):
    bits = lax.bitcast_convert_type(p, jnp.uint32)
    lo = lax.bitcast_convert_type(lax.shift_left(bits, jnp.uint32(16)), F32)
    hi = lax.bitcast_convert_type(bits & jnp.uint32(0xFFFF0000), F32)
    return jnp.concatenate([lo, hi], axis=1).astype(BF16)


def _rms(x, gain):
    return x * lax.rsqrt(jnp.mean(x * x, axis=-1, keepdims=True) + EPS) * gain


def _ada_kernel(c_ref, w_ref, b_ref, o_ref):
    sc = _silu(c_ref[...])
    o_ref[...] = _mm3(sc, w_ref[...]) + b_ref[...]


def _ada(c_pad, w_ada, b_ada):
    rows, d = c_pad.shape
    n = w_ada.shape[1]
    tn = d
    return pl.pallas_call(
        _ada_kernel,
        grid=(n // tn,),
        in_specs=[pl.BlockSpec((rows, d), lambda j: (0, 0)),
                  pl.BlockSpec((d, tn), lambda j: (0, j)),
                  pl.BlockSpec((1, tn), lambda j: (0, j))],
        out_specs=pl.BlockSpec((rows, tn), lambda j: (0, j)),
        out_shape=jax.ShapeDtypeStruct((rows, n), F32),
        compiler_params=_cparams(1),
    )(c_pad, w_ada, b_ada.reshape(1, n))


def _group_rms64(x, gain):
    tm, width = x.shape
    lane = lax.broadcasted_iota(I32, (tm, LANES), 1)
    low = lane < DA_QK_DIM
    parts = []
    for j in range(width // LANES):
        blk = x[:, j * LANES:(j + 1) * LANES]
        sq = blk * blk
        s_lo = jnp.sum(jnp.where(low, sq, 0.0), axis=-1, keepdims=True)
        s_hi = jnp.sum(jnp.where(low, 0.0, sq), axis=-1, keepdims=True)
        ms = jnp.where(low, s_lo, s_hi) * (1.0 / DA_QK_DIM)
        parts.append(blk * lax.rsqrt(ms + EPS))
    return jnp.concatenate(parts, axis=-1) * gain


def _proj_kernel(x_ref, mod_ref, g1_ref, wda_ref, wg_ref, wba_ref, qn_ref, kn_ref,
                 q_ref, k_ref, v_ref, gqkv_ref, z_ref, ba_ref):
    x = x_ref[...]
    mod = mod_ref[0]
    shift, scale = mod[0:1, :], mod[1:2, :]
    hb = (_rms(x, g1_ref[...]) * (1.0 + scale) + shift).astype(BF16)
    da_w = DA_HEADS * 2 * DA_QK_DIM
    da = jnp.dot(hb, wda_ref[...], preferred_element_type=F32)
    q = _group_rms64(da[:, :da_w], qn_ref[...]) * (DA_QK_DIM ** -0.5 * LOG2E)
    k = _group_rms64(da[:, da_w:2 * da_w], kn_ref[...])
    q_ref[...] = q.astype(BF16)
    k_ref[...] = k.T.astype(BF16)
    v_ref[...] = da[:, 2 * da_w:].astype(BF16)
    g = jnp.dot(hb, wg_ref[...], preferred_element_type=F32)
    conv_w = 3 * GDN_HEADS * GDN_DIM
    gqkv_ref[...] = g[:, :conv_w]
    z_ref[...] = g[:, conv_w:]
    ba_ref[...] = jnp.dot(hb, wba_ref[...], preferred_element_type=F32)


def _proj(x2, mod3, g1, w_da, w_g, w_ba, qn, kn, tm, tiles_per_batch):
    n, d = x2.shape
    da_w = DA_HEADS * 2 * DA_QK_DIM
    dv_w = DA_HEADS * DA_V_DIM
    conv_w = 3 * GDN_HEADS * GDN_DIM
    z_w = GDN_HEADS * GDN_DIM
    const = lambda i: (0, 0)
    row = lambda i: (i, 0)
    return pl.pallas_call(
        _proj_kernel,
        grid=(n // tm,),
        in_specs=[pl.BlockSpec((tm, d), row),
                  pl.BlockSpec((1, 6, d), lambda i: (i // tiles_per_batch, 0, 0)),
                  pl.BlockSpec((1, d), const),
                  pl.BlockSpec(w_da.shape, const),
                  pl.BlockSpec(w_g.shape, const),
                  pl.BlockSpec(w_ba.shape, const),
                  pl.BlockSpec((1, da_w), const),
                  pl.BlockSpec((1, da_w), const)],
        out_specs=[pl.BlockSpec((tm, da_w), row), pl.BlockSpec((da_w, tm), lambda i: (0, i)),
                   pl.BlockSpec((tm, dv_w), row), pl.BlockSpec((tm, conv_w), row),
                   pl.BlockSpec((tm, z_w), row), pl.BlockSpec((tm, LANES), row)],
        out_shape=[jax.ShapeDtypeStruct((n, da_w), BF16), jax.ShapeDtypeStruct((da_w, n), BF16),
                   jax.ShapeDtypeStruct((n, dv_w), BF16), jax.ShapeDtypeStruct((n, conv_w), F32),
                   jax.ShapeDtypeStruct((n, z_w), F32), jax.ShapeDtypeStruct((n, LANES), F32)],
        compiler_params=_cparams(1),
    )(x2, mod3, g1, w_da, w_g, w_ba, qn, kn)


def _attn_step(h, i, slopes_ref, qa_ref, qb_ref, kt_ref, al_ref, v_ref, lq1_ref, lk1_ref, lq2_ref, lk2_ref, on_ref,
               oa_ref, ob_ref, qs_ref, m_ref, acc_ref, p_ref, ddiag_ref, kta_ref, *, t, nq, lambda_init,
               between_blocks):
    slope2 = slopes_ref[h]
    reps = t // LANES

    @pl.when(i == 0)
    def _tables():
        r = lax.broadcasted_iota(I32, (t, t), 0)
        c = lax.broadcasted_iota(I32, (t, t), 1)
        ahead = jnp.maximum(c - r, 0).astype(F32)
        allowed = (c // MASK_CHUNK) <= (r // MASK_CHUNK)
        ddiag_ref[...] = jnp.where(allowed, (-2.0 * slope2) * ahead, NEG_INF)
        kta_ref[0:LANES, :] = kt_ref[...]
        kta_ref[LANES:2 * LANES, :] = al_ref[0]

    for tile, q_ref in enumerate((qa_ref, qb_ref)):
        q = q_ref[...]
        lane = lax.broadcasted_iota(I32, q.shape, 1)
        zero = jnp.zeros_like(q)
        ones = jnp.where(lane < ALIBI_ROWS, 1.0, 0.0).astype(BF16)
        qs_ref[tile, 0:t, :] = jnp.concatenate([jnp.where(lane < DA_QK_DIM, q, zero), ones], axis=1)
        qs_ref[tile, t:2 * t, :] = jnp.concatenate([jnp.where(lane < DA_QK_DIM, zero, q), ones], axis=1)
    m_ref[...] = jnp.full(m_ref.shape, NEG_INF, F32)
    acc_ref[...] = jnp.zeros(acc_ref.shape, F32)

    def scores(tile, j):
        start = pl.multiple_of(j * t, t)
        return jnp.dot(qs_ref[tile], kta_ref[:, pl.ds(start, t)], preferred_element_type=F32)

    def softmax_part(s, tile, slot, diagonal):
        alphas = []
        for half in range(2):
            rows = slice(half * t, (half + 1) * t)
            sb = s[rows] + ddiag_ref[...] if diagonal else s[rows]
            m_prev = m_ref[tile, rows]
            m_next = jnp.maximum(m_prev, jnp.max(sb, axis=-1, keepdims=True))
            alphas.append(jnp.exp2(m_prev - m_next))
            p_ref[slot, rows] = jnp.exp2(sb - jnp.tile(m_next, (1, reps))).astype(BF16)
            m_ref[tile, rows] = m_next
        return alphas

    def value_matmul(j, slot):
        start = pl.multiple_of(j * t, t)
        v = v_ref[pl.ds(start, t), :]
        vext = jnp.concatenate([v, jnp.ones_like(v)], axis=1)
        return jnp.dot(p_ref[slot], vext, preferred_element_type=F32)

    def accumulate(tile, alphas, pv):
        for half in range(2):
            rows = slice(half * t, (half + 1) * t)
            acc_ref[tile, rows] = jnp.tile(alphas[half], (1, 2)) * acc_ref[tile, rows] + pv[rows]

    blocks = []
    for k in range(nq - 1):
        tile = jnp.where(k >= i, 1, 0)
        blocks.append((tile, k - tile * i, False))
    blocks.append((0, i, True))
    blocks.append((1, nq - 1 - i, True))
    s_next = scores(blocks[0][0], blocks[0][1])
    for k, (tile, j, diagonal) in enumerate(blocks):
        s_cur = s_next
        if k + 1 < len(blocks):
            s_next = scores(blocks[k + 1][0], blocks[k + 1][1])
        alphas = softmax_part(s_cur, tile, k % 2, diagonal)
        accumulate(tile, alphas, value_matmul(j, k % 2))
        between_blocks()

    lam =(jnp.exp(jnp.sum(lq1_ref[...] * lk1_ref[...], axis=-1, keepdims=True))
           - jnp.exp(jnp.sum(lq2_ref[...] * lk2_ref[...], axis=-1, keepdims=True)) + lambda_init)
    for tile, o_ref in enumerate((oa_ref, ob_ref)):
        acc = acc_ref[tile]
        o_all = acc[:, 0:DA_V_DIM] / acc[:, DA_V_DIM:2 * DA_V_DIM]
        o = o_all[0:t] - lam * o_all[t:2 * t]
        o = _rms(o, on_ref[...]) * (1.0 - lambda_init)
        o_ref[...] = o.astype(o_ref.dtype)


def _alibi_slopes_log2():
    return np.asarray([2.0 ** (-8.0 * (hh + 1) / DA_HEADS) * LOG2E for hh in range(DA_HEADS)], np.float32)


def _alibi_key_rows(seq):
    col = _alibi_slopes_log2()[:, None] * np.arange(seq, dtype=np.float32)[None, :]
    b1 = col.astype(BF16)
    r1 = col - b1.astype(np.float32)
    b2 = r1.astype(BF16)
    b3 = (r1 - b2.astype(np.float32)).astype(BF16)
    rows = np.zeros((DA_HEADS, LANES, seq), BF16)
    rows[:, 0], rows[:, 1], rows[:, 2] = b1, b2, b3
    return jnp.asarray(rows)


N_ATTN_IN = 10
N_GDN_IN = 7
N_ATTN_SCRATCH = 6


def _mixer_kernel(slopes_ref, *refs, t, nq, batch, lambda_init):
    attn_in = refs[:N_ATTN_IN]
    gdn_in = refs[N_ATTN_IN:N_ATTN_IN + N_GDN_IN]
    outs = refs[N_ATTN_IN + N_GDN_IN:N_ATTN_IN + N_GDN_IN + 3]
    scratch = refs[N_ATTN_IN + N_GDN_IN + 3:]
    b, h, i = pl.program_id(0), pl.program_id(1), pl.program_id(2)
    chunk = (b * DA_HEADS + h) * (nq // 2) + i
    gdn = _gdn_phases(chunk, *gdn_in, outs[2], *scratch[N_ATTN_SCRATCH:], batch=batch)
    next(gdn)
    _attn_step(h, i, slopes_ref, *attn_in, outs[0], outs[1], *scratch[:N_ATTN_SCRATCH],
               t=t, nq=nq, lambda_init=lambda_init, between_blocks=lambda: next(gdn, None))
    for _ in gdn:
        pass


def _mixer(q, kt, v, lq1, lk1, lq2, lk2, out_norm_a, qkv3, z3, ba3, conv_taps, alog_row, dt_row, out_norm_b,
           batch, seq, t, lambda_init):
    nq = seq // t
    half = nq // 2
    width = DA_HEADS * DA_V_DIM
    c = GDN_CHUNK
    hd = GDN_HEADS * GDN_DIM
    cw = qkv3.shape[2]
    assert seq // c == batch * DA_HEADS * half
    slopes = jnp.asarray(_alibi_slopes_log2())
    vec = lambda b, h, i, sl: (0, 0)
    blk = lambda b, h, i, sl: (0, (b * DA_HEADS + h) * half + i, 0)
    grid_spec = pltpu.PrefetchScalarGridSpec(
        num_scalar_prefetch=1,
        grid=(batch, DA_HEADS, half),
        in_specs=[pl.BlockSpec((t, LANES), lambda b, h, i, sl: (b * nq + i, h)),
                  pl.BlockSpec((t, LANES), lambda b, h, i, sl: (b * nq + nq - 1 - i, h)),
                  pl.BlockSpec((LANES, seq), lambda b, h, i, sl: (h, b)),
                  pl.BlockSpec((1, LANES, seq), lambda b, h, i, sl: (h, 0, 0)),
                  pl.BlockSpec((seq, LANES), lambda b, h, i, sl: (b, h)),
                  pl.BlockSpec((1, DA_QK_DIM), vec), pl.BlockSpec((1, DA_QK_DIM), vec),
                  pl.BlockSpec((1, DA_QK_DIM), vec), pl.BlockSpec((1, DA_QK_DIM), vec),
                  pl.BlockSpec((1, DA_V_DIM), vec),
                  pl.BlockSpec((batch, c, cw), blk),
                  pl.BlockSpec((batch, c, hd), blk),
                  pl.BlockSpec((batch, c, LANES), blk),
                  pl.BlockSpec((CONV_K, cw), vec),
                  pl.BlockSpec((1, LANES), vec),
                  pl.BlockSpec((1, LANES), vec),
                  pl.BlockSpec((1, GDN_DIM), vec)],
        out_specs=[pl.BlockSpec((t, LANES), lambda b, h, i, sl: (b * half + i, h)),
                   pl.BlockSpec((t, LANES), lambda b, h, i, sl: (b * half + half - 1 - i, h)),
                   pl.BlockSpec((batch, c, hd), blk)],
        scratch_shapes=[pltpu.VMEM((2, 2 * t, 2 * LANES), BF16),
                        pltpu.VMEM((2, 2 * t, LANES), F32),
                        pltpu.VMEM((2, 2 * t, 2 * DA_V_DIM), F32),
                        pltpu.VMEM((2, 2 * t, t), BF16),
                        pltpu.VMEM((t, t), F32),
                        pltpu.VMEM((2 * LANES, seq), BF16),
                        pltpu.VMEM((batch, c + 8, cw), F32),
                        pltpu.VMEM((batch * GDN_HEADS, GDN_DIM, GDN_DIM), F32)])
    return pl.pallas_call(
        functools.partial(_mixer_kernel, t=t, nq=nq, batch=batch, lambda_init=lambda_init),
        grid_spec=grid_spec,
        out_shape=[jax.ShapeDtypeStruct((batch * seq // 2, width), BF16)] * 2
        + [jax.ShapeDtypeStruct((batch, seq, hd), BF16)],
        compiler_params=_cparams(3),
    )(slopes, q, q, kt, _alibi_key_rows(seq), v, lq1, lk1, lq2, lk2, out_norm_a,
      qkv3, z3, ba3, conv_taps, alog_row, dt_row, out_norm_b)


def _gdn_phases(step, u_ref, z_ref, ba_ref, cw_ref, alog_ref, dt_ref, on_ref, o_ref, stage_ref, state_ref, *, batch):
    c = GDN_CHUNK
    hd = GDN_HEADS * GDN_DIM

    @pl.when(step == 0)
    def _init():
        state_ref[...] = jnp.zeros(state_ref.shape, F32)
        stage_ref[:, 0:8, :] = jnp.zeros((batch, 8, stage_ref.shape[2]), F32)

    yield

    row = lax.broadcasted_iota(I32, (c, c), 0)
    col = lax.broadcasted_iota(I32, (c, c), 1)
    tri = row >= col
    strict = row > col
    tril_ones = jnp.where(tri, 1.0, 0.0).astype(BF16)

    chains = []
    for b in range(batch):
        stage_ref[b, 8:8 + c, :] = u_ref[b]
        u = stage_ref[b]
        y = cw_ref[CONV_K - 1:CONV_K, :] * u
        for back in range(1, CONV_K):
            y = y + cw_ref[CONV_K - 1 - back:CONV_K - back, :] * pltpu.roll(u, back, 0)
        stage_ref[b, 0:8, :] = stage_ref[b, c:c + 8, :]
        y = _silu(y[8:8 + c, :])

        ba = ba_ref[b]
        beta_all = _sigmoid(ba)
        g_all = -jnp.exp(alog_ref[...]) * _softplus(ba + dt_ref[...])
        gcum_all = _mm_exact_lhs(tril_ones, g_all)
        gcum_t = gcum_all.T

        for h in range(GDN_HEADS):
            q = y[:, h * GDN_DIM:(h + 1) * GDN_DIM]
            k = y[:, hd + h * GDN_DIM:hd + (h + 1) * GDN_DIM]
            v = y[:, 2 * hd + h * GDN_DIM:2 * hd + (h + 1) * GDN_DIM]
            q = q * lax.rsqrt(jnp.sum(q * q, axis=-1, keepdims=True) + EPS) * (GDN_DIM ** -0.5)
            k = k * lax.rsqrt(jnp.sum(k * k, axis=-1, keepdims=True) + EPS)
            beta = beta_all[:, h:h + 1]
            gc = gcum_all[:, GDN_HEADS + h:GDN_HEADS + h + 1]
            gr = gcum_t[GDN_HEADS + h:GDN_HEADS + h + 1, :]
            g_last = gc[c - 1:c, :]
            decay = jnp.where(tri, jnp.exp(jnp.where(tri, gc - gr, 0.0)), 0.0)
            e_gc = jnp.exp(gc)
            chains.append(dict(
                b=b, h=h, decay=decay, g_last=g_last,
                p=jnp.where(strict, -(beta * _mm_nt(k, k) * decay), 0.0),
                rhs=jnp.concatenate([v * beta, k * (beta * e_gc)], axis=1),
                qk=_mm_nt(q, k) * decay, q_dec=q * e_gc, k_dec_t=(k * jnp.exp(g_last - gc)).T))
        yield

    def same_block(size):
        return (row // size) == (col // size)

    for ch in chains:
        ch["pk"] = jnp.where(same_block(GDN_BASE), ch["p"], 0.0)
        ch["x"] = ch["pk"]
    for _ in range(int(math.log2(GDN_BASE)) - 1):
        for ch in chains:
            ch["pk"] = _mm(ch["pk"], ch["pk"])
        for ch in chains:
            ch["x"] = ch["x"] + ch["pk"] + _mm(ch["x"], ch["pk"])
        yield
    size = GDN_BASE
    while size < c:
        pair_only = same_block(2 * size) & jnp.logical_not(same_block(size))
        for ch in chains:
            e = jnp.where(pair_only, -ch["p"], 0.0)
            ch["y"] = e + _mm(ch["x"], e)
        for ch in chains:
            ch["x"] = ch["x"] - (ch["y"] + _mm(ch["y"], ch["x"]))
        size *= 2
        yield

    for ch in chains:
        ch["s_prev"] = state_ref[ch["b"] * GDN_HEADS + ch["h"]]
        ch["o_inter"] = _mm(ch["q_dec"], ch["s_prev"])
        ch["sol"] = ch["rhs"] + _mm(ch["x"], ch["rhs"])
    yield
    for ch in chains:
        ch["v_new"] = ch["sol"][:, :GDN_DIM] - _mm(ch["sol"][:, GDN_DIM:], ch["s_prev"])
    yield
    for ch in chains:
        ch["o"] = ch["o_inter"] + _mm(ch["qk"], ch["v_new"])
        ch["s_new"] = ch["s_prev"] * jnp.exp(ch["g_last"]) + _mm(ch["k_dec_t"], ch["v_new"])
    yield
    outs = [[None] * GDN_HEADS for _ in range(batch)]
    for ch in chains:
        zg = z_ref[ch["b"], :, ch["h"] * GDN_DIM:(ch["h"] + 1) * GDN_DIM]
        outs[ch["b"]][ch["h"]] = (_rms(ch["o"], on_ref[...]) * _silu(zg)).astype(o_ref.dtype)
    o_ref[...] = jnp.stack([jnp.concatenate(heads, axis=1) for heads in outs], axis=0)
    state_ref[...] = jnp.stack([ch["s_new"] for ch in chains], axis=0)


def _merge_kernel(x_ref, mod_ref, g1_ref, g2_ref, oa_lo_ref, oa_hi_ref, ob_ref, wgate_ref, wa_ref, wb_ref, wout_ref,
                  wrt_ref, brt_ref, x1_ref, h2_ref, ri_ref, rf_ref, cnt_ref, carry_ref, *, tiles_per_batch):
    @pl.when(pl.program_id(0) == 0)
    def _init():
        carry_ref[...] = jnp.zeros(carry_ref.shape, F32)

    d = x_ref.shape[1]
    mod = mod_ref[0]
    shift1, scale1, gate1 = mod[0:1, :], mod[1:2, :], mod[2:3, :]
    shift2, scale2 = mod[3:4, :], mod[4:5, :]
    first_half = (pl.program_id(0) % tiles_per_batch) < tiles_per_batch // 2
    tm = x_ref.shape[0] // MERGE_SPLIT
    pieces = [dict(rows=slice(s * tm, (s + 1) * tm)) for s in range(MERGE_SPLIT)]
    for pc in pieces:
        pc["x"] = x_ref[pc["rows"], :]
        hb = (_rms(pc["x"], g1_ref[...]) * (1.0 + scale1) + shift1).astype(BF16)
        pc["gates"] = _sigmoid(jnp.dot(hb, wgate_ref[...], preferred_element_type=F32))
    for pc in pieces:
        oa = jnp.where(first_half, oa_lo_ref[pc["rows"], :], oa_hi_ref[pc["rows"], :])
        pc["ya"] = jnp.dot(oa, wa_ref[...], preferred_element_type=F32)
        pc["yb"] = jnp.dot(ob_ref[pc["rows"], :], wb_ref[...], preferred_element_type=F32)
    for pc in pieces:
        pc["merged"] = pc["gates"][:, :d] * pc["ya"] + pc["gates"][:, d:] * pc["yb"]
    for pc in pieces:
        x1 = pc["x"] + gate1 * _mm(pc["merged"], wout_ref[...])
        x1_ref[pc["rows"], :] = x1
        pc["h2"] = _rms(x1, g2_ref[...]) * (1.0 + scale2) + shift2
        h2_ref[pc["rows"], :] = _pack_bf16_pairs(pc["h2"])
    for pc in pieces:
        pc["logits"] = _mm3_narrow(pc["h2"], wrt_ref[...]) + brt_ref[...]
    for pc in pieces:
        _route_tile(pc["logits"], ri_ref.at[:, pc["rows"]], rf_ref.at[:, pc["rows"]], cnt_ref, carry_ref)


def _merge(x2, mod3, g1, g2, oa_lo, oa_hi, ob, w_gate, w_a, w_b, w_out, w_rt, b_rt, tm, tiles_per_batch):
    n, d = x2.shape
    hp = tiles_per_batch // 2
    const = lambda i: (0, 0)
    row = lambda i: (i, 0)
    lo = lambda i: ((i // tiles_per_batch) * hp + jnp.minimum(i % tiles_per_batch, hp - 1), 0)
    hi = lambda i: ((i // tiles_per_batch) * hp + jnp.maximum(i % tiles_per_batch - hp, 0), 0)
    return pl.pallas_call(
        functools.partial(_merge_kernel, tiles_per_batch=tiles_per_batch),
        grid=(n // tm,),
        in_specs=[pl.BlockSpec((tm, d), row),
                  pl.BlockSpec((1, 6, d), lambda i: (i // tiles_per_batch, 0, 0)),
                  pl.BlockSpec((1, d), const), pl.BlockSpec((1, d), const),
                  pl.BlockSpec((tm, oa_lo.shape[1]), lo), pl.BlockSpec((tm, oa_hi.shape[1]), hi),
                  pl.BlockSpec((tm, ob.shape[1]), row),
                  pl.BlockSpec(w_gate.shape, const), pl.BlockSpec(w_a.shape, const),
                  pl.BlockSpec(w_b.shape, const), pl.BlockSpec(w_out.shape, const),
                  pl.BlockSpec(w_rt.shape, const), pl.BlockSpec((1, LANES), const)],
        out_specs=[pl.BlockSpec((tm, d), row), pl.BlockSpec((tm, d // 2), row),
                   pl.BlockSpec((8, tm), lambda i: (0, i)), pl.BlockSpec((8, tm), lambda i: (0, i)),
                   pl.BlockSpec((N_EXPERTS, LANES), const)],
        out_shape=[jax.ShapeDtypeStruct((n, d), F32), jax.ShapeDtypeStruct((n, d // 2), I32),
                   jax.ShapeDtypeStruct((8, n), I32), jax.ShapeDtypeStruct((8, n), F32),
                   jax.ShapeDtypeStruct((N_EXPERTS, LANES), F32)],
        scratch_shapes=[pltpu.VMEM((N_EXPERTS, LANES), F32)],
        compiler_params=_cparams(1),
    )(x2, mod3, g1, g2, oa_lo, oa_hi, ob, w_gate, w_a, w_b, w_out, w_rt, b_rt)


def _route_tile(lg, ri_ref, rf_ref, cnt_ref, carry_ref):
    tm = lg.shape[0]
    lgt = lg.T
    el_all = lgt[0:N_EXPERTS, :]
    gl_all = lgt[N_EXPERTS:N_EXPERTS + 8, :]
    er = lax.broadcasted_iota(I32, el_all.shape, 0)
    gr = lax.broadcasted_iota(I32, gl_all.shape, 0)
    big = jnp.int32(LANES)
    is_grp = gr < N_GROUPS
    gl = jnp.where(is_grp, gl_all, NEG_INF)
    gmax = jnp.max(gl, axis=0, keepdims=True)
    g_top = jnp.min(jnp.where(is_grp & (gl == gmax), gr, big), axis=0, keepdims=True)
    g_top_p = 1.0 / jnp.sum(jnp.where(is_grp, jnp.exp(gl - gmax), 0.0), axis=0, keepdims=True)
    in_grp = (er // EXPERTS_PER_GROUP) == g_top
    el = jnp.where(in_grp, el_all, NEG_INF)
    v1 = jnp.max(el, axis=0, keepdims=True)
    e1 = jnp.min(jnp.where(in_grp & (el == v1), er, big), axis=0, keepdims=True)
    rest = in_grp & (er != e1)
    el2 = jnp.where(rest, el_all, NEG_INF)
    v2 = jnp.max(el2, axis=0, keepdims=True)
    e2 = jnp.min(jnp.where(rest & (el2 == v2), er, big), axis=0, keepdims=True)
    ex = jnp.exp(v2 - v1)
    w1 = g_top_p / (1.0 + ex)
    w2 = g_top_p * ex / (1.0 + ex)

    oh1 = er == e1
    oh2 = er == e2
    ohs = (jnp.where(oh1, 1.0, 0.0) + jnp.where(oh2, 1.0, 0.0)).astype(BF16)
    r = lax.broadcasted_iota(I32, (tm, tm), 0)
    c = lax.broadcasted_iota(I32, (tm, tm), 1)
    earlier = jnp.where(r < c, 1.0, 0.0).astype(BF16)
    prior = jnp.dot(ohs, earlier, preferred_element_type=F32) + jnp.tile(carry_ref[...], (1, tm // LANES))
    rank1 = jnp.sum(jnp.where(oh1, prior, 0.0), axis=0, keepdims=True)
    rank2 = jnp.sum(jnp.where(oh2, prior, 0.0), axis=0, keepdims=True)
    carry_ref[...] = carry_ref[...] + jnp.sum(ohs.astype(F32), axis=1, keepdims=True)
    cnt_ref[...] = carry_ref[...]

    row = lax.broadcasted_iota(I32, (8, tm), 0)
    r1 = rank1.astype(I32)
    r2 = rank2.astype(I32)
    ri_ref[...] = jnp.where(row == 0, e1, jnp.where(row == 1, e2, jnp.where(row == 2, r1, jnp.where(row == 3, r2, 0))))
    rf_ref[...] = jnp.where(row == 0, w1, jnp.where(row == 1, w2, 0.0))


def _moe_kernel(be_ref, act_ref, slot_ref, nxt_ref, xb_ref, w1_hbm, w3_hbm, w2_hbm, o_ref,
                w1b_ref, w3b_ref, w2b_ref, w1f_ref, w3f_ref, w2f_ref, sem_ref):
    i = pl.program_id(0)
    prev = be_ref[jnp.maximum(i - 1, 0)]
    slot = slot_ref[i]

    def copies(expert, s):
        return [pltpu.make_async_copy(w1_hbm.at[expert], w1f_ref.at[s], sem_ref.at[0, s]),
                pltpu.make_async_copy(w3_hbm.at[expert], w3f_ref.at[s], sem_ref.at[1, s]),
                pltpu.make_async_copy(w2_hbm.at[expert], w2f_ref.at[s], sem_ref.at[2, s])]

    @pl.when(i == 0)
    def _first_request():
        for cp in copies(be_ref[0], 0):
            cp.start()

    @pl.when((i == 0) | (be_ref[i] != prev))
    def _expert_start():
        for cp in copies(be_ref[i], slot):
            cp.wait()
        w1b_ref[...] = w1f_ref[slot].astype(BF16)
        w3b_ref[...] = w3f_ref[slot].astype(BF16)
        w2b_ref[...] = w2f_ref[slot].astype(BF16)

        @pl.when(nxt_ref[i] >= 0)
        def _request_next():
            for cp in copies(nxt_ref[i], 1 - slot):
                cp.start()

    valid = act_ref[i]

    def expert_mlp(n_rows):
        xp = xb_ref[0:n_rows, :]
        r = lax.broadcasted_iota(I32, xp.shape, 0)
        xb = _unpack_bf16_pairs(jnp.where(r < valid, xp, 0))
        a = jnp.dot(xb, w1b_ref[...], preferred_element_type=F32)
        g = jnp.dot(xb, w3b_ref[...], preferred_element_type=F32)
        hid = (_silu(a) * g).astype(BF16)
        o_ref[0:n_rows, :] = _pack_bf16_pairs(jnp.dot(hid, w2b_ref[...], preferred_element_type=F32))

    @pl.when(valid > MOE_SUB)
    def _full():
        expert_mlp(MOE_ROWS)

    @pl.when((valid > 0) & (valid <= MOE_SUB))
    def _partial():
        expert_mlp(MOE_SUB)
        o_ref[MOE_SUB:MOE_ROWS, :] = jnp.zeros((MOE_ROWS - MOE_SUB, o_ref.shape[1]), o_ref.dtype)

    @pl.when(valid == 0)
    def _idle():
        o_ref[...] = jnp.zeros(o_ref.shape, o_ref.dtype)


def _moe(blk_expert, blk_rows, blk_slot, blk_next, xb, w1, w3, w2):
    cap = xb.shape[0]
    d = w1.shape[1]
    nb = cap // MOE_ROWS
    de = w1.shape[2]
    grid_spec = pltpu.PrefetchScalarGridSpec(
        num_scalar_prefetch=4,
        grid=(nb,),
        in_specs=[pl.BlockSpec((MOE_ROWS, d // 2), lambda i, be, act, sl, nx: (i, 0)),
                  pl.BlockSpec(memory_space=pl.ANY), pl.BlockSpec(memory_space=pl.ANY),
                  pl.BlockSpec(memory_space=pl.ANY)],
        out_specs=pl.BlockSpec((MOE_ROWS, d // 2), lambda i, be, act, sl, nx: (i, 0)),
        scratch_shapes=[pltpu.VMEM((d, de), BF16), pltpu.VMEM((d, de), BF16), pltpu.VMEM((de, d), BF16),
                        pltpu.VMEM((2, d, de), F32), pltpu.VMEM((2, d, de), F32), pltpu.VMEM((2, de, d), F32),
                        pltpu.SemaphoreType.DMA((3, 2))])
    return pl.pallas_call(
        _moe_kernel,
        grid_spec=grid_spec,
        out_shape=jax.ShapeDtypeStruct((cap, d // 2), I32),
        compiler_params=_cparams(1),
    )(blk_expert, blk_rows, blk_slot, blk_next, xb, w1, w3, w2)


def _dest_kernel(ri_ref, pstart_ref, o_ref):
    ri = ri_ref[...]
    tm = ri.shape[1]
    er = lax.broadcasted_iota(I32, (N_EXPERTS, tm), 0)
    pstart = jnp.tile(pstart_ref[...], (1, tm // LANES))
    base1 = jnp.sum(jnp.where(er == ri[0:1, :], pstart, 0), axis=0, keepdims=True)
    base2 = jnp.sum(jnp.where(er == ri[1:2, :], pstart, 0), axis=0, keepdims=True)
    row = lax.broadcasted_iota(I32, ri.shape, 0)
    o_ref[...] = jnp.where(row == 0, base1 + ri[2:3, :], jnp.where(row == 1, base2 + ri[3:4, :], 0))


def _dest(ri, pstart_col, tm):
    n = ri.shape[1]
    return pl.pallas_call(
        _dest_kernel,
        grid=(n // tm,),
        in_specs=[pl.BlockSpec((8, tm), lambda i: (0, i)), pl.BlockSpec((N_EXPERTS, LANES), lambda i: (0, 0))],
        out_specs=pl.BlockSpec((8, tm), lambda i: (0, i)),
        out_shape=jax.ShapeDtypeStruct((8, n), I32),
        compiler_params=_cparams(1),
    )(ri, pstart_col)


def _sc_mesh():
    return plsc.VectorSubcoreMesh(core_axis_name="core", subcore_axis_name="subcore")


def _sc_scatter_rows(x, idx, cap):
    n, d = x.shape
    windows = idx.shape[0]
    tiles = n // SC_WINDOW
    mesh = _sc_mesh()
    workers = mesh.num_cores * mesh.num_subcores
    per_worker = windows // workers

    @functools.partial(pl.kernel, out_type=jax.ShapeDtypeStruct((cap, d), x.dtype), mesh=mesh,
                       scratch_types=[pltpu.VMEM((1, SC_WINDOW), I32), pltpu.VMEM((SC_WINDOW, d), x.dtype)])
    def scatter(x_hbm, i_hbm, o_hbm, idx_v, rows_v):
        wid = lax.axis_index("subcore") * mesh.num_cores + lax.axis_index("core")

        @pl.loop(0, per_worker)
        def _(j):
            w = wid * per_worker + j
            pltpu.sync_copy(i_hbm.at[pl.ds(w, 1)], idx_v)
            pltpu.sync_copy(x_hbm.at[pl.ds(lax.rem(w, tiles) * SC_WINDOW, SC_WINDOW)], rows_v)
            pltpu.sync_copy(rows_v, o_hbm.at[idx_v.at[0]])

    return scatter(x, idx)


def _sc_gather_rows(table, idx):
    d = table.shape[1]
    windows = idx.shape[0]
    mesh = _sc_mesh()
    workers = mesh.num_cores * mesh.num_subcores
    per_worker = windows // workers

    @functools.partial(pl.kernel, out_type=jax.ShapeDtypeStruct((windows * SC_WINDOW, d), table.dtype), mesh=mesh,
                       scratch_types=[pltpu.VMEM((1, SC_WINDOW), I32), pltpu.VMEM((SC_WINDOW, d), table.dtype)])
    def gather(t_hbm, i_hbm, o_hbm, idx_v, rows_v):
        wid = lax.axis_index("subcore") * mesh.num_cores + lax.axis_index("core")

        @pl.loop(0, per_worker)
        def _(j):
            w = wid * per_worker + j
            pltpu.sync_copy(i_hbm.at[pl.ds(w, 1)], idx_v)
            pltpu.sync_copy(t_hbm.at[idx_v.at[0]], rows_v)
            pltpu.sync_copy(rows_v, o_hbm.at[pl.ds(w * SC_WINDOW, SC_WINDOW)])

    return gather(table, idx)


def _combine_kernel(x1_ref, mod_ref, rf_ref, ya_ref, yb_ref, o_ref):
    gate2 = mod_ref[0][5:6, :]
    rf = rf_ref[...].T
    ya = _unpack_bf16_pairs(ya_ref[...]).astype(F32)
    yb = _unpack_bf16_pairs(yb_ref[...]).astype(F32)
    y = rf[:, 0:1] * ya + rf[:, 1:2] * yb
    o_ref[...] = x1_ref[...] + gate2 * y


def _combine(x1, mod3, rf, y12, tm, tiles_per_batch):
    n, d = x1.shape
    tiles = n // tm
    row = lambda i: (i, 0)
    return pl.pallas_call(
        _combine_kernel,
        grid=(tiles,),
        in_specs=[pl.BlockSpec((tm, d), row),
                  pl.BlockSpec((1, 6, d), lambda i: (i // tiles_per_batch, 0, 0)),
                  pl.BlockSpec((8, tm), lambda i: (0, i)),
                  pl.BlockSpec((tm, d // 2), row), pl.BlockSpec((tm, d // 2), lambda i: (i + tiles, 0))],
        out_specs=pl.BlockSpec((tm, d), row),
        out_shape=jax.ShapeDtypeStruct((n, d), F32),
        compiler_params=_cparams(1),
    )(x1, mod3, rf, y12, y12)


def _pad_lanes(a, offset=0):
    return jnp.pad(a, ((0, 0), (offset, LANES - offset - a.shape[1])))


def _layer(x, c, layer, w_ada, b_ada, norm1_gain, w_in, da_q_norm, da_k_norm, lq1, lk1, lq2, lk2, da_out_norm,
           gdn_conv, gdn_a_log, gdn_dt_bias, gdn_out_norm, w_branch_a, w_branch_b, w_out, norm2_gain,
           w_group, b_group, w_router, b_router, w1, w3, w2):
    batch, seq, d = x.shape
    n = batch * seq
    tm = min(512, seq)
    tiles_per_batch = seq // tm
    t_attn = min(512, seq)
    lambda_init = 0.8 - 0.6 * math.exp(-0.3 * layer)

    mod = _ada(jnp.pad(c, ((0, 8 - batch), (0, 0))), w_ada, b_ada)[:batch]
    mod3 = mod.reshape(batch, 6, d)

    da_w = DA_HEADS * 2 * DA_QK_DIM
    dv_w = DA_HEADS * DA_V_DIM
    gd_w = GDN_HEADS * GDN_DIM
    o0 = 2 * da_w + dv_w
    o1 = o0 + 4 * gd_w
    o2 = o1 + 2 * GDN_HEADS
    w_da = w_in[:, :o0].astype(BF16)
    w_g = w_in[:, o0:o1].astype(BF16)
    w_ba = _pad_lanes(w_in[:, o1:o2]).astype(BF16)
    w_gate = w_in[:, o2:].astype(BF16)
    qn = jnp.tile(da_q_norm, 2 * DA_HEADS).reshape(1, da_w)
    kn = jnp.tile(da_k_norm, 2 * DA_HEADS).reshape(1, da_w)

    x2 = x.reshape(n, d)
    g1 = norm1_gain.reshape(1, d)
    q, kt, v, gqkv, z, ba = _proj(x2, mod3, g1, w_da, w_g, w_ba, qn, kn, tm, tiles_per_batch)

    assert t_attn == tm
    alog_row = _pad_lanes(gdn_a_log.reshape(1, -1), GDN_HEADS)
    dt_row = _pad_lanes(gdn_dt_bias.reshape(1, -1), GDN_HEADS)
    oa_lo, oa_hi, o_b = _mixer(q, kt, v, lq1.reshape(1, -1), lk1.reshape(1, -1), lq2.reshape(1, -1),
                               lk2.reshape(1, -1), da_out_norm.reshape(1, -1),
                               gqkv.reshape(batch, seq, -1), z.reshape(batch, seq, -1), ba.reshape(batch, seq, -1),
                               gdn_conv, alog_row, dt_row, gdn_out_norm.reshape(1, -1),
                               batch, seq, t_attn, lambda_init)
    o_b = o_b.reshape(n, gd_w)

    w_rt = _pad_lanes(jnp.concatenate([w_router, w_group], axis=1))
    b_rt = _pad_lanes(jnp.concatenate([b_router, b_group]).reshape(1, -1))
    x1, h2, ri, rf, cnt = _merge(x2, mod3, g1, norm2_gain.reshape(1, d), oa_lo, oa_hi, o_b, w_gate,
                                 w_branch_a.astype(BF16), w_branch_b.astype(BF16), w_out.astype(BF16),
                                 w_rt, b_rt, tm * MERGE_SPLIT, tiles_per_batch // MERGE_SPLIT)

    counts = cnt[:, 0].astype(I32)
    padded = (counts + MOE_ROWS - 1) // MOE_ROWS * MOE_ROWS
    pend = jnp.cumsum(padded)
    pstart = pend - padded
    cap = 2 * n + N_EXPERTS * MOE_ROWS
    nb = cap // MOE_ROWS
    blk_start = jnp.arange(nb, dtype=I32) * MOE_ROWS
    blk_expert = jnp.minimum(jnp.sum(pend[None, :] <= blk_start[:, None], axis=1), N_EXPERTS - 1).astype(I32)
    blk_rows = jnp.clip((pstart + counts)[blk_expert] - blk_start, 0, MOE_ROWS).astype(I32)
    blk_rows = jnp.where(blk_start < pend[-1], blk_rows, 0)
    blk_idx = jnp.arange(nb, dtype=I32)
    run_start = jnp.concatenate([jnp.ones((1,), bool), blk_expert[1:] != blk_expert[:-1]])
    blk_slot = ((jnp.cumsum(run_start.astype(I32)) - 1) % 2).astype(I32)
    start_at_or_after = lax.cummin(jnp.where(run_start, blk_idx, nb)[::-1])[::-1]
    next_start = jnp.concatenate([start_at_or_after[1:], jnp.full((1,), nb, I32)])
    blk_next = jnp.where(next_start < nb, blk_expert[jnp.minimum(next_start, nb - 1)], -1).astype(I32)
    pstart_col = jnp.broadcast_to(pstart[:, None], (N_EXPERTS, LANES))
    dest = _dest(ri, pstart_col, min(4 * tm, n))
    idx = dest[0:2].reshape(2 * n // SC_WINDOW, SC_WINDOW)

    xb = _sc_scatter_rows(h2, idx, cap)
    yb = _moe(blk_expert, blk_rows, blk_slot, blk_next, xb, w1, w3, w2)
    y12 = _sc_gather_rows(yb, idx)
    out = _combine(x1, mod3, rf, y12, 2 * tm, tiles_per_batch // 2)
    return out.reshape(batch, seq, d)


def kernel(x, c, w_ada, b_ada, norm1_gain, w_in, da_q_norm, da_k_norm, da_lambda_q1, da_lambda_k1, da_lambda_q2,
           da_lambda_k2, da_out_norm, gdn_conv, gdn_a_log, gdn_dt_bias, gdn_out_norm, w_branch_a, w_branch_b,
           w_out, norm2_gain, w_group, b_group, w_router, b_router, w1, w3, w2):
    for layer in range(w_ada.shape[0]):
        x = _layer(x, c, layer, w_ada[layer], b_ada[layer], norm1_gain[layer], w_in[layer], da_q_norm[layer],
                   da_k_norm[layer], da_lambda_q1[layer], da_lambda_k1[layer], da_lambda_q2[layer],
                   da_lambda_k2[layer], da_out_norm[layer], gdn_conv[layer], gdn_a_log[layer], gdn_dt_bias[layer],
                   gdn_out_norm[layer], w_branch_a[layer], w_branch_b[layer], w_out[layer], norm2_gain[layer],
                   w_group[layer], b_group[layer], w_router[layer], b_router[layer], w1[layer], w3[layer], w2[layer])
    return x
```

```python
import functools
import math

import jax
import jax.numpy as jnp
import numpy as np
from jax import lax
from jax.experimental import pallas as pl
from jax.experimental.pallas import tpu as pltpu
from jax.experimental.pallas import tpu_sc as plsc

F32 = jnp.float32
BF16 = jnp.bfloat16
I32 = jnp.int32

EPS = 1e-6
NEG_INF = -1e30
MASK_CHUNK = 64
LOG2E = 1.4426950408889634
ALIBI_ROWS = 3

DA_HEADS = 4
DA_QK_DIM = 64
DA_V_DIM = 128
GDN_HEADS = 4
GDN_DIM = 128
CONV_K = 4
N_GROUPS = 4
EXPERTS_PER_GROUP = 8
N_EXPERTS = N_GROUPS * EXPERTS_PER_GROUP
D_EXPERT = 512

LANES = 128
VMEM_LIMIT = 56 * 1024 * 1024

MERGE_SPLIT = 2
GDN_CHUNK = 128
GDN_BASE = 16
MOE_ROWS = 512
MOE_SUB = 256
SC_WINDOW = 128


def _cparams(n_axes):
    return pltpu.CompilerParams(dimension_semantics=("arbitrary",) * n_axes,
                                vmem_limit_bytes=VMEM_LIMIT)


def _mm(a, b):
    return jnp.dot(a.astype(BF16), b.astype(BF16), preferred_element_type=F32)


def _mm_nt(a, b):
    return lax.dot_general(a.astype(BF16), b.astype(BF16), (((1,), (1,)), ((), ())),
                           preferred_element_type=F32)


def _split2(a):
    hi = a.astype(BF16)
    lo = (a - hi.astype(F32)).astype(BF16)
    return hi, lo


def _mm3(a, b):
    ah, al = _split2(a)
    bh, bl = _split2(b)
    out = jnp.dot(ah, bh, preferred_element_type=F32)
    out = out + jnp.dot(ah, bl, preferred_element_type=F32)
    out = out + jnp.dot(al, bh, preferred_element_type=F32)
    return out


def _mm3_narrow(a, b):
    ah, al = _split2(a)
    bh, bl = _split2(b)
    n = b.shape[1]
    both = jnp.dot(ah, jnp.concatenate([bh, bl], axis=1), preferred_element_type=F32)
    return both[:, :n] + both[:, n:] + jnp.dot(al, bh, preferred_element_type=F32)


def _mm_exact_lhs(a_bf16, b):
    b1 = b.astype(BF16)
    r1 = b - b1.astype(F32)
    b2 = r1.astype(BF16)
    b3 = (r1 - b2.astype(F32)).astype(BF16)
    out = jnp.dot(a_bf16, b1, preferred_element_type=F32)
    out = out + jnp.dot(a_bf16, b2, preferred_element_type=F32)
    out = out + jnp.dot(a_bf16, b3, preferred_element_type=F32)
    return out


def _sigmoid(x):
    return 1.0 / (1.0 + jnp.exp(-x))


def _silu(x):
    return x * _sigmoid(x)


def _softplus(x):
    return jnp.maximum(x, 0.0) + jnp.log(1.0 + jnp.exp(-jnp.abs(x)))


def _pack_bf16_pairs(x):
    w = x.shape[1] // 2
    bits = lax.bitcast_convert_type(x.astype(BF16).astype(F32), jnp.uint32)
    lo = lax.shift_right_logical(bits[:, :w], jnp.uint32(16))
    hi = bits[:, w:] & jnp.uint32(0xFFFF0000)
    return lax.bitcast_convert_type(hi | lo, I32)


def _unpack_bf16_pairs(p):
    bits = lax.bitcast_convert_type(p, jnp.uint32)
    lo = lax.bitcast_convert_type(lax.shift_left(bits, jnp.uint32(16)), F32)
    hi = lax.bitcast_convert_type(bits & jnp.uint32(0xFFFF0000), F32)
    return jnp.concatenate([lo, hi], axis=1).astype(BF16)


def _rms(x, gain):
    return x * lax.rsqrt(jnp.mean(x * x, axis=-1, keepdims=True) + EPS) * gain


def _ada_kernel(c_ref, w_ref, b_ref, o_ref):
    sc = _silu(c_ref[...])
    o_ref[...] = _mm3(sc, w_ref[...]) + b_ref[...]


def _ada(c_pad, w_ada, b_ada):
    rows, d = c_pad.shape
    n = w_ada.shape[1]
    tn = d
    return pl.pallas_call(
        _ada_kernel,
        grid=(n // tn,),
        in_specs=[pl.BlockSpec((rows, d), lambda j: (0, 0)),
                  pl.BlockSpec((d, tn), lambda j: (0, j)),
                  pl.BlockSpec((1, tn), lambda j: (0, j))],
        out_specs=pl.BlockSpec((rows, tn), lambda j: (0, j)),
        out_shape=jax.ShapeDtypeStruct((rows, n), F32),
        compiler_params=_cparams(1),
    )(c_pad, w_ada, b_ada.reshape(1, n))


def _group_rms64(x, gain):
    tm, width = x.shape
    lane = lax.broadcasted_iota(I32, (tm, LANES), 1)
    low = lane < DA_QK_DIM
    parts = []
    for j in range(width // LANES):
        blk = x[:, j * LANES:(j + 1) * LANES]
        sq = blk * blk
        s_lo = jnp.sum(jnp.where(low, sq, 0.0), axis=-1, keepdims=True)
        s_hi = jnp.sum(jnp.where(low, 0.0, sq), axis=-1, keepdims=True)
        ms = jnp.where(low, s_lo, s_hi) * (1.0 / DA_QK_DIM)
        parts.append(blk * lax.rsqrt(ms + EPS))
    return jnp.concatenate(parts, axis=-1) * gain


def _proj_kernel(x_ref, mod_ref, g1_ref, wda_ref, wg_ref, wba_ref, qn_ref, kn_ref,
                 q_ref, k_ref, v_ref, gqkv_ref, z_ref, ba_ref):
    x = x_ref[...]
    mod = mod_ref[0]
    shift, scale = mod[0:1, :], mod[1:2, :]
    hb = (_rms(x, g1_ref[...]) * (1.0 + scale) + shift).astype(BF16)
    da_w = DA_HEADS * 2 * DA_QK_DIM
    da = jnp.dot(hb, wda_ref[...], preferred_element_type=F32)
    q = _group_rms64(da[:, :da_w], qn_ref[...]) * (DA_QK_DIM ** -0.5 * LOG2E)
    k = _group_rms64(da[:, da_w:2 * da_w], kn_ref[...])
    q_ref[...] = q.astype(BF16)
    k_ref[...] = k.T.astype(BF16)
    v_ref[...] = da[:, 2 * da_w:].astype(BF16)
    g = jnp.dot(hb, wg_ref[...], preferred_element_type=F32)
    conv_w = 3 * GDN_HEADS * GDN_DIM
    gqkv_ref[...] = g[:, :conv_w]
    z_ref[...] = g[:, conv_w:]
    ba_ref[...] = jnp.dot(hb, wba_ref[...], preferred_element_type=F32)


def _proj(x2, mod3, g1, w_da, w_g, w_ba, qn, kn, tm, tiles_per_batch):
    n, d = x2.shape
    da_w = DA_HEADS * 2 * DA_QK_DIM
    dv_w = DA_HEADS * DA_V_DIM
    conv_w = 3 * GDN_HEADS * GDN_DIM
    z_w = GDN_HEADS * GDN_DIM
    const = lambda i: (0, 0)
    row = lambda i: (i, 0)
    return pl.pallas_call(
        _proj_kernel,
        grid=(n // tm,),
        in_specs=[pl.BlockSpec((tm, d), row),
                  pl.BlockSpec((1, 6, d), lambda i: (i // tiles_per_batch, 0, 0)),
                  pl.BlockSpec((1, d), const),
                  pl.BlockSpec(w_da.shape, const),
                  pl.BlockSpec(w_g.shape, const),
                  pl.BlockSpec(w_ba.shape, const),
                  pl.BlockSpec((1, da_w), const),
                  pl.BlockSpec((1, da_w), const)],
        out_specs=[pl.BlockSpec((tm, da_w), row), pl.BlockSpec((da_w, tm), lambda i: (0, i)),
                   pl.BlockSpec((tm, dv_w), row), pl.BlockSpec((tm, conv_w), row),
                   pl.BlockSpec((tm, z_w), row), pl.BlockSpec((tm, LANES), row)],
        out_shape=[jax.ShapeDtypeStruct((n, da_w), BF16), jax.ShapeDtypeStruct((da_w, n), BF16),
                   jax.ShapeDtypeStruct((n, dv_w), BF16), jax.ShapeDtypeStruct((n, conv_w), F32),
                   jax.ShapeDtypeStruct((n, z_w), F32), jax.ShapeDtypeStruct((n, LANES), F32)],
        compiler_params=_cparams(1),
    )(x2, mod3, g1, w_da, w_g, w_ba, qn, kn)


def _attn_step(h, i, slopes_ref, qa_ref, qb_ref, kt_ref, al_ref, v_ref, lq1_ref, lk1_ref, lq2_ref, lk2_ref, on_ref,
               oa_ref, ob_ref, qs_ref, m_ref, acc_ref, p_ref, ddiag_ref, kta_ref, *, t, nq, lambda_init,
               between_blocks):
    slope2 = slopes_ref[h]
    reps = t // LANES

    @pl.when(i == 0)
    def _tables():
        r = lax.broadcasted_iota(I32, (t, t), 0)
        c = lax.broadcasted_iota(I32, (t, t), 1)
        ahead = jnp.maximum(c - r, 0).astype(F32)
        allowed = (c // MASK_CHUNK) <= (r // MASK_CHUNK)
        ddiag_ref[...] = jnp.where(allowed, (-2.0 * slope2) * ahead, NEG_INF)
        kta_ref[0:LANES, :] = kt_ref[...]
        kta_ref[LANES:2 * LANES, :] = al_ref[0]

    for tile, q_ref in enumerate((qa_ref, qb_ref)):
        q = q_ref[...]
        lane = lax.broadcasted_iota(I32, q.shape, 1)
        zero = jnp.zeros_like(q)
        ones = jnp.where(lane < ALIBI_ROWS, 1.0, 0.0).astype(BF16)
        qs_ref[tile, 0:t, :] = jnp.concatenate([jnp.where(lane < DA_QK_DIM, q, zero), ones], axis=1)
        qs_ref[tile, t:2 * t, :] = jnp.concatenate([jnp.where(lane < DA_QK_DIM, zero, q), ones], axis=1)
    m_ref[...] = jnp.full(m_ref.shape, NEG_INF, F32)
    acc_ref[...] = jnp.zeros(acc_ref.shape, F32)

    def scores(tile, j):
        start = pl.multiple_of(j * t, t)
        return jnp.dot(qs_ref[tile], kta_ref[:, pl.ds(start, t)], preferred_element_type=F32)

    def softmax_part(s, tile, slot, diagonal):
        alphas = []
        for half in range(2):
            rows = slice(half * t, (half + 1) * t)
            sb = s[rows] + ddiag_ref[...] if diagonal else s[rows]
            m_prev = m_ref[tile, rows]
            m_next = jnp.maximum(m_prev, jnp.max(sb, axis=-1, keepdims=True))
            alphas.append(jnp.exp2(m_prev - m_next))
            p_ref[slot, rows] = jnp.exp2(sb - jnp.tile(m_next, (1, reps))).astype(BF16)
            m_ref[tile, rows] = m_next
        return alphas

    def value_matmul(j, slot):
        start = pl.multiple_of(j * t, t)
        v = v_ref[pl.ds(start, t), :]
        vext = jnp.concatenate([v, jnp.ones_like(v)], axis=1)
        return jnp.dot(p_ref[slot], vext, preferred_element_type=F32)

    def accumulate(tile, alphas, pv):
        for half in range(2):
            rows = slice(half * t, (half + 1) * t)
            acc_ref[tile, rows] = jnp.tile(alphas[half], (1, 2)) * acc_ref[tile, rows] + pv[rows]

    blocks = []
    for k in range(nq - 1):
        tile = jnp.where(k >= i, 1, 0)
        blocks.append((tile, k - tile * i, False))
    blocks.append((0, i, True))
    blocks.append((1, nq - 1 - i, True))
    s_next = scores(blocks[0][0], blocks[0][1])
    for k, (tile, j, diagonal) in enumerate(blocks):
        s_cur = s_next
        if k + 1 < len(blocks):
            s_next = scores(blocks[k + 1][0], blocks[k + 1][1])
        alphas = softmax_part(s_cur, tile, k % 2, diagonal)
        accumulate(tile, alphas, value_matmul(j, k % 2))
        between_blocks()

    lam =(jnp.exp(jnp.sum(lq1_ref[...] * lk1_ref[...], axis=-1, keepdims=True))
           - jnp.exp(jnp.sum(lq2_ref[...] * lk2_ref[...], axis=-1, keepdims=True)) + lambda_init)
    for tile, o_ref in enumerate((oa_ref, ob_ref)):
        acc = acc_ref[tile]
        o_all = acc[:, 0:DA_V_DIM] / acc[:, DA_V_DIM:2 * DA_V_DIM]
        o = o_all[0:t] - lam * o_all[t:2 * t]
        o = _rms(o, on_ref[...]) * (1.0 - lambda_init)
        o_ref[...] = o.astype(o_ref.dtype)


def _alibi_slopes_log2():
    return np.asarray([2.0 ** (-8.0 * (hh + 1) / DA_HEADS) * LOG2E for hh in range(DA_HEADS)], np.float32)


def _alibi_key_rows(seq):
    col = _alibi_slopes_log2()[:, None] * np.arange(seq, dtype=np.float32)[None, :]
    b1 = col.astype(BF16)
    r1 = col - b1.astype(np.float32)
    b2 = r1.astype(BF16)
    b3 = (r1 - b2.astype(np.float32)).astype(BF16)
    rows = np.zeros((DA_HEADS, LANES, seq), BF16)
    rows[:, 0], rows[:, 1], rows[:, 2] = b1, b2, b3
    return jnp.asarray(rows)


N_ATTN_IN = 10
N_GDN_IN = 7
N_ATTN_SCRATCH = 6


def _mixer_kernel(slopes_ref, *refs, t, nq, batch, lambda_init):
    attn_in = refs[:N_ATTN_IN]
    gdn_in = refs[N_ATTN_IN:N_ATTN_IN + N_GDN_IN]
    outs = refs[N_ATTN_IN + N_GDN_IN:N_ATTN_IN + N_GDN_IN + 3]
    scratch = refs[N_ATTN_IN + N_GDN_IN + 3:]
    b, h, i = pl.program_id(0), pl.program_id(1), pl.program_id(2)
    chunk = (b * DA_HEADS + h) * (nq // 2) + i
    gdn = _gdn_phases(chunk, *gdn_in, outs[2], *scratch[N_ATTN_SCRATCH:], batch=batch)
    next(gdn)
    _attn_step(h, i, slopes_ref, *attn_in, outs[0], outs[1], *scratch[:N_ATTN_SCRATCH],
               t=t, nq=nq, lambda_init=lambda_init, between_blocks=lambda: next(gdn, None))
    for _ in gdn:
        pass


def _mixer(q, kt, v, lq1, lk1, lq2, lk2, out_norm_a, qkv3, z3, ba3, conv_taps, alog_row, dt_row, out_norm_b,
           batch, seq, t, lambda_init):
    nq = seq // t
    half = nq // 2
    width = DA_HEADS * DA_V_DIM
    c = GDN_CHUNK
    hd = GDN_HEADS * GDN_DIM
    cw = qkv3.shape[2]
    assert seq // c == batch * DA_HEADS * half
    slopes = jnp.asarray(_alibi_slopes_log2())
    vec = lambda b, h, i, sl: (0, 0)
    blk = lambda b, h, i, sl: (0, (b * DA_HEADS + h) * half + i, 0)
    grid_spec = pltpu.PrefetchScalarGridSpec(
        num_scalar_prefetch=1,
        grid=(batch, DA_HEADS, half),
        in_specs=[pl.BlockSpec((t, LANES), lambda b, h, i, sl: (b * nq + i, h)),
                  pl.BlockSpec((t, LANES), lambda b, h, i, sl: (b * nq + nq - 1 - i, h)),
                  pl.BlockSpec((LANES, seq), lambda b, h, i, sl: (h, b)),
                  pl.BlockSpec((1, LANES, seq), lambda b, h, i, sl: (h, 0, 0)),
                  pl.BlockSpec((seq, LANES), lambda b, h, i, sl: (b, h)),
                  pl.BlockSpec((1, DA_QK_DIM), vec), pl.BlockSpec((1, DA_QK_DIM), vec),
                  pl.BlockSpec((1, DA_QK_DIM), vec), pl.BlockSpec((1, DA_QK_DIM), vec),
                  pl.BlockSpec((1, DA_V_DIM), vec),
                  pl.BlockSpec((batch, c, cw), blk),
                  pl.BlockSpec((batch, c, hd), blk),
                  pl.BlockSpec((batch, c, LANES), blk),
                  pl.BlockSpec((CONV_K, cw), vec),
                  pl.BlockSpec((1, LANES), vec),
                  pl.BlockSpec((1, LANES), vec),
                  pl.BlockSpec((1, GDN_DIM), vec)],
        out_specs=[pl.BlockSpec((t, LANES), lambda b, h, i, sl: (b * half + i, h)),
                   pl.BlockSpec((t, LANES), lambda b, h, i, sl: (b * half + half - 1 - i, h)),
                   pl.BlockSpec((batch, c, hd), blk)],
        scratch_shapes=[pltpu.VMEM((2, 2 * t, 2 * LANES), BF16),
                        pltpu.VMEM((2, 2 * t, LANES), F32),
                        pltpu.VMEM((2, 2 * t, 2 * DA_V_DIM), F32),
                        pltpu.VMEM((2, 2 * t, t), BF16),
                        pltpu.VMEM((t, t), F32),
                        pltpu.VMEM((2 * LANES, seq), BF16),
                        pltpu.VMEM((batch, c + 8, cw), F32),
                        pltpu.VMEM((batch * GDN_HEADS, GDN_DIM, GDN_DIM), F32)])
    return pl.pallas_call(
        functools.partial(_mixer_kernel, t=t, nq=nq, batch=batch, lambda_init=lambda_init),
        grid_spec=grid_spec,
        out_shape=[jax.ShapeDtypeStruct((batch * seq // 2, width), BF16)] * 2
        + [jax.ShapeDtypeStruct((batch, seq, hd), BF16)],
        compiler_params=_cparams(3),
    )(slopes, q, q, kt, _alibi_key_rows(seq), v, lq1, lk1, lq2, lk2, out_norm_a,
      qkv3, z3, ba3, conv_taps, alog_row, dt_row, out_norm_b)


def _gdn_phases(step, u_ref, z_ref, ba_ref, cw_ref, alog_ref, dt_ref, on_ref, o_ref, stage_ref, state_ref, *, batch):
    c = GDN_CHUNK
    hd = GDN_HEADS * GDN_DIM

    @pl.when(step == 0)
    def _init():
        state_ref[...] = jnp.zeros(state_ref.shape, F32)
        stage_ref[:, 0:8, :] = jnp.zeros((batch, 8, stage_ref.shape[2]), F32)

    yield

    row = lax.broadcasted_iota(I32, (c, c), 0)
    col = lax.broadcasted_iota(I32, (c, c), 1)
    tri = row >= col
    strict = row > col
    tril_ones = jnp.where(tri, 1.0, 0.0).astype(BF16)

    chains = []
    for b in range(batch):
        stage_ref[b, 8:8 + c, :] = u_ref[b]
        u = stage_ref[b]
        y = cw_ref[CONV_K - 1:CONV_K, :] * u
        for back in range(1, CONV_K):
            y = y + cw_ref[CONV_K - 1 - back:CONV_K - back, :] * pltpu.roll(u, back, 0)
        stage_ref[b, 0:8, :] = stage_ref[b, c:c + 8, :]
        y = _silu(y[8:8 + c, :])

        ba = ba_ref[b]
        beta_all = _sigmoid(ba)
        g_all = -jnp.exp(alog_ref[...]) * _softplus(ba + dt_ref[...])
        gcum_all = _mm_exact_lhs(tril_ones, g_all)
        gcum_t = gcum_all.T

        for h in range(GDN_HEADS):
            q = y[:, h * GDN_DIM:(h + 1) * GDN_DIM]
            k = y[:, hd + h * GDN_DIM:hd + (h + 1) * GDN_DIM]
            v = y[:, 2 * hd + h * GDN_DIM:2 * hd + (h + 1) * GDN_DIM]
            q = q * lax.rsqrt(jnp.sum(q * q, axis=-1, keepdims=True) + EPS) * (GDN_DIM ** -0.5)
            k = k * lax.rsqrt(jnp.sum(k * k, axis=-1, keepdims=True) + EPS)
            beta = beta_all[:, h:h + 1]
            gc = gcum_all[:, GDN_HEADS + h:GDN_HEADS + h + 1]
            gr = gcum_t[GDN_HEADS + h:GDN_HEADS + h + 1, :]
            g_last = gc[c - 1:c, :]
            decay = jnp.where(tri, jnp.exp(jnp.where(tri, gc - gr, 0.0)), 0.0)
            e_gc = jnp.exp(gc)
            chains.append(dict(
                b=b, h=h, decay=decay, g_last=g_last,
                p=jnp.where(strict, -(beta * _mm_nt(k, k) * decay), 0.0),
                rhs=jnp.concatenate([v * beta, k * (beta * e_gc)], axis=1),
                qk=_mm_nt(q, k) * decay, q_dec=q * e_gc, k_dec_t=(k * jnp.exp(g_last - gc)).T))
        yield

    def same_block(size):
        return (row // size) == (col // size)

    for ch in chains:
        ch["pk"] = jnp.where(same_block(GDN_BASE), ch["p"], 0.0)
        ch["x"] = ch["pk"]
    for _ in range(int(math.log2(GDN_BASE)) - 1):
        for ch in chains:
            ch["pk"] = _mm(ch["pk"], ch["pk"])
        for ch in chains:
            ch["x"] = ch["x"] + ch["pk"] + _mm(ch["x"], ch["pk"])
        yield
    size = GDN_BASE
    while size < c:
        pair_only = same_block(2 * size) & jnp.logical_not(same_block(size))
        for ch in chains:
            e = jnp.where(pair_only, -ch["p"], 0.0)
            ch["y"] = e + _mm(ch["x"], e)
        for ch in chains:
            ch["x"] = ch["x"] - (ch["y"] + _mm(ch["y"], ch["x"]))
        size *= 2
        yield

    for ch in chains:
        ch["s_prev"] = state_ref[ch["b"] * GDN_HEADS + ch["h"]]
        ch["o_inter"] = _mm(ch["q_dec"], ch["s_prev"])
        ch["sol"] = ch["rhs"] + _mm(ch["x"], ch["rhs"])
    yield
    for ch in chains:
        ch["v_new"] = ch["sol"][:, :GDN_DIM] - _mm(ch["sol"][:, GDN_DIM:], ch["s_prev"])
    yield
    for ch in chains:
        ch["o"] = ch["o_inter"] + _mm(ch["qk"], ch["v_new"])
        ch["s_new"] = ch["s_prev"] * jnp.exp(ch["g_last"]) + _mm(ch["k_dec_t"], ch["v_new"])
    yield
    outs = [[None] * GDN_HEADS for _ in range(batch)]
    for ch in chains:
        zg = z_ref[ch["b"], :, ch["h"] * GDN_DIM:(ch["h"] + 1) * GDN_DIM]
        outs[ch["b"]][ch["h"]] = (_rms(ch["o"], on_ref[...]) * _silu(zg)).astype(o_ref.dtype)
    o_ref[...] = jnp.stack([jnp.concatenate(heads, axis=1) for heads in outs], axis=0)
    state_ref[...] = jnp.stack([ch["s_new"] for ch in chains], axis=0)


def _merge_kernel(x_ref, mod_ref, g1_ref, g2_ref, oa_lo_ref, oa_hi_ref, ob_ref, wgate_ref, wa_ref, wb_ref, wout_ref,
                  wrt_ref, brt_ref, x1_ref, h2_ref, ri_ref, rf_ref, cnt_ref, carry_ref, *, tiles_per_batch):
    @pl.when(pl.program_id(0) == 0)
    def _init():
        carry_ref[...] = jnp.zeros(carry_ref.shape, F32)

    d = x_ref.shape[1]
    mod = mod_ref[0]
    shift1, scale1, gate1 = mod[0:1, :], mod[1:2, :], mod[2:3, :]
    shift2, scale2 = mod[3:4, :], mod[4:5, :]
    first_half = (pl.program_id(0) % tiles_per_batch) < tiles_per_batch // 2
    tm = x_ref.shape[0] // MERGE_SPLIT
    pieces = [dict(rows=slice(s * tm, (s + 1) * tm)) for s in range(MERGE_SPLIT)]
    for pc in pieces:
        pc["x"] = x_ref[pc["rows"], :]
        hb = (_rms(pc["x"], g1_ref[...]) * (1.0 + scale1) + shift1).astype(BF16)
        pc["gates"] = _sigmoid(jnp.dot(hb, wgate_ref[...], preferred_element_type=F32))
    for pc in pieces:
        oa = jnp.where(first_half, oa_lo_ref[pc["rows"], :], oa_hi_ref[pc["rows"], :])
        pc["ya"] = jnp.dot(oa, wa_ref[...], preferred_element_type=F32)
        pc["yb"] = jnp.dot(ob_ref[pc["rows"], :], wb_ref[...], preferred_element_type=F32)
    for pc in pieces:
        pc["merged"] = pc["gates"][:, :d] * pc["ya"] + pc["gates"][:, d:] * pc["yb"]
    for pc in pieces:
        x1 = pc["x"] + gate1 * _mm(pc["merged"], wout_ref[...])
        x1_ref[pc["rows"], :] = x1
        pc["h2"] = _rms(x1, g2_ref[...]) * (1.0 + scale2) + shift2
        h2_ref[pc["rows"], :] = _pack_bf16_pairs(pc["h2"])
    for pc in pieces:
        pc["logits"] = _mm3_narrow(pc["h2"], wrt_ref[...]) + brt_ref[...]
    for pc in pieces:
        _route_tile(pc["logits"], ri_ref.at[:, pc["rows"]], rf_ref.at[:, pc["rows"]], cnt_ref, carry_ref)


def _merge(x2, mod3, g1, g2, oa_lo, oa_hi, ob, w_gate, w_a, w_b, w_out, w_rt, b_rt, tm, tiles_per_batch):
    n, d = x2.shape
    hp = tiles_per_batch // 2
    const = lambda i: (0, 0)
    row = lambda i: (i, 0)
    lo = lambda i: ((i // tiles_per_batch) * hp + jnp.minimum(i % tiles_per_batch, hp - 1), 0)
    hi = lambda i: ((i // tiles_per_batch) * hp + jnp.maximum(i % tiles_per_batch - hp, 0), 0)
    return pl.pallas_call(
        functools.partial(_merge_kernel, tiles_per_batch=tiles_per_batch),
        grid=(n // tm,),
        in_specs=[pl.BlockSpec((tm, d), row),
                  pl.BlockSpec((1, 6, d), lambda i: (i // tiles_per_batch, 0, 0)),
                  pl.BlockSpec((1, d), const), pl.BlockSpec((1, d), const),
                  pl.BlockSpec((tm, oa_lo.shape[1]), lo), pl.BlockSpec((tm, oa_hi.shape[1]), hi),
                  pl.BlockSpec((tm, ob.shape[1]), row),
                  pl.BlockSpec(w_gate.shape, const), pl.BlockSpec(w_a.shape, const),
                  pl.BlockSpec(w_b.shape, const), pl.BlockSpec(w_out.shape, const),
                  pl.BlockSpec(w_rt.shape, const), pl.BlockSpec((1, LANES), const)],
        out_specs=[pl.BlockSpec((tm, d), row), pl.BlockSpec((tm, d // 2), row),
                   pl.BlockSpec((8, tm), lambda i: (0, i)), pl.BlockSpec((8, tm), lambda i: (0, i)),
                   pl.BlockSpec((N_EXPERTS, LANES), const)],
        out_shape=[jax.ShapeDtypeStruct((n, d), F32), jax.ShapeDtypeStruct((n, d // 2), I32),
                   jax.ShapeDtypeStruct((8, n), I32), jax.ShapeDtypeStruct((8, n), F32),
                   jax.ShapeDtypeStruct((N_EXPERTS, LANES), F32)],
        scratch_shapes=[pltpu.VMEM((N_EXPERTS, LANES), F32)],
        compiler_params=_cparams(1),
    )(x2, mod3, g1, g2, oa_lo, oa_hi, ob, w_gate, w_a, w_b, w_out, w_rt, b_rt)


def _route_tile(lg, ri_ref, rf_ref, cnt_ref, carry_ref):
    tm = lg.shape[0]
    lgt = lg.T
    el_all = lgt[0:N_EXPERTS, :]
    gl_all = lgt[N_EXPERTS:N_EXPERTS + 8, :]
    er = lax.broadcasted_iota(I32, el_all.shape, 0)
    gr = lax.broadcasted_iota(I32, gl_all.shape, 0)
    big = jnp.int32(LANES)
    is_grp = gr < N_GROUPS
    gl = jnp.where(is_grp, gl_all, NEG_INF)
    gmax = jnp.max(gl, axis=0, keepdims=True)
    g_top = jnp.min(jnp.where(is_grp & (gl == gmax), gr, big), axis=0, keepdims=True)
    g_top_p = 1.0 / jnp.sum(jnp.where(is_grp, jnp.exp(gl - gmax), 0.0), axis=0, keepdims=True)
    in_grp = (er // EXPERTS_PER_GROUP) == g_top
    el = jnp.where(in_grp, el_all, NEG_INF)
    v1 = jnp.max(el, axis=0, keepdims=True)
    e1 = jnp.min(jnp.where(in_grp & (el == v1), er, big), axis=0, keepdims=True)
    rest = in_grp & (er != e1)
    el2 = jnp.where(rest, el_all, NEG_INF)
    v2 = jnp.max(el2, axis=0, keepdims=True)
    e2 = jnp.min(jnp.where(rest & (el2 == v2), er, big), axis=0, keepdims=True)
    ex = jnp.exp(v2 - v1)
    w1 = g_top_p / (1.0 + ex)
    w2 = g_top_p * ex / (1.0 + ex)

    oh1 = er == e1
    oh2 = er == e2
    ohs = (jnp.where(oh1, 1.0, 0.0) + jnp.where(oh2, 1.0, 0.0)).astype(BF16)
    r = lax.broadcasted_iota(I32, (tm, tm), 0)
    c = lax.broadcasted_iota(I32, (tm, tm), 1)
    earlier = jnp.where(r < c, 1.0, 0.0).astype(BF16)
    prior = jnp.dot(ohs, earlier, preferred_element_type=F32) + jnp.tile(carry_ref[...], (1, tm // LANES))
    rank1 = jnp.sum(jnp.where(oh1, prior, 0.0), axis=0, keepdims=True)
    rank2 = jnp.sum(jnp.where(oh2, prior, 0.0), axis=0, keepdims=True)
    carry_ref[...] = carry_ref[...] + jnp.sum(ohs.astype(F32), axis=1, keepdims=True)
    cnt_ref[...] = carry_ref[...]

    row = lax.broadcasted_iota(I32, (8, tm), 0)
    r1 = rank1.astype(I32)
    r2 = rank2.astype(I32)
    ri_ref[...] = jnp.where(row == 0, e1, jnp.where(row == 1, e2, jnp.where(row == 2, r1, jnp.where(row == 3, r2, 0))))
    rf_ref[...] = jnp.where(row == 0, w1, jnp.where(row == 1, w2, 0.0))


def _moe_kernel(be_ref, act_ref, slot_ref, nxt_ref, xb_ref, w1_hbm, w3_hbm, w2_hbm, o_ref,
                w1b_ref, w3b_ref, w2b_ref, w1f_ref, w3f_ref, w2f_ref, sem_ref):
    i = pl.program_id(0)
    prev = be_ref[jnp.maximum(i - 1, 0)]
    slot = slot_ref[i]

    def copies(expert, s):
        return [pltpu.make_async_copy(w1_hbm.at[expert], w1f_ref.at[s], sem_ref.at[0, s]),
                pltpu.make_async_copy(w3_hbm.at[expert], w3f_ref.at[s], sem_ref.at[1, s]),
                pltpu.make_async_copy(w2_hbm.at[expert], w2f_ref.at[s], sem_ref.at[2, s])]

    @pl.when(i == 0)
    def _first_request():
        for cp in copies(be_ref[0], 0):
            cp.start()

    @pl.when((i == 0) | (be_ref[i] != prev))
    def _expert_start():
        for cp in copies(be_ref[i], slot):
            cp.wait()
        w1b_ref[...] = w1f_ref[slot].astype(BF16)
        w3b_ref[...] = w3f_ref[slot].astype(BF16)
        w2b_ref[...] = w2f_ref[slot].astype(BF16)

        @pl.when(nxt_ref[i] >= 0)
        def _request_next():
            for cp in copies(nxt_ref[i], 1 - slot):
                cp.start(priority=1)

    valid = act_ref[i]

    def expert_mlp(n_rows):
        xp = xb_ref[0:n_rows, :]
        r = lax.broadcasted_iota(I32, xp.shape, 0)
        xb = _unpack_bf16_pairs(jnp.where(r < valid, xp, 0))
        a = jnp.dot(xb, w1b_ref[...], preferred_element_type=F32)
        g = jnp.dot(xb, w3b_ref[...], preferred_element_type=F32)
        hid = (_silu(a) * g).astype(BF16)
        o_ref[0:n_rows, :] = _pack_bf16_pairs(jnp.dot(hid, w2b_ref[...], preferred_element_type=F32))

    @pl.when(valid > MOE_SUB)
    def _full():
        expert_mlp(MOE_ROWS)

    @pl.when((valid > 0) & (valid <= MOE_SUB))
    def _partial():
        expert_mlp(MOE_SUB)
        o_ref[MOE_SUB:MOE_ROWS, :] = jnp.zeros((MOE_ROWS - MOE_SUB, o_ref.shape[1]), o_ref.dtype)

    @pl.when(valid == 0)
    def _idle():
        o_ref[...] = jnp.zeros(o_ref.shape, o_ref.dtype)


def _moe(blk_expert, blk_rows, blk_slot, blk_next, xb, w1, w3, w2):
    cap = xb.shape[0]
    d = w1.shape[1]
    nb = cap // MOE_ROWS
    de = w1.shape[2]
    grid_spec = pltpu.PrefetchScalarGridSpec(
        num_scalar_prefetch=4,
        grid=(nb,),
        in_specs=[pl.BlockSpec((MOE_ROWS, d // 2), lambda i, be, act, sl, nx: (i, 0)),
                  pl.BlockSpec(memory_space=pl.ANY), pl.BlockSpec(memory_space=pl.ANY),
                  pl.BlockSpec(memory_space=pl.ANY)],
        out_specs=pl.BlockSpec((MOE_ROWS, d // 2), lambda i, be, act, sl, nx: (i, 0)),
        scratch_shapes=[pltpu.VMEM((d, de), BF16), pltpu.VMEM((d, de), BF16), pltpu.VMEM((de, d), BF16),
                        pltpu.VMEM((2, d, de), F32), pltpu.VMEM((2, d, de), F32), pltpu.VMEM((2, de, d), F32),
                        pltpu.SemaphoreType.DMA((3, 2))])
    return pl.pallas_call(
        _moe_kernel,
        grid_spec=grid_spec,
        out_shape=jax.ShapeDtypeStruct((cap, d // 2), I32),
        compiler_params=_cparams(1),
    )(blk_expert, blk_rows, blk_slot, blk_next, xb, w1, w3, w2)


def _dest_kernel(ri_ref, pstart_ref, o_ref):
    ri = ri_ref[...]
    tm = ri.shape[1]
    er = lax.broadcasted_iota(I32, (N_EXPERTS, tm), 0)
    pstart = jnp.tile(pstart_ref[...], (1, tm // LANES))
    base1 = jnp.sum(jnp.where(er == ri[0:1, :], pstart, 0), axis=0, keepdims=True)
    base2 = jnp.sum(jnp.where(er == ri[1:2, :], pstart, 0), axis=0, keepdims=True)
    row = lax.broadcasted_iota(I32, ri.shape, 0)
    o_ref[...] = jnp.where(row == 0, base1 + ri[2:3, :], jnp.where(row == 1, base2 + ri[3:4, :], 0))


def _dest(ri, pstart_col, tm):
    n = ri.shape[1]
    return pl.pallas_call(
        _dest_kernel,
        grid=(n // tm,),
        in_specs=[pl.BlockSpec((8, tm), lambda i: (0, i)), pl.BlockSpec((N_EXPERTS, LANES), lambda i: (0, 0))],
        out_specs=pl.BlockSpec((8, tm), lambda i: (0, i)),
        out_shape=jax.ShapeDtypeStruct((8, n), I32),
        compiler_params=_cparams(1),
    )(ri, pstart_col)


def _sc_mesh():
    return plsc.VectorSubcoreMesh(core_axis_name="core", subcore_axis_name="subcore")


def _sc_scatter_rows(x, idx, cap):
    n, d = x.shape
    windows = idx.shape[0]
    tiles = n // SC_WINDOW
    mesh = _sc_mesh()
    workers = mesh.num_cores * mesh.num_subcores
    per_worker = windows // workers

    @functools.partial(pl.kernel, out_type=jax.ShapeDtypeStruct((cap, d), x.dtype), mesh=mesh,
                       scratch_types=[pltpu.VMEM((1, SC_WINDOW), I32), pltpu.VMEM((SC_WINDOW, d), x.dtype)])
    def scatter(x_hbm, i_hbm, o_hbm, idx_v, rows_v):
        wid = lax.axis_index("subcore") * mesh.num_cores + lax.axis_index("core")

        @pl.loop(0, per_worker)
        def _(j):
            w = wid * per_worker + j
            pltpu.sync_copy(i_hbm.at[pl.ds(w, 1)], idx_v)
            pltpu.sync_copy(x_hbm.at[pl.ds(lax.rem(w, tiles) * SC_WINDOW, SC_WINDOW)], rows_v)
            pltpu.sync_copy(rows_v, o_hbm.at[idx_v.at[0]])

    return scatter(x, idx)


def _sc_gather_rows(table, idx):
    d = table.shape[1]
    windows = idx.shape[0]
    mesh = _sc_mesh()
    workers = mesh.num_cores * mesh.num_subcores
    per_worker = windows // workers

    @functools.partial(pl.kernel, out_type=jax.ShapeDtypeStruct((windows * SC_WINDOW, d), table.dtype), mesh=mesh,
                       scratch_types=[pltpu.VMEM((1, SC_WINDOW), I32), pltpu.VMEM((SC_WINDOW, d), table.dtype)])
    def gather(t_hbm, i_hbm, o_hbm, idx_v, rows_v):
        wid = lax.axis_index("subcore") * mesh.num_cores + lax.axis_index("core")

        @pl.loop(0, per_worker)
        def _(j):
            w = wid * per_worker + j
            pltpu.sync_copy(i_hbm.at[pl.ds(w, 1)], idx_v)
            pltpu.sync_copy(t_hbm.at[idx_v.at[0]], rows_v)
            pltpu.sync_copy(rows_v, o_hbm.at[pl.ds(w * SC_WINDOW, SC_WINDOW)])

    return gather(table, idx)


def _combine_kernel(x1_ref, mod_ref, rf_ref, ya_ref, yb_ref, o_ref):
    gate2 = mod_ref[0][5:6, :]
    rf = rf_ref[...].T
    ya = _unpack_bf16_pairs(ya_ref[...]).astype(F32)
    yb = _unpack_bf16_pairs(yb_ref[...]).astype(F32)
    y = rf[:, 0:1] * ya + rf[:, 1:2] * yb
    o_ref[...] = x1_ref[...] + gate2 * y


def _combine(x1, mod3, rf, y12, tm, tiles_per_batch):
    n, d = x1.shape
    tiles = n // tm
    row = lambda i: (i, 0)
    return pl.pallas_call(
        _combine_kernel,
        grid=(tiles,),
        in_specs=[pl.BlockSpec((tm, d), row),
                  pl.BlockSpec((1, 6, d), lambda i: (i // tiles_per_batch, 0, 0)),
                  pl.BlockSpec((8, tm), lambda i: (0, i)),
                  pl.BlockSpec((tm, d // 2), row), pl.BlockSpec((tm, d // 2), lambda i: (i + tiles, 0))],
        out_specs=pl.BlockSpec((tm, d), row),
        out_shape=jax.ShapeDtypeStruct((n, d), F32),
        compiler_params=_cparams(1),
    )(x1, mod3, rf, y12, y12)


def _pad_lanes(a, offset=0):
    return jnp.pad(a, ((0, 0), (offset, LANES - offset - a.shape[1])))


def _layer(x, c, layer, w_ada, b_ada, norm1_gain, w_in, da_q_norm, da_k_norm, lq1, lk1, lq2, lk2, da_out_norm,
           gdn_conv, gdn_a_log, gdn_dt_bias, gdn_out_norm, w_branch_a, w_branch_b, w_out, norm2_gain,
           w_group, b_group, w_router, b_router, w1, w3, w2):
    batch, seq, d = x.shape
    n = batch * seq
    tm = min(512, seq)
    tiles_per_batch = seq // tm
    t_attn = min(512, seq)
    lambda_init = 0.8 - 0.6 * math.exp(-0.3 * layer)

    mod = _ada(jnp.pad(c, ((0, 8 - batch), (0, 0))), w_ada, b_ada)[:batch]
    mod3 = mod.reshape(batch, 6, d)

    da_w = DA_HEADS * 2 * DA_QK_DIM
    dv_w = DA_HEADS * DA_V_DIM
    gd_w = GDN_HEADS * GDN_DIM
    o0 = 2 * da_w + dv_w
    o1 = o0 + 4 * gd_w
    o2 = o1 + 2 * GDN_HEADS
    w_da = w_in[:, :o0].astype(BF16)
    w_g = w_in[:, o0:o1].astype(BF16)
    w_ba = _pad_lanes(w_in[:, o1:o2]).astype(BF16)
    w_gate = w_in[:, o2:].astype(BF16)
    qn = jnp.tile(da_q_norm, 2 * DA_HEADS).reshape(1, da_w)
    kn = jnp.tile(da_k_norm, 2 * DA_HEADS).reshape(1, da_w)

    x2 = x.reshape(n, d)
    g1 = norm1_gain.reshape(1, d)
    q, kt, v, gqkv, z, ba = _proj(x2, mod3, g1, w_da, w_g, w_ba, qn, kn, tm, tiles_per_batch)

    assert t_attn == tm
    alog_row = _pad_lanes(gdn_a_log.reshape(1, -1), GDN_HEADS)
    dt_row = _pad_lanes(gdn_dt_bias.reshape(1, -1), GDN_HEADS)
    oa_lo, oa_hi, o_b = _mixer(q, kt, v, lq1.reshape(1, -1), lk1.reshape(1, -1), lq2.reshape(1, -1),
                               lk2.reshape(1, -1), da_out_norm.reshape(1, -1),
                               gqkv.reshape(batch, seq, -1), z.reshape(batch, seq, -1), ba.reshape(batch, seq, -1),
                               gdn_conv, alog_row, dt_row, gdn_out_norm.reshape(1, -1),
                               batch, seq, t_attn, lambda_init)
    o_b = o_b.reshape(n, gd_w)

    w_rt = _pad_lanes(jnp.concatenate([w_router, w_group], axis=1))
    b_rt = _pad_lanes(jnp.concatenate([b_router, b_group]).reshape(1, -1))
    x1, h2, ri, rf, cnt = _merge(x2, mod3, g1, norm2_gain.reshape(1, d), oa_lo, oa_hi, o_b, w_gate,
                                 w_branch_a.astype(BF16), w_branch_b.astype(BF16), w_out.astype(BF16),
                                 w_rt, b_rt, tm * MERGE_SPLIT, tiles_per_batch // MERGE_SPLIT)

    counts = cnt[:, 0].astype(I32)
    padded = (counts + MOE_ROWS - 1) // MOE_ROWS * MOE_ROWS
    pend = jnp.cumsum(padded)
    pstart = pend - padded
    cap = 2 * n + N_EXPERTS * MOE_ROWS
    nb = cap // MOE_ROWS
    blk_start = jnp.arange(nb, dtype=I32) * MOE_ROWS
    blk_expert = jnp.minimum(jnp.sum(pend[None, :] <= blk_start[:, None], axis=1), N_EXPERTS - 1).astype(I32)
    blk_rows = jnp.clip((pstart + counts)[blk_expert] - blk_start, 0, MOE_ROWS).astype(I32)
    blk_rows = jnp.where(blk_start < pend[-1], blk_rows, 0)
    blk_idx = jnp.arange(nb, dtype=I32)
    run_start = jnp.concatenate([jnp.ones((1,), bool), blk_expert[1:] != blk_expert[:-1]])
    blk_slot = ((jnp.cumsum(run_start.astype(I32)) - 1) % 2).astype(I32)
    start_at_or_after = lax.cummin(jnp.where(run_start, blk_idx, nb)[::-1])[::-1]
    next_start = jnp.concatenate([start_at_or_after[1:], jnp.full((1,), nb, I32)])
    blk_next = jnp.where(next_start < nb, blk_expert[jnp.minimum(next_start, nb - 1)], -1).astype(I32)
    pstart_col = jnp.broadcast_to(pstart[:, None], (N_EXPERTS, LANES))
    dest = _dest(ri, pstart_col, min(4 * tm, n))
    idx = dest[0:2].reshape(2 * n // SC_WINDOW, SC_WINDOW)

    xb = _sc_scatter_rows(h2, idx, cap)
    yb = _moe(blk_expert, blk_rows, blk_slot, blk_next, xb, w1, w3, w2)
    y12 = _sc_gather_rows(yb, idx)
    out = _combine(x1, mod3, rf, y12, 2 * tm, tiles_per_batch // 2)
    return out.reshape(batch, seq, d)


def kernel(x, c, w_ada, b_ada, norm1_gain, w_in, da_q_norm, da_k_norm, da_lambda_q1, da_lambda_k1, da_lambda_q2,
           da_lambda_k2, da_out_norm, gdn_conv, gdn_a_log, gdn_dt_bias, gdn_out_norm, w_branch_a, w_branch_b,
           w_out, norm2_gain, w_group, b_group, w_router, b_router, w1, w3, w2):
    for layer in range(w_ada.shape[0]):
        x = _layer(x, c, layer, w_ada[layer], b_ada[layer], norm1_gain[layer], w_in[layer], da_q_norm[layer],
                   da_k_norm[layer], da_lambda_q1[layer], da_lambda_k1[layer], da_lambda_q2[layer],
                   da_lambda_k2[layer], da_out_norm[layer], gdn_conv[layer], gdn_a_log[layer], gdn_dt_bias[layer],
                   gdn_out_norm[layer], w_branch_a[layer], w_branch_b[layer], w_out[layer], norm2_gain[layer],
                   w_group[layer], b_group[layer], w_router[layer], b_router[layer], w1[layer], w3[layer], w2[layer])
    return x
```
